```python
import functools
import jax, jax.numpy as jnp
from jax import lax
import numpy as np

D_MODEL = 2048
BATCH = 4
SEQ = 2048
DEPTH = 2
DEC_BATCH = 128
DEC_SEQ = 1
PAST_LEN = 8192
PAGE_SIZE = 128

D_MIX = D_MODEL
D_ATTN = D_MIX // 2
D_LRU = D_MIX // 4
D_SC = D_MIX - D_ATTN - D_LRU
HEAD_DIM = 64
N_HEADS = D_ATTN // HEAD_DIM
N_KV_HEADS = 4
N_GROUP = N_HEADS // N_KV_HEADS
D_KV = N_KV_HEADS * HEAD_DIM
WINDOW = 128
ATTN_BLOCK = WINDOW
ROPE_THETA = 10000.0
N_LRU_HEADS = 8
LRU_BLK = D_LRU // N_LRU_HEADS
LRU_CONV_W = 4
LRU_C = 8.0
SC_CONV_W = 3
D_FF = (-(-8 * D_MODEL // 3) + 255) // 256 * 256
D_IN = D_ATTN + 2 * D_KV + 2 * D_LRU + 3 * D_SC
RMS_EPS = 1e-6

kernel_name = 'hymba_rglru_swa_shortconv_step'


def rms_norm(x, g):
    xf = x.astype(jnp.float32)
    y = xf * lax.rsqrt(jnp.mean(xf * xf, axis=-1, keepdims=True) + RMS_EPS)
    return (y * g.astype(jnp.float32)).astype(x.dtype)


def rope(x, pos):
    half = HEAD_DIM // 2
    inv = ROPE_THETA ** (-jnp.arange(half, dtype=jnp.float32) / half)
    ang = pos.astype(jnp.float32)[:, None] * inv[None, :]
    cos = jnp.cos(ang)[None, :, None, :]
    sin = jnp.sin(ang)[None, :, None, :]
    xf = x.astype(jnp.float32)
    x1, x2 = xf[..., :half], xf[..., half:]
    return jnp.concatenate([x1 * cos - x2 * sin, x2 * cos + x1 * sin], axis=-1).astype(x.dtype)


def causal_dwconv(u, buf, w):
    K = w.shape[0]
    T = u.shape[1]
    ext = jnp.concatenate([buf.astype(u.dtype), u], axis=1)
    y = ext[:, 0:T] * w[0]
    for k in range(1, K):
        y = y + ext[:, k:k + T] * w[k]
    return y, ext[:, T:]


def linear_scan(a, b, h0):
    b = b.at[:, 0].add(a[:, 0] * h0)
    def combine(l, r):
        al, bl = l
        ar, br = r
        return al * ar, ar * bl + br
    _, h = lax.associative_scan(combine, (a, b), axis=1)
    return h


def rglru_mixer(u_x, u_gate, conv_buf, h0, pos, conv_w, conv_b, w_a, b_a, w_i, b_i, lam):
    xc, new_buf = causal_dwconv(u_x, conv_buf, conv_w)
    xc = xc + conv_b
    N, T, _ = xc.shape
    xf = xc.astype(jnp.float32)
    xh = xf.reshape(N, T, N_LRU_HEADS, LRU_BLK)
    r = jax.nn.sigmoid(jnp.einsum('nthi,hij->nthj', xh, w_a.astype(jnp.float32)).reshape(N, T, D_LRU) + b_a.astype(jnp.float32))
    i = jax.nn.sigmoid(jnp.einsum('nthi,hij->nthj', xh, w_i.astype(jnp.float32)).reshape(N, T, D_LRU) + b_i.astype(jnp.float32))
    log_a = -LRU_C * r * jax.nn.softplus(-lam.astype(jnp.float32))
    a = jnp.exp(log_a)
    mult = jnp.sqrt(-jnp.expm1(2.0 * log_a))
    mult = jnp.where((pos == 0)[None, :, None], 1.0, mult)
    h = linear_scan(a, mult * i * xf, h0.astype(jnp.float32))
    y = h.astype(u_x.dtype) * jax.nn.gelu(u_gate)
    return y, new_buf, h[:, -1].astype(u_x.dtype)


def short_conv_mixer(u_b, u_c, u_h, buf, w):
    y, new_buf = causal_dwconv(u_c * u_h, buf, w)
    return u_b * y, new_buf


def sink_probs(s, mask, sink):
    s = jnp.where(mask, s, -jnp.inf)
    m = jnp.maximum(jnp.max(s, axis=-1, keepdims=True), sink)
    p = jnp.exp(s - m)
    return p / (jnp.sum(p, axis=-1, keepdims=True) + jnp.exp(sink - m))


def swa_prompt(q, k, v, sinks):
    N, T = q.shape[0], q.shape[1]
    L = ATTN_BLOCK
    nb = T // L
    qb = q.reshape(N, nb, L, N_KV_HEADS, N_GROUP, HEAD_DIM)
    def band(t):
        tp = jnp.concatenate([jnp.zeros_like(t[:, :L]), t], axis=1).reshape(N, nb + 1, L, N_KV_HEADS, HEAD_DIM)
        return jnp.concatenate([tp[:, :-1], tp[:, 1:]], axis=2)
    kb, vb = band(k), band(v)
    s = jnp.einsum('nbqkgd,nbskd->nbkgqs', qb, kb, preferred_element_type=jnp.float32) * (HEAD_DIM ** -0.5)
    qi = jnp.arange(L)[:, None]
    sj = jnp.arange(2 * L)[None, :]
    diff = L + qi - sj
    k_pos = (jnp.arange(nb)[:, None, None] - 1) * L + sj[None]
    mask = (diff >= 0) & (diff < WINDOW) & (k_pos >= 0)
    mask = mask[None, :, None, None]
    sink = sinks.astype(jnp.float32).reshape(N_KV_HEADS, N_GROUP)[None, None, :, :, None, None]
    p = sink_probs(s, mask, sink)
    o = jnp.einsum('nbkgqs,nbskd->nbqkgd', p.astype(vb.dtype), vb).reshape(N, T, D_ATTN)
    wb = min(WINDOW, T)
    return o, k[:, T - wb:], v[:, T - wb:]


def swa_decode(q, k, v, k_buf, v_buf, sinks, pos0):
    N, T = q.shape[0], q.shape[1]
    wb = k_buf.shape[1]
    kk = jnp.concatenate([k_buf.astype(k.dtype), k], axis=1)
    vv = jnp.concatenate([v_buf.astype(v.dtype), v], axis=1)
    qg = q.reshape(N, T, N_KV_HEADS, N_GROUP, HEAD_DIM)
    s = jnp.einsum('ntkgd,nskd->nkgts', qg, kk, preferred_element_type=jnp.float32) * (HEAD_DIM ** -0.5)
    q_pos = pos0 + jnp.arange(T)
    k_pos = pos0 - wb + jnp.arange(wb + T)
    diff = q_pos[:, None] - k_pos[None, :]
    mask = ((diff >= 0) & (diff < WINDOW))[None, None, None]
    sink = sinks.astype(jnp.float32).reshape(N_KV_HEADS, N_GROUP)[None, :, :, None, None]
    p = sink_probs(s, mask, sink)
    o = jnp.einsum('nkgts,nskd->ntkgd', p.astype(vv.dtype), vv).reshape(N, T, D_ATTN)
    return o, kk[:, T:], vv[:, T:]


def trunk_layer(x, pos, lru_h0, lru_buf, sc_buf, attend, norm_mix, w_in, norm_grp, w_out,
                lru_conv_w, lru_conv_b, lru_w_a, lru_b_a, lru_w_i, lru_b_i, lru_lambda,
                sc_conv_w, norm_ffn, ffn_w_gu, ffn_w_down):
    N, T, _ = x.shape
    h = rms_norm(x, norm_mix)
    z = h @ w_in
    sizes = [D_ATTN, D_KV, D_KV, D_LRU, D_LRU, D_SC, D_SC, D_SC]
    offs = []
    acc = 0
    for sz in sizes[:-1]:
        acc += sz
        offs.append(acc)
    q, k, v, u_x, u_gate, u_b, u_c, u_h = jnp.split(z, offs, axis=-1)
    q = rope(q.reshape(N, T, N_HEADS, HEAD_DIM), pos)
    k = rope(k.reshape(N, T, N_KV_HEADS, HEAD_DIM), pos)
    v = v.reshape(N, T, N_KV_HEADS, HEAD_DIM)
    o_attn, k_state, v_state = attend(q, k, v)
    o_lru, lru_buf_new, lru_h_new = rglru_mixer(u_x, u_gate, lru_buf, lru_h0, pos, lru_conv_w, lru_conv_b,
                                                lru_w_a, lru_b_a, lru_w_i, lru_b_i, lru_lambda)
    o_sc, sc_buf_new = short_conv_mixer(u_b, u_c, u_h, sc_buf, sc_conv_w)
    g_attn, g_lru, g_sc = jnp.split(norm_grp, [D_ATTN, D_ATTN + D_LRU])
    mixed = jnp.concatenate([rms_norm(o_attn, g_attn), rms_norm(o_lru, g_lru), rms_norm(o_sc, g_sc)], axis=-1)
    x = x + mixed @ w_out
    hf = rms_norm(x, norm_ffn)
    gate, up = jnp.split(hf @ ffn_w_gu, 2, axis=-1)
    x = x + (jax.nn.silu(gate) * up) @ ffn_w_down
    return x, (lru_h_new, lru_buf_new, k_state, v_state, sc_buf_new)


def setup_inputs(seed: int = 0) -> dict:
    key = jax.random.key(seed)
    ks = jax.random.split(key, 24)
    f32 = jnp.float32
    def nrm(k, shape, scale):
        return jax.random.normal(k, shape, f32) * scale
    def gain(k, shape):
        return 1.0 + 0.02 * jax.random.normal(k, shape, f32)
    wb = min(WINDOW, PAST_LEN)
    a0 = jax.random.uniform(ks[17], (DEPTH, D_LRU), f32, 0.9, 0.999)
    s0 = a0 ** (1.0 / LRU_C)
    return {
        'x_prompt': nrm(ks[0], (BATCH, SEQ, D_MODEL), 1.0),
        'x_sample': nrm(ks[1], (DEC_BATCH, DEC_SEQ, D_MODEL), 1.0),
        'state_lru_h': nrm(ks[2], (DEPTH, DEC_BATCH, D_LRU), 0.5),
        'state_lru_conv': nrm(ks[3], (DEPTH, DEC_BATCH, LRU_CONV_W - 1, D_LRU), 0.5),
        'cache_swa_k': nrm(ks[4], (DEPTH, DEC_BATCH, wb, N_KV_HEADS, HEAD_DIM), 1.0),
        'cache_swa_v': nrm(ks[5], (DEPTH, DEC_BATCH, wb, N_KV_HEADS, HEAD_DIM), 1.0),
        'state_sconv': nrm(ks[6], (DEPTH, DEC_BATCH, SC_CONV_W - 1, D_SC), 0.5),
        'norm_mix': gain(ks[7], (DEPTH, D_MODEL)),
        'w_in': nrm(ks[8], (DEPTH, D_MODEL, D_IN), D_MODEL ** -0.5),
        'norm_grp': gain(ks[9], (DEPTH, D_MIX)),
        'w_out': nrm(ks[10], (DEPTH, D_MIX, D_MODEL), D_MIX ** -0.5),
        'lru_conv_w': nrm(ks[11], (DEPTH, LRU_CONV_W, D_LRU), LRU_CONV_W ** -0.5),
        'lru_conv_b': nrm(ks[12], (DEPTH, D_LRU), 0.01),
        'lru_w_a': nrm(ks[13], (DEPTH, N_LRU_HEADS, LRU_BLK, LRU_BLK), LRU_BLK ** -0.5),
        'lru_b_a': nrm(ks[14], (DEPTH, D_LRU), 0.01),
        'lru_w_i': nrm(ks[15], (DEPTH, N_LRU_HEADS, LRU_BLK, LRU_BLK), LRU_BLK ** -0.5),
        'lru_b_i': nrm(ks[16], (DEPTH, D_LRU), 0.01),
        'lru_lambda': jnp.log(s0) - jnp.log1p(-s0),
        'sc_conv_w': nrm(ks[18], (DEPTH, SC_CONV_W, D_SC), SC_CONV_W ** -0.5),
        'attn_sinks': nrm(ks[19], (DEPTH, N_HEADS), 0.5),
        'norm_ffn': gain(ks[20], (DEPTH, D_MODEL)),
        'ffn_w_gu': nrm(ks[21], (DEPTH, D_MODEL, 2 * D_FF), D_MODEL ** -0.5),
        'ffn_w_down': nrm(ks[22], (DEPTH, D_FF, D_MODEL), D_FF ** -0.5),
        'norm_final': gain(ks[23], (D_MODEL,)),
    }


def reference(x_prompt, x_sample, state_lru_h, state_lru_conv, cache_swa_k, cache_swa_v, state_sconv,
              norm_mix, w_in, norm_grp, w_out, lru_conv_w, lru_conv_b, lru_w_a, lru_b_a, lru_w_i, lru_b_i,
              lru_lambda, sc_conv_w, attn_sinks, norm_ffn, ffn_w_gu, ffn_w_down, norm_final):
    nb_p, t_p = x_prompt.shape[0], x_prompt.shape[1]
    t_s = x_sample.shape[1]
    pos_p = jnp.arange(t_p, dtype=jnp.int32)
    pos_s = PAST_LEN + jnp.arange(t_s, dtype=jnp.int32)
    xp, xs = x_prompt, x_sample
    p_states, s_states = [], []
    for l in range(DEPTH):
        lw = dict(norm_mix=norm_mix[l], w_in=w_in[l], norm_grp=norm_grp[l], w_out=w_out[l],
                  lru_conv_w=lru_conv_w[l], lru_conv_b=lru_conv_b[l], lru_w_a=lru_w_a[l], lru_b_a=lru_b_a[l],
                  lru_w_i=lru_w_i[l], lru_b_i=lru_b_i[l], lru_lambda=lru_lambda[l], sc_conv_w=sc_conv_w[l],
                  norm_ffn=norm_ffn[l], ffn_w_gu=ffn_w_gu[l], ffn_w_down=ffn_w_down[l])
        xp, st_p = trunk_layer(
            xp, pos_p,
            jnp.zeros((nb_p, D_LRU), xp.dtype),
            jnp.zeros((nb_p, LRU_CONV_W - 1, D_LRU), xp.dtype),
            jnp.zeros((nb_p, SC_CONV_W - 1, D_SC), xp.dtype),
            functools.partial(swa_prompt, sinks=attn_sinks[l]),
            **lw)
        p_states.append(st_p)
        xs, st_s = trunk_layer(
            xs, pos_s, state_lru_h[l], state_lru_conv[l], state_sconv[l],
            functools.partial(swa_decode, k_buf=cache_swa_k[l], v_buf=cache_swa_v[l],
                              sinks=attn_sinks[l], pos0=PAST_LEN),
            **lw)
        s_states.append(st_s)
    y_prompt = rms_norm(xp, norm_final)
    y_sample = rms_norm(xs, norm_final)
    p_lru_h = jnp.stack([st[0] for st in p_states])
    p_lru_conv = jnp.stack([st[1] for st in p_states])
    p_swa_k = jnp.stack([st[2] for st in p_states])
    p_swa_v = jnp.stack([st[3] for st in p_states])
    p_sconv = jnp.stack([st[4] for st in p_states])
    s_lru_h = jnp.stack([st[0] for st in s_states])
    s_lru_conv = jnp.stack([st[1] for st in s_states])
    s_swa_k = jnp.stack([st[2] for st in s_states])
    s_swa_v = jnp.stack([st[3] for st in s_states])
    s_sconv = jnp.stack([st[4] for st in s_states])
    return (y_prompt, y_sample, p_lru_h, p_lru_conv, p_swa_k, p_swa_v, p_sconv,
            s_lru_h, s_lru_conv, s_swa_k, s_swa_v, s_sconv)
```

```python
import functools

import jax
import jax.numpy as jnp
from jax import lax
from jax.experimental import pallas as pl
from jax.experimental.pallas import tpu as pltpu

F32 = jnp.float32
BF16 = jnp.bfloat16

D_MODEL = 2048
D_ATTN = 1024
D_LRU = 512
D_SC = 512
HEAD_DIM = 64
N_HEADS = 16
N_KV_HEADS = 4
N_GROUP = 4
D_KV = 256
WINDOW = 128
ROPE_THETA = 10000.0
N_LRU_HEADS = 8
LRU_BLK = 64
LRU_CONV_W = 4
LRU_C = 8.0
SC_CONV_W = 3
D_FF = 5632
D_IN = 4096
RMS_EPS = 1e-6
PAST_LEN = 8192

LANES = 128
SUBLANES = 8
VMEM_LIMIT_BYTES = 56 * 1024 * 1024

_COL_UX, _COL_GATE, _COL_B, _COL_C, _COL_H = 3, 4, 5, 6, 7


def _cparams(sem):
    return pltpu.CompilerParams(dimension_semantics=sem, vmem_limit_bytes=VMEM_LIMIT_BYTES)


def _rms(x, g):
    return x * lax.rsqrt(jnp.mean(x * x, axis=-1, keepdims=True) + RMS_EPS) * g


def _in_proj_kernel(x_ref, g_ref, w_ref, cos_ref, sin_ref, z_ref, h_scr, *, bn):
    j = pl.program_id(1)

    @pl.when(j == 0)
    def _():
        h_scr[...] = _rms(x_ref[...], g_ref[...]).astype(BF16)

    acc = jnp.dot(h_scr[...], w_ref[...], preferred_element_type=F32)
    bm = acc.shape[0]
    lane = lax.broadcasted_iota(jnp.int32, (bm, LANES), 1)
    lo32 = (lane % HEAD_DIM) < (HEAD_DIM // 2)
    cos = cos_ref[...]
    sin = sin_ref[...]

    def rope(a):
        sw = jnp.where(lo32, pltpu.roll(a, LANES - HEAD_DIM // 2, 1), pltpu.roll(a, HEAD_DIM // 2, 1))
        return a * cos + sw * sin

    n_chunks = bn // LANES
    rope_cols = D_ATTN + D_KV
    for tile in range(D_IN // bn):
        n_rope = max(0, min(n_chunks, (rope_cols - tile * bn) // LANES))

        @pl.when(j == tile)
        def _(n_rope=n_rope):
            for c in range(n_chunks):
                a = acc[:, c * LANES:(c + 1) * LANES]
                z_ref[:, c * LANES:(c + 1) * LANES] = rope(a) if c < n_rope else a


def _in_proj(x, g, w_bf, cos_t, sin_t, *, bm, bn=1024):
    m = x.shape[0]
    n_tab = cos_t.shape[0] // bm
    return pl.pallas_call(
        functools.partial(_in_proj_kernel, bn=bn),
        grid=(m // bm, D_IN // bn),
        in_specs=[
            pl.BlockSpec((bm, D_MODEL), lambda i, j: (i, 0)),
            pl.BlockSpec((1, D_MODEL), lambda i, j: (0, 0)),
            pl.BlockSpec((D_MODEL, bn), lambda i, j: (0, j)),
            pl.BlockSpec((bm, LANES), lambda i, j: (i % n_tab, 0)),
            pl.BlockSpec((bm, LANES), lambda i, j: (i % n_tab, 0)),
        ],
        out_specs=pl.BlockSpec((bm, bn), lambda i, j: (i, j)),
        out_shape=jax.ShapeDtypeStruct((m, D_IN), F32),
        scratch_shapes=[pltpu.VMEM((bm, D_MODEL), BF16)],
        compiler_params=_cparams(("parallel", "arbitrary")),
        name="in_proj",
    )(x, g, w_bf, cos_t, sin_t)


def _attn_prompt_kernel(sink_ref, q_ref, kc_ref, kp_ref, vc_ref, vp_ref, g_ref, o_ref):
    b = pl.program_id(1)
    L = WINDOW
    qb = (q_ref[...] * (HEAD_DIM ** -0.5)).astype(BF16)
    kk = jnp.concatenate([kp_ref[...], kc_ref[...]], axis=0)
    vv = jnp.concatenate([vp_ref[...], vc_ref[...]], axis=0)

    lane = lax.broadcasted_iota(jnp.int32, (2 * L, LANES), 1)
    lo = lane < HEAD_DIM
    row = lax.broadcasted_iota(jnp.int32, (2 * L, 1), 0)
    top = row < L

    qi = lax.broadcasted_iota(jnp.int32, (2 * L, 4 * L), 0) % L
    sj = lax.broadcasted_iota(jnp.int32, (2 * L, 4 * L), 1) % (2 * L)
    diff = L + qi - sj
    valid = (diff >= 0) & (diff < WINDOW) & ((sj >= L) | (b > 0))

    zeros = jnp.zeros((2 * L, LANES), F32)
    ones_lo = jnp.where(lo, 1.0, 0.0).astype(F32)
    ones_hi = 1.0 - ones_lo

    outs = []
    for kh in range(N_KV_HEADS):
        c0 = LANES * (kh // 2)
        kx = kk[:, c0:c0 + LANES]
        vx = vv[:, c0:c0 + LANES]
        kr = pltpu.roll(kx, HEAD_DIM, 1)
        vr = pltpu.roll(vx, HEAD_DIM, 1)
        if kh % 2 == 0:
            k_lo, k_hi = jnp.where(lo, kx, zeros), jnp.where(lo, zeros, kr)
            v_lo, v_hi = jnp.where(lo, vx, zeros), jnp.where(lo, zeros, vr)
        else:
            k_lo, k_hi = jnp.where(lo, kr, zeros), jnp.where(lo, zeros, kx)
            v_lo, v_hi = jnp.where(lo, vr, zeros), jnp.where(lo, zeros, vx)
        kmat = jnp.concatenate([k_lo, k_hi], axis=0).astype(BF16)
        qs = jnp.concatenate([qb[:, 2 * LANES * kh:2 * LANES * kh + LANES],
                              qb[:, 2 * LANES * kh + LANES:2 * LANES * (kh + 1)]], axis=0)
        s = lax.dot_general(qs, kmat, (((1,), (1,)), ((), ())), preferred_element_type=F32)
        s = jnp.where(valid, s, -jnp.inf)
        sink_lo = jnp.where(top, sink_ref[4 * kh + 0], sink_ref[4 * kh + 2])
        sink_hi = jnp.where(top, sink_ref[4 * kh + 1], sink_ref[4 * kh + 3])
        m_lo = jnp.maximum(jnp.max(s[:, :2 * L], axis=1, keepdims=True), sink_lo)
        m_hi = jnp.maximum(jnp.max(s[:, 2 * L:], axis=1, keepdims=True), sink_hi)
        p = jnp.concatenate([jnp.exp(s[:, :2 * L] - m_lo), jnp.exp(s[:, 2 * L:] - m_hi)], axis=1).astype(BF16)
        vmat = jnp.concatenate([jnp.concatenate([v_lo, ones_lo], axis=1),
                                jnp.concatenate([v_hi, ones_hi], axis=1)], axis=0).astype(BF16)
        oe = jnp.dot(p, vmat, preferred_element_type=F32)
        denom = oe[:, LANES:] + jnp.where(lo, jnp.exp(sink_lo - m_lo), jnp.exp(sink_hi - m_hi))
        o = oe[:, :LANES] / denom
        outs.append(o[:L])
        outs.append(o[L:])
    o_all = jnp.concatenate(outs, axis=1)
    o_ref[...] = _rms(o_all, g_ref[...]).astype(o_ref.dtype)


def _attn_prompt(z, sinks, g_attn, *, n_seq, seq):
    L = WINDOW
    nb = seq // L
    kcol = D_ATTN // D_KV
    vcol = kcol + 1
    cur = lambda col: (lambda n, b: (n * nb + b, col))
    prev = lambda col: (lambda n, b: (n * nb + jnp.maximum(b - 1, 0), col))
    return pl.pallas_call(
        _attn_prompt_kernel,
        grid=(n_seq, nb),
        in_specs=[
            pl.BlockSpec(memory_space=pltpu.SMEM),
            pl.BlockSpec((L, D_ATTN), cur(0)),
            pl.BlockSpec((L, D_KV), cur(kcol)),
            pl.BlockSpec((L, D_KV), prev(kcol)),
            pl.BlockSpec((L, D_KV), cur(vcol)),
            pl.BlockSpec((L, D_KV), prev(vcol)),
            pl.BlockSpec((1, D_ATTN), lambda n, b: (0, 0)),
        ],
        out_specs=pl.BlockSpec((L, D_ATTN), cur(0)),
        out_shape=jax.ShapeDtypeStruct((n_seq * seq, D_ATTN), BF16),
        compiler_params=_cparams(("parallel", "arbitrary")),
        name="attn_prompt",
    )(sinks, z, z, z, z, z, g_attn)


def _neg_expm1(x, ex):
    return -jnp.where(x < -0.5, ex - 1.0, jnp.tanh(0.5 * x) * (ex + 1.0))


def _lru_gates(xc, wg_ref, ba, bi, lam):
    g = jnp.dot(xc.astype(BF16), wg_ref[...], preferred_element_type=F32)
    r = jax.nn.sigmoid(g[:, :D_LRU] + ba)
    gi = jax.nn.sigmoid(g[:, D_LRU:] + bi)
    nl = -lam
    softplus = jnp.maximum(nl, 0.0) + jnp.log1p(jnp.exp(-jnp.abs(nl)))
    log_a = -LRU_C * r * softplus
    a = jnp.exp(log_a)
    mult = jnp.sqrt(_neg_expm1(2.0 * log_a, a * a))
    return a, mult, gi


def _shift_rows(u, prev8, k):
    r = pltpu.roll(u, k, 0)
    pr = pltpu.roll(prev8, k, 0)
    row8 = lax.broadcasted_iota(jnp.int32, prev8.shape, 0)
    head = jnp.where(row8 < k, pr, r[:SUBLANES])
    return jnp.concatenate([head, r[SUBLANES:]], axis=0)


def _chunk_scan(a, b):
    n = a.shape[0]
    row = lax.broadcasted_iota(jnp.int32, a.shape, 0)
    d = 1
    while d < n:
        if d < SUBLANES:
            keep = row >= d
            b = jnp.where(keep, b + a * pltpu.roll(b, d, 0), b)
            a = jnp.where(keep, a * pltpu.roll(a, d, 0), a)
        else:
            b = jnp.concatenate([b[:d], b[d:] + a[d:] * b[:n - d]], axis=0)
            a = jnp.concatenate([a[:d], a[d:] * a[:n - d]], axis=0)
        d *= 2
    return a, b


def _lru_sc_kernel(ux_ref, gate_ref, ub_ref, uc_ref, uh_ref, cw_ref, cb_ref, wg_ref, ba_ref, bi_ref,
                   lam_ref, scw_ref, glru_ref, gsc_ref,
                   o_ref, h8_ref, x8_ref, g8_ref, cx_scr, cg_scr, ch_scr, *, tc):
    t = pl.program_id(1)

    @pl.when(t == 0)
    def _():
        cx_scr[...] = jnp.zeros_like(cx_scr)
        cg_scr[...] = jnp.zeros_like(cg_scr)
        ch_scr[...] = jnp.zeros_like(ch_scr)

    ux = ux_ref[...]
    px = cx_scr[...]
    xc = _shift_rows(ux, px, 3) * cw_ref[0:1, :]
    xc = xc + _shift_rows(ux, px, 2) * cw_ref[1:2, :]
    xc = xc + _shift_rows(ux, px, 1) * cw_ref[2:3, :]
    xc = xc + ux * cw_ref[3:4, :]
    xc = xc + cb_ref[...]

    a, mult, gi = _lru_gates(xc, wg_ref, ba_ref[...], bi_ref[...], lam_ref[...])
    pos = t * tc + lax.broadcasted_iota(jnp.int32, (tc, 1), 0)
    mult = jnp.where(pos == 0, 1.0, mult)
    a_cum, h_loc = _chunk_scan(a, mult * gi * xc)
    h = h_loc + a_cum * ch_scr[SUBLANES - 1:SUBLANES, :]
    o_lru = h * jax.nn.gelu(gate_ref[...], approximate=True)

    gch = uc_ref[...] * uh_ref[...]
    pg = cg_scr[...]
    y = _shift_rows(gch, pg, 2) * scw_ref[0:1, :]
    y = y + _shift_rows(gch, pg, 1) * scw_ref[1:2, :]
    y = y + gch * scw_ref[2:3, :]
    o_sc = ub_ref[...] * y

    o_ref[:, :D_LRU] = _rms(o_lru, glru_ref[...]).astype(o_ref.dtype)
    o_ref[:, D_LRU:] = _rms(o_sc, gsc_ref[...]).astype(o_ref.dtype)

    cx_scr[...] = ux[tc - SUBLANES:]
    cg_scr[...] = gch[tc - SUBLANES:]
    ch_scr[...] = h[tc - SUBLANES:]
    h8_ref[0] = h[tc - SUBLANES:]
    x8_ref[0] = ux[tc - SUBLANES:]
    g8_ref[0] = gch[tc - SUBLANES:]


def _lru_sc_prompt(z, lw, *, n_seq, seq, tc=256):
    nt = seq // tc
    zcol = lambda col: pl.BlockSpec((tc, D_LRU), lambda n, t: (n * nt + t, col))
    const = lambda shape: pl.BlockSpec(shape, lambda n, t: (0,) * len(shape))
    st = pl.BlockSpec((1, SUBLANES, D_LRU), lambda n, t: (n, 0, 0))
    st_shape = jax.ShapeDtypeStruct((n_seq, SUBLANES, D_LRU), F32)
    return pl.pallas_call(
        functools.partial(_lru_sc_kernel, tc=tc),
        grid=(n_seq, nt),
        in_specs=[zcol(_COL_UX), zcol(_COL_GATE), zcol(_COL_B), zcol(_COL_C), zcol(_COL_H),
                  const((LRU_CONV_W, D_LRU)), const((1, D_LRU)), const((D_LRU, 2 * D_LRU)),
                  const((1, D_LRU)), const((1, D_LRU)), const((1, D_LRU)),
                  const((SC_CONV_W, D_SC)), const((1, D_LRU)), const((1, D_SC))],
        out_specs=[pl.BlockSpec((tc, D_LRU + D_SC), lambda n, t: (n * nt + t, 0)), st, st, st],
        out_shape=[jax.ShapeDtypeStruct((n_seq * seq, D_LRU + D_SC), BF16), st_shape, st_shape, st_shape],
        scratch_shapes=[pltpu.VMEM((SUBLANES, D_LRU), F32)] * 3,
        compiler_params=_cparams(("parallel", "arbitrary")),
        name="lru_sc_prompt",
    )(z, z, z, z, z, lw["conv_w"], lw["conv_b"], lw["w_gates"], lw["b_a"], lw["b_i"], lw["lam"],
      lw["sc_w"], lw["g_lru"], lw["g_sc"])


def _decode_kernel(q_ref, kn_ref, vn_ref, kc_ref, vc_ref, sel_ref, sink_ref, gat_ref,
                   ux_ref, gate_ref, ub_ref, uc_ref, uh_ref, h0_ref, cbuf_ref, sbuf_ref,
                   cw_ref, cb_ref, wg_ref, ba_ref, bi_ref, lam_ref, scw_ref, glru_ref, gsc_ref,
                   oat_ref, ors_ref, ko_ref, vo_ref, hn_ref, cn_ref, sn_ref, *, nbk):
    wb = kc_ref.shape[1]
    hrow = lax.broadcasted_iota(jnp.int32, (N_HEADS, D_ATTN), 0)
    hcol = lax.broadcasted_iota(jnp.int32, (N_HEADS, D_ATTN), 1) // HEAD_DIM
    own = (hrow == hcol)
    qexp = jnp.concatenate(
        [jnp.where(own, jnp.broadcast_to(q_ref[i:i + 1, :] * (HEAD_DIM ** -0.5), (N_HEADS, D_ATTN)), 0.0)
         for i in range(nbk)], axis=0)
    qrow = jnp.dot(qexp.astype(BF16), sel_ref[...], preferred_element_type=F32)
    grow = lax.broadcasted_iota(jnp.int32, (N_HEADS, D_KV), 0) // N_GROUP
    gcol = lax.broadcasted_iota(jnp.int32, (N_HEADS, D_KV), 1) // HEAD_DIM
    kvmask = (grow == gcol)
    sink = sink_ref[...][:, 0:1]
    rowk = lax.broadcasted_iota(jnp.int32, (wb, D_KV), 0)
    gat = gat_ref[...]
    for i in range(nbk):
        kroll = jnp.where(rowk == wb - 1, jnp.broadcast_to(kn_ref[i:i + 1, :], (wb, D_KV)),
                          pltpu.roll(kc_ref[i], wb - 1, 0))
        vroll = jnp.where(rowk == wb - 1, jnp.broadcast_to(vn_ref[i:i + 1, :], (wb, D_KV)),
                          pltpu.roll(vc_ref[i], wb - 1, 0))
        ko_ref[i] = kroll
        vo_ref[i] = vroll
        qi = jnp.where(kvmask, qrow[i * N_HEADS:(i + 1) * N_HEADS], 0.0).astype(BF16)
        s = lax.dot_general(qi, kroll.astype(BF16), (((1,), (1,)), ((), ())), preferred_element_type=F32)
        m = jnp.maximum(jnp.max(s, axis=1, keepdims=True), sink)
        p = jnp.exp(s - m)
        p = p / (jnp.sum(p, axis=1, keepdims=True) + jnp.exp(sink - m))
        of = jnp.dot(p.astype(BF16), vroll.astype(BF16), preferred_element_type=F32)
        of = jnp.where(kvmask, of, 0.0)
        t = of[:, :LANES] + of[:, LANES:]
        o = t + pltpu.roll(t, HEAD_DIM, 1)
        ms = jnp.sum(jnp.sum(o * o, axis=1, keepdims=True), axis=0, keepdims=True) * (0.5 / D_ATTN)
        oat_ref[i] = (o * lax.rsqrt(ms + RMS_EPS) * gat).astype(oat_ref.dtype)

    ux = ux_ref[...]
    xc = cbuf_ref[:, 0, :] * cw_ref[0:1, :]
    xc = xc + cbuf_ref[:, 1, :] * cw_ref[1:2, :]
    xc = xc + cbuf_ref[:, 2, :] * cw_ref[2:3, :]
    xc = xc + ux * cw_ref[3:4, :]
    xc = xc + cb_ref[...]
    a, mult, gi = _lru_gates(xc, wg_ref, ba_ref[...], bi_ref[...], lam_ref[...])
    h = a * h0_ref[...] + mult * gi * xc
    o_lru = h * jax.nn.gelu(gate_ref[...], approximate=True)
    hn_ref[...] = h
    cn_ref[:, 0, :] = cbuf_ref[:, 1, :]
    cn_ref[:, 1, :] = cbuf_ref[:, 2, :]
    cn_ref[:, 2, :] = ux
    gch = uc_ref[...] * uh_ref[...]
    y = sbuf_ref[:, 0, :] * scw_ref[0:1, :]
    y = y + sbuf_ref[:, 1, :] * scw_ref[1:2, :]
    y = y + gch * scw_ref[2:3, :]
    o_sc = ub_ref[...] * y
    sn_ref[:, 0, :] = sbuf_ref[:, 1, :]
    sn_ref[:, 1, :] = gch
    ors_ref[:, :D_LRU] = _rms(o_lru, glru_ref[...]).astype(ors_ref.dtype)
    ors_ref[:, D_LRU:] = _rms(o_sc, gsc_ref[...]).astype(ors_ref.dtype)


def _decode_mix(z, kc, vc, h0, cbuf, sbuf, lw, *, nbk=16):
    ns = z.shape[0]
    wb = kc.shape[1]
    z512 = lambda col: pl.BlockSpec((nbk, D_LRU), lambda i: (i, col))
    const = lambda shape: pl.BlockSpec(shape, lambda i: (0,) * len(shape))
    cache = pl.BlockSpec((nbk, wb, D_KV), lambda i: (i, 0, 0))
    kcol = D_ATTN // D_KV
    outs = pl.pallas_call(
        functools.partial(_decode_kernel, nbk=nbk),
        grid=(ns // nbk,),
        in_specs=[pl.BlockSpec((nbk, D_ATTN), lambda i: (i, 0)),
                  pl.BlockSpec((nbk, D_KV), lambda i: (i, kcol)),
                  pl.BlockSpec((nbk, D_KV), lambda i: (i, kcol + 1)),
                  cache, cache,
                  const((D_ATTN, D_KV)), const((N_HEADS, LANES)), const((N_HEADS, LANES)),
                  z512(_COL_UX), z512(_COL_GATE), z512(_COL_B), z512(_COL_C), z512(_COL_H),
                  pl.BlockSpec((nbk, D_LRU), lambda i: (i, 0)),
                  pl.BlockSpec((nbk, LRU_CONV_W - 1, D_LRU), lambda i: (i, 0, 0)),
                  pl.BlockSpec((nbk, SC_CONV_W - 1, D_SC), lambda i: (i, 0, 0)),
                  const((LRU_CONV_W, D_LRU)), const((1, D_LRU)), const((D_LRU, 2 * D_LRU)),
                  const((1, D_LRU)), const((1, D_LRU)), const((1, D_LRU)),
                  const((SC_CONV_W, D_SC)), const((1, D_LRU)), const((1, D_SC))],
        out_specs=[pl.BlockSpec((nbk, N_HEADS, LANES), lambda i: (i, 0, 0)),
                   pl.BlockSpec((nbk, D_LRU + D_SC), lambda i: (i, 0)),
                   cache, cache,
                   pl.BlockSpec((nbk, D_LRU), lambda i: (i, 0)),
                   pl.BlockSpec((nbk, LRU_CONV_W - 1, D_LRU), lambda i: (i, 0, 0)),
                   pl.BlockSpec((nbk, SC_CONV_W - 1, D_SC), lambda i: (i, 0, 0))],
        out_shape=[jax.ShapeDtypeStruct((ns, N_HEADS, LANES), BF16),
                   jax.ShapeDtypeStruct((ns, D_LRU + D_SC), BF16),
                   jax.ShapeDtypeStruct(kc.shape, F32), jax.ShapeDtypeStruct(vc.shape, F32),
                   jax.ShapeDtypeStruct((ns, D_LRU), F32),
                   jax.ShapeDtypeStruct((ns, LRU_CONV_W - 1, D_LRU), F32),
                   jax.ShapeDtypeStruct((ns, SC_CONV_W - 1, D_SC), F32)],
        compiler_params=_cparams(("parallel",)),
        name="decode_mix",
    )(z, z, z, kc, vc, lw["sel"], lw["sink_tab"], lw["g_attn_tab"],
      z, z, z, z, z, h0, cbuf, sbuf,
      lw["conv_w"], lw["conv_b"], lw["w_gates"], lw["b_a"], lw["b_i"], lw["lam"],
      lw["sc_w"], lw["g_lru"], lw["g_sc"])
    return outs


def _out_proj_kernel(ma_ref, mb_ref, x_ref, w_ref, g_ref, x1_ref, hf_ref):
    acc = jnp.dot(ma_ref[...], w_ref[:D_ATTN, :], preferred_element_type=F32)
    acc = acc + jnp.dot(mb_ref[...], w_ref[D_ATTN:, :], preferred_element_type=F32)
    x1 = x_ref[...] + acc
    x1_ref[...] = x1
    hf_ref[...] = _rms(x1, g_ref[...]).astype(hf_ref.dtype)


def _out_proj(ma, mb, x, w_bf, g_ffn, *, bm):
    m = x.shape[0]
    return pl.pallas_call(
        _out_proj_kernel,
        grid=(m // bm,),
        in_specs=[pl.BlockSpec((bm, D_ATTN), lambda i: (i, 0)),
                  pl.BlockSpec((bm, D_LRU + D_SC), lambda i: (i, 0)),
                  pl.BlockSpec((bm, D_MODEL), lambda i: (i, 0)),
                  pl.BlockSpec((D_MODEL, D_MODEL), lambda i: (0, 0)),
                  pl.BlockSpec((1, D_MODEL), lambda i: (0, 0))],
        out_specs=[pl.BlockSpec((bm, D_MODEL), lambda i: (i, 0)),
                   pl.BlockSpec((bm, D_MODEL), lambda i: (i, 0))],
        out_shape=[jax.ShapeDtypeStruct((m, D_MODEL), F32), jax.ShapeDtypeStruct((m, D_MODEL), BF16)],
        compiler_params=_cparams(("parallel",)),
        name="out_proj",
    )(ma, mb, x, w_bf, g_ffn)


def _ffn_kernel(hf_ref, x1_ref, wg_ref, wu_ref, wd_ref, gfin_ref, o_ref, acc_scr, *, final_norm):
    f = pl.program_id(1)

    @pl.when(f == 0)
    def _():
        acc_scr[...] = jnp.zeros_like(acc_scr)

    hf = hf_ref[...]
    gate = jnp.dot(hf, wg_ref[...], preferred_element_type=F32)
    up = jnp.dot(hf, wu_ref[...], preferred_element_type=F32)
    hid = (gate * jax.nn.sigmoid(gate) * up).astype(BF16)
    acc_scr[...] += jnp.dot(hid, wd_ref[...], preferred_element_type=F32)

    @pl.when(f == pl.num_programs(1) - 1)
    def _():
        x2 = x1_ref[...] + acc_scr[...]
        o_ref[...] = _rms(x2, gfin_ref[...]) if final_norm else x2


def _ffn(hf, x1, w_gu_bf, w_d_bf, g_final, *, bm, tf=512, final_norm):
    m = hf.shape[0]
    nf = D_FF // tf
    return pl.pallas_call(
        functools.partial(_ffn_kernel, final_norm=final_norm),
        grid=(m // bm, nf),
        in_specs=[pl.BlockSpec((bm, D_MODEL), lambda i, f: (i, 0)),
                  pl.BlockSpec((bm, D_MODEL), lambda i, f: (i, 0)),
                  pl.BlockSpec((D_MODEL, tf), lambda i, f: (0, f)),
                  pl.BlockSpec((D_MODEL, tf), lambda i, f: (0, nf + f)),
                  pl.BlockSpec((tf, D_MODEL), lambda i, f: (f, 0)),
                  pl.BlockSpec((1, D_MODEL), lambda i, f: (0, 0))],
        out_specs=pl.BlockSpec((bm, D_MODEL), lambda i, f: (i, 0)),
        out_shape=jax.ShapeDtypeStruct((m, D_MODEL), F32),
        scratch_shapes=[pltpu.VMEM((bm, D_MODEL), F32)],
        compiler_params=_cparams(("parallel", "arbitrary")),
        name="ffn",
    )(hf, x1, w_gu_bf, w_gu_bf, w_d_bf, g_final)


def _rope_tables(pos):
    half = HEAD_DIM // 2
    inv = ROPE_THETA ** (-jnp.arange(half, dtype=F32) / half)
    ang = pos.astype(F32)[:, None] * inv[None, :]
    cos, sin = jnp.cos(ang), jnp.sin(ang)
    cos_t = jnp.tile(cos, (1, LANES // half))
    sin_t = jnp.tile(jnp.concatenate([-sin, sin], axis=1), (1, LANES // HEAD_DIM))
    return cos_t, sin_t


def _block_diag(w):
    hh, blk, _ = w.shape
    eye = jnp.eye(hh, dtype=w.dtype)
    return (eye[:, None, :, None] * w[:, :, None, :]).reshape(hh * blk, hh * blk)


def kernel(x_prompt, x_sample, state_lru_h, state_lru_conv, cache_swa_k, cache_swa_v, state_sconv,
           norm_mix, w_in, norm_grp, w_out, lru_conv_w, lru_conv_b, lru_w_a, lru_b_a, lru_w_i, lru_b_i,
           lru_lambda, sc_conv_w, attn_sinks, norm_ffn, ffn_w_gu, ffn_w_down, norm_final):
    n_p, t_p, _ = x_prompt.shape
    n_s, t_s, _ = x_sample.shape
    depth = w_in.shape[0]
    wb = cache_swa_k.shape[2]
    assert t_s == 1 and wb == WINDOW and t_p % 256 == 0 and n_s % 16 == 0

    bm_p = 512 if (n_p * t_p) % 512 == 0 and t_p % 512 == 0 else 256
    bm_s = n_s

    cos_p, sin_p = _rope_tables(jnp.arange(t_p, dtype=jnp.int32))
    cos_s, sin_s = _rope_tables(jnp.full((bm_s,), PAST_LEN, dtype=jnp.int32))

    sel = (jnp.arange(D_ATTN)[:, None] % HEAD_DIM == jnp.arange(D_KV)[None, :] % HEAD_DIM).astype(BF16)

    xp = x_prompt.reshape(n_p * t_p, D_MODEL)
    xs = x_sample.reshape(n_s, D_MODEL)
    row = lambda v: v.reshape(1, -1)
    p_states, s_states = [], []
    y_p = y_s = None
    for l in range(depth):
        w_in_bf = w_in[l].astype(BF16)
        w_out_bf = w_out[l].astype(BF16)
        w_gu_bf = ffn_w_gu[l].astype(BF16)
        w_d_bf = ffn_w_down[l].astype(BF16)
        g_attn, g_lru, g_sc = (norm_grp[l, :D_ATTN], norm_grp[l, D_ATTN:D_ATTN + D_LRU],
                               norm_grp[l, D_ATTN + D_LRU:])
        lw = dict(
            conv_w=lru_conv_w[l], conv_b=row(lru_conv_b[l]),
            w_gates=jnp.concatenate([_block_diag(lru_w_a[l]), _block_diag(lru_w_i[l])], axis=1).astype(BF16),
            b_a=row(lru_b_a[l]), b_i=row(lru_b_i[l]), lam=row(lru_lambda[l]),
            sc_w=sc_conv_w[l], g_lru=row(g_lru), g_sc=row(g_sc),
            sel=sel,
            sink_tab=jnp.broadcast_to(attn_sinks[l][:, None], (N_HEADS, LANES)),
            g_attn_tab=jnp.tile(g_attn.reshape(N_HEADS, HEAD_DIM), (1, LANES // HEAD_DIM)),
        )
        last = l == depth - 1

        z = _in_proj(xp, row(norm_mix[l]), w_in_bf, cos_p, sin_p, bm=bm_p)
        ma = _attn_prompt(z, attn_sinks[l], row(g_attn), n_seq=n_p, seq=t_p)
        mb, h8, x8, g8 = _lru_sc_prompt(z, lw, n_seq=n_p, seq=t_p)
        x1, hf = _out_proj(ma, mb, xp, w_out_bf, row(norm_ffn[l]), bm=bm_p)
        xp_new = _ffn(hf, x1, w_gu_bf, w_d_bf, row(norm_final), bm=bm_p, final_norm=last)
        z3 = z.reshape(n_p, t_p, D_IN)
        wbp = min(WINDOW, t_p)
        p_states.append((
            h8[:, SUBLANES - 1],
            x8[:, SUBLANES - (LRU_CONV_W - 1):],
            z3[:, t_p - wbp:, D_ATTN:D_ATTN + D_KV].reshape(n_p, wbp, N_KV_HEADS, HEAD_DIM),
            z3[:, t_p - wbp:, D_ATTN + D_KV:D_ATTN + 2 * D_KV].reshape(n_p, wbp, N_KV_HEADS, HEAD_DIM),
            g8[:, SUBLANES - (SC_CONV_W - 1):],
        ))
        xp = xp_new

        zs = _in_proj(xs, row(norm_mix[l]), w_in_bf, cos_s, sin_s, bm=bm_s)
        oat, ors, k_new, v_new, h_new, c_new, s_new = _decode_mix(
            zs, cache_swa_k[l].reshape(n_s, wb, D_KV), cache_swa_v[l].reshape(n_s, wb, D_KV),
            state_lru_h[l], state_lru_conv[l], state_sconv[l], lw)
        mas = oat[:, :, :HEAD_DIM].reshape(n_s, D_ATTN)
        x1s, hfs = _out_proj(mas, ors, xs, w_out_bf, row(norm_ffn[l]), bm=bm_s)
        xs = _ffn(hfs, x1s, w_gu_bf, w_d_bf, row(norm_final), bm=bm_s, final_norm=last)
        s_states.append((h_new, c_new,
                         k_new.reshape(n_s, wb, N_KV_HEADS, HEAD_DIM),
                         v_new.reshape(n_s, wb, N_KV_HEADS, HEAD_DIM), s_new))

    y_prompt = xp.reshape(n_p, t_p, D_MODEL)
    y_sample = xs.reshape(n_s, t_s, D_MODEL)
    stack = lambda states, k: jnp.stack([st[k] for st in states])
    return (y_prompt, y_sample,
            stack(p_states, 0), stack(p_states, 1), stack(p_states, 2), stack(p_states, 3), stack(p_states, 4),
            stack(s_states, 0), stack(s_states, 1), stack(s_states, 2), stack(s_states, 3), stack(s_states, 4))
```

```python
import functools

import jax
import jax.numpy as jnp
from jax import lax
from jax.experimental import pallas as pl
from jax.experimental.pallas import tpu as pltpu

F32 = jnp.float32
BF16 = jnp.bfloat16

D_MODEL = 2048
D_ATTN = 1024
D_LRU = 512
D_SC = 512
HEAD_DIM = 64
N_HEADS = 16
N_KV_HEADS = 4
N_GROUP = 4
D_KV = 256
WINDOW = 128
ROPE_THETA = 10000.0
N_LRU_HEADS = 8
LRU_BLK = 64
LRU_CONV_W = 4
LRU_C = 8.0
SC_CONV_W = 3
D_FF = 5632
D_IN = 4096
RMS_EPS = 1e-6
PAST_LEN = 8192

LANES = 128
SUBLANES = 8
VMEM_LIMIT_BYTES = 56 * 1024 * 1024

_COL_UX, _COL_GATE, _COL_B, _COL_C, _COL_H = 3, 4, 5, 6, 7


def _cparams(sem):
    return pltpu.CompilerParams(dimension_semantics=sem, vmem_limit_bytes=VMEM_LIMIT_BYTES)


def _rms(x, g):
    return x * lax.rsqrt(jnp.mean(x * x, axis=-1, keepdims=True) + RMS_EPS) * g


_IN_PROJ_CHUNK = 512


def _in_proj_kernel(x_ref, g_ref, w_ref, cos_ref, sin_ref, z_ref):
    h = _rms(x_ref[...], g_ref[...]).astype(BF16)
    bm = h.shape[0]
    lane = lax.broadcasted_iota(jnp.int32, (bm, LANES), 1)
    lo32 = (lane % HEAD_DIM) < (HEAD_DIM // 2)
    cos = cos_ref[...]
    sin = sin_ref[...]

    def rope(a):
        sw = jnp.where(lo32, pltpu.roll(a, LANES - HEAD_DIM // 2, 1), pltpu.roll(a, HEAD_DIM // 2, 1))
        return a * cos + sw * sin

    rope_cols = D_ATTN + D_KV
    for c0 in range(0, D_IN, _IN_PROJ_CHUNK):
        acc = jnp.dot(h, w_ref[:, c0:c0 + _IN_PROJ_CHUNK], preferred_element_type=F32)
        for c in range(0, _IN_PROJ_CHUNK, LANES):
            a = acc[:, c:c + LANES]
            z_ref[:, c0 + c:c0 + c + LANES] = rope(a) if c0 + c < rope_cols else a


def _resident(block_shape, index_map):
    return pl.BlockSpec(block_shape, index_map, pipeline_mode=pl.Buffered(1))


def _in_proj(x, g_all, w_all_bf, cos_t, sin_t, *, layer, bm):
    m = x.shape[0]
    n_tab = cos_t.shape[0] // bm
    return pl.pallas_call(
        _in_proj_kernel,
        grid=(m // bm,),
        in_specs=[
            pl.BlockSpec((bm, D_MODEL), lambda i: (i, 0)),
            pl.BlockSpec((None, 1, D_MODEL), lambda i: (layer, 0, 0)),
            _resident((None, D_MODEL, D_IN), lambda i: (layer, 0, 0)),
            pl.BlockSpec((bm, LANES), lambda i: (i % n_tab, 0)),
            pl.BlockSpec((bm, LANES), lambda i: (i % n_tab, 0)),
        ],
        out_specs=pl.BlockSpec((bm, D_IN), lambda i: (i, 0)),
        out_shape=jax.ShapeDtypeStruct((m, D_IN), F32),
        compiler_params=_cparams(("parallel",)),
        name="in_proj",
    )(x, g_all, w_all_bf, cos_t, sin_t)


def _attn_prompt_kernel(sink_ref, q_ref, kc_ref, kp_ref, vc_ref, vp_ref, g_ref, o_ref):
    b = pl.program_id(1)
    L = WINDOW
    qb = (q_ref[...] * (HEAD_DIM ** -0.5)).astype(BF16)
    kk = jnp.concatenate([kp_ref[...], kc_ref[...]], axis=0)
    vv = jnp.concatenate([vp_ref[...], vc_ref[...]], axis=0)

    lane = lax.broadcasted_iota(jnp.int32, (2 * L, LANES), 1)
    lo = lane < HEAD_DIM
    row = lax.broadcasted_iota(jnp.int32, (2 * L, 1), 0)
    top = row < L

    qi = lax.broadcasted_iota(jnp.int32, (2 * L, 4 * L), 0) % L
    sj = lax.broadcasted_iota(jnp.int32, (2 * L, 4 * L), 1) % (2 * L)
    diff = L + qi - sj
    valid = (diff >= 0) & (diff < WINDOW) & ((sj >= L) | (b > 0))

    zeros = jnp.zeros((2 * L, LANES), F32)
    ones_lo = jnp.where(lo, 1.0, 0.0).astype(F32)
    ones_hi = 1.0 - ones_lo

    outs = []
    for kh in range(N_KV_HEADS):
        c0 = LANES * (kh // 2)
        kx = kk[:, c0:c0 + LANES]
        vx = vv[:, c0:c0 + LANES]
        kr = pltpu.roll(kx, HEAD_DIM, 1)
        vr = pltpu.roll(vx, HEAD_DIM, 1)
        if kh % 2 == 0:
            k_lo, k_hi = jnp.where(lo, kx, zeros), jnp.where(lo, zeros, kr)
            v_lo, v_hi = jnp.where(lo, vx, zeros), jnp.where(lo, zeros, vr)
        else:
            k_lo, k_hi = jnp.where(lo, kr, zeros), jnp.where(lo, zeros, kx)
            v_lo, v_hi = jnp.where(lo, vr, zeros), jnp.where(lo, zeros, vx)
        kmat = jnp.concatenate([k_lo, k_hi], axis=0).astype(BF16)
        qs = jnp.concatenate([qb[:, 2 * LANES * kh:2 * LANES * kh + LANES],
                              qb[:, 2 * LANES * kh + LANES:2 * LANES * (kh + 1)]], axis=0)
        s = lax.dot_general(qs, kmat, (((1,), (1,)), ((), ())), preferred_element_type=F32)
        s = jnp.where(valid, s, -jnp.inf)
        sink_lo = jnp.where(top, sink_ref[4 * kh + 0], sink_ref[4 * kh + 2])
        sink_hi = jnp.where(top, sink_ref[4 * kh + 1], sink_ref[4 * kh + 3])
        m_lo = jnp.maximum(jnp.max(s[:, :2 * L], axis=1, keepdims=True), sink_lo)
        m_hi = jnp.maximum(jnp.max(s[:, 2 * L:], axis=1, keepdims=True), sink_hi)
        p = jnp.concatenate([jnp.exp(s[:, :2 * L] - m_lo), jnp.exp(s[:, 2 * L:] - m_hi)], axis=1).astype(BF16)
        vmat = jnp.concatenate([jnp.concatenate([v_lo, ones_lo], axis=1),
                                jnp.concatenate([v_hi, ones_hi], axis=1)], axis=0).astype(BF16)
        oe = jnp.dot(p, vmat, preferred_element_type=F32)
        denom = oe[:, LANES:] + jnp.where(lo, jnp.exp(sink_lo - m_lo), jnp.exp(sink_hi - m_hi))
        o = oe[:, :LANES] / denom
        outs.append(o[:L])
        outs.append(o[L:])
    o_all = jnp.concatenate(outs, axis=1)
    o_ref[...] = _rms(o_all, g_ref[...]).astype(o_ref.dtype)


def _attn_prompt(z, sinks, g_attn, *, n_seq, seq):
    L = WINDOW
    nb = seq // L
    kcol = D_ATTN // D_KV
    vcol = kcol + 1
    cur = lambda col: (lambda n, b: (n * nb + b, col))
    prev = lambda col: (lambda n, b: (n * nb + jnp.maximum(b - 1, 0), col))
    return pl.pallas_call(
        _attn_prompt_kernel,
        grid=(n_seq, nb),
        in_specs=[
            pl.BlockSpec(memory_space=pltpu.SMEM),
            pl.BlockSpec((L, D_ATTN), cur(0)),
            pl.BlockSpec((L, D_KV), cur(kcol)),
            pl.BlockSpec((L, D_KV), prev(kcol)),
            pl.BlockSpec((L, D_KV), cur(vcol)),
            pl.BlockSpec((L, D_KV), prev(vcol)),
            pl.BlockSpec((1, D_ATTN), lambda n, b: (0, 0)),
        ],
        out_specs=pl.BlockSpec((L, D_ATTN), cur(0)),
        out_shape=jax.ShapeDtypeStruct((n_seq * seq, D_ATTN), BF16),
        compiler_params=_cparams(("parallel", "arbitrary")),
        name="attn_prompt",
    )(sinks, z, z, z, z, z, g_attn)


def _neg_expm1(x, ex):
    return -jnp.where(x < -0.5, ex - 1.0, jnp.tanh(0.5 * x) * (ex + 1.0))


def _lru_gates(xc, wg_ref, ba, bi, lam):
    g = jnp.dot(xc.astype(BF16), wg_ref[...], preferred_element_type=F32)
    r = jax.nn.sigmoid(g[:, :D_LRU] + ba)
    gi = jax.nn.sigmoid(g[:, D_LRU:] + bi)
    nl = -lam
    softplus = jnp.maximum(nl, 0.0) + jnp.log1p(jnp.exp(-jnp.abs(nl)))
    log_a = -LRU_C * r * softplus
    a = jnp.exp(log_a)
    mult = jnp.sqrt(_neg_expm1(2.0 * log_a, a * a))
    return a, mult, gi


def _shift_rows(u, prev8, k):
    r = pltpu.roll(u, k, 0)
    pr = pltpu.roll(prev8, k, 0)
    row8 = lax.broadcasted_iota(jnp.int32, prev8.shape, 0)
    head = jnp.where(row8 < k, pr, r[:SUBLANES])
    return jnp.concatenate([head, r[SUBLANES:]], axis=0)


def _chunk_scan(a, b):
    n = a.shape[0]
    row = lax.broadcasted_iota(jnp.int32, a.shape, 0)
    d = 1
    while d < n:
        if d < SUBLANES:
            keep = row >= d
            b = jnp.where(keep, b + a * pltpu.roll(b, d, 0), b)
            a = jnp.where(keep, a * pltpu.roll(a, d, 0), a)
        else:
            b = jnp.concatenate([b[:d], b[d:] + a[d:] * b[:n - d]], axis=0)
            a = jnp.concatenate([a[:d], a[d:] * a[:n - d]], axis=0)
        d *= 2
    return a, b


def _lru_sc_kernel(ux_ref, gate_ref, ub_ref, uc_ref, uh_ref, cw_ref, cb_ref, wg_ref, ba_ref, bi_ref,
                   lam_ref, scw_ref, glru_ref, gsc_ref,
                   o_ref, h8_ref, x8_ref, g8_ref, cx_scr, cg_scr, ch_scr, *, tc):
    t = pl.program_id(1)

    @pl.when(t == 0)
    def _():
        cx_scr[...] = jnp.zeros_like(cx_scr)
        cg_scr[...] = jnp.zeros_like(cg_scr)
        ch_scr[...] = jnp.zeros_like(ch_scr)

    ux = ux_ref[...]
    px = cx_scr[...]
    xc = _shift_rows(ux, px, 3) * cw_ref[0:1, :]
    xc = xc + _shift_rows(ux, px, 2) * cw_ref[1:2, :]
    xc = xc + _shift_rows(ux, px, 1) * cw_ref[2:3, :]
    xc = xc + ux * cw_ref[3:4, :]
    xc = xc + cb_ref[...]

    a, mult, gi = _lru_gates(xc, wg_ref, ba_ref[...], bi_ref[...], lam_ref[...])
    pos = t * tc + lax.broadcasted_iota(jnp.int32, (tc, 1), 0)
    mult = jnp.where(pos == 0, 1.0, mult)
    a_cum, h_loc = _chunk_scan(a, mult * gi * xc)
    h = h_loc + a_cum * ch_scr[SUBLANES - 1:SUBLANES, :]
    o_lru = h * jax.nn.gelu(gate_ref[...], approximate=True)

    gch = uc_ref[...] * uh_ref[...]
    pg = cg_scr[...]
    y = _shift_rows(gch, pg, 2) * scw_ref[0:1, :]
    y = y + _shift_rows(gch, pg, 1) * scw_ref[1:2, :]
    y = y + gch * scw_ref[2:3, :]
    o_sc = ub_ref[...] * y

    o_ref[:, :D_LRU] = _rms(o_lru, glru_ref[...]).astype(o_ref.dtype)
    o_ref[:, D_LRU:] = _rms(o_sc, gsc_ref[...]).astype(o_ref.dtype)

    cx_scr[...] = ux[tc - SUBLANES:]
    cg_scr[...] = gch[tc - SUBLANES:]
    ch_scr[...] = h[tc - SUBLANES:]
    h8_ref[0] = h[tc - SUBLANES:]
    x8_ref[0] = ux[tc - SUBLANES:]
    g8_ref[0] = gch[tc - SUBLANES:]


def _lru_sc_prompt(z, lw, *, n_seq, seq, tc=256):
    nt = seq // tc
    zcol = lambda col: pl.BlockSpec((tc, D_LRU), lambda n, t: (n * nt + t, col))
    const = lambda shape: pl.BlockSpec(shape, lambda n, t: (0,) * len(shape))
    st = pl.BlockSpec((1, SUBLANES, D_LRU), lambda n, t: (n, 0, 0))
    st_shape = jax.ShapeDtypeStruct((n_seq, SUBLANES, D_LRU), F32)
    return pl.pallas_call(
        functools.partial(_lru_sc_kernel, tc=tc),
        grid=(n_seq, nt),
        in_specs=[zcol(_COL_UX), zcol(_COL_GATE), zcol(_COL_B), zcol(_COL_C), zcol(_COL_H),
                  const((LRU_CONV_W, D_LRU)), const((1, D_LRU)), const((D_LRU, 2 * D_LRU)),
                  const((1, D_LRU)), const((1, D_LRU)), const((1, D_LRU)),
                  const((SC_CONV_W, D_SC)), const((1, D_LRU)), const((1, D_SC))],
        out_specs=[pl.BlockSpec((tc, D_LRU + D_SC), lambda n, t: (n * nt + t, 0)), st, st, st],
        out_shape=[jax.ShapeDtypeStruct((n_seq * seq, D_LRU + D_SC), BF16), st_shape, st_shape, st_shape],
        scratch_shapes=[pltpu.VMEM((SUBLANES, D_LRU), F32)] * 3,
        compiler_params=_cparams(("parallel", "arbitrary")),
        name="lru_sc_prompt",
    )(z, z, z, z, z, lw["conv_w"], lw["conv_b"], lw["w_gates"], lw["b_a"], lw["b_i"], lw["lam"],
      lw["sc_w"], lw["g_lru"], lw["g_sc"])


def _decode_kernel(q_ref, kn_ref, vn_ref, kc_ref, vc_ref, sel_ref, sink_ref, gat_ref,
                   ux_ref, gate_ref, ub_ref, uc_ref, uh_ref, h0_ref, cbuf_ref, sbuf_ref,
                   cw_ref, cb_ref, wg_ref, ba_ref, bi_ref, lam_ref, scw_ref, glru_ref, gsc_ref,
                   oat_ref, ors_ref, ko_ref, vo_ref, hn_ref, cn_ref, sn_ref, *, nbk):
    wb = kc_ref.shape[1]
    hrow = lax.broadcasted_iota(jnp.int32, (N_HEADS, D_ATTN), 0)
    hcol = lax.broadcasted_iota(jnp.int32, (N_HEADS, D_ATTN), 1) // HEAD_DIM
    own = (hrow == hcol)
    qexp = jnp.concatenate(
        [jnp.where(own, jnp.broadcast_to(q_ref[i:i + 1, :] * (HEAD_DIM ** -0.5), (N_HEADS, D_ATTN)), 0.0)
         for i in range(nbk)], axis=0)
    qrow = jnp.dot(qexp.astype(BF16), sel_ref[...], preferred_element_type=F32)
    grow = lax.broadcasted_iota(jnp.int32, (N_HEADS, D_KV), 0) // N_GROUP
    gcol = lax.broadcasted_iota(jnp.int32, (N_HEADS, D_KV), 1) // HEAD_DIM
    kvmask = (grow == gcol)
    sink = sink_ref[...][:, 0:1]
    rowk = lax.broadcasted_iota(jnp.int32, (wb, D_KV), 0)
    gat = gat_ref[...]
    for i in range(nbk):
        kroll = jnp.where(rowk == wb - 1, jnp.broadcast_to(kn_ref[i:i + 1, :], (wb, D_KV)),
                          pltpu.roll(kc_ref[i], wb - 1, 0))
        vroll = jnp.where(rowk == wb - 1, jnp.broadcast_to(vn_ref[i:i + 1, :], (wb, D_KV)),
                          pltpu.roll(vc_ref[i], wb - 1, 0))
        ko_ref[i] = kroll
        vo_ref[i] = vroll
        qi = jnp.where(kvmask, qrow[i * N_HEADS:(i + 1) * N_HEADS], 0.0).astype(BF16)
        s = lax.dot_general(qi, kroll.astype(BF16), (((1,), (1,)), ((), ())), preferred_element_type=F32)
        m = jnp.maximum(jnp.max(s, axis=1, keepdims=True), sink)
        p = jnp.exp(s - m)
        p = p / (jnp.sum(p, axis=1, keepdims=True) + jnp.exp(sink - m))
        of = jnp.dot(p.astype(BF16), vroll.astype(BF16), preferred_element_type=F32)
        of = jnp.where(kvmask, of, 0.0)
        t = of[:, :LANES] + of[:, LANES:]
        o = t + pltpu.roll(t, HEAD_DIM, 1)
        ms = jnp.sum(jnp.sum(o * o, axis=1, keepdims=True), axis=0, keepdims=True) * (0.5 / D_ATTN)
        oat_ref[i] = (o * lax.rsqrt(ms + RMS_EPS) * gat).astype(oat_ref.dtype)

    ux = ux_ref[...]
    xc = cbuf_ref[:, 0, :] * cw_ref[0:1, :]
    xc = xc + cbuf_ref[:, 1, :] * cw_ref[1:2, :]
    xc = xc + cbuf_ref[:, 2, :] * cw_ref[2:3, :]
    xc = xc + ux * cw_ref[3:4, :]
    xc = xc + cb_ref[...]
    a, mult, gi = _lru_gates(xc, wg_ref, ba_ref[...], bi_ref[...], lam_ref[...])
    h = a * h0_ref[...] + mult * gi * xc
    o_lru = h * jax.nn.gelu(gate_ref[...], approximate=True)
    hn_ref[...] = h
    cn_ref[:, 0, :] = cbuf_ref[:, 1, :]
    cn_ref[:, 1, :] = cbuf_ref[:, 2, :]
    cn_ref[:, 2, :] = ux
    gch = uc_ref[...] * uh_ref[...]
    y = sbuf_ref[:, 0, :] * scw_ref[0:1, :]
    y = y + sbuf_ref[:, 1, :] * scw_ref[1:2, :]
    y = y + gch * scw_ref[2:3, :]
    o_sc = ub_ref[...] * y
    sn_ref[:, 0, :] = sbuf_ref[:, 1, :]
    sn_ref[:, 1, :] = gch
    ors_ref[:, :D_LRU] = _rms(o_lru, glru_ref[...]).astype(ors_ref.dtype)
    ors_ref[:, D_LRU:] = _rms(o_sc, gsc_ref[...]).astype(ors_ref.dtype)


def _decode_mix(z, kc, vc, h0, cbuf, sbuf, lw, *, nbk=16):
    ns = z.shape[0]
    wb = kc.shape[1]
    z512 = lambda col: pl.BlockSpec((nbk, D_LRU), lambda i: (i, col))
    const = lambda shape: pl.BlockSpec(shape, lambda i: (0,) * len(shape))
    cache = pl.BlockSpec((nbk, wb, D_KV), lambda i: (i, 0, 0))
    kcol = D_ATTN // D_KV
    outs = pl.pallas_call(
        functools.partial(_decode_kernel, nbk=nbk),
        grid=(ns // nbk,),
        in_specs=[pl.BlockSpec((nbk, D_ATTN), lambda i: (i, 0)),
                  pl.BlockSpec((nbk, D_KV), lambda i: (i, kcol)),
                  pl.BlockSpec((nbk, D_KV), lambda i: (i, kcol + 1)),
                  cache, cache,
                  const((D_ATTN, D_KV)), const((N_HEADS, LANES)), const((N_HEADS, LANES)),
                  z512(_COL_UX), z512(_COL_GATE), z512(_COL_B), z512(_COL_C), z512(_COL_H),
                  pl.BlockSpec((nbk, D_LRU), lambda i: (i, 0)),
                  pl.BlockSpec((nbk, LRU_CONV_W - 1, D_LRU), lambda i: (i, 0, 0)),
                  pl.BlockSpec((nbk, SC_CONV_W - 1, D_SC), lambda i: (i, 0, 0)),
                  const((LRU_CONV_W, D_LRU)), const((1, D_LRU)), const((D_LRU, 2 * D_LRU)),
                  const((1, D_LRU)), const((1, D_LRU)), const((1, D_LRU)),
                  const((SC_CONV_W, D_SC)), const((1, D_LRU)), const((1, D_SC))],
        out_specs=[pl.BlockSpec((nbk, N_HEADS, LANES), lambda i: (i, 0, 0)),
                   pl.BlockSpec((nbk, D_LRU + D_SC), lambda i: (i, 0)),
                   cache, cache,
                   pl.BlockSpec((nbk, D_LRU), lambda i: (i, 0)),
                   pl.BlockSpec((nbk, LRU_CONV_W - 1, D_LRU), lambda i: (i, 0, 0)),
                   pl.BlockSpec((nbk, SC_CONV_W - 1, D_SC), lambda i: (i, 0, 0))],
        out_shape=[jax.ShapeDtypeStruct((ns, N_HEADS, LANES), BF16),
                   jax.ShapeDtypeStruct((ns, D_LRU + D_SC), BF16),
                   jax.ShapeDtypeStruct(kc.shape, F32), jax.ShapeDtypeStruct(vc.shape, F32),
                   jax.ShapeDtypeStruct((ns, D_LRU), F32),
                   jax.ShapeDtypeStruct((ns, LRU_CONV_W - 1, D_LRU), F32),
                   jax.ShapeDtypeStruct((ns, SC_CONV_W - 1, D_SC), F32)],
        compiler_params=_cparams(("parallel",)),
        name="decode_mix",
    )(z, z, z, kc, vc, lw["sel"], lw["sink_tab"], lw["g_attn_tab"],
      z, z, z, z, z, h0, cbuf, sbuf,
      lw["conv_w"], lw["conv_b"], lw["w_gates"], lw["b_a"], lw["b_i"], lw["lam"],
      lw["sc_w"], lw["g_lru"], lw["g_sc"])
    return outs


_OUT_PROJ_CHUNK = 512


def _out_proj_kernel(ma_ref, mb_ref, x_ref, w_ref, g_ref, x1_ref, hf_ref):
    ma = ma_ref[...]
    mb = mb_ref[...]
    ssq = None
    for c0 in range(0, D_MODEL, _OUT_PROJ_CHUNK):
        cs = slice(c0, c0 + _OUT_PROJ_CHUNK)
        acc = jnp.dot(ma, w_ref[:D_ATTN, cs], preferred_element_type=F32)
        acc = acc + jnp.dot(mb, w_ref[D_ATTN:, cs], preferred_element_type=F32)
        x1 = x_ref[:, cs] + acc
        x1_ref[:, cs] = x1
        part = jnp.sum(x1 * x1, axis=-1, keepdims=True)
        ssq = part if ssq is None else ssq + part
    scale = lax.rsqrt(ssq * (1.0 / D_MODEL) + RMS_EPS)
    for c0 in range(0, D_MODEL, _OUT_PROJ_CHUNK):
        cs = slice(c0, c0 + _OUT_PROJ_CHUNK)
        hf_ref[:, cs] = (x1_ref[:, cs] * scale * g_ref[:, cs]).astype(hf_ref.dtype)


def _out_proj(ma, mb, x, w_all_bf, g_all, *, layer, bm):
    m = x.shape[0]
    return pl.pallas_call(
        _out_proj_kernel,
        grid=(m // bm,),
        in_specs=[pl.BlockSpec((bm, D_ATTN), lambda i: (i, 0)),
                  pl.BlockSpec((bm, D_LRU + D_SC), lambda i: (i, 0)),
                  pl.BlockSpec((bm, D_MODEL), lambda i: (i, 0)),
                  _resident((None, D_MODEL, D_MODEL), lambda i: (layer, 0, 0)),
                  pl.BlockSpec((None, 1, D_MODEL), lambda i: (layer, 0, 0))],
        out_specs=[pl.BlockSpec((bm, D_MODEL), lambda i: (i, 0)),
                   pl.BlockSpec((bm, D_MODEL), lambda i: (i, 0))],
        out_shape=[jax.ShapeDtypeStruct((m, D_MODEL), F32), jax.ShapeDtypeStruct((m, D_MODEL), BF16)],
        compiler_params=_cparams(("parallel",)),
        name="out_proj",
    )(ma, mb, x, w_all_bf, g_all)


_FFN_DOWN_CHUNK = 512
_FFN_X1_CHUNK = 256


def _ffn_kernel(hf_ref, x1_ref, wg_ref, wu_ref, wd_ref, gfin_ref, o_ref, *, final_norm):
    f = pl.program_id(1)
    n_x1 = D_MODEL // _FFN_X1_CHUNK

    @pl.when(f == 0)
    def _():
        o_ref[...] = jnp.zeros_like(o_ref)

    hf = hf_ref[...]
    gate = jnp.dot(hf, wg_ref[...], preferred_element_type=F32)
    up = jnp.dot(hf, wu_ref[...], preferred_element_type=F32)
    hid = (gate * jax.nn.sigmoid(gate) * up).astype(BF16)
    for c0 in range(0, D_MODEL, _FFN_DOWN_CHUNK):
        cs = slice(c0, c0 + _FFN_DOWN_CHUNK)
        o_ref[:, cs] += jnp.dot(hid, wd_ref[:, cs], preferred_element_type=F32)

    for c in range(n_x1):
        @pl.when(f == c)
        def _(c=c):
            cs = slice(c * _FFN_X1_CHUNK, (c + 1) * _FFN_X1_CHUNK)
            o_ref[:, cs] += x1_ref[...]

    if final_norm:
        @pl.when(f == pl.num_programs(1) - 1)
        def _():
            o_ref[...] = _rms(o_ref[...], gfin_ref[...])


def _ffn(hf, x1, w_gu_bf, w_d_bf, g_final, *, layer, bm, tf=512, final_norm):
    m = hf.shape[0]
    nf = D_FF // tf
    n_x1 = D_MODEL // _FFN_X1_CHUNK
    assert nf >= n_x1
    return pl.pallas_call(
        functools.partial(_ffn_kernel, final_norm=final_norm),
        grid=(m // bm, nf),
        in_specs=[pl.BlockSpec((bm, D_MODEL), lambda i, f: (i, 0)),
                  pl.BlockSpec((bm, _FFN_X1_CHUNK), lambda i, f: (i, jnp.minimum(f, n_x1 - 1))),
                  pl.BlockSpec((None, D_MODEL, tf), lambda i, f: (layer, 0, f)),
                  pl.BlockSpec((None, D_MODEL, tf), lambda i, f: (layer, 0, nf + f)),
                  pl.BlockSpec((None, tf, D_MODEL), lambda i, f: (layer, f, 0)),
                  pl.BlockSpec((1, D_MODEL), lambda i, f: (0, 0))],
        out_specs=pl.BlockSpec((bm, D_MODEL), lambda i, f: (i, 0)),
        out_shape=jax.ShapeDtypeStruct((m, D_MODEL), F32),
        compiler_params=_cparams(("parallel", "arbitrary")),
        name="ffn",
    )(hf, x1, w_gu_bf, w_gu_bf, w_d_bf, g_final)


def _rope_tables(pos):
    half = HEAD_DIM // 2
    inv = ROPE_THETA ** (-jnp.arange(half, dtype=F32) / half)
    ang = pos.astype(F32)[:, None] * inv[None, :]
    cos, sin = jnp.cos(ang), jnp.sin(ang)
    cos_t = jnp.tile(cos, (1, LANES // half))
    sin_t = jnp.tile(jnp.concatenate([-sin, sin], axis=1), (1, LANES // HEAD_DIM))
    return cos_t, sin_t


def _block_diag(w):
    hh, blk, _ = w.shape
    eye = jnp.eye(hh, dtype=w.dtype)
    return (eye[:, None, :, None] * w[:, :, None, :]).reshape(hh * blk, hh * blk)


def kernel(x_prompt, x_sample, state_lru_h, state_lru_conv, cache_swa_k, cache_swa_v, state_sconv,
           norm_mix, w_in, norm_grp, w_out, lru_conv_w, lru_conv_b, lru_w_a, lru_b_a, lru_w_i, lru_b_i,
           lru_lambda, sc_conv_w, attn_sinks, norm_ffn, ffn_w_gu, ffn_w_down, norm_final):
    n_p, t_p, _ = x_prompt.shape
    n_s, t_s, _ = x_sample.shape
    depth = w_in.shape[0]
    wb = cache_swa_k.shape[2]
    assert t_s == 1 and wb == WINDOW and t_p % 256 == 0 and n_s % 16 == 0

    bm_p = 512 if (n_p * t_p) % 512 == 0 and t_p % 512 == 0 else 256
    bm_ffn = 1024 if (n_p * t_p) % 1024 == 0 else bm_p
    bm_s = n_s

    cos_p, sin_p = _rope_tables(jnp.arange(t_p, dtype=jnp.int32))
    cos_s, sin_s = _rope_tables(jnp.full((bm_s,), PAST_LEN, dtype=jnp.int32))

    sel = (jnp.arange(D_ATTN)[:, None] % HEAD_DIM == jnp.arange(D_KV)[None, :] % HEAD_DIM).astype(BF16)

    xp = x_prompt.reshape(n_p * t_p, D_MODEL)
    xs = x_sample.reshape(n_s, D_MODEL)
    row = lambda v: v.reshape(1, -1)
    p_states, s_states = [], []
    w_in_bf = w_in.astype(BF16)
    w_out_bf = w_out.astype(BF16)
    w_gu_bf = ffn_w_gu.astype(BF16)
    w_d_bf = ffn_w_down.astype(BF16)
    g_mix = norm_mix.reshape(depth, 1, D_MODEL)
    g_ffn = norm_ffn.reshape(depth, 1, D_MODEL)
    for l in range(depth):
        g_attn, g_lru, g_sc = (norm_grp[l, :D_ATTN], norm_grp[l, D_ATTN:D_ATTN + D_LRU],
                               norm_grp[l, D_ATTN + D_LRU:])
        lw = dict(
            conv_w=lru_conv_w[l], conv_b=row(lru_conv_b[l]),
            w_gates=jnp.concatenate([_block_diag(lru_w_a[l]), _block_diag(lru_w_i[l])], axis=1).astype(BF16),
            b_a=row(lru_b_a[l]), b_i=row(lru_b_i[l]), lam=row(lru_lambda[l]),
            sc_w=sc_conv_w[l], g_lru=row(g_lru), g_sc=row(g_sc),
            sel=sel,
            sink_tab=jnp.broadcast_to(attn_sinks[l][:, None], (N_HEADS, LANES)),
            g_attn_tab=jnp.tile(g_attn.reshape(N_HEADS, HEAD_DIM), (1, LANES // HEAD_DIM)),
        )
        last = l == depth - 1

        z = _in_proj(xp, g_mix, w_in_bf, cos_p, sin_p, layer=l, bm=bm_p)
        ma = _attn_prompt(z, attn_sinks[l], row(g_attn), n_seq=n_p, seq=t_p)
        mb, h8, x8, g8 = _lru_sc_prompt(z, lw, n_seq=n_p, seq=t_p)
        x1, hf = _out_proj(ma, mb, xp, w_out_bf, g_ffn, layer=l, bm=bm_p)
        xp_new = _ffn(hf, x1, w_gu_bf, w_d_bf, row(norm_final), layer=l, bm=bm_ffn, final_norm=last)
        z3 = z.reshape(n_p, t_p, D_IN)
        wbp = min(WINDOW, t_p)
        p_states.append((
            h8[:, SUBLANES - 1],
            x8[:, SUBLANES - (LRU_CONV_W - 1):],
            z3[:, t_p - wbp:, D_ATTN:D_ATTN + D_KV].reshape(n_p, wbp, N_KV_HEADS, HEAD_DIM),
            z3[:, t_p - wbp:, D_ATTN + D_KV:D_ATTN + 2 * D_KV].reshape(n_p, wbp, N_KV_HEADS, HEAD_DIM),
            g8[:, SUBLANES - (SC_CONV_W - 1):],
        ))
        xp = xp_new

        zs = _in_proj(xs, g_mix, w_in_bf, cos_s, sin_s, layer=l, bm=bm_s)
        oat, ors, k_new, v_new, h_new, c_new, s_new = _decode_mix(
            zs, cache_swa_k[l].reshape(n_s, wb, D_KV), cache_swa_v[l].reshape(n_s, wb, D_KV),
            state_lru_h[l], state_lru_conv[l], state_sconv[l], lw)
        mas = oat[:, :, :HEAD_DIM].reshape(n_s, D_ATTN)
        x1s, hfs = _out_proj(mas, ors, xs, w_out_bf, g_ffn, layer=l, bm=bm_s)
        xs = _ffn(hfs, x1s, w_gu_bf, w_d_bf, row(norm_final), layer=l, bm=bm_s, final_norm=last)
        s_states.append((h_new, c_new,
                         k_new.reshape(n_s, wb, N_KV_HEADS, HEAD_DIM),
                         v_new.reshape(n_s, wb, N_KV_HEADS, HEAD_DIM), s_new))

    y_prompt = xp.reshape(n_p, t_p, D_MODEL)
    y_sample = xs.reshape(n_s, t_s, D_MODEL)
    stack = lambda states, k: jnp.stack([st[k] for st in states])
    return (y_prompt, y_sample,
            stack(p_states, 0), stack(p_states, 1), stack(p_states, 2), stack(p_states, 3), stack(p_states, 4),
            stack(s_states, 0), stack(s_states, 1), stack(s_states, 2), stack(s_states, 3), stack(s_states, 4))
```

```python
import functools

import jax
import jax.numpy as jnp
from jax import lax
from jax.experimental import pallas as pl
from jax.experimental.pallas import tpu as pltpu

F32 = jnp.float32
BF16 = jnp.bfloat16

D_MODEL = 2048
D_ATTN = 1024
D_LRU = 512
D_SC = 512
HEAD_DIM = 64
N_HEADS = 16
N_KV_HEADS = 4
N_GROUP = 4
D_KV = 256
WINDOW = 128
ROPE_THETA = 10000.0
N_LRU_HEADS = 8
LRU_BLK = 64
LRU_CONV_W = 4
LRU_C = 8.0
SC_CONV_W = 3
D_FF = 5632
D_IN = 4096
RMS_EPS = 1e-6
PAST_LEN = 8192

LANES = 128
SUBLANES = 8
VMEM_LIMIT_BYTES = 56 * 1024 * 1024

_COL_UX, _COL_GATE, _COL_B, _COL_C, _COL_H = 3, 4, 5, 6, 7


def _cparams(sem):
    return pltpu.CompilerParams(dimension_semantics=sem, vmem_limit_bytes=VMEM_LIMIT_BYTES)


def _rms(x, g):
    return x * lax.rsqrt(jnp.mean(x * x, axis=-1, keepdims=True) + RMS_EPS) * g


_IN_PROJ_CHUNK = 512


def _in_proj_kernel(x_ref, g_ref, w_ref, cos_ref, sin_ref, z_ref):
    h = _rms(x_ref[...], g_ref[...]).astype(BF16)
    bm = h.shape[0]
    lane = lax.broadcasted_iota(jnp.int32, (bm, LANES), 1)
    lo32 = (lane % HEAD_DIM) < (HEAD_DIM // 2)
    cos = cos_ref[...]
    sin = sin_ref[...]

    def rope(a):
        sw = jnp.where(lo32, pltpu.roll(a, LANES - HEAD_DIM // 2, 1), pltpu.roll(a, HEAD_DIM // 2, 1))
        return a * cos + sw * sin

    rope_cols = D_ATTN + D_KV
    for c0 in range(0, D_IN, _IN_PROJ_CHUNK):
        acc = jnp.dot(h, w_ref[:, c0:c0 + _IN_PROJ_CHUNK], preferred_element_type=F32)
        for c in range(0, _IN_PROJ_CHUNK, LANES):
            a = acc[:, c:c + LANES]
            z_ref[:, c0 + c:c0 + c + LANES] = rope(a) if c0 + c < rope_cols else a


def _resident(block_shape, index_map):
    return pl.BlockSpec(block_shape, index_map, pipeline_mode=pl.Buffered(1))


def _in_proj(x, g_all, w_all_bf, cos_t, sin_t, *, layer, bm):
    m = x.shape[0]
    n_tab = cos_t.shape[0] // bm
    return pl.pallas_call(
        _in_proj_kernel,
        grid=(m // bm,),
        in_specs=[
            pl.BlockSpec((bm, D_MODEL), lambda i: (i, 0)),
            pl.BlockSpec((None, 1, D_MODEL), lambda i: (layer, 0, 0)),
            _resident((None, D_MODEL, D_IN), lambda i: (layer, 0, 0)),
            pl.BlockSpec((bm, LANES), lambda i: (i % n_tab, 0)),
            pl.BlockSpec((bm, LANES), lambda i: (i % n_tab, 0)),
        ],
        out_specs=pl.BlockSpec((bm, D_IN), lambda i: (i, 0)),
        out_shape=jax.ShapeDtypeStruct((m, D_IN), F32),
        compiler_params=_cparams(("parallel",)),
        name="in_proj",
    )(x, g_all, w_all_bf, cos_t, sin_t)


def _attn_prompt_kernel(sink_ref, q_ref, kc_ref, kp_ref, vc_ref, vp_ref, g_ref, o_ref):
    b = pl.program_id(1)
    L = WINDOW
    qb = (q_ref[...] * (HEAD_DIM ** -0.5)).astype(BF16)
    kk = jnp.concatenate([kp_ref[...], kc_ref[...]], axis=0)
    vv = jnp.concatenate([vp_ref[...], vc_ref[...]], axis=0)

    lane = lax.broadcasted_iota(jnp.int32, (2 * L, LANES), 1)
    lo = lane < HEAD_DIM
    row = lax.broadcasted_iota(jnp.int32, (2 * L, 1), 0)
    top = row < L

    qi = lax.broadcasted_iota(jnp.int32, (2 * L, 4 * L), 0) % L
    sj = lax.broadcasted_iota(jnp.int32, (2 * L, 4 * L), 1) % (2 * L)
    diff = L + qi - sj
    valid = (diff >= 0) & (diff < WINDOW) & ((sj >= L) | (b > 0))

    zeros = jnp.zeros((2 * L, LANES), F32)
    ones_lo = jnp.where(lo, 1.0, 0.0).astype(F32)
    ones_hi = 1.0 - ones_lo

    outs = []
    for kh in range(N_KV_HEADS):
        c0 = LANES * (kh // 2)
        kx = kk[:, c0:c0 + LANES]
        vx = vv[:, c0:c0 + LANES]
        kr = pltpu.roll(kx, HEAD_DIM, 1)
        vr = pltpu.roll(vx, HEAD_DIM, 1)
        if kh % 2 == 0:
            k_lo, k_hi = jnp.where(lo, kx, zeros), jnp.where(lo, zeros, kr)
            v_lo, v_hi = jnp.where(lo, vx, zeros), jnp.where(lo, zeros, vr)
        else:
            k_lo, k_hi = jnp.where(lo, kr, zeros), jnp.where(lo, zeros, kx)
            v_lo, v_hi = jnp.where(lo, vr, zeros), jnp.where(lo, zeros, vx)
        kmat = jnp.concatenate([k_lo, k_hi], axis=0).astype(BF16)
        qs = jnp.concatenate([qb[:, 2 * LANES * kh:2 * LANES * kh + LANES],
                              qb[:, 2 * LANES * kh + LANES:2 * LANES * (kh + 1)]], axis=0)
        s = lax.dot_general(qs, kmat, (((1,), (1,)), ((), ())), preferred_element_type=F32)
        s = jnp.where(valid, s, -jnp.inf)
        sink_lo = jnp.where(top, sink_ref[4 * kh + 0], sink_ref[4 * kh + 2])
        sink_hi = jnp.where(top, sink_ref[4 * kh + 1], sink_ref[4 * kh + 3])
        m_lo = jnp.maximum(jnp.max(s[:, :2 * L], axis=1, keepdims=True), sink_lo)
        m_hi = jnp.maximum(jnp.max(s[:, 2 * L:], axis=1, keepdims=True), sink_hi)
        p = jnp.concatenate([jnp.exp(s[:, :2 * L] - m_lo), jnp.exp(s[:, 2 * L:] - m_hi)], axis=1).astype(BF16)
        vmat = jnp.concatenate([jnp.concatenate([v_lo, ones_lo], axis=1),
                                jnp.concatenate([v_hi, ones_hi], axis=1)], axis=0).astype(BF16)
        oe = jnp.dot(p, vmat, preferred_element_type=F32)
        denom = oe[:, LANES:] + jnp.where(lo, jnp.exp(sink_lo - m_lo), jnp.exp(sink_hi - m_hi))
        o = oe[:, :LANES] / denom
        outs.append(o[:L])
        outs.append(o[L:])
    o_all = jnp.concatenate(outs, axis=1)
    o_ref[...] = _rms(o_all, g_ref[...]).astype(o_ref.dtype)


def _attn_prompt(z, sinks, g_attn, *, n_seq, seq):
    L = WINDOW
    nb = seq // L
    kcol = D_ATTN // D_KV
    vcol = kcol + 1
    cur = lambda col: (lambda n, b: (n * nb + b, col))
    prev = lambda col: (lambda n, b: (n * nb + jnp.maximum(b - 1, 0), col))
    return pl.pallas_call(
        _attn_prompt_kernel,
        grid=(n_seq, nb),
        in_specs=[
            pl.BlockSpec(memory_space=pltpu.SMEM),
            pl.BlockSpec((L, D_ATTN), cur(0)),
            pl.BlockSpec((L, D_KV), cur(kcol)),
            pl.BlockSpec((L, D_KV), prev(kcol)),
            pl.BlockSpec((L, D_KV), cur(vcol)),
            pl.BlockSpec((L, D_KV), prev(vcol)),
            pl.BlockSpec((1, D_ATTN), lambda n, b: (0, 0)),
        ],
        out_specs=pl.BlockSpec((L, D_ATTN), cur(0)),
        out_shape=jax.ShapeDtypeStruct((n_seq * seq, D_ATTN), BF16),
        compiler_params=_cparams(("parallel", "arbitrary")),
        name="attn_prompt",
    )(sinks, z, z, z, z, z, g_attn)


def _neg_expm1(x, ex):
    return -jnp.where(x < -0.5, ex - 1.0, jnp.tanh(0.5 * x) * (ex + 1.0))


def _lru_gates(xc, wg_ref, ba, bi, lam):
    g = jnp.dot(xc.astype(BF16), wg_ref[...], preferred_element_type=F32)
    r = jax.nn.sigmoid(g[:, :D_LRU] + ba)
    gi = jax.nn.sigmoid(g[:, D_LRU:] + bi)
    nl = -lam
    softplus = jnp.maximum(nl, 0.0) + jnp.log1p(jnp.exp(-jnp.abs(nl)))
    log_a = -LRU_C * r * softplus
    a = jnp.exp(log_a)
    mult = jnp.sqrt(_neg_expm1(2.0 * log_a, a * a))
    return a, mult, gi


def _shift_rows(u, prev8, k):
    r = pltpu.roll(u, k, 0)
    pr = pltpu.roll(prev8, k, 0)
    row8 = lax.broadcasted_iota(jnp.int32, prev8.shape, 0)
    head = jnp.where(row8 < k, pr, r[:SUBLANES])
    return jnp.concatenate([head, r[SUBLANES:]], axis=0)


def _chunk_scan(a, b):
    n = a.shape[0]
    row = lax.broadcasted_iota(jnp.int32, a.shape, 0)
    d = 1
    while d < n:
        if d < SUBLANES:
            keep = row >= d
            b = jnp.where(keep, b + a * pltpu.roll(b, d, 0), b)
            a = jnp.where(keep, a * pltpu.roll(a, d, 0), a)
        else:
            b = jnp.concatenate([b[:d], b[d:] + a[d:] * b[:n - d]], axis=0)
            a = jnp.concatenate([a[:d], a[d:] * a[:n - d]], axis=0)
        d *= 2
    return a, b


def _lru_sc_kernel(ux_ref, gate_ref, ub_ref, uc_ref, uh_ref, cw_ref, cb_ref, wg_ref, ba_ref, bi_ref,
                   lam_ref, scw_ref, glru_ref, gsc_ref,
                   o_ref, h8_ref, x8_ref, g8_ref, cx_scr, cg_scr, ch_scr, *, tc):
    t = pl.program_id(1)

    @pl.when(t == 0)
    def _():
        cx_scr[...] = jnp.zeros_like(cx_scr)
        cg_scr[...] = jnp.zeros_like(cg_scr)
        ch_scr[...] = jnp.zeros_like(ch_scr)

    ux = ux_ref[...]
    px = cx_scr[...]
    xc = _shift_rows(ux, px, 3) * cw_ref[0:1, :]
    xc = xc + _shift_rows(ux, px, 2) * cw_ref[1:2, :]
    xc = xc + _shift_rows(ux, px, 1) * cw_ref[2:3, :]
    xc = xc + ux * cw_ref[3:4, :]
    xc = xc + cb_ref[...]

    a, mult, gi = _lru_gates(xc, wg_ref, ba_ref[...], bi_ref[...], lam_ref[...])
    pos = t * tc + lax.broadcasted_iota(jnp.int32, (tc, 1), 0)
    mult = jnp.where(pos == 0, 1.0, mult)
    a_cum, h_loc = _chunk_scan(a, mult * gi * xc)
    h = h_loc + a_cum * ch_scr[SUBLANES - 1:SUBLANES, :]
    o_lru = h * jax.nn.gelu(gate_ref[...], approximate=True)

    gch = uc_ref[...] * uh_ref[...]
    pg = cg_scr[...]
    y = _shift_rows(gch, pg, 2) * scw_ref[0:1, :]
    y = y + _shift_rows(gch, pg, 1) * scw_ref[1:2, :]
    y = y + gch * scw_ref[2:3, :]
    o_sc = ub_ref[...] * y

    o_ref[:, :D_LRU] = _rms(o_lru, glru_ref[...]).astype(o_ref.dtype)
    o_ref[:, D_LRU:] = _rms(o_sc, gsc_ref[...]).astype(o_ref.dtype)

    cx_scr[...] = ux[tc - SUBLANES:]
    cg_scr[...] = gch[tc - SUBLANES:]
    ch_scr[...] = h[tc - SUBLANES:]
    h8_ref[0] = h[tc - SUBLANES:]
    x8_ref[0] = ux[tc - SUBLANES:]
    g8_ref[0] = gch[tc - SUBLANES:]


def _lru_sc_prompt(z, lw, *, n_seq, seq, tc=256):
    nt = seq // tc
    zcol = lambda col: pl.BlockSpec((tc, D_LRU), lambda n, t: (n * nt + t, col))
    const = lambda shape: pl.BlockSpec(shape, lambda n, t: (0,) * len(shape))
    st = pl.BlockSpec((1, SUBLANES, D_LRU), lambda n, t: (n, 0, 0))
    st_shape = jax.ShapeDtypeStruct((n_seq, SUBLANES, D_LRU), F32)
    return pl.pallas_call(
        functools.partial(_lru_sc_kernel, tc=tc),
        grid=(n_seq, nt),
        in_specs=[zcol(_COL_UX), zcol(_COL_GATE), zcol(_COL_B), zcol(_COL_C), zcol(_COL_H),
                  const((LRU_CONV_W, D_LRU)), const((1, D_LRU)), const((D_LRU, 2 * D_LRU)),
                  const((1, D_LRU)), const((1, D_LRU)), const((1, D_LRU)),
                  const((SC_CONV_W, D_SC)), const((1, D_LRU)), const((1, D_SC))],
        out_specs=[pl.BlockSpec((tc, D_LRU + D_SC), lambda n, t: (n * nt + t, 0)), st, st, st],
        out_shape=[jax.ShapeDtypeStruct((n_seq * seq, D_LRU + D_SC), BF16), st_shape, st_shape, st_shape],
        scratch_shapes=[pltpu.VMEM((SUBLANES, D_LRU), F32)] * 3,
        compiler_params=_cparams(("parallel", "arbitrary")),
        name="lru_sc_prompt",
    )(z, z, z, z, z, lw["conv_w"], lw["conv_b"], lw["w_gates"], lw["b_a"], lw["b_i"], lw["lam"],
      lw["sc_w"], lw["g_lru"], lw["g_sc"])


def _decode_kernel(q_ref, knt_ref, kc_ref, vc_ref, sel_ref, sink_ref, gat_ref,
                   ux_ref, gate_ref, ub_ref, uc_ref, uh_ref, h0_ref, cbuf_ref, sbuf_ref,
                   cw_ref, cb_ref, wg_ref, ba_ref, bi_ref, lam_ref, scw_ref, glru_ref, gsc_ref,
                   oat_ref, ors_ref, ko_ref, vo_ref, hn_ref, cn_ref, sn_ref, *, nbk):
    wb = kc_ref.shape[2]
    hrow = lax.broadcasted_iota(jnp.int32, (N_HEADS, D_ATTN), 0)
    hcol = lax.broadcasted_iota(jnp.int32, (N_HEADS, D_ATTN), 1) // HEAD_DIM
    own = (hrow == hcol)
    qexp = jnp.concatenate(
        [jnp.where(own, jnp.broadcast_to(q_ref[i:i + 1, :] * (HEAD_DIM ** -0.5), (N_HEADS, D_ATTN)), 0.0)
         for i in range(nbk)], axis=0)
    qrow = jnp.dot(qexp.astype(BF16), sel_ref[...], preferred_element_type=F32)
    rows = nbk * N_HEADS
    grow = (lax.broadcasted_iota(jnp.int32, (rows, D_KV), 0) % N_HEADS) // N_GROUP
    gcol = lax.broadcasted_iota(jnp.int32, (rows, D_KV), 1) // HEAD_DIM
    kvmask = (grow == gcol)
    qm = jnp.where(kvmask, qrow, 0.0).astype(BF16)
    sink = jnp.concatenate([sink_ref[...][:, 0:1]] * nbk, axis=0)
    gat = jnp.concatenate([gat_ref[...]] * nbk, axis=0)
    lanek = lax.broadcasted_iota(jnp.int32, (D_KV, wb), 1)
    newest = lanek == wb - 1
    for i in range(nbk):
        ko_ref[i] = jnp.where(newest, jnp.broadcast_to(knt_ref[:D_KV, i:i + 1], (D_KV, wb)),
                              pltpu.roll(kc_ref[i], wb - 1, 1))
        vo_ref[i] = jnp.where(newest, jnp.broadcast_to(knt_ref[D_KV:, i:i + 1], (D_KV, wb)),
                              pltpu.roll(vc_ref[i], wb - 1, 1))
    s = jnp.concatenate(
        [jnp.dot(qm[i * N_HEADS:(i + 1) * N_HEADS], ko_ref[i].astype(BF16), preferred_element_type=F32)
         for i in range(nbk)], axis=0)
    m = jnp.maximum(jnp.max(s, axis=1, keepdims=True), sink)
    p = jnp.exp(s - m)
    p = (p / (jnp.sum(p, axis=1, keepdims=True) + jnp.exp(sink - m))).astype(BF16)
    of = jnp.concatenate(
        [lax.dot_general(p[i * N_HEADS:(i + 1) * N_HEADS], vo_ref[i].astype(BF16), (((1,), (1,)), ((), ())),
                         preferred_element_type=F32) for i in range(nbk)], axis=0)
    of = jnp.where(kvmask, of, 0.0)
    t = of[:, :LANES] + of[:, LANES:]
    o = t + pltpu.roll(t, HEAD_DIM, 1)
    rs = jnp.sum(o * o, axis=1, keepdims=True)
    for i in range(nbk):
        sl = slice(i * N_HEADS, (i + 1) * N_HEADS)
        ms = jnp.sum(rs[sl], axis=0, keepdims=True) * (0.5 / D_ATTN)
        oat_ref[i] = (o[sl] * lax.rsqrt(ms + RMS_EPS) * gat[sl]).astype(oat_ref.dtype)

    ux = ux_ref[...]
    xc = cbuf_ref[0] * cw_ref[0:1, :]
    xc = xc + cbuf_ref[1] * cw_ref[1:2, :]
    xc = xc + cbuf_ref[2] * cw_ref[2:3, :]
    xc = xc + ux * cw_ref[3:4, :]
    xc = xc + cb_ref[...]
    a, mult, gi = _lru_gates(xc, wg_ref, ba_ref[...], bi_ref[...], lam_ref[...])
    h = a * h0_ref[...] + mult * gi * xc
    o_lru = h * jax.nn.gelu(gate_ref[...], approximate=True)
    hn_ref[...] = h
    cn_ref[0] = cbuf_ref[1]
    cn_ref[1] = cbuf_ref[2]
    cn_ref[2] = ux
    gch = uc_ref[...] * uh_ref[...]
    y = sbuf_ref[:, 0, :] * scw_ref[0:1, :]
    y = y + sbuf_ref[:, 1, :] * scw_ref[1:2, :]
    y = y + gch * scw_ref[2:3, :]
    o_sc = ub_ref[...] * y
    sn_ref[:, 0, :] = sbuf_ref[:, 1, :]
    sn_ref[:, 1, :] = gch
    ors_ref[:, :D_LRU] = _rms(o_lru, glru_ref[...]).astype(ors_ref.dtype)
    ors_ref[:, D_LRU:] = _rms(o_sc, gsc_ref[...]).astype(ors_ref.dtype)


def _decode_mix(z, knt, kct_all, vct_all, h0_all, cbuf_all, sbuf_all, lw, *, layer, nbk=16):
    ns = z.shape[0]
    wb = kct_all.shape[3]
    z512 = lambda col: pl.BlockSpec((nbk, D_LRU), lambda i: (i, col))
    const = lambda shape: pl.BlockSpec(shape, lambda i: (0,) * len(shape))
    cache_in = pl.BlockSpec((None, nbk, D_KV, wb), lambda i: (layer, i, 0, 0))
    cache_out = pl.BlockSpec((nbk, D_KV, wb), lambda i: (i, 0, 0))
    outs = pl.pallas_call(
        functools.partial(_decode_kernel, nbk=nbk),
        grid=(ns // nbk,),
        in_specs=[pl.BlockSpec((nbk, D_ATTN), lambda i: (i, 0)),
                  pl.BlockSpec((None, 2 * D_KV, nbk), lambda i: (i, 0, 0)),
                  cache_in, cache_in,
                  const((D_ATTN, D_KV)), const((N_HEADS, LANES)), const((N_HEADS, LANES)),
                  z512(_COL_UX), z512(_COL_GATE), z512(_COL_B), z512(_COL_C), z512(_COL_H),
                  pl.BlockSpec((None, nbk, D_LRU), lambda i: (layer, i, 0)),
                  pl.BlockSpec((None, LRU_CONV_W - 1, nbk, D_LRU), lambda i: (layer, 0, i, 0)),
                  pl.BlockSpec((None, nbk, SC_CONV_W - 1, D_SC), lambda i: (layer, i, 0, 0)),
                  const((LRU_CONV_W, D_LRU)), const((1, D_LRU)), const((D_LRU, 2 * D_LRU)),
                  const((1, D_LRU)), const((1, D_LRU)), const((1, D_LRU)),
                  const((SC_CONV_W, D_SC)), const((1, D_LRU)), const((1, D_SC))],
        out_specs=[pl.BlockSpec((nbk, N_HEADS, LANES), lambda i: (i, 0, 0)),
                   pl.BlockSpec((nbk, D_LRU + D_SC), lambda i: (i, 0)),
                   cache_out, cache_out,
                   pl.BlockSpec((nbk, D_LRU), lambda i: (i, 0)),
                   pl.BlockSpec((LRU_CONV_W - 1, nbk, D_LRU), lambda i: (0, i, 0)),
                   pl.BlockSpec((nbk, SC_CONV_W - 1, D_SC), lambda i: (i, 0, 0))],
        out_shape=[jax.ShapeDtypeStruct((ns, N_HEADS, LANES), BF16),
                   jax.ShapeDtypeStruct((ns, D_LRU + D_SC), BF16),
                   jax.ShapeDtypeStruct((ns, D_KV, wb), F32), jax.ShapeDtypeStruct((ns, D_KV, wb), F32),
                   jax.ShapeDtypeStruct((ns, D_LRU), F32),
                   jax.ShapeDtypeStruct((LRU_CONV_W - 1, ns, D_LRU), F32),
                   jax.ShapeDtypeStruct((ns, SC_CONV_W - 1, D_SC), F32)],
        compiler_params=_cparams(("parallel",)),
        name="decode_mix",
    )(z, knt, kct_all, vct_all, lw["sel"], lw["sink_tab"], lw["g_attn_tab"],
      z, z, z, z, z, h0_all, cbuf_all, sbuf_all,
      lw["conv_w"], lw["conv_b"], lw["w_gates"], lw["b_a"], lw["b_i"], lw["lam"],
      lw["sc_w"], lw["g_lru"], lw["g_sc"])
    return outs


_OUT_PROJ_CHUNK = 512


def _out_proj_kernel(ma_ref, mb_ref, x_ref, w_ref, g_ref, x1_ref, hf_ref):
    ma = ma_ref[...]
    mb = mb_ref[...]
    ssq = None
    for c0 in range(0, D_MODEL, _OUT_PROJ_CHUNK):
        cs = slice(c0, c0 + _OUT_PROJ_CHUNK)
        acc = jnp.dot(ma, w_ref[:D_ATTN, cs], preferred_element_type=F32)
        acc = acc + jnp.dot(mb, w_ref[D_ATTN:, cs], preferred_element_type=F32)
        x1 = x_ref[:, cs] + acc
        x1_ref[:, cs] = x1
        part = jnp.sum(x1 * x1, axis=-1, keepdims=True)
        ssq = part if ssq is None else ssq + part
    scale = lax.rsqrt(ssq * (1.0 / D_MODEL) + RMS_EPS)
    for c0 in range(0, D_MODEL, _OUT_PROJ_CHUNK):
        cs = slice(c0, c0 + _OUT_PROJ_CHUNK)
        hf_ref[:, cs] = (x1_ref[:, cs] * scale * g_ref[:, cs]).astype(hf_ref.dtype)


def _out_proj(ma, mb, x, w_all_bf, g_all, *, layer, bm):
    m = x.shape[0]
    return pl.pallas_call(
        _out_proj_kernel,
        grid=(m // bm,),
        in_specs=[pl.BlockSpec((bm, D_ATTN), lambda i: (i, 0)),
                  pl.BlockSpec((bm, D_LRU + D_SC), lambda i: (i, 0)),
                  pl.BlockSpec((bm, D_MODEL), lambda i: (i, 0)),
                  _resident((None, D_MODEL, D_MODEL), lambda i: (layer, 0, 0)),
                  pl.BlockSpec((None, 1, D_MODEL), lambda i: (layer, 0, 0))],
        out_specs=[pl.BlockSpec((bm, D_MODEL), lambda i: (i, 0)),
                   pl.BlockSpec((bm, D_MODEL), lambda i: (i, 0))],
        out_shape=[jax.ShapeDtypeStruct((m, D_MODEL), F32), jax.ShapeDtypeStruct((m, D_MODEL), BF16)],
        compiler_params=_cparams(("parallel",)),
        name="out_proj",
    )(ma, mb, x, w_all_bf, g_all)


_FFN_DOWN_CHUNK = 512
_FFN_X1_CHUNK = 256


def _ffn_kernel(hf_ref, x1_ref, wg_ref, wu_ref, wd_ref, gfin_ref, o_ref, *, final_norm):
    f = pl.program_id(1)
    n_x1 = D_MODEL // _FFN_X1_CHUNK

    @pl.when(f == 0)
    def _():
        o_ref[...] = jnp.zeros_like(o_ref)

    hf = hf_ref[...]
    gate = jnp.dot(hf, wg_ref[...], preferred_element_type=F32)
    up = jnp.dot(hf, wu_ref[...], preferred_element_type=F32)
    hid = (gate * jax.nn.sigmoid(gate) * up).astype(BF16)
    for c0 in range(0, D_MODEL, _FFN_DOWN_CHUNK):
        cs = slice(c0, c0 + _FFN_DOWN_CHUNK)
        o_ref[:, cs] += jnp.dot(hid, wd_ref[:, cs], preferred_element_type=F32)

    for c in range(n_x1):
        @pl.when(f == c)
        def _(c=c):
            cs = slice(c * _FFN_X1_CHUNK, (c + 1) * _FFN_X1_CHUNK)
            o_ref[:, cs] += x1_ref[...]

    if final_norm:
        @pl.when(f == pl.num_programs(1) - 1)
        def _():
            o_ref[...] = _rms(o_ref[...], gfin_ref[...])


def _ffn(hf, x1, w_gu_bf, w_d_bf, g_final, *, layer, bm, tf=512, final_norm):
    m = hf.shape[0]
    nf = D_FF // tf
    n_x1 = D_MODEL // _FFN_X1_CHUNK
    assert nf >= n_x1
    return pl.pallas_call(
        functools.partial(_ffn_kernel, final_norm=final_norm),
        grid=(m // bm, nf),
        in_specs=[pl.BlockSpec((bm, D_MODEL), lambda i, f: (i, 0)),
                  pl.BlockSpec((bm, _FFN_X1_CHUNK), lambda i, f: (i, jnp.minimum(f, n_x1 - 1))),
                  pl.BlockSpec((None, D_MODEL, tf), lambda i, f: (layer, 0, f)),
                  pl.BlockSpec((None, D_MODEL, tf), lambda i, f: (layer, 0, nf + f)),
                  pl.BlockSpec((None, tf, D_MODEL), lambda i, f: (layer, f, 0)),
                  pl.BlockSpec((1, D_MODEL), lambda i, f: (0, 0))],
        out_specs=pl.BlockSpec((bm, D_MODEL), lambda i, f: (i, 0)),
        out_shape=jax.ShapeDtypeStruct((m, D_MODEL), F32),
        compiler_params=_cparams(("parallel", "arbitrary")),
        name="ffn",
    )(hf, x1, w_gu_bf, w_gu_bf, w_d_bf, g_final)


def _rope_tables(pos):
    half = HEAD_DIM // 2
    inv = ROPE_THETA ** (-jnp.arange(half, dtype=F32) / half)
    ang = pos.astype(F32)[:, None] * inv[None, :]
    cos, sin = jnp.cos(ang), jnp.sin(ang)
    cos_t = jnp.tile(cos, (1, LANES // half))
    sin_t = jnp.tile(jnp.concatenate([-sin, sin], axis=1), (1, LANES // HEAD_DIM))
    return cos_t, sin_t


def _block_diag(w):
    hh, blk, _ = w.shape
    eye = jnp.eye(hh, dtype=w.dtype)
    return (eye[:, None, :, None] * w[:, :, None, :]).reshape(hh * blk, hh * blk)


def kernel(x_prompt, x_sample, state_lru_h, state_lru_conv, cache_swa_k, cache_swa_v, state_sconv,
           norm_mix, w_in, norm_grp, w_out, lru_conv_w, lru_conv_b, lru_w_a, lru_b_a, lru_w_i, lru_b_i,
           lru_lambda, sc_conv_w, attn_sinks, norm_ffn, ffn_w_gu, ffn_w_down, norm_final):
    n_p, t_p, _ = x_prompt.shape
    n_s, t_s, _ = x_sample.shape
    depth = w_in.shape[0]
    wb = cache_swa_k.shape[2]
    assert t_s == 1 and wb == WINDOW and t_p % 256 == 0 and n_s % 16 == 0

    bm_p = 512 if (n_p * t_p) % 512 == 0 and t_p % 512 == 0 else 256
    bm_ffn = 1024 if (n_p * t_p) % 1024 == 0 else bm_p
    bm_s = n_s

    cos_p, sin_p = _rope_tables(jnp.arange(t_p, dtype=jnp.int32))
    cos_s, sin_s = _rope_tables(jnp.full((bm_s,), PAST_LEN, dtype=jnp.int32))

    sel = (jnp.arange(D_ATTN)[:, None] % HEAD_DIM == jnp.arange(D_KV)[None, :] % HEAD_DIM).astype(BF16)

    xp = x_prompt.reshape(n_p * t_p, D_MODEL)
    xs = x_sample.reshape(n_s, D_MODEL)
    row = lambda v: v.reshape(1, -1)
    p_states, s_states = [], []
    w_in_bf = w_in.astype(BF16)
    w_out_bf = w_out.astype(BF16)
    w_gu_bf = ffn_w_gu.astype(BF16)
    w_d_bf = ffn_w_down.astype(BF16)
    nbk = 16
    kct_all = cache_swa_k.transpose(0, 1, 3, 4, 2).reshape(depth, n_s, D_KV, wb)
    vct_all = cache_swa_v.transpose(0, 1, 3, 4, 2).reshape(depth, n_s, D_KV, wb)
    cbuf_all = state_lru_conv.transpose(0, 2, 1, 3)
    g_mix = norm_mix.reshape(depth, 1, D_MODEL)
    g_ffn = norm_ffn.reshape(depth, 1, D_MODEL)
    for l in range(depth):
        g_attn, g_lru, g_sc = (norm_grp[l, :D_ATTN], norm_grp[l, D_ATTN:D_ATTN + D_LRU],
                               norm_grp[l, D_ATTN + D_LRU:])
        lw = dict(
            conv_w=lru_conv_w[l], conv_b=row(lru_conv_b[l]),
            w_gates=jnp.concatenate([_block_diag(lru_w_a[l]), _block_diag(lru_w_i[l])], axis=1).astype(BF16),
            b_a=row(lru_b_a[l]), b_i=row(lru_b_i[l]), lam=row(lru_lambda[l]),
            sc_w=sc_conv_w[l], g_lru=row(g_lru), g_sc=row(g_sc),
            sel=sel,
            sink_tab=jnp.broadcast_to(attn_sinks[l][:, None], (N_HEADS, LANES)),
            g_attn_tab=jnp.tile(g_attn.reshape(N_HEADS, HEAD_DIM), (1, LANES // HEAD_DIM)),
        )
        last = l == depth - 1

        z = _in_proj(xp, g_mix, w_in_bf, cos_p, sin_p, layer=l, bm=bm_p)
        ma = _attn_prompt(z, attn_sinks[l], row(g_attn), n_seq=n_p, seq=t_p)
        mb, h8, x8, g8 = _lru_sc_prompt(z, lw, n_seq=n_p, seq=t_p)
        x1, hf = _out_proj(ma, mb, xp, w_out_bf, g_ffn, layer=l, bm=bm_p)
        xp_new = _ffn(hf, x1, w_gu_bf, w_d_bf, row(norm_final), layer=l, bm=bm_ffn, final_norm=last)
        z3 = z.reshape(n_p, t_p, D_IN)
        wbp = min(WINDOW, t_p)
        p_states.append((
            h8[:, SUBLANES - 1],
            x8[:, SUBLANES - (LRU_CONV_W - 1):],
            z3[:, t_p - wbp:, D_ATTN:D_ATTN + D_KV].reshape(n_p, wbp, N_KV_HEADS, HEAD_DIM),
            z3[:, t_p - wbp:, D_ATTN + D_KV:D_ATTN + 2 * D_KV].reshape(n_p, wbp, N_KV_HEADS, HEAD_DIM),
            g8[:, SUBLANES - (SC_CONV_W - 1):],
        ))
        xp = xp_new

        zs = _in_proj(xs, g_mix, w_in_bf, cos_s, sin_s, layer=l, bm=bm_s)
        knt = zs[:, D_ATTN:D_ATTN + 2 * D_KV].reshape(n_s // nbk, nbk, 2 * D_KV).transpose(0, 2, 1)
        oat, ors, k_new, v_new, h_new, c_new, s_new = _decode_mix(
            zs, knt, kct_all, vct_all, state_lru_h, cbuf_all, state_sconv, lw, layer=l, nbk=nbk)
        k_new = k_new.reshape(n_s, N_KV_HEADS, HEAD_DIM, wb).transpose(0, 3, 1, 2)
        v_new = v_new.reshape(n_s, N_KV_HEADS, HEAD_DIM, wb).transpose(0, 3, 1, 2)
        c_new = c_new.transpose(1, 0, 2)
        mas = oat[:, :, :HEAD_DIM].reshape(n_s, D_ATTN)
        x1s, hfs = _out_proj(mas, ors, xs, w_out_bf, g_ffn, layer=l, bm=bm_s)
        xs = _ffn(hfs, x1s, w_gu_bf, w_d_bf, row(norm_final), layer=l, bm=bm_s, final_norm=last)
        s_states.append((h_new, c_new, k_new, v_new, s_new))

    y_prompt = xp.reshape(n_p, t_p, D_MODEL)
    y_sample = xs.reshape(n_s, t_s, D_MODEL)
    stack = lambda states, k: jnp.stack([st[k] for st in states])
    return (y_prompt, y_sample,
            stack(p_states, 0), stack(p_states, 1), stack(p_states, 2), stack(p_states, 3), stack(p_states, 4),
            stack(s_states, 0), stack(s_states, 1), stack(s_states, 2), stack(s_states, 3), stack(s_states, 4))
```

```python
import functools

import jax
import jax.numpy as jnp
from jax import lax
from jax.experimental import pallas as pl
from jax.experimental.pallas import tpu as pltpu

F32 = jnp.float32
BF16 = jnp.bfloat16

D_MODEL = 2048
D_ATTN = 1024
D_LRU = 512
D_SC = 512
HEAD_DIM = 64
N_HEADS = 16
N_KV_HEADS = 4
N_GROUP = 4
D_KV = 256
WINDOW = 128
ROPE_THETA = 10000.0
N_LRU_HEADS = 8
LRU_BLK = 64
LRU_CONV_W = 4
LRU_C = 8.0
SC_CONV_W = 3
D_FF = 5632
D_IN = 4096
RMS_EPS = 1e-6
PAST_LEN = 8192

LANES = 128
SUBLANES = 8
VMEM_LIMIT_BYTES = 56 * 1024 * 1024

_COL_UX, _COL_GATE, _COL_B, _COL_C, _COL_H = 3, 4, 5, 6, 7


def _cparams(sem):
    return pltpu.CompilerParams(dimension_semantics=sem, vmem_limit_bytes=VMEM_LIMIT_BYTES)


def _rms(x, g):
    return x * lax.rsqrt(jnp.mean(x * x, axis=-1, keepdims=True) + RMS_EPS) * g


_IN_PROJ_CHUNK = 512


def _in_proj_kernel(x_ref, g_ref, w_ref, cos_ref, sin_ref, z_ref):
    h = _rms(x_ref[...], g_ref[...]).astype(BF16)
    bm = h.shape[0]
    lane = lax.broadcasted_iota(jnp.int32, (bm, LANES), 1)
    lo32 = (lane % HEAD_DIM) < (HEAD_DIM // 2)
    cos = cos_ref[...]
    sin = sin_ref[...]

    def rope(a):
        sw = jnp.where(lo32, pltpu.roll(a, LANES - HEAD_DIM // 2, 1), pltpu.roll(a, HEAD_DIM // 2, 1))
        return a * cos + sw * sin

    rope_cols = D_ATTN + D_KV
    for c0 in range(0, D_IN, _IN_PROJ_CHUNK):
        acc = jnp.dot(h, w_ref[:, c0:c0 + _IN_PROJ_CHUNK], preferred_element_type=F32)
        for c in range(0, _IN_PROJ_CHUNK, LANES):
            a = acc[:, c:c + LANES]
            z_ref[:, c0 + c:c0 + c + LANES] = rope(a) if c0 + c < rope_cols else a


def _resident(block_shape, index_map):
    return pl.BlockSpec(block_shape, index_map, pipeline_mode=pl.Buffered(1))


def _in_proj(x, g_all, w_all_bf, cos_t, sin_t, *, layer, bm):
    m = x.shape[0]
    n_tab = cos_t.shape[0] // bm
    return pl.pallas_call(
        _in_proj_kernel,
        grid=(m // bm,),
        in_specs=[
            pl.BlockSpec((bm, D_MODEL), lambda i: (i, 0)),
            pl.BlockSpec((None, 1, D_MODEL), lambda i: (layer, 0, 0)),
            _resident((None, D_MODEL, D_IN), lambda i: (layer, 0, 0)),
            pl.BlockSpec((bm, LANES), lambda i: (i % n_tab, 0)),
            pl.BlockSpec((bm, LANES), lambda i: (i % n_tab, 0)),
        ],
        out_specs=pl.BlockSpec((bm, D_IN), lambda i: (i, 0)),
        out_shape=jax.ShapeDtypeStruct((m, D_IN), F32),
        compiler_params=_cparams(("parallel",)),
        name="in_proj",
    )(x, g_all, w_all_bf, cos_t, sin_t)


def _attn_prompt_kernel(sink_ref, q_ref, kc_ref, kp_ref, vc_ref, vp_ref, g_ref, *rest, n_sub):
    if len(rest) == 1:
        (o_ref,) = rest
    else:
        wgu_ref, wd_ref, o_ref, wgu_bf_ref, wd_bf_ref = rest
        wgu_bf_ref[...] = wgu_ref[...].astype(BF16)
        wd_bf_ref[...] = wd_ref[...].astype(BF16)
    b = pl.program_id(1)
    L = WINDOW

    lane = lax.broadcasted_iota(jnp.int32, (2 * L, LANES), 1)
    lo = lane < HEAD_DIM
    row = lax.broadcasted_iota(jnp.int32, (2 * L, 1), 0)
    top = row < L

    qi = lax.broadcasted_iota(jnp.int32, (2 * L, 4 * L), 0) % L
    sj = lax.broadcasted_iota(jnp.int32, (2 * L, 4 * L), 1) % (2 * L)
    diff = L + qi - sj
    band = (diff >= 0) & (diff < WINDOW)
    bias_inner = jnp.where(band, 0.0, -jnp.inf).astype(F32)
    bias_first = jnp.where(band & ((sj >= L) | (b > 0)), 0.0, -jnp.inf).astype(F32)

    zeros = jnp.zeros((2 * L, LANES), F32)
    ones_lo = jnp.where(lo, 1.0, 0.0).astype(F32)
    ones_hi = 1.0 - ones_lo

    for sub in range(n_sub):
        rows = slice(sub * L, (sub + 1) * L)
        k_prev = kp_ref[...] if sub == 0 else kc_ref[(sub - 1) * L:sub * L, :]
        v_prev = vp_ref[...] if sub == 0 else vc_ref[(sub - 1) * L:sub * L, :]
        out = _attn_block(sink_ref, q_ref[rows, :], k_prev, kc_ref[rows, :], v_prev, vc_ref[rows, :],
                          bias_first if sub == 0 else bias_inner, lo, top, zeros, ones_lo, ones_hi)
        o_ref[rows, :] = _rms(out, g_ref[...]).astype(o_ref.dtype)


def _attn_block(sink_ref, q, k_prev, k_cur, v_prev, v_cur, bias, lo, top, zeros, ones_lo, ones_hi):
    L = WINDOW
    qb = (q * (HEAD_DIM ** -0.5)).astype(BF16)
    kk = jnp.concatenate([k_prev, k_cur], axis=0)
    vv = jnp.concatenate([v_prev, v_cur], axis=0)
    outs = []
    for kh in range(N_KV_HEADS):
        c0 = LANES * (kh // 2)
        kx = kk[:, c0:c0 + LANES]
        vx = vv[:, c0:c0 + LANES]
        kr = pltpu.roll(kx, HEAD_DIM, 1)
        vr = pltpu.roll(vx, HEAD_DIM, 1)
        if kh % 2 == 0:
            k_lo, k_hi = jnp.where(lo, kx, zeros), jnp.where(lo, zeros, kr)
            v_lo, v_hi = jnp.where(lo, vx, zeros), jnp.where(lo, zeros, vr)
        else:
            k_lo, k_hi = jnp.where(lo, kr, zeros), jnp.where(lo, zeros, kx)
            v_lo, v_hi = jnp.where(lo, vr, zeros), jnp.where(lo, zeros, vx)
        kmat = jnp.concatenate([k_lo, k_hi], axis=0).astype(BF16)
        qs = jnp.concatenate([qb[:, 2 * LANES * kh:2 * LANES * kh + LANES],
                              qb[:, 2 * LANES * kh + LANES:2 * LANES * (kh + 1)]], axis=0)
        s = lax.dot_general(qs, kmat, (((1,), (1,)), ((), ())), preferred_element_type=F32)
        s = s + bias
        sink_lo = jnp.where(top, sink_ref[4 * kh + 0], sink_ref[4 * kh + 2])
        sink_hi = jnp.where(top, sink_ref[4 * kh + 1], sink_ref[4 * kh + 3])
        m_lo = jnp.maximum(jnp.max(s[:, :2 * L], axis=1, keepdims=True), sink_lo)
        m_hi = jnp.maximum(jnp.max(s[:, 2 * L:], axis=1, keepdims=True), sink_hi)
        p = jnp.concatenate([jnp.exp(s[:, :2 * L] - m_lo), jnp.exp(s[:, 2 * L:] - m_hi)], axis=1).astype(BF16)
        vmat = jnp.concatenate([jnp.concatenate([v_lo, ones_lo], axis=1),
                                jnp.concatenate([v_hi, ones_hi], axis=1)], axis=0).astype(BF16)
        oe = jnp.dot(p, vmat, preferred_element_type=F32)
        denom = oe[:, LANES:] + jnp.where(lo, jnp.exp(sink_lo - m_lo), jnp.exp(sink_hi - m_hi))
        o = oe[:, :LANES] / denom
        outs.append(o[:L])
        outs.append(o[L:])
    return jnp.concatenate(outs, axis=1)


_ATTN_SUB_BLOCKS = 2


_CAST_SLAB_BYTES = 8 * 1024 * 1024


def _attn_steps(n_seq, seq):
    return n_seq * (seq // (WINDOW * _ATTN_SUB_BLOCKS))


def _can_cast_in_attn(n_seq, seq):
    steps = _attn_steps(n_seq, seq)
    bf16_rows = 2 * SUBLANES
    return (D_MODEL % steps == 0 and D_FF % steps == 0
            and (D_MODEL // steps) % bf16_rows == 0 and (D_FF // steps) % bf16_rows == 0
            and (D_MODEL // steps) * 2 * D_FF * 4 <= _CAST_SLAB_BYTES)


def _attn_prompt(z, sinks, g_attn, *, n_seq, seq, cast=None):
    L = WINDOW
    n_sub = _ATTN_SUB_BLOCKS
    nb = seq // (L * n_sub)
    kcol = D_ATTN // D_KV
    vcol = kcol + 1
    cur = lambda col: (lambda n, b: (n * nb + b, col))
    prev = lambda col: (lambda n, b: (jnp.maximum((n * nb + b) * n_sub - 1, 0), col))
    in_specs = [
        pl.BlockSpec(memory_space=pltpu.SMEM),
        pl.BlockSpec((L * n_sub, D_ATTN), cur(0)),
        pl.BlockSpec((L * n_sub, D_KV), cur(kcol)),
        pl.BlockSpec((L, D_KV), prev(kcol)),
        pl.BlockSpec((L * n_sub, D_KV), cur(vcol)),
        pl.BlockSpec((L, D_KV), prev(vcol)),
        pl.BlockSpec((1, D_ATTN), lambda n, b: (0, 0)),
    ]
    out_specs = [pl.BlockSpec((L * n_sub, D_ATTN), cur(0))]
    out_shape = [jax.ShapeDtypeStruct((n_seq * seq, D_ATTN), BF16)]
    args = [sinks, z, z, z, z, z, g_attn]
    if cast is not None:
        w_gu_all, w_d_all, layer = cast
        steps = n_seq * nb
        r_gu, r_d = D_MODEL // steps, D_FF // steps
        in_specs += [pl.BlockSpec((None, r_gu, 2 * D_FF), lambda n, b: (layer, n * nb + b, 0)),
                     pl.BlockSpec((None, r_d, D_MODEL), lambda n, b: (layer, n * nb + b, 0))]
        out_specs += [pl.BlockSpec((None, r_gu, 2 * D_FF), lambda n, b: (0, n * nb + b, 0)),
                      pl.BlockSpec((None, r_d, D_MODEL), lambda n, b: (0, n * nb + b, 0))]
        out_shape += [jax.ShapeDtypeStruct((1, D_MODEL, 2 * D_FF), BF16),
                      jax.ShapeDtypeStruct((1, D_FF, D_MODEL), BF16)]
        args += [w_gu_all, w_d_all]
    outs = pl.pallas_call(
        functools.partial(_attn_prompt_kernel, n_sub=n_sub),
        grid=(n_seq, nb),
        in_specs=in_specs,
        out_specs=out_specs,
        out_shape=out_shape,
        compiler_params=_cparams(("parallel", "arbitrary")),
        name="attn_prompt",
    )(*args)
    return outs[0] if cast is None else outs


def _lru_gates(xc, wg_ref, ba, bi, lam):
    g = jnp.dot(xc.astype(BF16), wg_ref[...], preferred_element_type=F32)
    r = jax.nn.sigmoid(g[:, :D_LRU] + ba)
    gi = jax.nn.sigmoid(g[:, D_LRU:] + bi)
    nl = -lam
    softplus = jnp.maximum(nl, 0.0) + jnp.log1p(jnp.exp(-jnp.abs(nl)))
    log_a = -LRU_C * r * softplus
    a = jnp.exp(log_a)
    th = jnp.tanh(log_a)
    m2 = (-2.0 * th) / (1.0 - th)
    mult = jnp.where(m2 > 0.0, m2 * lax.rsqrt(m2), 0.0)
    return a, mult, gi


def _shift_rows(u, prev8, k):
    r = pltpu.roll(u, k, 0)
    pr = pltpu.roll(prev8, k, 0)
    row8 = lax.broadcasted_iota(jnp.int32, prev8.shape, 0)
    head = jnp.where(row8 < k, pr, r[:SUBLANES])
    return jnp.concatenate([head, r[SUBLANES:]], axis=0)


def _chunk_scan(a, b):
    n = a.shape[0]
    row = lax.broadcasted_iota(jnp.int32, a.shape, 0)
    d = 1
    while d < n:
        if d < SUBLANES:
            keep = row >= d
            b = jnp.where(keep, b + a * pltpu.roll(b, d, 0), b)
            a = jnp.where(keep, a * pltpu.roll(a, d, 0), a)
        else:
            b = jnp.concatenate([b[:d], b[d:] + a[d:] * b[:n - d]], axis=0)
            a = jnp.concatenate([a[:d], a[d:] * a[:n - d]], axis=0)
        d *= 2
    return a, b


def _scan_pitch(ln):
    assert ln % SUBLANES == 0
    return ln if ln % (2 * SUBLANES) == SUBLANES else ln + SUBLANES


def _strided_scan(a, b, h_prev, a_scr, b_scr):
    tc, ch = a.shape
    ln = tc // SUBLANES
    pitch = _scan_pitch(ln)
    nslab = ch // LANES
    for s in range(SUBLANES):
        for c in range(nslab):
            a_scr[c, pitch * s:pitch * s + ln, :] = a[ln * s:ln * (s + 1), c * LANES:(c + 1) * LANES]
            b_scr[c, pitch * s:pitch * s + ln, :] = b[ln * s:ln * (s + 1), c * LANES:(c + 1) * LANES]
    row8 = lax.broadcasted_iota(jnp.int32, (SUBLANES, LANES), 0)
    for c in range(nslab):
        h = jnp.zeros((SUBLANES, LANES), F32)
        acum = jnp.ones((SUBLANES, LANES), F32)
        for j in range(ln):
            idx = pl.ds(j, SUBLANES, stride=pitch)
            at = a_scr[c, idx, :]
            h = at * h + b_scr[c, idx, :]
            acum = at * acum
            b_scr[c, idx, :] = h
            a_scr[c, idx, :] = acum
        a_tot, b_tot = _chunk_scan(acum, h)
        hp = h_prev[:, c * LANES:(c + 1) * LANES]
        cin = jnp.where(row8 == 0, hp, pltpu.roll(b_tot + a_tot * hp, 1, 0))
        for j in range(ln):
            idx = pl.ds(j, SUBLANES, stride=pitch)
            b_scr[c, idx, :] = b_scr[c, idx, :] + a_scr[c, idx, :] * cin
    return jnp.concatenate(
        [jnp.concatenate([b_scr[c, pitch * s:pitch * s + ln, :] for c in range(nslab)], axis=1)
         for s in range(SUBLANES)], axis=0)


def _lru_sc_kernel(ux_ref, gate_ref, ub_ref, uc_ref, uh_ref, cw_ref, cb_ref, wg_ref, ba_ref, bi_ref,
                   lam_ref, scw_ref, glru_ref, gsc_ref,
                   o_ref, h8_ref, x8_ref, g8_ref, cx_scr, cg_scr, ch_scr, sa_scr, sb_scr, *, tc):
    t = pl.program_id(1)

    @pl.when(t == 0)
    def _():
        cx_scr[...] = jnp.zeros_like(cx_scr)
        cg_scr[...] = jnp.zeros_like(cg_scr)
        ch_scr[...] = jnp.zeros_like(ch_scr)

    ux = ux_ref[...]
    px = cx_scr[...]
    xc = _shift_rows(ux, px, 3) * cw_ref[0:1, :]
    xc = xc + _shift_rows(ux, px, 2) * cw_ref[1:2, :]
    xc = xc + _shift_rows(ux, px, 1) * cw_ref[2:3, :]
    xc = xc + ux * cw_ref[3:4, :]
    xc = xc + cb_ref[...]

    a, mult, gi = _lru_gates(xc, wg_ref, ba_ref[...], bi_ref[...], lam_ref[...])
    pos = t * tc + lax.broadcasted_iota(jnp.int32, (tc, 1), 0)
    mult = jnp.where(pos == 0, 1.0, mult)
    h = _strided_scan(a, mult * gi * xc, ch_scr[SUBLANES - 1:SUBLANES, :], sa_scr, sb_scr)
    o_lru = h * jax.nn.gelu(gate_ref[...], approximate=True)

    gch = uc_ref[...] * uh_ref[...]
    pg = cg_scr[...]
    y = _shift_rows(gch, pg, 2) * scw_ref[0:1, :]
    y = y + _shift_rows(gch, pg, 1) * scw_ref[1:2, :]
    y = y + gch * scw_ref[2:3, :]
    o_sc = ub_ref[...] * y

    o_ref[:, :D_LRU] = _rms(o_lru, glru_ref[...]).astype(o_ref.dtype)
    o_ref[:, D_LRU:] = _rms(o_sc, gsc_ref[...]).astype(o_ref.dtype)

    cx_scr[...] = ux[tc - SUBLANES:]
    cg_scr[...] = gch[tc - SUBLANES:]
    ch_scr[...] = h[tc - SUBLANES:]
    h8_ref[0] = h[tc - SUBLANES:]
    x8_ref[0] = ux[tc - SUBLANES:]
    g8_ref[0] = gch[tc - SUBLANES:]


def _lru_sc_prompt(z, lw, *, n_seq, seq, tc=256):
    nt = seq // tc
    zcol = lambda col: pl.BlockSpec((tc, D_LRU), lambda n, t: (n * nt + t, col))
    const = lambda shape: pl.BlockSpec(shape, lambda n, t: (0,) * len(shape))
    st = pl.BlockSpec((1, SUBLANES, D_LRU), lambda n, t: (n, 0, 0))
    st_shape = jax.ShapeDtypeStruct((n_seq, SUBLANES, D_LRU), F32)
    return pl.pallas_call(
        functools.partial(_lru_sc_kernel, tc=tc),
        grid=(n_seq, nt),
        in_specs=[zcol(_COL_UX), zcol(_COL_GATE), zcol(_COL_B), zcol(_COL_C), zcol(_COL_H),
                  const((LRU_CONV_W, D_LRU)), const((1, D_LRU)), const((D_LRU, 2 * D_LRU)),
                  const((1, D_LRU)), const((1, D_LRU)), const((1, D_LRU)),
                  const((SC_CONV_W, D_SC)), const((1, D_LRU)), const((1, D_SC))],
        out_specs=[pl.BlockSpec((tc, D_LRU + D_SC), lambda n, t: (n * nt + t, 0)), st, st, st],
        out_shape=[jax.ShapeDtypeStruct((n_seq * seq, D_LRU + D_SC), BF16), st_shape, st_shape, st_shape],
        scratch_shapes=[pltpu.VMEM((SUBLANES, D_LRU), F32)] * 3
        + [pltpu.VMEM((D_LRU // LANES, SUBLANES * _scan_pitch(tc // SUBLANES), LANES), F32)] * 2,
        compiler_params=_cparams(("parallel", "arbitrary")),
        name="lru_sc_prompt",
    )(z, z, z, z, z, lw["conv_w"], lw["conv_b"], lw["w_gates"], lw["b_a"], lw["b_i"], lw["lam"],
      lw["sc_w"], lw["g_lru"], lw["g_sc"])


def _decode_kernel(q_ref, knt_ref, kc_ref, vc_ref, sel_ref, sink_ref, gat_ref,
                   ux_ref, gate_ref, ub_ref, uc_ref, uh_ref, h0_ref, cbuf_ref, sbuf_ref,
                   cw_ref, cb_ref, wg_ref, ba_ref, bi_ref, lam_ref, scw_ref, glru_ref, gsc_ref,
                   oat_ref, ors_ref, ko_ref, vo_ref, hn_ref, cn_ref, sn_ref, *, nbk):
    wb = kc_ref.shape[2]
    hrow = lax.broadcasted_iota(jnp.int32, (N_HEADS, D_ATTN), 0)
    hcol = lax.broadcasted_iota(jnp.int32, (N_HEADS, D_ATTN), 1) // HEAD_DIM
    own = (hrow == hcol)
    qexp = jnp.concatenate(
        [jnp.where(own, jnp.broadcast_to(q_ref[i:i + 1, :] * (HEAD_DIM ** -0.5), (N_HEADS, D_ATTN)), 0.0)
         for i in range(nbk)], axis=0)
    qrow = jnp.dot(qexp.astype(BF16), sel_ref[...], preferred_element_type=F32)
    rows = nbk * N_HEADS
    grow = (lax.broadcasted_iota(jnp.int32, (rows, D_KV), 0) % N_HEADS) // N_GROUP
    gcol = lax.broadcasted_iota(jnp.int32, (rows, D_KV), 1) // HEAD_DIM
    kvmask = (grow == gcol)
    qm = jnp.where(kvmask, qrow, 0.0).astype(BF16)
    sink = jnp.concatenate([sink_ref[...][:, 0:1]] * nbk, axis=0)
    gat = jnp.concatenate([gat_ref[...]] * nbk, axis=0)
    lanek = lax.broadcasted_iota(jnp.int32, (D_KV, wb), 1)
    newest = lanek == wb - 1
    for i in range(nbk):
        ko_ref[i] = jnp.where(newest, jnp.broadcast_to(knt_ref[:D_KV, i:i + 1], (D_KV, wb)),
                              pltpu.roll(kc_ref[i], wb - 1, 1))
        vo_ref[i] = jnp.where(newest, jnp.broadcast_to(knt_ref[D_KV:, i:i + 1], (D_KV, wb)),
                              pltpu.roll(vc_ref[i], wb - 1, 1))
    s = jnp.concatenate(
        [jnp.dot(qm[i * N_HEADS:(i + 1) * N_HEADS], ko_ref[i].astype(BF16), preferred_element_type=F32)
         for i in range(nbk)], axis=0)
    m = jnp.maximum(jnp.max(s, axis=1, keepdims=True), sink)
    p = jnp.exp(s - m)
    p = (p / (jnp.sum(p, axis=1, keepdims=True) + jnp.exp(sink - m))).astype(BF16)
    of = jnp.concatenate(
        [lax.dot_general(p[i * N_HEADS:(i + 1) * N_HEADS], vo_ref[i].astype(BF16), (((1,), (1,)), ((), ())),
                         preferred_element_type=F32) for i in range(nbk)], axis=0)
    of = jnp.where(kvmask, of, 0.0)
    t = of[:, :LANES] + of[:, LANES:]
    o = t + pltpu.roll(t, HEAD_DIM, 1)
    rs = jnp.sum(o * o, axis=1, keepdims=True)
    for i in range(nbk):
        sl = slice(i * N_HEADS, (i + 1) * N_HEADS)
        ms = jnp.sum(rs[sl], axis=0, keepdims=True) * (0.5 / D_ATTN)
        oat_ref[i] = (o[sl] * lax.rsqrt(ms + RMS_EPS) * gat[sl]).astype(oat_ref.dtype)

    ux = ux_ref[...]
    xc = cbuf_ref[0] * cw_ref[0:1, :]
    xc = xc + cbuf_ref[1] * cw_ref[1:2, :]
    xc = xc + cbuf_ref[2] * cw_ref[2:3, :]
    xc = xc + ux * cw_ref[3:4, :]
    xc = xc + cb_ref[...]
    a, mult, gi = _lru_gates(xc, wg_ref, ba_ref[...], bi_ref[...], lam_ref[...])
    h = a * h0_ref[...] + mult * gi * xc
    o_lru = h * jax.nn.gelu(gate_ref[...], approximate=True)
    hn_ref[...] = h
    cn_ref[0] = cbuf_ref[1]
    cn_ref[1] = cbuf_ref[2]
    cn_ref[2] = ux
    gch = uc_ref[...] * uh_ref[...]
    y = sbuf_ref[:, 0, :] * scw_ref[0:1, :]
    y = y + sbuf_ref[:, 1, :] * scw_ref[1:2, :]
    y = y + gch * scw_ref[2:3, :]
    o_sc = ub_ref[...] * y
    sn_ref[:, 0, :] = sbuf_ref[:, 1, :]
    sn_ref[:, 1, :] = gch
    ors_ref[:, :D_LRU] = _rms(o_lru, glru_ref[...]).astype(ors_ref.dtype)
    ors_ref[:, D_LRU:] = _rms(o_sc, gsc_ref[...]).astype(ors_ref.dtype)


def _decode_mix(z, knt, kct_all, vct_all, h0_all, cbuf_all, sbuf_all, lw, *, layer, nbk=16):
    ns = z.shape[0]
    wb = kct_all.shape[3]
    z512 = lambda col: pl.BlockSpec((nbk, D_LRU), lambda i: (i, col))
    const = lambda shape: pl.BlockSpec(shape, lambda i: (0,) * len(shape))
    cache_in = pl.BlockSpec((None, nbk, D_KV, wb), lambda i: (layer, i, 0, 0))
    cache_out = pl.BlockSpec((nbk, D_KV, wb), lambda i: (i, 0, 0))
    outs = pl.pallas_call(
        functools.partial(_decode_kernel, nbk=nbk),
        grid=(ns // nbk,),
        in_specs=[pl.BlockSpec((nbk, D_ATTN), lambda i: (i, 0)),
                  pl.BlockSpec((None, 2 * D_KV, nbk), lambda i: (i, 0, 0)),
                  cache_in, cache_in,
                  const((D_ATTN, D_KV)), const((N_HEADS, LANES)), const((N_HEADS, LANES)),
                  z512(_COL_UX), z512(_COL_GATE), z512(_COL_B), z512(_COL_C), z512(_COL_H),
                  pl.BlockSpec((None, nbk, D_LRU), lambda i: (layer, i, 0)),
                  pl.BlockSpec((None, LRU_CONV_W - 1, nbk, D_LRU), lambda i: (layer, 0, i, 0)),
                  pl.BlockSpec((None, nbk, SC_CONV_W - 1, D_SC), lambda i: (layer, i, 0, 0)),
                  const((LRU_CONV_W, D_LRU)), const((1, D_LRU)), const((D_LRU, 2 * D_LRU)),
                  const((1, D_LRU)), const((1, D_LRU)), const((1, D_LRU)),
                  const((SC_CONV_W, D_SC)), const((1, D_LRU)), const((1, D_SC))],
        out_specs=[pl.BlockSpec((nbk, N_HEADS, LANES), lambda i: (i, 0, 0)),
                   pl.BlockSpec((nbk, D_LRU + D_SC), lambda i: (i, 0)),
                   cache_out, cache_out,
                   pl.BlockSpec((nbk, D_LRU), lambda i: (i, 0)),
                   pl.BlockSpec((LRU_CONV_W - 1, nbk, D_LRU), lambda i: (0, i, 0)),
                   pl.BlockSpec((nbk, SC_CONV_W - 1, D_SC), lambda i: (i, 0, 0))],
        out_shape=[jax.ShapeDtypeStruct((ns, N_HEADS, LANES), BF16),
                   jax.ShapeDtypeStruct((ns, D_LRU + D_SC), BF16),
                   jax.ShapeDtypeStruct((ns, D_KV, wb), F32), jax.ShapeDtypeStruct((ns, D_KV, wb), F32),
                   jax.ShapeDtypeStruct((ns, D_LRU), F32),
                   jax.ShapeDtypeStruct((LRU_CONV_W - 1, ns, D_LRU), F32),
                   jax.ShapeDtypeStruct((ns, SC_CONV_W - 1, D_SC), F32)],
        compiler_params=_cparams(("parallel",)),
        name="decode_mix",
    )(z, knt, kct_all, vct_all, lw["sel"], lw["sink_tab"], lw["g_attn_tab"],
      z, z, z, z, z, h0_all, cbuf_all, sbuf_all,
      lw["conv_w"], lw["conv_b"], lw["w_gates"], lw["b_a"], lw["b_i"], lw["lam"],
      lw["sc_w"], lw["g_lru"], lw["g_sc"])
    return outs


_OUT_PROJ_CHUNK = 512


def _out_proj_kernel(ma_ref, mb_ref, x_ref, w_ref, g_ref, x1_ref, hf_ref):
    ma = ma_ref[...]
    mb = mb_ref[...]
    ssq = None
    for c0 in range(0, D_MODEL, _OUT_PROJ_CHUNK):
        cs = slice(c0, c0 + _OUT_PROJ_CHUNK)
        acc = jnp.dot(ma, w_ref[:D_ATTN, cs], preferred_element_type=F32)
        acc = acc + jnp.dot(mb, w_ref[D_ATTN:, cs], preferred_element_type=F32)
        x1 = x_ref[:, cs] + acc
        x1_ref[:, cs] = x1
        part = jnp.sum(x1 * x1, axis=-1, keepdims=True)
        ssq = part if ssq is None else ssq + part
    scale = lax.rsqrt(ssq * (1.0 / D_MODEL) + RMS_EPS)
    for c0 in range(0, D_MODEL, _OUT_PROJ_CHUNK):
        cs = slice(c0, c0 + _OUT_PROJ_CHUNK)
        hf_ref[:, cs] = (x1_ref[:, cs] * scale * g_ref[:, cs]).astype(hf_ref.dtype)


def _out_proj(ma, mb, x, w_all_bf, g_all, *, layer, bm):
    m = x.shape[0]
    return pl.pallas_call(
        _out_proj_kernel,
        grid=(m // bm,),
        in_specs=[pl.BlockSpec((bm, D_ATTN), lambda i: (i, 0)),
                  pl.BlockSpec((bm, D_LRU + D_SC), lambda i: (i, 0)),
                  pl.BlockSpec((bm, D_MODEL), lambda i: (i, 0)),
                  _resident((None, D_MODEL, D_MODEL), lambda i: (layer, 0, 0)),
                  pl.BlockSpec((None, 1, D_MODEL), lambda i: (layer, 0, 0))],
        out_specs=[pl.BlockSpec((bm, D_MODEL), lambda i: (i, 0)),
                   pl.BlockSpec((bm, D_MODEL), lambda i: (i, 0))],
        out_shape=[jax.ShapeDtypeStruct((m, D_MODEL), F32), jax.ShapeDtypeStruct((m, D_MODEL), BF16)],
        compiler_params=_cparams(("parallel",)),
        name="out_proj",
    )(ma, mb, x, w_all_bf, g_all)


_FFN_DOWN_CHUNK = 512
_FFN_X1_CHUNK = 256


def _ffn_kernel(hf_ref, x1_ref, wg_ref, wu_ref, wd_ref, gfin_ref, o_ref, *, final_norm):
    f = pl.program_id(1)
    n_x1 = D_MODEL // _FFN_X1_CHUNK

    @pl.when(f == 0)
    def _():
        o_ref[...] = jnp.zeros_like(o_ref)

    hf = hf_ref[...]
    gate = jnp.dot(hf, wg_ref[...], preferred_element_type=F32)
    up = jnp.dot(hf, wu_ref[...], preferred_element_type=F32)
    hid = (gate * jax.nn.sigmoid(gate) * up).astype(BF16)
    for c0 in range(0, D_MODEL, _FFN_DOWN_CHUNK):
        cs = slice(c0, c0 + _FFN_DOWN_CHUNK)
        o_ref[:, cs] += jnp.dot(hid, wd_ref[:, cs], preferred_element_type=F32)

    for c in range(n_x1):
        @pl.when(f == c)
        def _(c=c):
            cs = slice(c * _FFN_X1_CHUNK, (c + 1) * _FFN_X1_CHUNK)
            o_ref[:, cs] += x1_ref[...]

    if final_norm:
        @pl.when(f == pl.num_programs(1) - 1)
        def _():
            o_ref[...] = _rms(o_ref[...], gfin_ref[...])


def _ffn(hf, x1, w_gu_bf, w_d_bf, g_final, *, layer, bm, tf=512, final_norm):
    m = hf.shape[0]
    nf = D_FF // tf
    n_x1 = D_MODEL // _FFN_X1_CHUNK
    assert nf >= n_x1
    return pl.pallas_call(
        functools.partial(_ffn_kernel, final_norm=final_norm),
        grid=(m // bm, nf),
        in_specs=[pl.BlockSpec((bm, D_MODEL), lambda i, f: (i, 0)),
                  pl.BlockSpec((bm, _FFN_X1_CHUNK), lambda i, f: (i, jnp.minimum(f, n_x1 - 1))),
                  pl.BlockSpec((None, D_MODEL, tf), lambda i, f: (layer, 0, f)),
                  pl.BlockSpec((None, D_MODEL, tf), lambda i, f: (layer, 0, nf + f)),
                  pl.BlockSpec((None, tf, D_MODEL), lambda i, f: (layer, f, 0)),
                  pl.BlockSpec((1, D_MODEL), lambda i, f: (0, 0))],
        out_specs=pl.BlockSpec((bm, D_MODEL), lambda i, f: (i, 0)),
        out_shape=jax.ShapeDtypeStruct((m, D_MODEL), F32),
        compiler_params=_cparams(("parallel", "arbitrary")),
        name="ffn",
    )(hf, x1, w_gu_bf, w_gu_bf, w_d_bf, g_final)


def _rope_tables(pos):
    half = HEAD_DIM // 2
    inv = ROPE_THETA ** (-jnp.arange(half, dtype=F32) / half)
    ang = pos.astype(F32)[:, None] * inv[None, :]
    cos, sin = jnp.cos(ang), jnp.sin(ang)
    cos_t = jnp.tile(cos, (1, LANES // half))
    sin_t = jnp.tile(jnp.concatenate([-sin, sin], axis=1), (1, LANES // HEAD_DIM))
    return cos_t, sin_t


def _block_diag(w):
    hh, blk, _ = w.shape
    eye = jnp.eye(hh, dtype=w.dtype)
    return (eye[:, None, :, None] * w[:, :, None, :]).reshape(hh * blk, hh * blk)


def kernel(x_prompt, x_sample, state_lru_h, state_lru_conv, cache_swa_k, cache_swa_v, state_sconv,
           norm_mix, w_in, norm_grp, w_out, lru_conv_w, lru_conv_b, lru_w_a, lru_b_a, lru_w_i, lru_b_i,
           lru_lambda, sc_conv_w, attn_sinks, norm_ffn, ffn_w_gu, ffn_w_down, norm_final):
    n_p, t_p, _ = x_prompt.shape
    n_s, t_s, _ = x_sample.shape
    depth = w_in.shape[0]
    wb = cache_swa_k.shape[2]
    assert t_s == 1 and wb == WINDOW and t_p % 256 == 0 and n_s % 16 == 0

    bm_p = 512 if (n_p * t_p) % 512 == 0 and t_p % 512 == 0 else 256
    bm_ffn = 1024 if (n_p * t_p) % 1024 == 0 else bm_p
    bm_s = n_s

    cos_p, sin_p = _rope_tables(jnp.arange(t_p, dtype=jnp.int32))
    cos_s, sin_s = _rope_tables(jnp.full((bm_s,), PAST_LEN, dtype=jnp.int32))

    sel = (jnp.arange(D_ATTN)[:, None] % HEAD_DIM == jnp.arange(D_KV)[None, :] % HEAD_DIM).astype(BF16)

    xp = x_prompt.reshape(n_p * t_p, D_MODEL)
    xs = x_sample.reshape(n_s, D_MODEL)
    row = lambda v: v.reshape(1, -1)
    p_states, s_states = [], []
    w_in_bf = w_in.astype(BF16)
    w_out_bf = w_out.astype(BF16)
    cast_in_attn = _can_cast_in_attn(n_p, t_p)
    if not cast_in_attn:
        w_gu_bf = ffn_w_gu.astype(BF16)
        w_d_bf = ffn_w_down.astype(BF16)
    nbk = 16
    kct_all = cache_swa_k.transpose(0, 1, 3, 4, 2).reshape(depth, n_s, D_KV, wb)
    vct_all = cache_swa_v.transpose(0, 1, 3, 4, 2).reshape(depth, n_s, D_KV, wb)
    cbuf_all = state_lru_conv.transpose(0, 2, 1, 3)
    g_mix = norm_mix.reshape(depth, 1, D_MODEL)
    g_ffn = norm_ffn.reshape(depth, 1, D_MODEL)
    for l in range(depth):
        g_attn, g_lru, g_sc = (norm_grp[l, :D_ATTN], norm_grp[l, D_ATTN:D_ATTN + D_LRU],
                               norm_grp[l, D_ATTN + D_LRU:])
        lw = dict(
            conv_w=lru_conv_w[l], conv_b=row(lru_conv_b[l]),
            w_gates=jnp.concatenate([_block_diag(lru_w_a[l]), _block_diag(lru_w_i[l])], axis=1).astype(BF16),
            b_a=row(lru_b_a[l]), b_i=row(lru_b_i[l]), lam=row(lru_lambda[l]),
            sc_w=sc_conv_w[l], g_lru=row(g_lru), g_sc=row(g_sc),
            sel=sel,
            sink_tab=jnp.broadcast_to(attn_sinks[l][:, None], (N_HEADS, LANES)),
            g_attn_tab=jnp.tile(g_attn.reshape(N_HEADS, HEAD_DIM), (1, LANES // HEAD_DIM)),
        )
        last = l == depth - 1

        z = _in_proj(xp, g_mix, w_in_bf, cos_p, sin_p, layer=l, bm=bm_p)
        if cast_in_attn:
            ma, w_gu_l, w_d_l = _attn_prompt(z, attn_sinks[l], row(g_attn), n_seq=n_p, seq=t_p,
                                             cast=(ffn_w_gu, ffn_w_down, l))
            l_ffn = 0
        else:
            ma = _attn_prompt(z, attn_sinks[l], row(g_attn), n_seq=n_p, seq=t_p)
            w_gu_l, w_d_l, l_ffn = w_gu_bf, w_d_bf, l
        mb, h8, x8, g8 = _lru_sc_prompt(z, lw, n_seq=n_p, seq=t_p)
        x1, hf = _out_proj(ma, mb, xp, w_out_bf, g_ffn, layer=l, bm=bm_p)
        xp_new = _ffn(hf, x1, w_gu_l, w_d_l, row(norm_final), layer=l_ffn, bm=bm_ffn, final_norm=last)
        z3 = z.reshape(n_p, t_p, D_IN)
        wbp = min(WINDOW, t_p)
        p_states.append((
            h8[:, SUBLANES - 1],
            x8[:, SUBLANES - (LRU_CONV_W - 1):],
            z3[:, t_p - wbp:, D_ATTN:D_ATTN + D_KV].reshape(n_p, wbp, N_KV_HEADS, HEAD_DIM),
            z3[:, t_p - wbp:, D_ATTN + D_KV:D_ATTN + 2 * D_KV].reshape(n_p, wbp, N_KV_HEADS, HEAD_DIM),
            g8[:, SUBLANES - (SC_CONV_W - 1):],
        ))
        xp = xp_new

        zs = _in_proj(xs, g_mix, w_in_bf, cos_s, sin_s, layer=l, bm=bm_s)
        knt = zs[:, D_ATTN:D_ATTN + 2 * D_KV].reshape(n_s // nbk, nbk, 2 * D_KV).transpose(0, 2, 1)
        oat, ors, k_new, v_new, h_new, c_new, s_new = _decode_mix(
            zs, knt, kct_all, vct_all, state_lru_h, cbuf_all, state_sconv, lw, layer=l, nbk=nbk)
        k_new = k_new.reshape(n_s, N_KV_HEADS, HEAD_DIM, wb).transpose(0, 3, 1, 2)
        v_new = v_new.reshape(n_s, N_KV_HEADS, HEAD_DIM, wb).transpose(0, 3, 1, 2)
        c_new = c_new.transpose(1, 0, 2)
        mas = oat[:, :, :HEAD_DIM].reshape(n_s, D_ATTN)
        x1s, hfs = _out_proj(mas, ors, xs, w_out_bf, g_ffn, layer=l, bm=bm_s)
        xs = _ffn(hfs, x1s, w_gu_l, w_d_l, row(norm_final), layer=l_ffn, bm=bm_s, final_norm=last)
        s_states.append((h_new, c_new, k_new, v_new, s_new))

    y_prompt = xp.reshape(n_p, t_p, D_MODEL)
    y_sample = xs.reshape(n_s, t_s, D_MODEL)
    stack = lambda states, k: jnp.stack([st[k] for st in states])
    return (y_prompt, y_sample,
            stack(p_states, 0), stack(p_states, 1), stack(p_states, 2), stack(p_states, 3), stack(p_states, 4),
            stack(s_states, 0), stack(s_states, 1), stack(s_states, 2), stack(s_states, 3), stack(s_states, 4))
```

```python
import functools

import jax
import jax.numpy as jnp
from jax import lax
from jax.experimental import pallas as pl
from jax.experimental.pallas import tpu as pltpu

F32 = jnp.float32
BF16 = jnp.bfloat16

D_MODEL = 2048
D_ATTN = 1024
D_LRU = 512
D_SC = 512
HEAD_DIM = 64
N_HEADS = 16
N_KV_HEADS = 4
N_GROUP = 4
D_KV = 256
WINDOW = 128
ROPE_THETA = 10000.0
N_LRU_HEADS = 8
LRU_BLK = 64
LRU_CONV_W = 4
LRU_C = 8.0
SC_CONV_W = 3
D_FF = 5632
D_IN = 4096
RMS_EPS = 1e-6
PAST_LEN = 8192

LANES = 128
SUBLANES = 8
VMEM_LIMIT_BYTES = 56 * 1024 * 1024

_COL_UX, _COL_GATE, _COL_B, _COL_C, _COL_H = 3, 4, 5, 6, 7


def _cparams(sem):
    return pltpu.CompilerParams(dimension_semantics=sem, vmem_limit_bytes=VMEM_LIMIT_BYTES)


def _rms(x, g):
    return x * lax.rsqrt(jnp.mean(x * x, axis=-1, keepdims=True) + RMS_EPS) * g


_IN_PROJ_CHUNK = 512


def _in_proj_kernel(x_ref, g_ref, w_ref, cos_ref, sin_ref, z_ref):
    h = _rms(x_ref[...], g_ref[...]).astype(BF16)
    bm = h.shape[0]
    lane = lax.broadcasted_iota(jnp.int32, (bm, LANES), 1)
    lo32 = (lane % HEAD_DIM) < (HEAD_DIM // 2)
    cos = cos_ref[...]
    sin = sin_ref[...]

    def rope(a):
        sw = jnp.where(lo32, pltpu.roll(a, LANES - HEAD_DIM // 2, 1), pltpu.roll(a, HEAD_DIM // 2, 1))
        return a * cos + sw * sin

    rope_cols = D_ATTN + D_KV
    for c0 in range(0, D_IN, _IN_PROJ_CHUNK):
        acc = jnp.dot(h, w_ref[:, c0:c0 + _IN_PROJ_CHUNK], preferred_element_type=F32)
        for c in range(0, _IN_PROJ_CHUNK, LANES):
            a = acc[:, c:c + LANES]
            z_ref[:, c0 + c:c0 + c + LANES] = rope(a) if c0 + c < rope_cols else a


def _resident(block_shape, index_map):
    return pl.BlockSpec(block_shape, index_map, pipeline_mode=pl.Buffered(1))


def _in_proj(x, g_all, w_all_bf, cos_t, sin_t, *, layer, w_layer, bm):
    m = x.shape[0]
    n_tab = cos_t.shape[0] // bm
    return pl.pallas_call(
        _in_proj_kernel,
        grid=(m // bm,),
        in_specs=[
            pl.BlockSpec((bm, D_MODEL), lambda i: (i, 0)),
            pl.BlockSpec((None, 1, D_MODEL), lambda i: (layer, 0, 0)),
            _resident((None, D_MODEL, D_IN), lambda i: (w_layer, 0, 0)),
            pl.BlockSpec((bm, LANES), lambda i: (i % n_tab, 0)),
            pl.BlockSpec((bm, LANES), lambda i: (i % n_tab, 0)),
        ],
        out_specs=pl.BlockSpec((bm, D_IN), lambda i: (i, 0)),
        out_shape=jax.ShapeDtypeStruct((m, D_IN), F32),
        compiler_params=_cparams(("parallel",)),
        name="in_proj",
    )(x, g_all, w_all_bf, cos_t, sin_t)


def _cast_specs(w_all, layer, steps, step_index):
    _, k, n = w_all.shape
    r = k // steps
    assert r * steps == k and r % (2 * SUBLANES) == 0
    return (pl.BlockSpec((None, r, n), lambda *ids: (layer, step_index(*ids), 0)),
            pl.BlockSpec((None, r, n), lambda *ids: (0, step_index(*ids), 0)),
            jax.ShapeDtypeStruct((1, k, n), BF16))


def _attn_prompt_kernel(sink_ref, q_ref, kc_ref, kp_ref, vc_ref, vp_ref, g_ref, *rest, n_sub, n_cast):
    o_ref = rest[n_cast]
    for src, dst in zip(rest[:n_cast], rest[n_cast + 1:]):
        dst[...] = src[...].astype(BF16)
    b = pl.program_id(1)
    L = WINDOW

    lane = lax.broadcasted_iota(jnp.int32, (2 * L, LANES), 1)
    lo = lane < HEAD_DIM
    row = lax.broadcasted_iota(jnp.int32, (2 * L, 1), 0)
    top = row < L

    qi = lax.broadcasted_iota(jnp.int32, (2 * L, 4 * L), 0) % L
    sj = lax.broadcasted_iota(jnp.int32, (2 * L, 4 * L), 1) % (2 * L)
    diff = L + qi - sj
    band = (diff >= 0) & (diff < WINDOW)
    bias_inner = jnp.where(band, 0.0, -jnp.inf).astype(F32)
    bias_first = jnp.where(band & ((sj >= L) | (b > 0)), 0.0, -jnp.inf).astype(F32)

    zeros = jnp.zeros((2 * L, LANES), F32)
    ones_lo = jnp.where(lo, 1.0, 0.0).astype(F32)
    ones_hi = 1.0 - ones_lo

    for sub in range(n_sub):
        rows = slice(sub * L, (sub + 1) * L)
        k_prev = kp_ref[...] if sub == 0 else kc_ref[(sub - 1) * L:sub * L, :]
        v_prev = vp_ref[...] if sub == 0 else vc_ref[(sub - 1) * L:sub * L, :]
        out = _attn_block(sink_ref, q_ref[rows, :], k_prev, kc_ref[rows, :], v_prev, vc_ref[rows, :],
                          bias_first if sub == 0 else bias_inner, lo, top, zeros, ones_lo, ones_hi)
        o_ref[rows, :] = _rms(out, g_ref[...]).astype(o_ref.dtype)


def _attn_block(sink_ref, q, k_prev, k_cur, v_prev, v_cur, bias, lo, top, zeros, ones_lo, ones_hi):
    L = WINDOW
    qb = (q * (HEAD_DIM ** -0.5)).astype(BF16)
    kk = jnp.concatenate([k_prev, k_cur], axis=0)
    vv = jnp.concatenate([v_prev, v_cur], axis=0)
    outs = []
    for kh in range(N_KV_HEADS):
        c0 = LANES * (kh // 2)
        kx = kk[:, c0:c0 + LANES]
        vx = vv[:, c0:c0 + LANES]
        kr = pltpu.roll(kx, HEAD_DIM, 1)
        vr = pltpu.roll(vx, HEAD_DIM, 1)
        if kh % 2 == 0:
            k_lo, k_hi = jnp.where(lo, kx, zeros), jnp.where(lo, zeros, kr)
            v_lo, v_hi = jnp.where(lo, vx, zeros), jnp.where(lo, zeros, vr)
        else:
            k_lo, k_hi = jnp.where(lo, kr, zeros), jnp.where(lo, zeros, kx)
            v_lo, v_hi = jnp.where(lo, vr, zeros), jnp.where(lo, zeros, vx)
        kmat = jnp.concatenate([k_lo, k_hi], axis=0).astype(BF16)
        qs = jnp.concatenate([qb[:, 2 * LANES * kh:2 * LANES * kh + LANES],
                              qb[:, 2 * LANES * kh + LANES:2 * LANES * (kh + 1)]], axis=0)
        s = lax.dot_general(qs, kmat, (((1,), (1,)), ((), ())), preferred_element_type=F32)
        s = s + bias
        sink_lo = jnp.where(top, sink_ref[4 * kh + 0], sink_ref[4 * kh + 2])
        sink_hi = jnp.where(top, sink_ref[4 * kh + 1], sink_ref[4 * kh + 3])
        m_lo = jnp.maximum(jnp.max(s[:, :2 * L], axis=1, keepdims=True), sink_lo)
        m_hi = jnp.maximum(jnp.max(s[:, 2 * L:], axis=1, keepdims=True), sink_hi)
        p = jnp.concatenate([jnp.exp(s[:, :2 * L] - m_lo), jnp.exp(s[:, 2 * L:] - m_hi)], axis=1).astype(BF16)
        vmat = jnp.concatenate([jnp.concatenate([v_lo, ones_lo], axis=1),
                                jnp.concatenate([v_hi, ones_hi], axis=1)], axis=0).astype(BF16)
        oe = jnp.dot(p, vmat, preferred_element_type=F32)
        denom = oe[:, LANES:] + jnp.where(lo, jnp.exp(sink_lo - m_lo), jnp.exp(sink_hi - m_hi))
        o = oe[:, :LANES] / denom
        outs.append(o[:L])
        outs.append(o[L:])
    return jnp.concatenate(outs, axis=1)


_ATTN_SUB_BLOCKS = 2


_CAST_SLAB_BYTES = 8 * 1024 * 1024


def _attn_steps(n_seq, seq):
    return n_seq * (seq // (WINDOW * _ATTN_SUB_BLOCKS))


def _can_cast_in_attn(n_seq, seq):
    steps = _attn_steps(n_seq, seq)
    bf16_rows = 2 * SUBLANES
    return (D_MODEL % steps == 0 and D_FF % steps == 0
            and (D_MODEL // steps) % bf16_rows == 0 and (D_FF // steps) % bf16_rows == 0
            and (D_MODEL // steps) * 2 * D_FF * 4 <= _CAST_SLAB_BYTES)


def _attn_prompt(z, sinks, g_attn, *, n_seq, seq, cast=()):
    L = WINDOW
    n_sub = _ATTN_SUB_BLOCKS
    nb = seq // (L * n_sub)
    kcol = D_ATTN // D_KV
    vcol = kcol + 1
    cur = lambda col: (lambda n, b: (n * nb + b, col))
    prev = lambda col: (lambda n, b: (jnp.maximum((n * nb + b) * n_sub - 1, 0), col))
    in_specs = [
        pl.BlockSpec(memory_space=pltpu.SMEM),
        pl.BlockSpec((L * n_sub, D_ATTN), cur(0)),
        pl.BlockSpec((L * n_sub, D_KV), cur(kcol)),
        pl.BlockSpec((L, D_KV), prev(kcol)),
        pl.BlockSpec((L * n_sub, D_KV), cur(vcol)),
        pl.BlockSpec((L, D_KV), prev(vcol)),
        pl.BlockSpec((1, D_ATTN), lambda n, b: (0, 0)),
    ]
    out_specs = [pl.BlockSpec((L * n_sub, D_ATTN), cur(0))]
    out_shape = [jax.ShapeDtypeStruct((n_seq * seq, D_ATTN), BF16)]
    args = [sinks, z, z, z, z, z, g_attn]
    for w_all, layer in cast:
        i_spec, o_spec, o_shape = _cast_specs(w_all, layer, n_seq * nb, lambda n, b: n * nb + b)
        in_specs.append(i_spec)
        out_specs.append(o_spec)
        out_shape.append(o_shape)
        args.append(w_all)
    outs = pl.pallas_call(
        functools.partial(_attn_prompt_kernel, n_sub=n_sub, n_cast=len(cast)),
        grid=(n_seq, nb),
        in_specs=in_specs,
        out_specs=out_specs,
        out_shape=out_shape,
        compiler_params=_cparams(("parallel", "arbitrary")),
        name="attn_prompt",
    )(*args)
    return outs[0] if not cast else outs


def _lru_gates(xc, wg_ref, ba, bi, lam):
    g = jnp.dot(xc.astype(BF16), wg_ref[...], preferred_element_type=F32)
    r = jax.nn.sigmoid(g[:, :D_LRU] + ba)
    gi = jax.nn.sigmoid(g[:, D_LRU:] + bi)
    nl = -lam
    softplus = jnp.maximum(nl, 0.0) + jnp.log1p(jnp.exp(-jnp.abs(nl)))
    log_a = -LRU_C * r * softplus
    a = jnp.exp(log_a)
    th = jnp.tanh(log_a)
    m2 = (-2.0 * th) / (1.0 - th)
    mult = jnp.where(m2 > 0.0, m2 * lax.rsqrt(m2), 0.0)
    return a, mult, gi


def _shift_rows(u, prev8, k):
    r = pltpu.roll(u, k, 0)
    pr = pltpu.roll(prev8, k, 0)
    row8 = lax.broadcasted_iota(jnp.int32, prev8.shape, 0)
    head = jnp.where(row8 < k, pr, r[:SUBLANES])
    return jnp.concatenate([head, r[SUBLANES:]], axis=0)


def _chunk_scan(a, b):
    n = a.shape[0]
    row = lax.broadcasted_iota(jnp.int32, a.shape, 0)
    d = 1
    while d < n:
        if d < SUBLANES:
            keep = row >= d
            b = jnp.where(keep, b + a * pltpu.roll(b, d, 0), b)
            a = jnp.where(keep, a * pltpu.roll(a, d, 0), a)
        else:
            b = jnp.concatenate([b[:d], b[d:] + a[d:] * b[:n - d]], axis=0)
            a = jnp.concatenate([a[:d], a[d:] * a[:n - d]], axis=0)
        d *= 2
    return a, b


def _scan_pitch(ln):
    assert ln % SUBLANES == 0
    return ln if ln % (2 * SUBLANES) == SUBLANES else ln + SUBLANES


def _strided_scan(a, b, h_prev, a_scr, b_scr):
    tc, ch = a.shape
    ln = tc // SUBLANES
    pitch = _scan_pitch(ln)
    nslab = ch // LANES
    for s in range(SUBLANES):
        for c in range(nslab):
            a_scr[c, pitch * s:pitch * s + ln, :] = a[ln * s:ln * (s + 1), c * LANES:(c + 1) * LANES]
            b_scr[c, pitch * s:pitch * s + ln, :] = b[ln * s:ln * (s + 1), c * LANES:(c + 1) * LANES]
    row8 = lax.broadcasted_iota(jnp.int32, (SUBLANES, LANES), 0)
    for c in range(nslab):
        h = jnp.zeros((SUBLANES, LANES), F32)
        acum = jnp.ones((SUBLANES, LANES), F32)
        for j in range(ln):
            idx = pl.ds(j, SUBLANES, stride=pitch)
            at = a_scr[c, idx, :]
            h = at * h + b_scr[c, idx, :]
            acum = at * acum
            b_scr[c, idx, :] = h
            a_scr[c, idx, :] = acum
        a_tot, b_tot = _chunk_scan(acum, h)
        hp = h_prev[:, c * LANES:(c + 1) * LANES]
        cin = jnp.where(row8 == 0, hp, pltpu.roll(b_tot + a_tot * hp, 1, 0))
        for j in range(ln):
            idx = pl.ds(j, SUBLANES, stride=pitch)
            b_scr[c, idx, :] = b_scr[c, idx, :] + a_scr[c, idx, :] * cin
    return jnp.concatenate(
        [jnp.concatenate([b_scr[c, pitch * s:pitch * s + ln, :] for c in range(nslab)], axis=1)
         for s in range(SUBLANES)], axis=0)


def _lru_sc_kernel(ux_ref, gate_ref, ub_ref, uc_ref, uh_ref, cw_ref, cb_ref, wg_ref, ba_ref, bi_ref,
                   lam_ref, scw_ref, glru_ref, gsc_ref, *rest, tc, n_cast):
    o_ref, h8_ref, x8_ref, g8_ref = rest[n_cast:n_cast + 4]
    cx_scr, cg_scr, ch_scr, sa_scr, sb_scr = rest[2 * n_cast + 4:]
    for src, dst in zip(rest[:n_cast], rest[n_cast + 4:2 * n_cast + 4]):
        dst[...] = src[...].astype(BF16)
    t = pl.program_id(1)

    @pl.when(t == 0)
    def _():
        cx_scr[...] = jnp.zeros_like(cx_scr)
        cg_scr[...] = jnp.zeros_like(cg_scr)
        ch_scr[...] = jnp.zeros_like(ch_scr)

    ux = ux_ref[...]
    px = cx_scr[...]
    xc = _shift_rows(ux, px, 3) * cw_ref[0:1, :]
    xc = xc + _shift_rows(ux, px, 2) * cw_ref[1:2, :]
    xc = xc + _shift_rows(ux, px, 1) * cw_ref[2:3, :]
    xc = xc + ux * cw_ref[3:4, :]
    xc = xc + cb_ref[...]

    a, mult, gi = _lru_gates(xc, wg_ref, ba_ref[...], bi_ref[...], lam_ref[...])
    pos = t * tc + lax.broadcasted_iota(jnp.int32, (tc, 1), 0)
    mult = jnp.where(pos == 0, 1.0, mult)
    h = _strided_scan(a, mult * gi * xc, ch_scr[SUBLANES - 1:SUBLANES, :], sa_scr, sb_scr)
    o_lru = h * jax.nn.gelu(gate_ref[...], approximate=True)

    gch = uc_ref[...] * uh_ref[...]
    pg = cg_scr[...]
    y = _shift_rows(gch, pg, 2) * scw_ref[0:1, :]
    y = y + _shift_rows(gch, pg, 1) * scw_ref[1:2, :]
    y = y + gch * scw_ref[2:3, :]
    o_sc = ub_ref[...] * y

    o_ref[:, :D_LRU] = _rms(o_lru, glru_ref[...]).astype(o_ref.dtype)
    o_ref[:, D_LRU:] = _rms(o_sc, gsc_ref[...]).astype(o_ref.dtype)

    cx_scr[...] = ux[tc - SUBLANES:]
    cg_scr[...] = gch[tc - SUBLANES:]
    ch_scr[...] = h[tc - SUBLANES:]
    h8_ref[0] = h[tc - SUBLANES:]
    x8_ref[0] = ux[tc - SUBLANES:]
    g8_ref[0] = gch[tc - SUBLANES:]


_LRU_CHUNK = WINDOW * _ATTN_SUB_BLOCKS


def _lru_sc_prompt(z, lw, *, n_seq, seq, cast=()):
    tc = _LRU_CHUNK
    nt = seq // tc
    zcol = lambda col: pl.BlockSpec((tc, D_LRU), lambda n, t: (n * nt + t, col))
    const = lambda shape: pl.BlockSpec(shape, lambda n, t: (0,) * len(shape))
    st = pl.BlockSpec((1, SUBLANES, D_LRU), lambda n, t: (n, 0, 0))
    st_shape = jax.ShapeDtypeStruct((n_seq, SUBLANES, D_LRU), F32)
    in_specs = [zcol(_COL_UX), zcol(_COL_GATE), zcol(_COL_B), zcol(_COL_C), zcol(_COL_H),
                const((LRU_CONV_W, D_LRU)), const((1, D_LRU)), const((D_LRU, 2 * D_LRU)),
                const((1, D_LRU)), const((1, D_LRU)), const((1, D_LRU)),
                const((SC_CONV_W, D_SC)), const((1, D_LRU)), const((1, D_SC))]
    out_specs = [pl.BlockSpec((tc, D_LRU + D_SC), lambda n, t: (n * nt + t, 0)), st, st, st]
    out_shape = [jax.ShapeDtypeStruct((n_seq * seq, D_LRU + D_SC), BF16), st_shape, st_shape, st_shape]
    args = [z, z, z, z, z, lw["conv_w"], lw["conv_b"], lw["w_gates"], lw["b_a"], lw["b_i"], lw["lam"],
            lw["sc_w"], lw["g_lru"], lw["g_sc"]]
    for w_all, layer in cast:
        i_spec, o_spec, o_shape = _cast_specs(w_all, layer, n_seq * nt, lambda n, t: n * nt + t)
        in_specs.append(i_spec)
        out_specs.append(o_spec)
        out_shape.append(o_shape)
        args.append(w_all)
    return pl.pallas_call(
        functools.partial(_lru_sc_kernel, tc=tc, n_cast=len(cast)),
        grid=(n_seq, nt),
        in_specs=in_specs,
        out_specs=out_specs,
        out_shape=out_shape,
        scratch_shapes=[pltpu.VMEM((SUBLANES, D_LRU), F32)] * 3
        + [pltpu.VMEM((D_LRU // LANES, SUBLANES * _scan_pitch(tc // SUBLANES), LANES), F32)] * 2,
        compiler_params=_cparams(("parallel", "arbitrary")),
        name="lru_sc_prompt",
    )(*args)


def _decode_kernel(q_ref, knt_ref, kc_ref, vc_ref, sel_ref, sink_ref, gat_ref,
                   ux_ref, gate_ref, ub_ref, uc_ref, uh_ref, h0_ref, cbuf_ref, sbuf_ref,
                   cw_ref, cb_ref, wg_ref, ba_ref, bi_ref, lam_ref, scw_ref, glru_ref, gsc_ref,
                   oat_ref, ors_ref, ko_ref, vo_ref, hn_ref, cn_ref, sn_ref, *, nbk):
    wb = kc_ref.shape[2]
    hrow = lax.broadcasted_iota(jnp.int32, (N_HEADS, D_ATTN), 0)
    hcol = lax.broadcasted_iota(jnp.int32, (N_HEADS, D_ATTN), 1) // HEAD_DIM
    own = (hrow == hcol)
    qexp = jnp.concatenate(
        [jnp.where(own, jnp.broadcast_to(q_ref[i:i + 1, :] * (HEAD_DIM ** -0.5), (N_HEADS, D_ATTN)), 0.0)
         for i in range(nbk)], axis=0)
    qrow = jnp.dot(qexp.astype(BF16), sel_ref[...], preferred_element_type=F32)
    rows = nbk * N_HEADS
    grow = (lax.broadcasted_iota(jnp.int32, (rows, D_KV), 0) % N_HEADS) // N_GROUP
    gcol = lax.broadcasted_iota(jnp.int32, (rows, D_KV), 1) // HEAD_DIM
    kvmask = (grow == gcol)
    qm = jnp.where(kvmask, qrow, 0.0).astype(BF16)
    sink = jnp.concatenate([sink_ref[...][:, 0:1]] * nbk, axis=0)
    gat = jnp.concatenate([gat_ref[...]] * nbk, axis=0)
    lanek = lax.broadcasted_iota(jnp.int32, (D_KV, wb), 1)
    newest = lanek == wb - 1
    for i in range(nbk):
        ko_ref[i] = jnp.where(newest, jnp.broadcast_to(knt_ref[:D_KV, i:i + 1], (D_KV, wb)),
                              pltpu.roll(kc_ref[i], wb - 1, 1))
        vo_ref[i] = jnp.where(newest, jnp.broadcast_to(knt_ref[D_KV:, i:i + 1], (D_KV, wb)),
                              pltpu.roll(vc_ref[i], wb - 1, 1))
    s = jnp.concatenate(
        [jnp.dot(qm[i * N_HEADS:(i + 1) * N_HEADS], ko_ref[i].astype(BF16), preferred_element_type=F32)
         for i in range(nbk)], axis=0)
    m = jnp.maximum(jnp.max(s, axis=1, keepdims=True), sink)
    p = jnp.exp(s - m)
    p = (p / (jnp.sum(p, axis=1, keepdims=True) + jnp.exp(sink - m))).astype(BF16)
    of = jnp.concatenate(
        [lax.dot_general(p[i * N_HEADS:(i + 1) * N_HEADS], vo_ref[i].astype(BF16), (((1,), (1,)), ((), ())),
                         preferred_element_type=F32) for i in range(nbk)], axis=0)
    of = jnp.where(kvmask, of, 0.0)
    t = of[:, :LANES] + of[:, LANES:]
    o = t + pltpu.roll(t, HEAD_DIM, 1)
    rs = jnp.sum(o * o, axis=1, keepdims=True)
    for i in range(nbk):
        sl = slice(i * N_HEADS, (i + 1) * N_HEADS)
        ms = jnp.sum(rs[sl], axis=0, keepdims=True) * (0.5 / D_ATTN)
        oat_ref[i] = (o[sl] * lax.rsqrt(ms + RMS_EPS) * gat[sl]).astype(oat_ref.dtype)

    ux = ux_ref[...]
    xc = cbuf_ref[0] * cw_ref[0:1, :]
    xc = xc + cbuf_ref[1] * cw_ref[1:2, :]
    xc = xc + cbuf_ref[2] * cw_ref[2:3, :]
    xc = xc + ux * cw_ref[3:4, :]
    xc = xc + cb_ref[...]
    a, mult, gi = _lru_gates(xc, wg_ref, ba_ref[...], bi_ref[...], lam_ref[...])
    h = a * h0_ref[...] + mult * gi * xc
    o_lru = h * jax.nn.gelu(gate_ref[...], approximate=True)
    hn_ref[...] = h
    cn_ref[0] = cbuf_ref[1]
    cn_ref[1] = cbuf_ref[2]
    cn_ref[2] = ux
    gch = uc_ref[...] * uh_ref[...]
    y = sbuf_ref[:, 0, :] * scw_ref[0:1, :]
    y = y + sbuf_ref[:, 1, :] * scw_ref[1:2, :]
    y = y + gch * scw_ref[2:3, :]
    o_sc = ub_ref[...] * y
    sn_ref[:, 0, :] = sbuf_ref[:, 1, :]
    sn_ref[:, 1, :] = gch
    ors_ref[:, :D_LRU] = _rms(o_lru, glru_ref[...]).astype(ors_ref.dtype)
    ors_ref[:, D_LRU:] = _rms(o_sc, gsc_ref[...]).astype(ors_ref.dtype)


def _decode_mix(z, knt, kct_all, vct_all, h0_all, cbuf_all, sbuf_all, lw, *, layer, nbk=16):
    ns = z.shape[0]
    wb = kct_all.shape[3]
    z512 = lambda col: pl.BlockSpec((nbk, D_LRU), lambda i: (i, col))
    const = lambda shape: pl.BlockSpec(shape, lambda i: (0,) * len(shape))
    cache_in = pl.BlockSpec((None, nbk, D_KV, wb), lambda i: (layer, i, 0, 0))
    cache_out = pl.BlockSpec((nbk, D_KV, wb), lambda i: (i, 0, 0))
    outs = pl.pallas_call(
        functools.partial(_decode_kernel, nbk=nbk),
        grid=(ns // nbk,),
        in_specs=[pl.BlockSpec((nbk, D_ATTN), lambda i: (i, 0)),
                  pl.BlockSpec((None, 2 * D_KV, nbk), lambda i: (i, 0, 0)),
                  cache_in, cache_in,
                  const((D_ATTN, D_KV)), const((N_HEADS, LANES)), const((N_HEADS, LANES)),
                  z512(_COL_UX), z512(_COL_GATE), z512(_COL_B), z512(_COL_C), z512(_COL_H),
                  pl.BlockSpec((None, nbk, D_LRU), lambda i: (layer, i, 0)),
                  pl.BlockSpec((None, LRU_CONV_W - 1, nbk, D_LRU), lambda i: (layer, 0, i, 0)),
                  pl.BlockSpec((None, nbk, SC_CONV_W - 1, D_SC), lambda i: (layer, i, 0, 0)),
                  const((LRU_CONV_W, D_LRU)), const((1, D_LRU)), const((D_LRU, 2 * D_LRU)),
                  const((1, D_LRU)), const((1, D_LRU)), const((1, D_LRU)),
                  const((SC_CONV_W, D_SC)), const((1, D_LRU)), const((1, D_SC))],
        out_specs=[pl.BlockSpec((nbk, N_HEADS, LANES), lambda i: (i, 0, 0)),
                   pl.BlockSpec((nbk, D_LRU + D_SC), lambda i: (i, 0)),
                   cache_out, cache_out,
                   pl.BlockSpec((nbk, D_LRU), lambda i: (i, 0)),
                   pl.BlockSpec((LRU_CONV_W - 1, nbk, D_LRU), lambda i: (0, i, 0)),
                   pl.BlockSpec((nbk, SC_CONV_W - 1, D_SC), lambda i: (i, 0, 0))],
        out_shape=[jax.ShapeDtypeStruct((ns, N_HEADS, LANES), BF16),
                   jax.ShapeDtypeStruct((ns, D_LRU + D_SC), BF16),
                   jax.ShapeDtypeStruct((ns, D_KV, wb), F32), jax.ShapeDtypeStruct((ns, D_KV, wb), F32),
                   jax.ShapeDtypeStruct((ns, D_LRU), F32),
                   jax.ShapeDtypeStruct((LRU_CONV_W - 1, ns, D_LRU), F32),
                   jax.ShapeDtypeStruct((ns, SC_CONV_W - 1, D_SC), F32)],
        compiler_params=_cparams(("parallel",)),
        name="decode_mix",
    )(z, knt, kct_all, vct_all, lw["sel"], lw["sink_tab"], lw["g_attn_tab"],
      z, z, z, z, z, h0_all, cbuf_all, sbuf_all,
      lw["conv_w"], lw["conv_b"], lw["w_gates"], lw["b_a"], lw["b_i"], lw["lam"],
      lw["sc_w"], lw["g_lru"], lw["g_sc"])
    return outs


_OUT_PROJ_CHUNK = 512


def _out_proj_kernel(ma_ref, mb_ref, x_ref, w_ref, g_ref, x1_ref, hf_ref):
    ma = ma_ref[...]
    mb = mb_ref[...]
    ssq = None
    for c0 in range(0, D_MODEL, _OUT_PROJ_CHUNK):
        cs = slice(c0, c0 + _OUT_PROJ_CHUNK)
        acc = jnp.dot(ma, w_ref[:D_ATTN, cs], preferred_element_type=F32)
        acc = acc + jnp.dot(mb, w_ref[D_ATTN:, cs], preferred_element_type=F32)
        x1 = x_ref[:, cs] + acc
        x1_ref[:, cs] = x1
        part = jnp.sum(x1 * x1, axis=-1, keepdims=True)
        ssq = part if ssq is None else ssq + part
    scale = lax.rsqrt(ssq * (1.0 / D_MODEL) + RMS_EPS)
    for c0 in range(0, D_MODEL, _OUT_PROJ_CHUNK):
        cs = slice(c0, c0 + _OUT_PROJ_CHUNK)
        hf_ref[:, cs] = (x1_ref[:, cs] * scale * g_ref[:, cs]).astype(hf_ref.dtype)


def _out_proj(ma, mb, x, w_all_bf, g_all, *, layer, w_layer, bm):
    m = x.shape[0]
    return pl.pallas_call(
        _out_proj_kernel,
        grid=(m // bm,),
        in_specs=[pl.BlockSpec((bm, D_ATTN), lambda i: (i, 0)),
                  pl.BlockSpec((bm, D_LRU + D_SC), lambda i: (i, 0)),
                  pl.BlockSpec((bm, D_MODEL), lambda i: (i, 0)),
                  _resident((None, D_MODEL, D_MODEL), lambda i: (w_layer, 0, 0)),
                  pl.BlockSpec((None, 1, D_MODEL), lambda i: (layer, 0, 0))],
        out_specs=[pl.BlockSpec((bm, D_MODEL), lambda i: (i, 0)),
                   pl.BlockSpec((bm, D_MODEL), lambda i: (i, 0))],
        out_shape=[jax.ShapeDtypeStruct((m, D_MODEL), F32), jax.ShapeDtypeStruct((m, D_MODEL), BF16)],
        compiler_params=_cparams(("parallel",)),
        name="out_proj",
    )(ma, mb, x, w_all_bf, g_all)


_FFN_DOWN_CHUNK = 512
_FFN_X1_CHUNK = 256


def _ffn_kernel(hf_ref, x1_ref, wg_ref, wu_ref, wd_ref, gfin_ref, o_ref, *, final_norm, cast_refs=()):
    f = pl.program_id(1)
    n_x1 = D_MODEL // _FFN_X1_CHUNK

    @pl.when(f == 0)
    def _():
        o_ref[...] = jnp.zeros_like(o_ref)

    hf = hf_ref[...]
    gate = jnp.dot(hf, wg_ref[...], preferred_element_type=F32)
    up = jnp.dot(hf, wu_ref[...], preferred_element_type=F32)
    hid = (gate * jax.nn.sigmoid(gate) * up).astype(BF16)
    for c0 in range(0, D_MODEL, _FFN_DOWN_CHUNK):
        cs = slice(c0, c0 + _FFN_DOWN_CHUNK)
        o_ref[:, cs] += jnp.dot(hid, wd_ref[:, cs], preferred_element_type=F32)

    for c in range(n_x1):
        @pl.when(f == c)
        def _(c=c):
            cs = slice(c * _FFN_X1_CHUNK, (c + 1) * _FFN_X1_CHUNK)
            o_ref[:, cs] += x1_ref[...]

    if final_norm:
        @pl.when(f == pl.num_programs(1) - 1)
        def _():
            o_ref[...] = _rms(o_ref[...], gfin_ref[...])

    if cast_refs:
        src, dst = cast_refs

        @pl.when(f < D_IN // _FFN_CAST_COLS)
        def _():
            dst[...] = src[...].astype(BF16)


_FFN_CAST_COLS = 512


def _ffn_kernel_cast(hf_ref, x1_ref, wg_ref, wu_ref, wd_ref, gfin_ref, wn_ref, o_ref, wn_bf_ref, *, final_norm):
    _ffn_kernel(hf_ref, x1_ref, wg_ref, wu_ref, wd_ref, gfin_ref, o_ref, final_norm=final_norm,
                cast_refs=(wn_ref, wn_bf_ref))


def _can_cast_in_ffn(m, bm, tf=512):
    n_i = m // bm
    return (D_MODEL % n_i == 0 and (D_MODEL // n_i) % (2 * SUBLANES) == 0
            and D_FF // tf >= D_IN // _FFN_CAST_COLS)


def _ffn(hf, x1, w_gu_bf, w_d_bf, g_final, *, layer, bm, tf=512, final_norm, cast_next=None):
    m = hf.shape[0]
    nf = D_FF // tf
    n_x1 = D_MODEL // _FFN_X1_CHUNK
    assert nf >= n_x1
    in_specs = [pl.BlockSpec((bm, D_MODEL), lambda i, f: (i, 0)),
                pl.BlockSpec((bm, _FFN_X1_CHUNK), lambda i, f: (i, jnp.minimum(f, n_x1 - 1))),
                pl.BlockSpec((None, D_MODEL, tf), lambda i, f: (layer, 0, f)),
                pl.BlockSpec((None, D_MODEL, tf), lambda i, f: (layer, 0, nf + f)),
                pl.BlockSpec((None, tf, D_MODEL), lambda i, f: (layer, f, 0)),
                pl.BlockSpec((1, D_MODEL), lambda i, f: (0, 0))]
    out_spec = pl.BlockSpec((bm, D_MODEL), lambda i, f: (i, 0))
    out_shape = jax.ShapeDtypeStruct((m, D_MODEL), F32)
    args = (hf, x1, w_gu_bf, w_gu_bf, w_d_bf, g_final)
    if cast_next is None:
        return pl.pallas_call(
            functools.partial(_ffn_kernel, final_norm=final_norm),
            grid=(m // bm, nf), in_specs=in_specs, out_specs=out_spec, out_shape=out_shape,
            compiler_params=_cparams(("parallel", "arbitrary")), name="ffn",
        )(*args)
    w_next, layer_next = cast_next
    rows = D_MODEL // (m // bm)
    n_cc = D_IN // _FFN_CAST_COLS
    col = lambda f: jnp.minimum(f, n_cc - 1)
    return pl.pallas_call(
        functools.partial(_ffn_kernel_cast, final_norm=final_norm),
        grid=(m // bm, nf),
        in_specs=in_specs + [pl.BlockSpec((None, rows, _FFN_CAST_COLS), lambda i, f: (layer_next, i, col(f)))],
        out_specs=[out_spec, pl.BlockSpec((None, rows, _FFN_CAST_COLS), lambda i, f: (0, i, col(f)))],
        out_shape=[out_shape, jax.ShapeDtypeStruct((1, D_MODEL, D_IN), BF16)],
        compiler_params=_cparams(("parallel", "arbitrary")), name="ffn",
    )(*args, w_next)


def _rope_tables(pos):
    half = HEAD_DIM // 2
    inv = ROPE_THETA ** (-jnp.arange(half, dtype=F32) / half)
    ang = pos.astype(F32)[:, None] * inv[None, :]
    cos, sin = jnp.cos(ang), jnp.sin(ang)
    cos_t = jnp.tile(cos, (1, LANES // half))
    sin_t = jnp.tile(jnp.concatenate([-sin, sin], axis=1), (1, LANES // HEAD_DIM))
    return cos_t, sin_t


def _block_diag(w):
    hh, blk, _ = w.shape
    eye = jnp.eye(hh, dtype=w.dtype)
    return (eye[:, None, :, None] * w[:, :, None, :]).reshape(hh * blk, hh * blk)


def kernel(x_prompt, x_sample, state_lru_h, state_lru_conv, cache_swa_k, cache_swa_v, state_sconv,
           norm_mix, w_in, norm_grp, w_out, lru_conv_w, lru_conv_b, lru_w_a, lru_b_a, lru_w_i, lru_b_i,
           lru_lambda, sc_conv_w, attn_sinks, norm_ffn, ffn_w_gu, ffn_w_down, norm_final):
    n_p, t_p, _ = x_prompt.shape
    n_s, t_s, _ = x_sample.shape
    depth = w_in.shape[0]
    wb = cache_swa_k.shape[2]
    assert t_s == 1 and wb == WINDOW and t_p % 256 == 0 and n_s % 16 == 0

    bm_p = 512 if (n_p * t_p) % 512 == 0 and t_p % 512 == 0 else 256
    bm_ffn = 1024 if (n_p * t_p) % 1024 == 0 else bm_p
    bm_s = n_s

    cos_p, sin_p = _rope_tables(jnp.arange(t_p, dtype=jnp.int32))
    cos_s, sin_s = _rope_tables(jnp.full((bm_s,), PAST_LEN, dtype=jnp.int32))

    sel = (jnp.arange(D_ATTN)[:, None] % HEAD_DIM == jnp.arange(D_KV)[None, :] % HEAD_DIM).astype(BF16)

    xp = x_prompt.reshape(n_p * t_p, D_MODEL)
    xs = x_sample.reshape(n_s, D_MODEL)
    row = lambda v: v.reshape(1, -1)
    p_states, s_states = [], []
    piggy = _can_cast_in_attn(n_p, t_p) and _can_cast_in_ffn(n_p * t_p, bm_ffn)
    if piggy:
        w_in_l, l_in = w_in[:1].astype(BF16), 0
    else:
        w_in_bf = w_in.astype(BF16)
        w_out_bf = w_out.astype(BF16)
        w_gu_bf = ffn_w_gu.astype(BF16)
        w_d_bf = ffn_w_down.astype(BF16)
    nbk = 16
    kct_all = cache_swa_k.transpose(0, 1, 3, 4, 2).reshape(depth, n_s, D_KV, wb)
    vct_all = cache_swa_v.transpose(0, 1, 3, 4, 2).reshape(depth, n_s, D_KV, wb)
    cbuf_all = state_lru_conv.transpose(0, 2, 1, 3)
    g_mix = norm_mix.reshape(depth, 1, D_MODEL)
    g_ffn = norm_ffn.reshape(depth, 1, D_MODEL)
    for l in range(depth):
        g_attn, g_lru, g_sc = (norm_grp[l, :D_ATTN], norm_grp[l, D_ATTN:D_ATTN + D_LRU],
                               norm_grp[l, D_ATTN + D_LRU:])
        lw = dict(
            conv_w=lru_conv_w[l], conv_b=row(lru_conv_b[l]),
            w_gates=jnp.concatenate([_block_diag(lru_w_a[l]), _block_diag(lru_w_i[l])], axis=1).astype(BF16),
            b_a=row(lru_b_a[l]), b_i=row(lru_b_i[l]), lam=row(lru_lambda[l]),
            sc_w=sc_conv_w[l], g_lru=row(g_lru), g_sc=row(g_sc),
            sel=sel,
            sink_tab=jnp.broadcast_to(attn_sinks[l][:, None], (N_HEADS, LANES)),
            g_attn_tab=jnp.tile(g_attn.reshape(N_HEADS, HEAD_DIM), (1, LANES // HEAD_DIM)),
        )
        last = l == depth - 1

        if not piggy:
            w_in_l, w_out_l, w_gu_l, w_d_l, l_in, l_w = w_in_bf, w_out_bf, w_gu_bf, w_d_bf, l, l
        z = _in_proj(xp, g_mix, w_in_l, cos_p, sin_p, layer=l, w_layer=l_in, bm=bm_p)
        if piggy:
            ma, w_gu_l = _attn_prompt(z, attn_sinks[l], row(g_attn), n_seq=n_p, seq=t_p, cast=[(ffn_w_gu, l)])
            mb, h8, x8, g8, w_d_l, w_out_l = _lru_sc_prompt(z, lw, n_seq=n_p, seq=t_p,
                                                            cast=[(ffn_w_down, l), (w_out, l)])
            l_w = 0
        else:
            ma = _attn_prompt(z, attn_sinks[l], row(g_attn), n_seq=n_p, seq=t_p)
            mb, h8, x8, g8 = _lru_sc_prompt(z, lw, n_seq=n_p, seq=t_p)
        x1, hf = _out_proj(ma, mb, xp, w_out_l, g_ffn, layer=l, w_layer=l_w, bm=bm_p)
        w_in_cur = w_in_l
        if piggy and not last:
            xp_new, w_in_l = _ffn(hf, x1, w_gu_l, w_d_l, row(norm_final), layer=l_w, bm=bm_ffn,
                                  final_norm=last, cast_next=(w_in, l + 1))
        else:
            xp_new = _ffn(hf, x1, w_gu_l, w_d_l, row(norm_final), layer=l_w, bm=bm_ffn, final_norm=last)
        z3 = z.reshape(n_p, t_p, D_IN)
        wbp = min(WINDOW, t_p)
        p_states.append((
            h8[:, SUBLANES - 1],
            x8[:, SUBLANES - (LRU_CONV_W - 1):],
            z3[:, t_p - wbp:, D_ATTN:D_ATTN + D_KV].reshape(n_p, wbp, N_KV_HEADS, HEAD_DIM),
            z3[:, t_p - wbp:, D_ATTN + D_KV:D_ATTN + 2 * D_KV].reshape(n_p, wbp, N_KV_HEADS, HEAD_DIM),
            g8[:, SUBLANES - (SC_CONV_W - 1):],
        ))
        xp = xp_new

        zs = _in_proj(xs, g_mix, w_in_cur, cos_s, sin_s, layer=l, w_layer=l_in, bm=bm_s)
        knt = zs[:, D_ATTN:D_ATTN + 2 * D_KV].reshape(n_s // nbk, nbk, 2 * D_KV).transpose(0, 2, 1)
        oat, ors, k_new, v_new, h_new, c_new, s_new = _decode_mix(
            zs, knt, kct_all, vct_all, state_lru_h, cbuf_all, state_sconv, lw, layer=l, nbk=nbk)
        k_new = k_new.reshape(n_s, N_KV_HEADS, HEAD_DIM, wb).transpose(0, 3, 1, 2)
        v_new = v_new.reshape(n_s, N_KV_HEADS, HEAD_DIM, wb).transpose(0, 3, 1, 2)
        c_new = c_new.transpose(1, 0, 2)
        mas = oat[:, :, :HEAD_DIM].reshape(n_s, D_ATTN)
        x1s, hfs = _out_proj(mas, ors, xs, w_out_l, g_ffn, layer=l, w_layer=l_w, bm=bm_s)
        xs = _ffn(hfs, x1s, w_gu_l, w_d_l, row(norm_final), layer=l_w, bm=bm_s, final_norm=last)
        s_states.append((h_new, c_new, k_new, v_new, s_new))

    y_prompt = xp.reshape(n_p, t_p, D_MODEL)
    y_sample = xs.reshape(n_s, t_s, D_MODEL)
    stack = lambda states, k: jnp.stack([st[k] for st in states])
    return (y_prompt, y_sample,
            stack(p_states, 0), stack(p_states, 1), stack(p_states, 2), stack(p_states, 3), stack(p_states, 4),
            stack(s_states, 0), stack(s_states, 1), stack(s_states, 2), stack(s_states, 3), stack(s_states, 4))
```

```python
import functools

import jax
import jax.numpy as jnp
from jax import lax
from jax.experimental import pallas as pl
from jax.experimental.pallas import tpu as pltpu

F32 = jnp.float32
BF16 = jnp.bfloat16

D_MODEL = 2048
D_ATTN = 1024
D_LRU = 512
D_SC = 512
HEAD_DIM = 64
N_HEADS = 16
N_KV_HEADS = 4
N_GROUP = 4
D_KV = 256
WINDOW = 128
ROPE_THETA = 10000.0
N_LRU_HEADS = 8
LRU_BLK = 64
LRU_CONV_W = 4
LRU_C = 8.0
SC_CONV_W = 3
D_FF = 5632
D_IN = 4096
RMS_EPS = 1e-6
PAST_LEN = 8192

LANES = 128
SUBLANES = 8
VMEM_LIMIT_BYTES = 56 * 1024 * 1024

_COL_UX, _COL_GATE, _COL_B, _COL_C, _COL_H = 3, 4, 5, 6, 7


def _cparams(sem):
    return pltpu.CompilerParams(dimension_semantics=sem, vmem_limit_bytes=VMEM_LIMIT_BYTES)


def _rms(x, g):
    return x * lax.rsqrt(jnp.mean(x * x, axis=-1, keepdims=True) + RMS_EPS) * g


_IN_PROJ_CHUNK = 512


def _in_proj_kernel(x_ref, g_ref, w_ref, cos_ref, sin_ref, z_ref):
    h = _rms(x_ref[...], g_ref[...]).astype(BF16)
    bm = h.shape[0]
    lane = lax.broadcasted_iota(jnp.int32, (bm, LANES), 1)
    lo32 = (lane % HEAD_DIM) < (HEAD_DIM // 2)
    cos = cos_ref[...]
    sin = sin_ref[...]

    def rope(a):
        sw = jnp.where(lo32, pltpu.roll(a, LANES - HEAD_DIM // 2, 1), pltpu.roll(a, HEAD_DIM // 2, 1))
        return a * cos + sw * sin

    rope_cols = D_ATTN + D_KV
    for c0 in range(0, D_IN, _IN_PROJ_CHUNK):
        acc = jnp.dot(h, w_ref[:, c0:c0 + _IN_PROJ_CHUNK], preferred_element_type=F32)
        for c in range(0, _IN_PROJ_CHUNK, LANES):
            a = acc[:, c:c + LANES]
            z_ref[:, c0 + c:c0 + c + LANES] = rope(a) if c0 + c < rope_cols else a


def _resident(block_shape, index_map):
    return pl.BlockSpec(block_shape, index_map, pipeline_mode=pl.Buffered(1))


def _in_proj(x, g_all, w_all_bf, cos_t, sin_t, *, layer, w_layer, bm):
    m = x.shape[0]
    n_tab = cos_t.shape[0] // bm
    return pl.pallas_call(
        _in_proj_kernel,
        grid=(m // bm,),
        in_specs=[
            pl.BlockSpec((bm, D_MODEL), lambda i: (i, 0)),
            pl.BlockSpec((None, 1, D_MODEL), lambda i: (layer, 0, 0)),
            _resident((None, D_MODEL, D_IN), lambda i: (w_layer, 0, 0)),
            pl.BlockSpec((bm, LANES), lambda i: (i % n_tab, 0)),
            pl.BlockSpec((bm, LANES), lambda i: (i % n_tab, 0)),
        ],
        out_specs=pl.BlockSpec((bm, D_IN), lambda i: (i, 0)),
        out_shape=jax.ShapeDtypeStruct((m, D_IN), F32),
        compiler_params=_cparams(("parallel",)),
        name="in_proj",
    )(x, g_all, w_all_bf, cos_t, sin_t)


def _cast_specs(w_all, layer, steps, step_index):
    _, k, n = w_all.shape
    r = k // steps
    assert r * steps == k and r % (2 * SUBLANES) == 0
    return (pl.BlockSpec((None, r, n), lambda *ids: (layer, step_index(*ids), 0)),
            pl.BlockSpec((None, r, n), lambda *ids: (0, step_index(*ids), 0)),
            jax.ShapeDtypeStruct((1, k, n), BF16))


def _attn_prompt_kernel(sink_ref, q_ref, kc_ref, kp_ref, vc_ref, vp_ref, g_ref, *rest, n_sub, n_cast):
    o_ref = rest[n_cast]
    for src, dst in zip(rest[:n_cast], rest[n_cast + 1:]):
        dst[...] = src[...].astype(BF16)
    b = pl.program_id(1)
    L = WINDOW

    lane = lax.broadcasted_iota(jnp.int32, (2 * L, LANES), 1)
    lo = lane < HEAD_DIM
    row = lax.broadcasted_iota(jnp.int32, (2 * L, 1), 0)
    top = row < L

    qi = lax.broadcasted_iota(jnp.int32, (2 * L, 4 * L), 0) % L
    sj = lax.broadcasted_iota(jnp.int32, (2 * L, 4 * L), 1) % (2 * L)
    diff = L + qi - sj
    band = (diff >= 0) & (diff < WINDOW)
    bias_inner = jnp.where(band, 0.0, -jnp.inf).astype(F32)
    bias_first = jnp.where(band & ((sj >= L) | (b > 0)), 0.0, -jnp.inf).astype(F32)

    zeros = jnp.zeros((2 * L, LANES), F32)
    ones_lo = jnp.where(lo, 1.0, 0.0).astype(F32)
    ones_hi = 1.0 - ones_lo

    for sub in range(n_sub):
        rows = slice(sub * L, (sub + 1) * L)
        k_prev = kp_ref[...] if sub == 0 else kc_ref[(sub - 1) * L:sub * L, :]
        v_prev = vp_ref[...] if sub == 0 else vc_ref[(sub - 1) * L:sub * L, :]
        out = _attn_block(sink_ref, q_ref[rows, :], k_prev, kc_ref[rows, :], v_prev, vc_ref[rows, :],
                          bias_first if sub == 0 else bias_inner, lo, top, zeros, ones_lo, ones_hi)
        o_ref[rows, :] = _rms(out, g_ref[...]).astype(o_ref.dtype)


def _attn_block(sink_ref, q, k_prev, k_cur, v_prev, v_cur, bias, lo, top, zeros, ones_lo, ones_hi):
    L = WINDOW
    qb = (q * (HEAD_DIM ** -0.5)).astype(BF16)
    kk = jnp.concatenate([k_prev, k_cur], axis=0)
    vv = jnp.concatenate([v_prev, v_cur], axis=0)
    outs = []
    for kh in range(N_KV_HEADS):
        c0 = LANES * (kh // 2)
        kx = kk[:, c0:c0 + LANES]
        vx = vv[:, c0:c0 + LANES]
        kr = pltpu.roll(kx, HEAD_DIM, 1)
        vr = pltpu.roll(vx, HEAD_DIM, 1)
        if kh % 2 == 0:
            k_lo, k_hi = jnp.where(lo, kx, zeros), jnp.where(lo, zeros, kr)
            v_lo, v_hi = jnp.where(lo, vx, zeros), jnp.where(lo, zeros, vr)
        else:
            k_lo, k_hi = jnp.where(lo, kr, zeros), jnp.where(lo, zeros, kx)
            v_lo, v_hi = jnp.where(lo, vr, zeros), jnp.where(lo, zeros, vx)
        kmat = jnp.concatenate([k_lo, k_hi], axis=0).astype(BF16)
        qs = jnp.concatenate([qb[:, 2 * LANES * kh:2 * LANES * kh + LANES],
                              qb[:, 2 * LANES * kh + LANES:2 * LANES * (kh + 1)]], axis=0)
        s = lax.dot_general(qs, kmat, (((1,), (1,)), ((), ())), preferred_element_type=F32)
        s = s + bias
        sink_lo = jnp.where(top, sink_ref[4 * kh + 0], sink_ref[4 * kh + 2])
        sink_hi = jnp.where(top, sink_ref[4 * kh + 1], sink_ref[4 * kh + 3])
        m_lo = jnp.maximum(jnp.max(s[:, :2 * L], axis=1, keepdims=True), sink_lo)
        m_hi = jnp.maximum(jnp.max(s[:, 2 * L:], axis=1, keepdims=True), sink_hi)
        p = jnp.concatenate([jnp.exp(s[:, :2 * L] - m_lo), jnp.exp(s[:, 2 * L:] - m_hi)], axis=1).astype(BF16)
        vmat = jnp.concatenate([jnp.concatenate([v_lo, ones_lo], axis=1),
                                jnp.concatenate([v_hi, ones_hi], axis=1)], axis=0).astype(BF16)
        oe = jnp.dot(p, vmat, preferred_element_type=F32)
        denom = oe[:, LANES:] + jnp.where(lo, jnp.exp(sink_lo - m_lo), jnp.exp(sink_hi - m_hi))
        o = oe[:, :LANES] / denom
        outs.append(o[:L])
        outs.append(o[L:])
    return jnp.concatenate(outs, axis=1)


_ATTN_SUB_BLOCKS = 2


_CAST_SLAB_BYTES = 8 * 1024 * 1024


def _attn_steps(n_seq, seq):
    return n_seq * (seq // (WINDOW * _ATTN_SUB_BLOCKS))


def _can_cast_in_attn(n_seq, seq):
    steps = _attn_steps(n_seq, seq)
    bf16_rows = 2 * SUBLANES
    return (D_MODEL % steps == 0 and D_FF % steps == 0
            and (D_MODEL // steps) % bf16_rows == 0 and (D_FF // steps) % bf16_rows == 0
            and (D_MODEL // steps) * 2 * D_FF * 4 <= _CAST_SLAB_BYTES)


def _attn_prompt(z, sinks, g_attn, *, n_seq, seq, cast=()):
    L = WINDOW
    n_sub = _ATTN_SUB_BLOCKS
    nb = seq // (L * n_sub)
    kcol = D_ATTN // D_KV
    vcol = kcol + 1
    cur = lambda col: (lambda n, b: (n * nb + b, col))
    prev = lambda col: (lambda n, b: (jnp.maximum((n * nb + b) * n_sub - 1, 0), col))
    in_specs = [
        pl.BlockSpec(memory_space=pltpu.SMEM),
        pl.BlockSpec((L * n_sub, D_ATTN), cur(0)),
        pl.BlockSpec((L * n_sub, D_KV), cur(kcol)),
        pl.BlockSpec((L, D_KV), prev(kcol)),
        pl.BlockSpec((L * n_sub, D_KV), cur(vcol)),
        pl.BlockSpec((L, D_KV), prev(vcol)),
        pl.BlockSpec((1, D_ATTN), lambda n, b: (0, 0)),
    ]
    out_specs = [pl.BlockSpec((L * n_sub, D_ATTN), cur(0))]
    out_shape = [jax.ShapeDtypeStruct((n_seq * seq, D_ATTN), BF16)]
    args = [sinks, z, z, z, z, z, g_attn]
    for w_all, layer in cast:
        i_spec, o_spec, o_shape = _cast_specs(w_all, layer, n_seq * nb, lambda n, b: n * nb + b)
        in_specs.append(i_spec)
        out_specs.append(o_spec)
        out_shape.append(o_shape)
        args.append(w_all)
    outs = pl.pallas_call(
        functools.partial(_attn_prompt_kernel, n_sub=n_sub, n_cast=len(cast)),
        grid=(n_seq, nb),
        in_specs=in_specs,
        out_specs=out_specs,
        out_shape=out_shape,
        compiler_params=_cparams(("parallel", "arbitrary")),
        name="attn_prompt",
    )(*args)
    return outs[0] if not cast else outs


def _lru_gates(xc, wg_ref, ba, bi, lam):
    g = jnp.dot(xc.astype(BF16), wg_ref[...], preferred_element_type=F32)
    r = jax.nn.sigmoid(g[:, :D_LRU] + ba)
    gi = jax.nn.sigmoid(g[:, D_LRU:] + bi)
    nl = -lam
    softplus = jnp.maximum(nl, 0.0) + jnp.log1p(jnp.exp(-jnp.abs(nl)))
    log_a = -LRU_C * r * softplus
    a = jnp.exp(log_a)
    th = jnp.tanh(log_a)
    m2 = (-2.0 * th) / (1.0 - th)
    mult = jnp.where(m2 > 0.0, m2 * lax.rsqrt(m2), 0.0)
    return a, mult, gi


def _shift_rows(u, prev8, k):
    r = pltpu.roll(u, k, 0)
    pr = pltpu.roll(prev8, k, 0)
    row8 = lax.broadcasted_iota(jnp.int32, prev8.shape, 0)
    head = jnp.where(row8 < k, pr, r[:SUBLANES])
    return jnp.concatenate([head, r[SUBLANES:]], axis=0)


def _chunk_scan(a, b):
    n = a.shape[0]
    row = lax.broadcasted_iota(jnp.int32, a.shape, 0)
    d = 1
    while d < n:
        if d < SUBLANES:
            keep = row >= d
            b = jnp.where(keep, b + a * pltpu.roll(b, d, 0), b)
            a = jnp.where(keep, a * pltpu.roll(a, d, 0), a)
        else:
            b = jnp.concatenate([b[:d], b[d:] + a[d:] * b[:n - d]], axis=0)
            a = jnp.concatenate([a[:d], a[d:] * a[:n - d]], axis=0)
        d *= 2
    return a, b


def _scan_pitch(ln):
    assert ln % SUBLANES == 0
    return ln if ln % (2 * SUBLANES) == SUBLANES else ln + SUBLANES


def _strided_scan(a, b, h_prev, a_scr, b_scr):
    tc, ch = a.shape
    ln = tc // SUBLANES
    pitch = _scan_pitch(ln)
    nslab = ch // LANES
    for s in range(SUBLANES):
        for c in range(nslab):
            a_scr[c, pitch * s:pitch * s + ln, :] = a[ln * s:ln * (s + 1), c * LANES:(c + 1) * LANES]
            b_scr[c, pitch * s:pitch * s + ln, :] = b[ln * s:ln * (s + 1), c * LANES:(c + 1) * LANES]
    row8 = lax.broadcasted_iota(jnp.int32, (SUBLANES, LANES), 0)
    for c in range(nslab):
        h = jnp.zeros((SUBLANES, LANES), F32)
        acum = jnp.ones((SUBLANES, LANES), F32)
        for j in range(ln):
            idx = pl.ds(j, SUBLANES, stride=pitch)
            at = a_scr[c, idx, :]
            h = at * h + b_scr[c, idx, :]
            acum = at * acum
            b_scr[c, idx, :] = h
            a_scr[c, idx, :] = acum
        a_tot, b_tot = _chunk_scan(acum, h)
        hp = h_prev[:, c * LANES:(c + 1) * LANES]
        cin = jnp.where(row8 == 0, hp, pltpu.roll(b_tot + a_tot * hp, 1, 0))
        for j in range(ln):
            idx = pl.ds(j, SUBLANES, stride=pitch)
            b_scr[c, idx, :] = b_scr[c, idx, :] + a_scr[c, idx, :] * cin
    return jnp.concatenate(
        [jnp.concatenate([b_scr[c, pitch * s:pitch * s + ln, :] for c in range(nslab)], axis=1)
         for s in range(SUBLANES)], axis=0)


def _lru_sc_kernel(ux_ref, gate_ref, ub_ref, uc_ref, uh_ref, cw_ref, cb_ref, wg_ref, ba_ref, bi_ref,
                   lam_ref, scw_ref, glru_ref, gsc_ref, *rest, tc, n_cast):
    o_ref, h8_ref, x8_ref, g8_ref = rest[n_cast:n_cast + 4]
    cx_scr, cg_scr, ch_scr, sa_scr, sb_scr = rest[2 * n_cast + 4:]
    for src, dst in zip(rest[:n_cast], rest[n_cast + 4:2 * n_cast + 4]):
        dst[...] = src[...].astype(BF16)
    t = pl.program_id(1)

    @pl.when(t == 0)
    def _():
        cx_scr[...] = jnp.zeros_like(cx_scr)
        cg_scr[...] = jnp.zeros_like(cg_scr)
        ch_scr[...] = jnp.zeros_like(ch_scr)

    wrefs = (cw_ref, cb_ref, wg_ref, ba_ref, bi_ref, lam_ref, scw_ref, glru_ref, gsc_ref)
    lru_n, sc_n, h8, x8, g8 = _lru_sc_rows(ux_ref[...], gate_ref[...], ub_ref[...], uc_ref[...], uh_ref[...],
                                           t * tc, wrefs, (cx_scr, cg_scr, ch_scr), (sa_scr, sb_scr))
    o_ref[:, :D_LRU] = lru_n.astype(o_ref.dtype)
    o_ref[:, D_LRU:] = sc_n.astype(o_ref.dtype)
    h8_ref[0] = h8
    x8_ref[0] = x8
    g8_ref[0] = g8


def _lru_sc_rows(ux, gate, ub, uc, uh, pos0, wrefs, carries, scan_scr):
    cw_ref, cb_ref, wg_ref, ba_ref, bi_ref, lam_ref, scw_ref, glru_ref, gsc_ref = wrefs
    cx_scr, cg_scr, ch_scr = carries
    tc = ux.shape[0]
    px = cx_scr[...]
    xc = _shift_rows(ux, px, 3) * cw_ref[0:1, :]
    xc = xc + _shift_rows(ux, px, 2) * cw_ref[1:2, :]
    xc = xc + _shift_rows(ux, px, 1) * cw_ref[2:3, :]
    xc = xc + ux * cw_ref[3:4, :]
    xc = xc + cb_ref[...]

    a, mult, gi = _lru_gates(xc, wg_ref, ba_ref[...], bi_ref[...], lam_ref[...])
    pos = pos0 + lax.broadcasted_iota(jnp.int32, (tc, 1), 0)
    mult = jnp.where(pos == 0, 1.0, mult)
    h = _strided_scan(a, mult * gi * xc, ch_scr[SUBLANES - 1:SUBLANES, :], *scan_scr)
    o_lru = h * jax.nn.gelu(gate, approximate=True)

    gch = uc * uh
    pg = cg_scr[...]
    y = _shift_rows(gch, pg, 2) * scw_ref[0:1, :]
    y = y + _shift_rows(gch, pg, 1) * scw_ref[1:2, :]
    y = y + gch * scw_ref[2:3, :]
    o_sc = ub * y

    h8, x8, g8 = h[tc - SUBLANES:], ux[tc - SUBLANES:], gch[tc - SUBLANES:]
    cx_scr[...] = x8
    cg_scr[...] = g8
    ch_scr[...] = h8
    return _rms(o_lru, glru_ref[...]), _rms(o_sc, gsc_ref[...]), h8, x8, g8


_LRU_CHUNK = WINDOW * _ATTN_SUB_BLOCKS


def _lru_sc_prompt(z, lw, *, n_seq, seq, cast=()):
    tc = _LRU_CHUNK
    nt = seq // tc
    zcol = lambda col: pl.BlockSpec((tc, D_LRU), lambda n, t: (n * nt + t, col))
    const = lambda shape: pl.BlockSpec(shape, lambda n, t: (0,) * len(shape))
    st = pl.BlockSpec((1, SUBLANES, D_LRU), lambda n, t: (n, 0, 0))
    st_shape = jax.ShapeDtypeStruct((n_seq, SUBLANES, D_LRU), F32)
    in_specs = [zcol(_COL_UX), zcol(_COL_GATE), zcol(_COL_B), zcol(_COL_C), zcol(_COL_H),
                const((LRU_CONV_W, D_LRU)), const((1, D_LRU)), const((D_LRU, 2 * D_LRU)),
                const((1, D_LRU)), const((1, D_LRU)), const((1, D_LRU)),
                const((SC_CONV_W, D_SC)), const((1, D_LRU)), const((1, D_SC))]
    out_specs = [pl.BlockSpec((tc, D_LRU + D_SC), lambda n, t: (n * nt + t, 0)), st, st, st]
    out_shape = [jax.ShapeDtypeStruct((n_seq * seq, D_LRU + D_SC), BF16), st_shape, st_shape, st_shape]
    args = [z, z, z, z, z, lw["conv_w"], lw["conv_b"], lw["w_gates"], lw["b_a"], lw["b_i"], lw["lam"],
            lw["sc_w"], lw["g_lru"], lw["g_sc"]]
    for w_all, layer in cast:
        i_spec, o_spec, o_shape = _cast_specs(w_all, layer, n_seq * nt, lambda n, t: n * nt + t)
        in_specs.append(i_spec)
        out_specs.append(o_spec)
        out_shape.append(o_shape)
        args.append(w_all)
    return pl.pallas_call(
        functools.partial(_lru_sc_kernel, tc=tc, n_cast=len(cast)),
        grid=(n_seq, nt),
        in_specs=in_specs,
        out_specs=out_specs,
        out_shape=out_shape,
        scratch_shapes=[pltpu.VMEM((SUBLANES, D_LRU), F32)] * 3
        + [pltpu.VMEM((D_LRU // LANES, SUBLANES * _scan_pitch(tc // SUBLANES), LANES), F32)] * 2,
        compiler_params=_cparams(("parallel", "arbitrary")),
        name="lru_sc_prompt",
    )(*args)


_D_QKV = D_ATTN + 2 * D_KV
_D_U = D_IN - _D_QKV


def _proj_lru_kernel(x_ref, g_ref, w_ref, cos_ref, sin_ref, cw_ref, cb_ref, wg_ref, ba_ref, bi_ref,
                     lam_ref, scw_ref, glru_ref, gsc_ref, *rest, blocks_per_seq, n_cast):
    qkv_ref, mb_ref, h8_ref, x8_ref, g8_ref = rest[n_cast:n_cast + 5]
    zu_scr, cx_scr, cg_scr, ch_scr, sa_scr, sb_scr = rest[2 * n_cast + 5:]
    for src, dst in zip(rest[:n_cast], rest[n_cast + 5:2 * n_cast + 5]):
        dst[...] = src[...].astype(BF16)
    i = pl.program_id(0)
    blk = i % blocks_per_seq

    @pl.when(blk == 0)
    def _():
        cx_scr[...] = jnp.zeros_like(cx_scr)
        cg_scr[...] = jnp.zeros_like(cg_scr)
        ch_scr[...] = jnp.zeros_like(ch_scr)

    h = _rms(x_ref[...], g_ref[...]).astype(BF16)
    bm = h.shape[0]
    for c0 in range(_D_QKV, D_IN, _IN_PROJ_CHUNK):
        zu_scr[:, c0 - _D_QKV:c0 - _D_QKV + _IN_PROJ_CHUNK] = jnp.dot(
            h, w_ref[:, c0:c0 + _IN_PROJ_CHUNK], preferred_element_type=F32)

    wrefs = (cw_ref, cb_ref, wg_ref, ba_ref, bi_ref, lam_ref, scw_ref, glru_ref, gsc_ref)
    tc = _LRU_CHUNK
    col = lambda k: slice(k * D_LRU, (k + 1) * D_LRU)
    for r0 in range(0, bm, tc):
        rows = slice(r0, r0 + tc)
        lru_n, sc_n, h8, x8, g8 = _lru_sc_rows(
            zu_scr[rows, col(0)], zu_scr[rows, col(1)], zu_scr[rows, col(2)], zu_scr[rows, col(3)],
            zu_scr[rows, col(4)], blk * bm + r0, wrefs, (cx_scr, cg_scr, ch_scr), (sa_scr, sb_scr))
        mb_ref[rows, :D_LRU] = lru_n.astype(mb_ref.dtype)
        mb_ref[rows, D_LRU:] = sc_n.astype(mb_ref.dtype)
    h8_ref[0] = h8
    x8_ref[0] = x8
    g8_ref[0] = g8

    lane = lax.broadcasted_iota(jnp.int32, (bm, LANES), 1)
    lo32 = (lane % HEAD_DIM) < (HEAD_DIM // 2)
    cos = cos_ref[...]
    sin = sin_ref[...]

    def rope(a):
        sw = jnp.where(lo32, pltpu.roll(a, LANES - HEAD_DIM // 2, 1), pltpu.roll(a, HEAD_DIM // 2, 1))
        return a * cos + sw * sin

    rope_cols = D_ATTN + D_KV
    for c0 in range(0, _D_QKV, _IN_PROJ_CHUNK):
        acc = jnp.dot(h, w_ref[:, c0:c0 + _IN_PROJ_CHUNK], preferred_element_type=F32)
        for c in range(0, _IN_PROJ_CHUNK, LANES):
            a = acc[:, c:c + LANES]
            qkv_ref[:, c0 + c:c0 + c + LANES] = rope(a) if c0 + c < rope_cols else a


def _proj_lru(x, g_all, w_all_bf, cos_t, sin_t, lw, *, layer, w_layer, n_seq, seq, bm, cast=()):
    m = x.shape[0]
    n_tab = cos_t.shape[0] // bm
    bps = seq // bm
    const = lambda shape: pl.BlockSpec(shape, lambda i: (0,) * len(shape))
    st = pl.BlockSpec((1, SUBLANES, D_LRU), lambda i: (i // bps, 0, 0))
    st_shape = jax.ShapeDtypeStruct((n_seq, SUBLANES, D_LRU), F32)
    in_specs = [pl.BlockSpec((bm, D_MODEL), lambda i: (i, 0)),
                pl.BlockSpec((None, 1, D_MODEL), lambda i: (layer, 0, 0)),
                _resident((None, D_MODEL, D_IN), lambda i: (w_layer, 0, 0)),
                pl.BlockSpec((bm, LANES), lambda i: (i % n_tab, 0)),
                pl.BlockSpec((bm, LANES), lambda i: (i % n_tab, 0)),
                const((LRU_CONV_W, D_LRU)), const((1, D_LRU)), const((D_LRU, 2 * D_LRU)),
                const((1, D_LRU)), const((1, D_LRU)), const((1, D_LRU)),
                const((SC_CONV_W, D_SC)), const((1, D_LRU)), const((1, D_SC))]
    out_specs = [pl.BlockSpec((bm, _D_QKV), lambda i: (i, 0)),
                 pl.BlockSpec((bm, D_LRU + D_SC), lambda i: (i, 0)), st, st, st]
    out_shape = [jax.ShapeDtypeStruct((m, _D_QKV), F32),
                 jax.ShapeDtypeStruct((m, D_LRU + D_SC), BF16), st_shape, st_shape, st_shape]
    args = [x, g_all, w_all_bf, cos_t, sin_t, lw["conv_w"], lw["conv_b"], lw["w_gates"], lw["b_a"],
            lw["b_i"], lw["lam"], lw["sc_w"], lw["g_lru"], lw["g_sc"]]
    for w_all, lyr in cast:
        i_spec, o_spec, o_shape = _cast_specs(w_all, lyr, m // bm, lambda i: i)
        in_specs.append(i_spec)
        out_specs.append(o_spec)
        out_shape.append(o_shape)
        args.append(w_all)
    return pl.pallas_call(
        functools.partial(_proj_lru_kernel, blocks_per_seq=bps, n_cast=len(cast)),
        grid=(m // bm,),
        in_specs=in_specs,
        out_specs=out_specs,
        out_shape=out_shape,
        scratch_shapes=[pltpu.VMEM((bm, _D_U), F32)] + [pltpu.VMEM((SUBLANES, D_LRU), F32)] * 3
        + [pltpu.VMEM((D_LRU // LANES, SUBLANES * _scan_pitch(_LRU_CHUNK // SUBLANES), LANES), F32)] * 2,
        compiler_params=_cparams(("arbitrary",)),
        name="proj_lru",
    )(*args)


def _decode_kernel(q_ref, knt_ref, kc_ref, vc_ref, sel_ref, sink_ref, gat_ref,
                   ux_ref, gate_ref, ub_ref, uc_ref, uh_ref, h0_ref, cbuf_ref, sbuf_ref,
                   cw_ref, cb_ref, wg_ref, ba_ref, bi_ref, lam_ref, scw_ref, glru_ref, gsc_ref,
                   oat_ref, ors_ref, ko_ref, vo_ref, hn_ref, cn_ref, sn_ref, *, nbk):
    wb = kc_ref.shape[2]
    hrow = lax.broadcasted_iota(jnp.int32, (N_HEADS, D_ATTN), 0)
    hcol = lax.broadcasted_iota(jnp.int32, (N_HEADS, D_ATTN), 1) // HEAD_DIM
    own = (hrow == hcol)
    qexp = jnp.concatenate(
        [jnp.where(own, jnp.broadcast_to(q_ref[i:i + 1, :] * (HEAD_DIM ** -0.5), (N_HEADS, D_ATTN)), 0.0)
         for i in range(nbk)], axis=0)
    qrow = jnp.dot(qexp.astype(BF16), sel_ref[...], preferred_element_type=F32)
    rows = nbk * N_HEADS
    grow = (lax.broadcasted_iota(jnp.int32, (rows, D_KV), 0) % N_HEADS) // N_GROUP
    gcol = lax.broadcasted_iota(jnp.int32, (rows, D_KV), 1) // HEAD_DIM
    kvmask = (grow == gcol)
    qm = jnp.where(kvmask, qrow, 0.0).astype(BF16)
    sink = jnp.concatenate([sink_ref[...][:, 0:1]] * nbk, axis=0)
    gat = jnp.concatenate([gat_ref[...]] * nbk, axis=0)
    lanek = lax.broadcasted_iota(jnp.int32, (D_KV, wb), 1)
    newest = lanek == wb - 1
    for i in range(nbk):
        ko_ref[i] = jnp.where(newest, jnp.broadcast_to(knt_ref[:D_KV, i:i + 1], (D_KV, wb)),
                              pltpu.roll(kc_ref[i], wb - 1, 1))
        vo_ref[i] = jnp.where(newest, jnp.broadcast_to(knt_ref[D_KV:, i:i + 1], (D_KV, wb)),
                              pltpu.roll(vc_ref[i], wb - 1, 1))
    s = jnp.concatenate(
        [jnp.dot(qm[i * N_HEADS:(i + 1) * N_HEADS], ko_ref[i].astype(BF16), preferred_element_type=F32)
         for i in range(nbk)], axis=0)
    m = jnp.maximum(jnp.max(s, axis=1, keepdims=True), sink)
    p = jnp.exp(s - m)
    p = (p / (jnp.sum(p, axis=1, keepdims=True) + jnp.exp(sink - m))).astype(BF16)
    of = jnp.concatenate(
        [lax.dot_general(p[i * N_HEADS:(i + 1) * N_HEADS], vo_ref[i].astype(BF16), (((1,), (1,)), ((), ())),
                         preferred_element_type=F32) for i in range(nbk)], axis=0)
    of = jnp.where(kvmask, of, 0.0)
    t = of[:, :LANES] + of[:, LANES:]
    o = t + pltpu.roll(t, HEAD_DIM, 1)
    rs = jnp.sum(o * o, axis=1, keepdims=True)
    for i in range(nbk):
        sl = slice(i * N_HEADS, (i + 1) * N_HEADS)
        ms = jnp.sum(rs[sl], axis=0, keepdims=True) * (0.5 / D_ATTN)
        oat_ref[i] = (o[sl] * lax.rsqrt(ms + RMS_EPS) * gat[sl]).astype(oat_ref.dtype)

    ux = ux_ref[...]
    xc = cbuf_ref[0] * cw_ref[0:1, :]
    xc = xc + cbuf_ref[1] * cw_ref[1:2, :]
    xc = xc + cbuf_ref[2] * cw_ref[2:3, :]
    xc = xc + ux * cw_ref[3:4, :]
    xc = xc + cb_ref[...]
    a, mult, gi = _lru_gates(xc, wg_ref, ba_ref[...], bi_ref[...], lam_ref[...])
    h = a * h0_ref[...] + mult * gi * xc
    o_lru = h * jax.nn.gelu(gate_ref[...], approximate=True)
    hn_ref[...] = h
    cn_ref[0] = cbuf_ref[1]
    cn_ref[1] = cbuf_ref[2]
    cn_ref[2] = ux
    gch = uc_ref[...] * uh_ref[...]
    y = sbuf_ref[:, 0, :] * scw_ref[0:1, :]
    y = y + sbuf_ref[:, 1, :] * scw_ref[1:2, :]
    y = y + gch * scw_ref[2:3, :]
    o_sc = ub_ref[...] * y
    sn_ref[:, 0, :] = sbuf_ref[:, 1, :]
    sn_ref[:, 1, :] = gch
    ors_ref[:, :D_LRU] = _rms(o_lru, glru_ref[...]).astype(ors_ref.dtype)
    ors_ref[:, D_LRU:] = _rms(o_sc, gsc_ref[...]).astype(ors_ref.dtype)


def _decode_mix(z, knt, kct_all, vct_all, h0_all, cbuf_all, sbuf_all, lw, *, layer, nbk=16):
    ns = z.shape[0]
    wb = kct_all.shape[3]
    z512 = lambda col: pl.BlockSpec((nbk, D_LRU), lambda i: (i, col))
    const = lambda shape: pl.BlockSpec(shape, lambda i: (0,) * len(shape))
    cache_in = pl.BlockSpec((None, nbk, D_KV, wb), lambda i: (layer, i, 0, 0))
    cache_out = pl.BlockSpec((nbk, D_KV, wb), lambda i: (i, 0, 0))
    outs = pl.pallas_call(
        functools.partial(_decode_kernel, nbk=nbk),
        grid=(ns // nbk,),
        in_specs=[pl.BlockSpec((nbk, D_ATTN), lambda i: (i, 0)),
                  pl.BlockSpec((None, 2 * D_KV, nbk), lambda i: (i, 0, 0)),
                  cache_in, cache_in,
                  const((D_ATTN, D_KV)), const((N_HEADS, LANES)), const((N_HEADS, LANES)),
                  z512(_COL_UX), z512(_COL_GATE), z512(_COL_B), z512(_COL_C), z512(_COL_H),
                  pl.BlockSpec((None, nbk, D_LRU), lambda i: (layer, i, 0)),
                  pl.BlockSpec((None, LRU_CONV_W - 1, nbk, D_LRU), lambda i: (layer, 0, i, 0)),
                  pl.BlockSpec((None, nbk, SC_CONV_W - 1, D_SC), lambda i: (layer, i, 0, 0)),
                  const((LRU_CONV_W, D_LRU)), const((1, D_LRU)), const((D_LRU, 2 * D_LRU)),
                  const((1, D_LRU)), const((1, D_LRU)), const((1, D_LRU)),
                  const((SC_CONV_W, D_SC)), const((1, D_LRU)), const((1, D_SC))],
        out_specs=[pl.BlockSpec((nbk, N_HEADS, LANES), lambda i: (i, 0, 0)),
                   pl.BlockSpec((nbk, D_LRU + D_SC), lambda i: (i, 0)),
                   cache_out, cache_out,
                   pl.BlockSpec((nbk, D_LRU), lambda i: (i, 0)),
                   pl.BlockSpec((LRU_CONV_W - 1, nbk, D_LRU), lambda i: (0, i, 0)),
                   pl.BlockSpec((nbk, SC_CONV_W - 1, D_SC), lambda i: (i, 0, 0))],
        out_shape=[jax.ShapeDtypeStruct((ns, N_HEADS, LANES), BF16),
                   jax.ShapeDtypeStruct((ns, D_LRU + D_SC), BF16),
                   jax.ShapeDtypeStruct((ns, D_KV, wb), F32), jax.ShapeDtypeStruct((ns, D_KV, wb), F32),
                   jax.ShapeDtypeStruct((ns, D_LRU), F32),
                   jax.ShapeDtypeStruct((LRU_CONV_W - 1, ns, D_LRU), F32),
                   jax.ShapeDtypeStruct((ns, SC_CONV_W - 1, D_SC), F32)],
        compiler_params=_cparams(("parallel",)),
        name="decode_mix",
    )(z, knt, kct_all, vct_all, lw["sel"], lw["sink_tab"], lw["g_attn_tab"],
      z, z, z, z, z, h0_all, cbuf_all, sbuf_all,
      lw["conv_w"], lw["conv_b"], lw["w_gates"], lw["b_a"], lw["b_i"], lw["lam"],
      lw["sc_w"], lw["g_lru"], lw["g_sc"])
    return outs


_OUT_PROJ_CHUNK = 512


def _out_proj_kernel(ma_ref, mb_ref, x_ref, w_ref, g_ref, x1_ref, hf_ref):
    ma = ma_ref[...]
    mb = mb_ref[...]
    ssq = None
    for c0 in range(0, D_MODEL, _OUT_PROJ_CHUNK):
        cs = slice(c0, c0 + _OUT_PROJ_CHUNK)
        acc = jnp.dot(ma, w_ref[:D_ATTN, cs], preferred_element_type=F32)
        acc = acc + jnp.dot(mb, w_ref[D_ATTN:, cs], preferred_element_type=F32)
        x1 = x_ref[:, cs] + acc
        x1_ref[:, cs] = x1
        part = jnp.sum(x1 * x1, axis=-1, keepdims=True)
        ssq = part if ssq is None else ssq + part
    scale = lax.rsqrt(ssq * (1.0 / D_MODEL) + RMS_EPS)
    for c0 in range(0, D_MODEL, _OUT_PROJ_CHUNK):
        cs = slice(c0, c0 + _OUT_PROJ_CHUNK)
        hf_ref[:, cs] = (x1_ref[:, cs] * scale * g_ref[:, cs]).astype(hf_ref.dtype)


def _out_proj(ma, mb, x, w_all_bf, g_all, *, layer, w_layer, bm):
    m = x.shape[0]
    return pl.pallas_call(
        _out_proj_kernel,
        grid=(m // bm,),
        in_specs=[pl.BlockSpec((bm, D_ATTN), lambda i: (i, 0)),
                  pl.BlockSpec((bm, D_LRU + D_SC), lambda i: (i, 0)),
                  pl.BlockSpec((bm, D_MODEL), lambda i: (i, 0)),
                  _resident((None, D_MODEL, D_MODEL), lambda i: (w_layer, 0, 0)),
                  pl.BlockSpec((None, 1, D_MODEL), lambda i: (layer, 0, 0))],
        out_specs=[pl.BlockSpec((bm, D_MODEL), lambda i: (i, 0)),
                   pl.BlockSpec((bm, D_MODEL), lambda i: (i, 0))],
        out_shape=[jax.ShapeDtypeStruct((m, D_MODEL), F32), jax.ShapeDtypeStruct((m, D_MODEL), BF16)],
        compiler_params=_cparams(("parallel",)),
        name="out_proj",
    )(ma, mb, x, w_all_bf, g_all)


_FFN_DOWN_CHUNK = 512
_FFN_X1_CHUNK = 256


def _ffn_kernel(hf_ref, x1_ref, wg_ref, wu_ref, wd_ref, gfin_ref, o_ref, *, final_norm, cast_refs=()):
    f = pl.program_id(1)
    n_x1 = D_MODEL // _FFN_X1_CHUNK

    @pl.when(f == 0)
    def _():
        o_ref[...] = jnp.zeros_like(o_ref)

    hf = hf_ref[...]
    gate = jnp.dot(hf, wg_ref[...], preferred_element_type=F32)
    up = jnp.dot(hf, wu_ref[...], preferred_element_type=F32)
    hid = (gate * jax.nn.sigmoid(gate) * up).astype(BF16)
    for c0 in range(0, D_MODEL, _FFN_DOWN_CHUNK):
        cs = slice(c0, c0 + _FFN_DOWN_CHUNK)
        o_ref[:, cs] += jnp.dot(hid, wd_ref[:, cs], preferred_element_type=F32)

    for c in range(n_x1):
        @pl.when(f == c)
        def _(c=c):
            cs = slice(c * _FFN_X1_CHUNK, (c + 1) * _FFN_X1_CHUNK)
            o_ref[:, cs] += x1_ref[...]

    if final_norm:
        @pl.when(f == pl.num_programs(1) - 1)
        def _():
            o_ref[...] = _rms(o_ref[...], gfin_ref[...])

    if cast_refs:
        src, dst = cast_refs

        @pl.when(f < D_IN // _FFN_CAST_COLS)
        def _():
            dst[...] = src[...].astype(BF16)


_FFN_CAST_COLS = 512


def _ffn_kernel_cast(hf_ref, x1_ref, wg_ref, wu_ref, wd_ref, gfin_ref, wn_ref, o_ref, wn_bf_ref, *, final_norm):
    _ffn_kernel(hf_ref, x1_ref, wg_ref, wu_ref, wd_ref, gfin_ref, o_ref, final_norm=final_norm,
                cast_refs=(wn_ref, wn_bf_ref))


def _can_cast_in_ffn(m, bm, tf=512):
    n_i = m // bm
    return (D_MODEL % n_i == 0 and (D_MODEL // n_i) % (2 * SUBLANES) == 0
            and D_FF // tf >= D_IN // _FFN_CAST_COLS)


def _ffn(hf, x1, w_gu_bf, w_d_bf, g_final, *, layer, bm, tf=512, final_norm, cast_next=None):
    m = hf.shape[0]
    nf = D_FF // tf
    n_x1 = D_MODEL // _FFN_X1_CHUNK
    assert nf >= n_x1
    in_specs = [pl.BlockSpec((bm, D_MODEL), lambda i, f: (i, 0)),
                pl.BlockSpec((bm, _FFN_X1_CHUNK), lambda i, f: (i, jnp.minimum(f, n_x1 - 1))),
                pl.BlockSpec((None, D_MODEL, tf), lambda i, f: (layer, 0, f)),
                pl.BlockSpec((None, D_MODEL, tf), lambda i, f: (layer, 0, nf + f)),
                pl.BlockSpec((None, tf, D_MODEL), lambda i, f: (layer, f, 0)),
                pl.BlockSpec((1, D_MODEL), lambda i, f: (0, 0))]
    out_spec = pl.BlockSpec((bm, D_MODEL), lambda i, f: (i, 0))
    out_shape = jax.ShapeDtypeStruct((m, D_MODEL), F32)
    args = (hf, x1, w_gu_bf, w_gu_bf, w_d_bf, g_final)
    if cast_next is None:
        return pl.pallas_call(
            functools.partial(_ffn_kernel, final_norm=final_norm),
            grid=(m // bm, nf), in_specs=in_specs, out_specs=out_spec, out_shape=out_shape,
            compiler_params=_cparams(("parallel", "arbitrary")), name="ffn",
        )(*args)
    w_next, layer_next = cast_next
    rows = D_MODEL // (m // bm)
    n_cc = D_IN // _FFN_CAST_COLS
    col = lambda f: jnp.minimum(f, n_cc - 1)
    return pl.pallas_call(
        functools.partial(_ffn_kernel_cast, final_norm=final_norm),
        grid=(m // bm, nf),
        in_specs=in_specs + [pl.BlockSpec((None, rows, _FFN_CAST_COLS), lambda i, f: (layer_next, i, col(f)))],
        out_specs=[out_spec, pl.BlockSpec((None, rows, _FFN_CAST_COLS), lambda i, f: (0, i, col(f)))],
        out_shape=[out_shape, jax.ShapeDtypeStruct((1, D_MODEL, D_IN), BF16)],
        compiler_params=_cparams(("parallel", "arbitrary")), name="ffn",
    )(*args, w_next)


def _rope_tables(pos):
    half = HEAD_DIM // 2
    inv = ROPE_THETA ** (-jnp.arange(half, dtype=F32) / half)
    ang = pos.astype(F32)[:, None] * inv[None, :]
    cos, sin = jnp.cos(ang), jnp.sin(ang)
    cos_t = jnp.tile(cos, (1, LANES // half))
    sin_t = jnp.tile(jnp.concatenate([-sin, sin], axis=1), (1, LANES // HEAD_DIM))
    return cos_t, sin_t


def _block_diag(w):
    hh, blk, _ = w.shape
    eye = jnp.eye(hh, dtype=w.dtype)
    return (eye[:, None, :, None] * w[:, :, None, :]).reshape(hh * blk, hh * blk)


def kernel(x_prompt, x_sample, state_lru_h, state_lru_conv, cache_swa_k, cache_swa_v, state_sconv,
           norm_mix, w_in, norm_grp, w_out, lru_conv_w, lru_conv_b, lru_w_a, lru_b_a, lru_w_i, lru_b_i,
           lru_lambda, sc_conv_w, attn_sinks, norm_ffn, ffn_w_gu, ffn_w_down, norm_final):
    n_p, t_p, _ = x_prompt.shape
    n_s, t_s, _ = x_sample.shape
    depth = w_in.shape[0]
    wb = cache_swa_k.shape[2]
    assert t_s == 1 and wb == WINDOW and t_p % 256 == 0 and n_s % 16 == 0

    bm_p = 512 if (n_p * t_p) % 512 == 0 and t_p % 512 == 0 else 256
    bm_ffn = 1024 if (n_p * t_p) % 1024 == 0 else bm_p
    bm_s = n_s

    cos_p, sin_p = _rope_tables(jnp.arange(t_p, dtype=jnp.int32))
    cos_s, sin_s = _rope_tables(jnp.full((bm_s,), PAST_LEN, dtype=jnp.int32))

    sel = (jnp.arange(D_ATTN)[:, None] % HEAD_DIM == jnp.arange(D_KV)[None, :] % HEAD_DIM).astype(BF16)

    xp = x_prompt.reshape(n_p * t_p, D_MODEL)
    xs = x_sample.reshape(n_s, D_MODEL)
    row = lambda v: v.reshape(1, -1)
    p_states, s_states = [], []
    n_blk = (n_p * t_p) // bm_p
    piggy = (_can_cast_in_attn(n_p, t_p) and _can_cast_in_ffn(n_p * t_p, bm_ffn)
             and D_FF % n_blk == 0 and (D_FF // n_blk) % (2 * SUBLANES) == 0
             and (D_FF // n_blk) * D_MODEL * 4 <= _CAST_SLAB_BYTES)
    if piggy:
        w_in_l, l_in = w_in[:1].astype(BF16), 0
    else:
        w_in_bf = w_in.astype(BF16)
        w_out_bf = w_out.astype(BF16)
        w_gu_bf = ffn_w_gu.astype(BF16)
        w_d_bf = ffn_w_down.astype(BF16)
    nbk = 16
    kct_all = cache_swa_k.transpose(0, 1, 3, 4, 2).reshape(depth, n_s, D_KV, wb)
    vct_all = cache_swa_v.transpose(0, 1, 3, 4, 2).reshape(depth, n_s, D_KV, wb)
    cbuf_all = state_lru_conv.transpose(0, 2, 1, 3)
    g_mix = norm_mix.reshape(depth, 1, D_MODEL)
    g_ffn = norm_ffn.reshape(depth, 1, D_MODEL)
    for l in range(depth):
        g_attn, g_lru, g_sc = (norm_grp[l, :D_ATTN], norm_grp[l, D_ATTN:D_ATTN + D_LRU],
                               norm_grp[l, D_ATTN + D_LRU:])
        lw = dict(
            conv_w=lru_conv_w[l], conv_b=row(lru_conv_b[l]),
            w_gates=jnp.concatenate([_block_diag(lru_w_a[l]), _block_diag(lru_w_i[l])], axis=1).astype(BF16),
            b_a=row(lru_b_a[l]), b_i=row(lru_b_i[l]), lam=row(lru_lambda[l]),
            sc_w=sc_conv_w[l], g_lru=row(g_lru), g_sc=row(g_sc),
            sel=sel,
            sink_tab=jnp.broadcast_to(attn_sinks[l][:, None], (N_HEADS, LANES)),
            g_attn_tab=jnp.tile(g_attn.reshape(N_HEADS, HEAD_DIM), (1, LANES // HEAD_DIM)),
        )
        last = l == depth - 1

        if not piggy:
            w_in_l, w_out_l, w_gu_l, w_d_l, l_in, l_w = w_in_bf, w_out_bf, w_gu_bf, w_d_bf, l, l
        if piggy:
            qkv, mb, h8, x8, g8, w_d_l = _proj_lru(xp, g_mix, w_in_l, cos_p, sin_p, lw, layer=l, w_layer=l_in,
                                                   n_seq=n_p, seq=t_p, bm=bm_p, cast=[(ffn_w_down, l)])
            ma, w_gu_l, w_out_l = _attn_prompt(qkv, attn_sinks[l], row(g_attn), n_seq=n_p, seq=t_p,
                                               cast=[(ffn_w_gu, l), (w_out, l)])
            l_w = 0
        else:
            qkv, mb, h8, x8, g8 = _proj_lru(xp, g_mix, w_in_l, cos_p, sin_p, lw, layer=l, w_layer=l_in,
                                            n_seq=n_p, seq=t_p, bm=bm_p)
            ma = _attn_prompt(qkv, attn_sinks[l], row(g_attn), n_seq=n_p, seq=t_p)
        x1, hf = _out_proj(ma, mb, xp, w_out_l, g_ffn, layer=l, w_layer=l_w, bm=bm_p)
        w_in_cur = w_in_l
        if piggy and not last:
            xp_new, w_in_l = _ffn(hf, x1, w_gu_l, w_d_l, row(norm_final), layer=l_w, bm=bm_ffn,
                                  final_norm=last, cast_next=(w_in, l + 1))
        else:
            xp_new = _ffn(hf, x1, w_gu_l, w_d_l, row(norm_final), layer=l_w, bm=bm_ffn, final_norm=last)
        z3 = qkv.reshape(n_p, t_p, _D_QKV)
        wbp = min(WINDOW, t_p)
        p_states.append((
            h8[:, SUBLANES - 1],
            x8[:, SUBLANES - (LRU_CONV_W - 1):],
            z3[:, t_p - wbp:, D_ATTN:D_ATTN + D_KV].reshape(n_p, wbp, N_KV_HEADS, HEAD_DIM),
            z3[:, t_p - wbp:, D_ATTN + D_KV:D_ATTN + 2 * D_KV].reshape(n_p, wbp, N_KV_HEADS, HEAD_DIM),
            g8[:, SUBLANES - (SC_CONV_W - 1):],
        ))
        xp = xp_new

        zs = _in_proj(xs, g_mix, w_in_cur, cos_s, sin_s, layer=l, w_layer=l_in, bm=bm_s)
        knt = zs[:, D_ATTN:D_ATTN + 2 * D_KV].reshape(n_s // nbk, nbk, 2 * D_KV).transpose(0, 2, 1)
        oat, ors, k_new, v_new, h_new, c_new, s_new = _decode_mix(
            zs, knt, kct_all, vct_all, state_lru_h, cbuf_all, state_sconv, lw, layer=l, nbk=nbk)
        k_new = k_new.reshape(n_s, N_KV_HEADS, HEAD_DIM, wb).transpose(0, 3, 1, 2)
        v_new = v_new.reshape(n_s, N_KV_HEADS, HEAD_DIM, wb).transpose(0, 3, 1, 2)
        c_new = c_new.transpose(1, 0, 2)
        mas = oat[:, :, :HEAD_DIM].reshape(n_s, D_ATTN)
        x1s, hfs = _out_proj(mas, ors, xs, w_out_l, g_ffn, layer=l, w_layer=l_w, bm=bm_s)
        xs = _ffn(hfs, x1s, w_gu_l, w_d_l, row(norm_final), layer=l_w, bm=bm_s, final_norm=last)
        s_states.append((h_new, c_new, k_new, v_new, s_new))

    y_prompt = xp.reshape(n_p, t_p, D_MODEL)
    y_sample = xs.reshape(n_s, t_s, D_MODEL)
    stack = lambda states, k: jnp.stack([st[k] for st in states])
    return (y_prompt, y_sample,
            stack(p_states, 0), stack(p_states, 1), stack(p_states, 2), stack(p_states, 3), stack(p_states, 4),
            stack(s_states, 0), stack(s_states, 1), stack(s_states, 2), stack(s_states, 3), stack(s_states, 4))
```

```python
import functools

import jax
import jax.numpy as jnp
from jax import lax
from jax.experimental import pallas as pl
from jax.experimental.pallas import tpu as pltpu

F32 = jnp.float32
BF16 = jnp.bfloat16

D_MODEL = 2048
D_ATTN = 1024
D_LRU = 512
D_SC = 512
HEAD_DIM = 64
N_HEADS = 16
N_KV_HEADS = 4
N_GROUP = 4
D_KV = 256
WINDOW = 128
ROPE_THETA = 10000.0
N_LRU_HEADS = 8
LRU_BLK = 64
LRU_CONV_W = 4
LRU_C = 8.0
SC_CONV_W = 3
D_FF = 5632
D_IN = 4096
RMS_EPS = 1e-6
PAST_LEN = 8192

LANES = 128
SUBLANES = 8
VMEM_LIMIT_BYTES = 56 * 1024 * 1024

_COL_UX, _COL_GATE, _COL_B, _COL_C, _COL_H = 3, 4, 5, 6, 7


def _cparams(sem):
    return pltpu.CompilerParams(dimension_semantics=sem, vmem_limit_bytes=VMEM_LIMIT_BYTES)


def _rms(x, g):
    return x * lax.rsqrt(jnp.mean(x * x, axis=-1, keepdims=True) + RMS_EPS) * g


_IN_PROJ_CHUNK = 512


def _in_proj_kernel(x_ref, g_ref, w_ref, cos_ref, sin_ref, z_ref):
    h = _rms(x_ref[...], g_ref[...]).astype(BF16)
    bm = h.shape[0]
    lane = lax.broadcasted_iota(jnp.int32, (bm, LANES), 1)
    lo32 = (lane % HEAD_DIM) < (HEAD_DIM // 2)
    cos = cos_ref[...]
    sin = sin_ref[...]

    def rope(a):
        sw = jnp.where(lo32, pltpu.roll(a, LANES - HEAD_DIM // 2, 1), pltpu.roll(a, HEAD_DIM // 2, 1))
        return a * cos + sw * sin

    rope_cols = D_ATTN + D_KV
    for c0 in range(0, D_IN, _IN_PROJ_CHUNK):
        acc = jnp.dot(h, w_ref[:, c0:c0 + _IN_PROJ_CHUNK], preferred_element_type=F32)
        for c in range(0, _IN_PROJ_CHUNK, LANES):
            a = acc[:, c:c + LANES]
            z_ref[:, c0 + c:c0 + c + LANES] = rope(a) if c0 + c < rope_cols else a


def _resident(block_shape, index_map):
    return pl.BlockSpec(block_shape, index_map, pipeline_mode=pl.Buffered(1))


def _in_proj(x, g_all, w_all_bf, cos_t, sin_t, *, layer, w_layer, bm):
    m = x.shape[0]
    n_tab = cos_t.shape[0] // bm
    return pl.pallas_call(
        _in_proj_kernel,
        grid=(m // bm,),
        in_specs=[
            pl.BlockSpec((bm, D_MODEL), lambda i: (i, 0)),
            pl.BlockSpec((None, 1, D_MODEL), lambda i: (layer, 0, 0)),
            _resident((None, D_MODEL, D_IN), lambda i: (w_layer, 0, 0)),
            pl.BlockSpec((bm, LANES), lambda i: (i % n_tab, 0)),
            pl.BlockSpec((bm, LANES), lambda i: (i % n_tab, 0)),
        ],
        out_specs=pl.BlockSpec((bm, D_IN), lambda i: (i, 0)),
        out_shape=jax.ShapeDtypeStruct((m, D_IN), F32),
        compiler_params=_cparams(("parallel",)),
        name="in_proj",
    )(x, g_all, w_all_bf, cos_t, sin_t)


def _cast_specs(w_all, layer, steps, step_index):
    _, k, n = w_all.shape
    r = k // steps
    assert r * steps == k and r % (2 * SUBLANES) == 0
    return (pl.BlockSpec((None, r, n), lambda *ids: (layer, step_index(*ids), 0)),
            pl.BlockSpec((None, r, n), lambda *ids: (0, step_index(*ids), 0)),
            jax.ShapeDtypeStruct((1, k, n), BF16))


def _attn_prompt_kernel(sink_ref, q_ref, kc_ref, kp_ref, vc_ref, vp_ref, g_ref, *rest, n_sub, n_cast):
    o_ref = rest[n_cast]
    for src, dst in zip(rest[:n_cast], rest[n_cast + 1:]):
        dst[...] = src[...].astype(BF16)
    b = pl.program_id(1)
    L = WINDOW

    lane = lax.broadcasted_iota(jnp.int32, (2 * L, LANES), 1)
    lo = lane < HEAD_DIM
    row = lax.broadcasted_iota(jnp.int32, (2 * L, 1), 0)
    top = row < L

    qi = lax.broadcasted_iota(jnp.int32, (2 * L, 4 * L), 0) % L
    sj = lax.broadcasted_iota(jnp.int32, (2 * L, 4 * L), 1) % (2 * L)
    diff = L + qi - sj
    band = (diff >= 0) & (diff < WINDOW)
    bias_inner = jnp.where(band, 0.0, -jnp.inf).astype(F32)
    bias_first = jnp.where(band & ((sj >= L) | (b > 0)), 0.0, -jnp.inf).astype(F32)

    zeros = jnp.zeros((2 * L, LANES), F32)
    ones_lo = jnp.where(lo, 1.0, 0.0).astype(F32)
    ones_hi = 1.0 - ones_lo

    for sub in range(n_sub):
        rows = slice(sub * L, (sub + 1) * L)
        k_prev = kp_ref[...] if sub == 0 else kc_ref[(sub - 1) * L:sub * L, :]
        v_prev = vp_ref[...] if sub == 0 else vc_ref[(sub - 1) * L:sub * L, :]
        out = _attn_block(sink_ref, q_ref[rows, :], k_prev, kc_ref[rows, :], v_prev, vc_ref[rows, :],
                          bias_first if sub == 0 else bias_inner, lo, top, zeros, ones_lo, ones_hi)
        o_ref[rows, :] = _rms(out, g_ref[...]).astype(o_ref.dtype)


def _attn_block(sink_ref, q, k_prev, k_cur, v_prev, v_cur, bias, lo, top, zeros, ones_lo, ones_hi):
    L = WINDOW
    qb = (q * (HEAD_DIM ** -0.5)).astype(BF16)
    kk = jnp.concatenate([k_prev, k_cur], axis=0)
    vv = jnp.concatenate([v_prev, v_cur], axis=0)
    outs = []
    for kh in range(N_KV_HEADS):
        c0 = LANES * (kh // 2)
        kx = kk[:, c0:c0 + LANES]
        vx = vv[:, c0:c0 + LANES]
        kr = pltpu.roll(kx, HEAD_DIM, 1)
        vr = pltpu.roll(vx, HEAD_DIM, 1)
        if kh % 2 == 0:
            k_lo, k_hi = jnp.where(lo, kx, zeros), jnp.where(lo, zeros, kr)
            v_lo, v_hi = jnp.where(lo, vx, zeros), jnp.where(lo, zeros, vr)
        else:
            k_lo, k_hi = jnp.where(lo, kr, zeros), jnp.where(lo, zeros, kx)
            v_lo, v_hi = jnp.where(lo, vr, zeros), jnp.where(lo, zeros, vx)
        kmat = jnp.concatenate([k_lo, k_hi], axis=0).astype(BF16)
        qs = jnp.concatenate([qb[:, 2 * LANES * kh:2 * LANES * kh + LANES],
                              qb[:, 2 * LANES * kh + LANES:2 * LANES * (kh + 1)]], axis=0)
        s = lax.dot_general(qs, kmat, (((1,), (1,)), ((), ())), preferred_element_type=F32)
        s = s + bias
        sink_lo = jnp.where(top, sink_ref[4 * kh + 0], sink_ref[4 * kh + 2])
        sink_hi = jnp.where(top, sink_ref[4 * kh + 1], sink_ref[4 * kh + 3])
        m_lo = jnp.maximum(jnp.max(s[:, :2 * L], axis=1, keepdims=True), sink_lo)
        m_hi = jnp.maximum(jnp.max(s[:, 2 * L:], axis=1, keepdims=True), sink_hi)
        p = jnp.concatenate([jnp.exp(s[:, :2 * L] - m_lo), jnp.exp(s[:, 2 * L:] - m_hi)], axis=1).astype(BF16)
        vmat = jnp.concatenate([jnp.concatenate([v_lo, ones_lo], axis=1),
                                jnp.concatenate([v_hi, ones_hi], axis=1)], axis=0).astype(BF16)
        oe = jnp.dot(p, vmat, preferred_element_type=F32)
        denom = oe[:, LANES:] + jnp.where(lo, jnp.exp(sink_lo - m_lo), jnp.exp(sink_hi - m_hi))
        o = oe[:, :LANES] / denom
        outs.append(o[:L])
        outs.append(o[L:])
    return jnp.concatenate(outs, axis=1)


_ATTN_SUB_BLOCKS = 2


_CAST_SLAB_BYTES = 8 * 1024 * 1024


def _attn_steps(n_seq, seq):
    return n_seq * (seq // (WINDOW * _ATTN_SUB_BLOCKS))


def _can_cast_in_attn(n_seq, seq):
    steps = _attn_steps(n_seq, seq)
    bf16_rows = 2 * SUBLANES
    return (D_MODEL % steps == 0 and D_FF % steps == 0
            and (D_MODEL // steps) % bf16_rows == 0 and (D_FF // steps) % bf16_rows == 0
            and (D_MODEL // steps) * 2 * D_FF * 4 <= _CAST_SLAB_BYTES)


def _attn_prompt(z, sinks, g_attn, *, n_seq, seq, cast=()):
    L = WINDOW
    n_sub = _ATTN_SUB_BLOCKS
    nb = seq // (L * n_sub)
    kcol = D_ATTN // D_KV
    vcol = kcol + 1
    cur = lambda col: (lambda n, b: (n * nb + b, col))
    prev = lambda col: (lambda n, b: (jnp.maximum((n * nb + b) * n_sub - 1, 0), col))
    in_specs = [
        pl.BlockSpec(memory_space=pltpu.SMEM),
        pl.BlockSpec((L * n_sub, D_ATTN), cur(0)),
        pl.BlockSpec((L * n_sub, D_KV), cur(kcol)),
        pl.BlockSpec((L, D_KV), prev(kcol)),
        pl.BlockSpec((L * n_sub, D_KV), cur(vcol)),
        pl.BlockSpec((L, D_KV), prev(vcol)),
        pl.BlockSpec((1, D_ATTN), lambda n, b: (0, 0)),
    ]
    out_specs = [pl.BlockSpec((L * n_sub, D_ATTN), cur(0))]
    out_shape = [jax.ShapeDtypeStruct((n_seq * seq, D_ATTN), BF16)]
    args = [sinks, z, z, z, z, z, g_attn]
    for w_all, layer in cast:
        i_spec, o_spec, o_shape = _cast_specs(w_all, layer, n_seq * nb, lambda n, b: n * nb + b)
        in_specs.append(i_spec)
        out_specs.append(o_spec)
        out_shape.append(o_shape)
        args.append(w_all)
    outs = pl.pallas_call(
        functools.partial(_attn_prompt_kernel, n_sub=n_sub, n_cast=len(cast)),
        grid=(n_seq, nb),
        in_specs=in_specs,
        out_specs=out_specs,
        out_shape=out_shape,
        compiler_params=_cparams(("parallel", "arbitrary")),
        name="attn_prompt",
    )(*args)
    return outs[0] if not cast else outs


def _lru_gates(xc, wg_ref, ba, bi, lam):
    g = jnp.dot(xc.astype(BF16), wg_ref[...], preferred_element_type=F32)
    r = jax.nn.sigmoid(g[:, :D_LRU] + ba)
    gi = jax.nn.sigmoid(g[:, D_LRU:] + bi)
    nl = -lam
    softplus = jnp.maximum(nl, 0.0) + jnp.log1p(jnp.exp(-jnp.abs(nl)))
    log_a = -LRU_C * r * softplus
    a = jnp.exp(log_a)
    th = jnp.tanh(log_a)
    m2 = (-2.0 * th) / (1.0 - th)
    mult = jnp.where(m2 > 0.0, m2 * lax.rsqrt(m2), 0.0)
    return a, mult, gi


def _shift_rows(u, prev8, k):
    r = pltpu.roll(u, k, 0)
    pr = pltpu.roll(prev8, k, 0)
    row8 = lax.broadcasted_iota(jnp.int32, prev8.shape, 0)
    head = jnp.where(row8 < k, pr, r[:SUBLANES])
    return jnp.concatenate([head, r[SUBLANES:]], axis=0)


def _chunk_scan(a, b):
    n = a.shape[0]
    row = lax.broadcasted_iota(jnp.int32, a.shape, 0)
    d = 1
    while d < n:
        if d < SUBLANES:
            keep = row >= d
            b = jnp.where(keep, b + a * pltpu.roll(b, d, 0), b)
            a = jnp.where(keep, a * pltpu.roll(a, d, 0), a)
        else:
            b = jnp.concatenate([b[:d], b[d:] + a[d:] * b[:n - d]], axis=0)
            a = jnp.concatenate([a[:d], a[d:] * a[:n - d]], axis=0)
        d *= 2
    return a, b


def _scan_pitch(ln):
    assert ln % SUBLANES == 0
    return ln if ln % (2 * SUBLANES) == SUBLANES else ln + SUBLANES


def _strided_scan(a, b, h_prev, a_scr, b_scr):
    tc, ch = a.shape
    ln = tc // SUBLANES
    pitch = _scan_pitch(ln)
    nslab = ch // LANES
    for s in range(SUBLANES):
        for c in range(nslab):
            a_scr[c, pitch * s:pitch * s + ln, :] = a[ln * s:ln * (s + 1), c * LANES:(c + 1) * LANES]
            b_scr[c, pitch * s:pitch * s + ln, :] = b[ln * s:ln * (s + 1), c * LANES:(c + 1) * LANES]
    row8 = lax.broadcasted_iota(jnp.int32, (SUBLANES, LANES), 0)
    for c in range(nslab):
        h = jnp.zeros((SUBLANES, LANES), F32)
        acum = jnp.ones((SUBLANES, LANES), F32)
        for j in range(ln):
            idx = pl.ds(j, SUBLANES, stride=pitch)
            at = a_scr[c, idx, :]
            h = at * h + b_scr[c, idx, :]
            acum = at * acum
            b_scr[c, idx, :] = h
            a_scr[c, idx, :] = acum
        a_tot, b_tot = _chunk_scan(acum, h)
        hp = h_prev[:, c * LANES:(c + 1) * LANES]
        cin = jnp.where(row8 == 0, hp, pltpu.roll(b_tot + a_tot * hp, 1, 0))
        for j in range(ln):
            idx = pl.ds(j, SUBLANES, stride=pitch)
            b_scr[c, idx, :] = b_scr[c, idx, :] + a_scr[c, idx, :] * cin
    return jnp.concatenate(
        [jnp.concatenate([b_scr[c, pitch * s:pitch * s + ln, :] for c in range(nslab)], axis=1)
         for s in range(SUBLANES)], axis=0)


def _lru_sc_kernel(ux_ref, gate_ref, ub_ref, uc_ref, uh_ref, cw_ref, cb_ref, wg_ref, ba_ref, bi_ref,
                   lam_ref, scw_ref, glru_ref, gsc_ref, *rest, tc, n_cast):
    o_ref, h8_ref, x8_ref, g8_ref = rest[n_cast:n_cast + 4]
    cx_scr, cg_scr, ch_scr, sa_scr, sb_scr = rest[2 * n_cast + 4:]
    for src, dst in zip(rest[:n_cast], rest[n_cast + 4:2 * n_cast + 4]):
        dst[...] = src[...].astype(BF16)
    t = pl.program_id(1)

    @pl.when(t == 0)
    def _():
        cx_scr[...] = jnp.zeros_like(cx_scr)
        cg_scr[...] = jnp.zeros_like(cg_scr)
        ch_scr[...] = jnp.zeros_like(ch_scr)

    wrefs = (cw_ref, cb_ref, wg_ref, ba_ref, bi_ref, lam_ref, scw_ref, glru_ref, gsc_ref)
    lru_n, sc_n, h8, x8, g8 = _lru_sc_rows(ux_ref[...], gate_ref[...], ub_ref[...], uc_ref[...], uh_ref[...],
                                           t * tc, wrefs, (cx_scr, cg_scr, ch_scr), (sa_scr, sb_scr))
    o_ref[:, :D_LRU] = lru_n.astype(o_ref.dtype)
    o_ref[:, D_LRU:] = sc_n.astype(o_ref.dtype)
    h8_ref[0] = h8
    x8_ref[0] = x8
    g8_ref[0] = g8


def _lru_sc_rows(ux, gate, ub, uc, uh, pos0, wrefs, carries, scan_scr):
    cw_ref, cb_ref, wg_ref, ba_ref, bi_ref, lam_ref, scw_ref, glru_ref, gsc_ref = wrefs
    cx_scr, cg_scr, ch_scr = carries
    tc = ux.shape[0]
    px = cx_scr[...]
    xc = _shift_rows(ux, px, 3) * cw_ref[0:1, :]
    xc = xc + _shift_rows(ux, px, 2) * cw_ref[1:2, :]
    xc = xc + _shift_rows(ux, px, 1) * cw_ref[2:3, :]
    xc = xc + ux * cw_ref[3:4, :]
    xc = xc + cb_ref[...]

    a, mult, gi = _lru_gates(xc, wg_ref, ba_ref[...], bi_ref[...], lam_ref[...])
    pos = pos0 + lax.broadcasted_iota(jnp.int32, (tc, 1), 0)
    mult = jnp.where(pos == 0, 1.0, mult)
    h = _strided_scan(a, mult * gi * xc, ch_scr[SUBLANES - 1:SUBLANES, :], *scan_scr)
    o_lru = h * jax.nn.gelu(gate, approximate=True)

    gch = uc * uh
    pg = cg_scr[...]
    y = _shift_rows(gch, pg, 2) * scw_ref[0:1, :]
    y = y + _shift_rows(gch, pg, 1) * scw_ref[1:2, :]
    y = y + gch * scw_ref[2:3, :]
    o_sc = ub * y

    h8, x8, g8 = h[tc - SUBLANES:], ux[tc - SUBLANES:], gch[tc - SUBLANES:]
    cx_scr[...] = x8
    cg_scr[...] = g8
    ch_scr[...] = h8
    return _rms(o_lru, glru_ref[...]), _rms(o_sc, gsc_ref[...]), h8, x8, g8


_LRU_CHUNK = WINDOW * _ATTN_SUB_BLOCKS


def _lru_sc_prompt(z, lw, *, n_seq, seq, cast=()):
    tc = _LRU_CHUNK
    nt = seq // tc
    zcol = lambda col: pl.BlockSpec((tc, D_LRU), lambda n, t: (n * nt + t, col))
    const = lambda shape: pl.BlockSpec(shape, lambda n, t: (0,) * len(shape))
    st = pl.BlockSpec((1, SUBLANES, D_LRU), lambda n, t: (n, 0, 0))
    st_shape = jax.ShapeDtypeStruct((n_seq, SUBLANES, D_LRU), F32)
    in_specs = [zcol(_COL_UX), zcol(_COL_GATE), zcol(_COL_B), zcol(_COL_C), zcol(_COL_H),
                const((LRU_CONV_W, D_LRU)), const((1, D_LRU)), const((D_LRU, 2 * D_LRU)),
                const((1, D_LRU)), const((1, D_LRU)), const((1, D_LRU)),
                const((SC_CONV_W, D_SC)), const((1, D_LRU)), const((1, D_SC))]
    out_specs = [pl.BlockSpec((tc, D_LRU + D_SC), lambda n, t: (n * nt + t, 0)), st, st, st]
    out_shape = [jax.ShapeDtypeStruct((n_seq * seq, D_LRU + D_SC), BF16), st_shape, st_shape, st_shape]
    args = [z, z, z, z, z, lw["conv_w"], lw["conv_b"], lw["w_gates"], lw["b_a"], lw["b_i"], lw["lam"],
            lw["sc_w"], lw["g_lru"], lw["g_sc"]]
    for w_all, layer in cast:
        i_spec, o_spec, o_shape = _cast_specs(w_all, layer, n_seq * nt, lambda n, t: n * nt + t)
        in_specs.append(i_spec)
        out_specs.append(o_spec)
        out_shape.append(o_shape)
        args.append(w_all)
    return pl.pallas_call(
        functools.partial(_lru_sc_kernel, tc=tc, n_cast=len(cast)),
        grid=(n_seq, nt),
        in_specs=in_specs,
        out_specs=out_specs,
        out_shape=out_shape,
        scratch_shapes=[pltpu.VMEM((SUBLANES, D_LRU), F32)] * 3
        + [pltpu.VMEM((D_LRU // LANES, SUBLANES * _scan_pitch(tc // SUBLANES), LANES), F32)] * 2,
        compiler_params=_cparams(("parallel", "arbitrary")),
        name="lru_sc_prompt",
    )(*args)


_D_QKV = D_ATTN + 2 * D_KV
_D_U = D_IN - _D_QKV


def _proj_lru_kernel(x_ref, g_ref, w_ref, cos_ref, sin_ref, cw_ref, cb_ref, wg_ref, ba_ref, bi_ref,
                     lam_ref, scw_ref, glru_ref, gsc_ref, *rest, blocks_per_seq, n_cast):
    qkv_ref, mb_ref, h8_ref, x8_ref, g8_ref = rest[n_cast:n_cast + 5]
    zu_scr, cx_scr, cg_scr, ch_scr, sa_scr, sb_scr = rest[2 * n_cast + 5:]
    for src, dst in zip(rest[:n_cast], rest[n_cast + 5:2 * n_cast + 5]):
        dst[...] = src[...].astype(BF16)
    i = pl.program_id(0)
    blk = i % blocks_per_seq

    @pl.when(blk == 0)
    def _():
        cx_scr[...] = jnp.zeros_like(cx_scr)
        cg_scr[...] = jnp.zeros_like(cg_scr)
        ch_scr[...] = jnp.zeros_like(ch_scr)

    h = _rms(x_ref[...], g_ref[...]).astype(BF16)
    bm = h.shape[0]
    for c0 in range(_D_QKV, D_IN, _IN_PROJ_CHUNK):
        zu_scr[:, c0 - _D_QKV:c0 - _D_QKV + _IN_PROJ_CHUNK] = jnp.dot(
            h, w_ref[:, c0:c0 + _IN_PROJ_CHUNK], preferred_element_type=F32)

    wrefs = (cw_ref, cb_ref, wg_ref, ba_ref, bi_ref, lam_ref, scw_ref, glru_ref, gsc_ref)
    tc = _LRU_CHUNK
    col = lambda k: slice(k * D_LRU, (k + 1) * D_LRU)
    for r0 in range(0, bm, tc):
        rows = slice(r0, r0 + tc)
        lru_n, sc_n, h8, x8, g8 = _lru_sc_rows(
            zu_scr[rows, col(0)], zu_scr[rows, col(1)], zu_scr[rows, col(2)], zu_scr[rows, col(3)],
            zu_scr[rows, col(4)], blk * bm + r0, wrefs, (cx_scr, cg_scr, ch_scr), (sa_scr, sb_scr))
        mb_ref[rows, :D_LRU] = lru_n.astype(mb_ref.dtype)
        mb_ref[rows, D_LRU:] = sc_n.astype(mb_ref.dtype)
    h8_ref[0] = h8
    x8_ref[0] = x8
    g8_ref[0] = g8

    lane = lax.broadcasted_iota(jnp.int32, (bm, LANES), 1)
    lo32 = (lane % HEAD_DIM) < (HEAD_DIM // 2)
    cos = cos_ref[...]
    sin = sin_ref[...]

    def rope(a):
        sw = jnp.where(lo32, pltpu.roll(a, LANES - HEAD_DIM // 2, 1), pltpu.roll(a, HEAD_DIM // 2, 1))
        return a * cos + sw * sin

    rope_cols = D_ATTN + D_KV
    for c0 in range(0, _D_QKV, _IN_PROJ_CHUNK):
        acc = jnp.dot(h, w_ref[:, c0:c0 + _IN_PROJ_CHUNK], preferred_element_type=F32)
        for c in range(0, _IN_PROJ_CHUNK, LANES):
            a = acc[:, c:c + LANES]
            qkv_ref[:, c0 + c:c0 + c + LANES] = rope(a) if c0 + c < rope_cols else a


def _proj_lru(x, g_all, w_all_bf, cos_t, sin_t, lw, *, layer, w_layer, n_seq, seq, bm, cast=()):
    m = x.shape[0]
    n_tab = cos_t.shape[0] // bm
    bps = seq // bm
    const = lambda shape: pl.BlockSpec(shape, lambda i: (0,) * len(shape))
    st = pl.BlockSpec((1, SUBLANES, D_LRU), lambda i: (i // bps, 0, 0))
    st_shape = jax.ShapeDtypeStruct((n_seq, SUBLANES, D_LRU), F32)
    in_specs = [pl.BlockSpec((bm, D_MODEL), lambda i: (i, 0)),
                pl.BlockSpec((None, 1, D_MODEL), lambda i: (layer, 0, 0)),
                _resident((None, D_MODEL, D_IN), lambda i: (w_layer, 0, 0)),
                pl.BlockSpec((bm, LANES), lambda i: (i % n_tab, 0)),
                pl.BlockSpec((bm, LANES), lambda i: (i % n_tab, 0)),
                const((LRU_CONV_W, D_LRU)), const((1, D_LRU)), const((D_LRU, 2 * D_LRU)),
                const((1, D_LRU)), const((1, D_LRU)), const((1, D_LRU)),
                const((SC_CONV_W, D_SC)), const((1, D_LRU)), const((1, D_SC))]
    out_specs = [pl.BlockSpec((bm, _D_QKV), lambda i: (i, 0)),
                 pl.BlockSpec((bm, D_LRU + D_SC), lambda i: (i, 0)), st, st, st]
    out_shape = [jax.ShapeDtypeStruct((m, _D_QKV), F32),
                 jax.ShapeDtypeStruct((m, D_LRU + D_SC), BF16), st_shape, st_shape, st_shape]
    args = [x, g_all, w_all_bf, cos_t, sin_t, lw["conv_w"], lw["conv_b"], lw["w_gates"], lw["b_a"],
            lw["b_i"], lw["lam"], lw["sc_w"], lw["g_lru"], lw["g_sc"]]
    for w_all, lyr in cast:
        i_spec, o_spec, o_shape = _cast_specs(w_all, lyr, m // bm, lambda i: i)
        in_specs.append(i_spec)
        out_specs.append(o_spec)
        out_shape.append(o_shape)
        args.append(w_all)
    return pl.pallas_call(
        functools.partial(_proj_lru_kernel, blocks_per_seq=bps, n_cast=len(cast)),
        grid=(m // bm,),
        in_specs=in_specs,
        out_specs=out_specs,
        out_shape=out_shape,
        scratch_shapes=[pltpu.VMEM((bm, _D_U), F32)] + [pltpu.VMEM((SUBLANES, D_LRU), F32)] * 3
        + [pltpu.VMEM((D_LRU // LANES, SUBLANES * _scan_pitch(_LRU_CHUNK // SUBLANES), LANES), F32)] * 2,
        compiler_params=_cparams(("arbitrary",)),
        name="proj_lru",
    )(*args)


def _decode_kernel(q_ref, knt_ref, kc_ref, vc_ref, sel_ref, sink_ref, gat_ref,
                   ux_ref, gate_ref, ub_ref, uc_ref, uh_ref, h0_ref, cbuf_ref, sbuf_ref,
                   cw_ref, cb_ref, wg_ref, ba_ref, bi_ref, lam_ref, scw_ref, glru_ref, gsc_ref,
                   *rest, nbk):
    oat_ref, ors_ref, ko_ref, vo_ref, hn_ref, cn_ref, sn_ref = rest[-7:]
    if len(ko_ref.shape) == 4:
        for d in range(1, ko_ref.shape[0]):
            ko_ref[d] = jnp.zeros(ko_ref.shape[1:], ko_ref.dtype)
            vo_ref[d] = jnp.zeros(vo_ref.shape[1:], vo_ref.dtype)
        ko_ref, vo_ref = ko_ref.at[0], vo_ref.at[0]
    wb = kc_ref.shape[2]
    hrow = lax.broadcasted_iota(jnp.int32, (N_HEADS, D_ATTN), 0)
    hcol = lax.broadcasted_iota(jnp.int32, (N_HEADS, D_ATTN), 1) // HEAD_DIM
    own = (hrow == hcol)
    qexp = jnp.concatenate(
        [jnp.where(own, jnp.broadcast_to(q_ref[i:i + 1, :] * (HEAD_DIM ** -0.5), (N_HEADS, D_ATTN)), 0.0)
         for i in range(nbk)], axis=0)
    qrow = jnp.dot(qexp.astype(BF16), sel_ref[...], preferred_element_type=F32)
    rows = nbk * N_HEADS
    grow = (lax.broadcasted_iota(jnp.int32, (rows, D_KV), 0) % N_HEADS) // N_GROUP
    gcol = lax.broadcasted_iota(jnp.int32, (rows, D_KV), 1) // HEAD_DIM
    kvmask = (grow == gcol)
    qm = jnp.where(kvmask, qrow, 0.0).astype(BF16)
    sink = jnp.concatenate([sink_ref[...][:, 0:1]] * nbk, axis=0)
    gat = jnp.concatenate([gat_ref[...]] * nbk, axis=0)
    lanek = lax.broadcasted_iota(jnp.int32, (D_KV, wb), 1)
    newest = lanek == wb - 1
    for i in range(nbk):
        ko_ref[i] = jnp.where(newest, jnp.broadcast_to(knt_ref[:D_KV, i:i + 1], (D_KV, wb)),
                              pltpu.roll(kc_ref[i], wb - 1, 1))
        vo_ref[i] = jnp.where(newest, jnp.broadcast_to(knt_ref[D_KV:, i:i + 1], (D_KV, wb)),
                              pltpu.roll(vc_ref[i], wb - 1, 1))
    s = jnp.concatenate(
        [jnp.dot(qm[i * N_HEADS:(i + 1) * N_HEADS], ko_ref[i].astype(BF16), preferred_element_type=F32)
         for i in range(nbk)], axis=0)
    m = jnp.maximum(jnp.max(s, axis=1, keepdims=True), sink)
    p = jnp.exp(s - m)
    p = (p / (jnp.sum(p, axis=1, keepdims=True) + jnp.exp(sink - m))).astype(BF16)
    of = jnp.concatenate(
        [lax.dot_general(p[i * N_HEADS:(i + 1) * N_HEADS], vo_ref[i].astype(BF16), (((1,), (1,)), ((), ())),
                         preferred_element_type=F32) for i in range(nbk)], axis=0)
    of = jnp.where(kvmask, of, 0.0)
    t = of[:, :LANES] + of[:, LANES:]
    o = t + pltpu.roll(t, HEAD_DIM, 1)
    rs = jnp.sum(o * o, axis=1, keepdims=True)
    for i in range(nbk):
        sl = slice(i * N_HEADS, (i + 1) * N_HEADS)
        ms = jnp.sum(rs[sl], axis=0, keepdims=True) * (0.5 / D_ATTN)
        oat_ref[i] = (o[sl] * lax.rsqrt(ms + RMS_EPS) * gat[sl]).astype(oat_ref.dtype)

    ux = ux_ref[...]
    xc = cbuf_ref[0] * cw_ref[0:1, :]
    xc = xc + cbuf_ref[1] * cw_ref[1:2, :]
    xc = xc + cbuf_ref[2] * cw_ref[2:3, :]
    xc = xc + ux * cw_ref[3:4, :]
    xc = xc + cb_ref[...]
    a, mult, gi = _lru_gates(xc, wg_ref, ba_ref[...], bi_ref[...], lam_ref[...])
    h = a * h0_ref[...] + mult * gi * xc
    o_lru = h * jax.nn.gelu(gate_ref[...], approximate=True)
    hn_ref[...] = h
    cn_ref[0] = cbuf_ref[1]
    cn_ref[1] = cbuf_ref[2]
    cn_ref[2] = ux
    gch = uc_ref[...] * uh_ref[...]
    y = sbuf_ref[:, 0, :] * scw_ref[0:1, :]
    y = y + sbuf_ref[:, 1, :] * scw_ref[1:2, :]
    y = y + gch * scw_ref[2:3, :]
    o_sc = ub_ref[...] * y
    sn_ref[:, 0, :] = sbuf_ref[:, 1, :]
    sn_ref[:, 1, :] = gch
    ors_ref[:, :D_LRU] = _rms(o_lru, glru_ref[...]).astype(ors_ref.dtype)
    ors_ref[:, D_LRU:] = _rms(o_sc, gsc_ref[...]).astype(ors_ref.dtype)


def _decode_mix(z, knt, kct_all, vct_all, h0_all, cbuf_all, sbuf_all, lw, *, layer, nbk=16, stacked=None):
    depth, ns, _, wb = kct_all.shape
    z512 = lambda col: pl.BlockSpec((nbk, D_LRU), lambda i: (i, col))
    const = lambda shape: pl.BlockSpec(shape, lambda i: (0,) * len(shape))
    cache_in = pl.BlockSpec((None, nbk, D_KV, wb), lambda i: (layer, i, 0, 0))
    cache_out = cache_in if stacked is not None else pl.BlockSpec((depth, nbk, D_KV, wb), lambda i: (0, i, 0, 0))
    n_in = 24
    extra_specs = [] if stacked is None else [pl.BlockSpec(memory_space=pl.ANY)] * 2
    extra_args = () if stacked is None else tuple(stacked)
    aliases = {} if stacked is None else {n_in: 2, n_in + 1: 3}
    outs = pl.pallas_call(
        functools.partial(_decode_kernel, nbk=nbk),
        grid=(ns // nbk,),
        input_output_aliases=aliases,
        in_specs=[pl.BlockSpec((nbk, D_ATTN), lambda i: (i, 0)),
                  pl.BlockSpec((None, 2 * D_KV, nbk), lambda i: (i, 0, 0)),
                  cache_in, cache_in,
                  const((D_ATTN, D_KV)), const((N_HEADS, LANES)), const((N_HEADS, LANES)),
                  z512(_COL_UX), z512(_COL_GATE), z512(_COL_B), z512(_COL_C), z512(_COL_H),
                  pl.BlockSpec((None, nbk, D_LRU), lambda i: (layer, i, 0)),
                  pl.BlockSpec((None, LRU_CONV_W - 1, nbk, D_LRU), lambda i: (layer, 0, i, 0)),
                  pl.BlockSpec((None, nbk, SC_CONV_W - 1, D_SC), lambda i: (layer, i, 0, 0)),
                  const((LRU_CONV_W, D_LRU)), const((1, D_LRU)), const((D_LRU, 2 * D_LRU)),
                  const((1, D_LRU)), const((1, D_LRU)), const((1, D_LRU)),
                  const((SC_CONV_W, D_SC)), const((1, D_LRU)), const((1, D_SC))] + extra_specs,
        out_specs=[pl.BlockSpec((nbk, N_HEADS, LANES), lambda i: (i, 0, 0)),
                   pl.BlockSpec((nbk, D_LRU + D_SC), lambda i: (i, 0)),
                   cache_out, cache_out,
                   pl.BlockSpec((nbk, D_LRU), lambda i: (i, 0)),
                   pl.BlockSpec((LRU_CONV_W - 1, nbk, D_LRU), lambda i: (0, i, 0)),
                   pl.BlockSpec((nbk, SC_CONV_W - 1, D_SC), lambda i: (i, 0, 0))],
        out_shape=[jax.ShapeDtypeStruct((ns, N_HEADS, LANES), BF16),
                   jax.ShapeDtypeStruct((ns, D_LRU + D_SC), BF16),
                   jax.ShapeDtypeStruct((depth, ns, D_KV, wb), F32),
                   jax.ShapeDtypeStruct((depth, ns, D_KV, wb), F32),
                   jax.ShapeDtypeStruct((ns, D_LRU), F32),
                   jax.ShapeDtypeStruct((LRU_CONV_W - 1, ns, D_LRU), F32),
                   jax.ShapeDtypeStruct((ns, SC_CONV_W - 1, D_SC), F32)],
        compiler_params=_cparams(("parallel",)),
        name="decode_mix",
    )(z, knt, kct_all, vct_all, lw["sel"], lw["sink_tab"], lw["g_attn_tab"],
      z, z, z, z, z, h0_all, cbuf_all, sbuf_all,
      lw["conv_w"], lw["conv_b"], lw["w_gates"], lw["b_a"], lw["b_i"], lw["lam"],
      lw["sc_w"], lw["g_lru"], lw["g_sc"], *extra_args)
    return outs


_OUT_PROJ_CHUNK = 512


def _out_proj_kernel(ma_ref, mb_ref, x_ref, w_ref, g_ref, x1_ref, hf_ref):
    ma = ma_ref[...]
    mb = mb_ref[...]
    ssq = None
    for c0 in range(0, D_MODEL, _OUT_PROJ_CHUNK):
        cs = slice(c0, c0 + _OUT_PROJ_CHUNK)
        acc = jnp.dot(ma, w_ref[:D_ATTN, cs], preferred_element_type=F32)
        acc = acc + jnp.dot(mb, w_ref[D_ATTN:, cs], preferred_element_type=F32)
        x1 = x_ref[:, cs] + acc
        x1_ref[:, cs] = x1
        part = jnp.sum(x1 * x1, axis=-1, keepdims=True)
        ssq = part if ssq is None else ssq + part
    scale = lax.rsqrt(ssq * (1.0 / D_MODEL) + RMS_EPS)
    for c0 in range(0, D_MODEL, _OUT_PROJ_CHUNK):
        cs = slice(c0, c0 + _OUT_PROJ_CHUNK)
        hf_ref[:, cs] = (x1_ref[:, cs] * scale * g_ref[:, cs]).astype(hf_ref.dtype)


def _out_proj(ma, mb, x, w_all_bf, g_all, *, layer, w_layer, bm):
    m = x.shape[0]
    return pl.pallas_call(
        _out_proj_kernel,
        grid=(m // bm,),
        in_specs=[pl.BlockSpec((bm, D_ATTN), lambda i: (i, 0)),
                  pl.BlockSpec((bm, D_LRU + D_SC), lambda i: (i, 0)),
                  pl.BlockSpec((bm, D_MODEL), lambda i: (i, 0)),
                  _resident((None, D_MODEL, D_MODEL), lambda i: (w_layer, 0, 0)),
                  pl.BlockSpec((None, 1, D_MODEL), lambda i: (layer, 0, 0))],
        out_specs=[pl.BlockSpec((bm, D_MODEL), lambda i: (i, 0)),
                   pl.BlockSpec((bm, D_MODEL), lambda i: (i, 0))],
        out_shape=[jax.ShapeDtypeStruct((m, D_MODEL), F32), jax.ShapeDtypeStruct((m, D_MODEL), BF16)],
        compiler_params=_cparams(("parallel",)),
        name="out_proj",
    )(ma, mb, x, w_all_bf, g_all)


_FFN_DOWN_CHUNK = 512
_FFN_X1_CHUNK = 256


def _ffn_kernel(hf_ref, x1_ref, wg_ref, wu_ref, wd_ref, gfin_ref, o_ref, *, final_norm, cast_refs=()):
    f = pl.program_id(1)
    n_x1 = D_MODEL // _FFN_X1_CHUNK

    @pl.when(f == 0)
    def _():
        o_ref[...] = jnp.zeros_like(o_ref)

    hf = hf_ref[...]
    gate = jnp.dot(hf, wg_ref[...], preferred_element_type=F32)
    up = jnp.dot(hf, wu_ref[...], preferred_element_type=F32)
    hid = (gate * jax.nn.sigmoid(gate) * up).astype(BF16)
    for c0 in range(0, D_MODEL, _FFN_DOWN_CHUNK):
        cs = slice(c0, c0 + _FFN_DOWN_CHUNK)
        o_ref[:, cs] += jnp.dot(hid, wd_ref[:, cs], preferred_element_type=F32)

    for c in range(n_x1):
        @pl.when(f == c)
        def _(c=c):
            cs = slice(c * _FFN_X1_CHUNK, (c + 1) * _FFN_X1_CHUNK)
            o_ref[:, cs] += x1_ref[...]

    if final_norm:
        @pl.when(f == pl.num_programs(1) - 1)
        def _():
            o_ref[...] = _rms(o_ref[...], gfin_ref[...])

    if cast_refs:
        src, dst = cast_refs

        @pl.when(f < D_IN // _FFN_CAST_COLS)
        def _():
            dst[...] = src[...].astype(BF16)


_FFN_CAST_COLS = 512


def _ffn_kernel_cast(hf_ref, x1_ref, wg_ref, wu_ref, wd_ref, gfin_ref, wn_ref, o_ref, wn_bf_ref, *, final_norm):
    _ffn_kernel(hf_ref, x1_ref, wg_ref, wu_ref, wd_ref, gfin_ref, o_ref, final_norm=final_norm,
                cast_refs=(wn_ref, wn_bf_ref))


def _can_cast_in_ffn(m, bm, tf=512):
    n_i = m // bm
    return (D_MODEL % n_i == 0 and (D_MODEL // n_i) % (2 * SUBLANES) == 0
            and D_FF // tf >= D_IN // _FFN_CAST_COLS)


def _ffn(hf, x1, w_gu_bf, w_d_bf, g_final, *, layer, bm, tf=512, final_norm, cast_next=None):
    m = hf.shape[0]
    nf = D_FF // tf
    n_x1 = D_MODEL // _FFN_X1_CHUNK
    assert nf >= n_x1
    in_specs = [pl.BlockSpec((bm, D_MODEL), lambda i, f: (i, 0)),
                pl.BlockSpec((bm, _FFN_X1_CHUNK), lambda i, f: (i, jnp.minimum(f, n_x1 - 1))),
                pl.BlockSpec((None, D_MODEL, tf), lambda i, f: (layer, 0, f)),
                pl.BlockSpec((None, D_MODEL, tf), lambda i, f: (layer, 0, nf + f)),
                pl.BlockSpec((None, tf, D_MODEL), lambda i, f: (layer, f, 0)),
                pl.BlockSpec((1, D_MODEL), lambda i, f: (0, 0))]
    out_spec = pl.BlockSpec((bm, D_MODEL), lambda i, f: (i, 0))
    out_shape = jax.ShapeDtypeStruct((m, D_MODEL), F32)
    args = (hf, x1, w_gu_bf, w_gu_bf, w_d_bf, g_final)
    if cast_next is None:
        return pl.pallas_call(
            functools.partial(_ffn_kernel, final_norm=final_norm),
            grid=(m // bm, nf), in_specs=in_specs, out_specs=out_spec, out_shape=out_shape,
            compiler_params=_cparams(("parallel", "arbitrary")), name="ffn",
        )(*args)
    w_next, layer_next = cast_next
    rows = D_MODEL // (m // bm)
    n_cc = D_IN // _FFN_CAST_COLS
    col = lambda f: jnp.minimum(f, n_cc - 1)
    return pl.pallas_call(
        functools.partial(_ffn_kernel_cast, final_norm=final_norm),
        grid=(m // bm, nf),
        in_specs=in_specs + [pl.BlockSpec((None, rows, _FFN_CAST_COLS), lambda i, f: (layer_next, i, col(f)))],
        out_specs=[out_spec, pl.BlockSpec((None, rows, _FFN_CAST_COLS), lambda i, f: (0, i, col(f)))],
        out_shape=[out_shape, jax.ShapeDtypeStruct((1, D_MODEL, D_IN), BF16)],
        compiler_params=_cparams(("parallel", "arbitrary")), name="ffn",
    )(*args, w_next)


def _rope_tables(pos):
    half = HEAD_DIM // 2
    inv = ROPE_THETA ** (-jnp.arange(half, dtype=F32) / half)
    ang = pos.astype(F32)[:, None] * inv[None, :]
    cos, sin = jnp.cos(ang), jnp.sin(ang)
    cos_t = jnp.tile(cos, (1, LANES // half))
    sin_t = jnp.tile(jnp.concatenate([-sin, sin], axis=1), (1, LANES // HEAD_DIM))
    return cos_t, sin_t


def _block_diag(w):
    hh, blk, _ = w.shape
    eye = jnp.eye(hh, dtype=w.dtype)
    return (eye[:, None, :, None] * w[:, :, None, :]).reshape(hh * blk, hh * blk)


def kernel(x_prompt, x_sample, state_lru_h, state_lru_conv, cache_swa_k, cache_swa_v, state_sconv,
           norm_mix, w_in, norm_grp, w_out, lru_conv_w, lru_conv_b, lru_w_a, lru_b_a, lru_w_i, lru_b_i,
           lru_lambda, sc_conv_w, attn_sinks, norm_ffn, ffn_w_gu, ffn_w_down, norm_final):
    n_p, t_p, _ = x_prompt.shape
    n_s, t_s, _ = x_sample.shape
    depth = w_in.shape[0]
    wb = cache_swa_k.shape[2]
    assert t_s == 1 and wb == WINDOW and t_p % 256 == 0 and n_s % 16 == 0

    bm_p = 512 if (n_p * t_p) % 512 == 0 and t_p % 512 == 0 else 256
    bm_ffn = 1024 if (n_p * t_p) % 1024 == 0 else bm_p
    bm_s = n_s

    cos_p, sin_p = _rope_tables(jnp.arange(t_p, dtype=jnp.int32))
    cos_s, sin_s = _rope_tables(jnp.full((bm_s,), PAST_LEN, dtype=jnp.int32))

    sel = (jnp.arange(D_ATTN)[:, None] % HEAD_DIM == jnp.arange(D_KV)[None, :] % HEAD_DIM).astype(BF16)

    xp = x_prompt.reshape(n_p * t_p, D_MODEL)
    xs = x_sample.reshape(n_s, D_MODEL)
    row = lambda v: v.reshape(1, -1)
    p_states, s_states = [], []
    n_blk = (n_p * t_p) // bm_p
    piggy = (_can_cast_in_attn(n_p, t_p) and _can_cast_in_ffn(n_p * t_p, bm_ffn)
             and D_FF % n_blk == 0 and (D_FF // n_blk) % (2 * SUBLANES) == 0
             and (D_FF // n_blk) * D_MODEL * 4 <= _CAST_SLAB_BYTES)
    if piggy:
        w_in_l, l_in = w_in[:1].astype(BF16), 0
    else:
        w_in_bf = w_in.astype(BF16)
        w_out_bf = w_out.astype(BF16)
        w_gu_bf = ffn_w_gu.astype(BF16)
        w_d_bf = ffn_w_down.astype(BF16)
    nbk = 16
    kct_all = cache_swa_k.transpose(0, 1, 3, 4, 2).reshape(depth, n_s, D_KV, wb)
    vct_all = cache_swa_v.transpose(0, 1, 3, 4, 2).reshape(depth, n_s, D_KV, wb)
    cbuf_all = state_lru_conv.transpose(0, 2, 1, 3)
    g_mix = norm_mix.reshape(depth, 1, D_MODEL)
    g_ffn = norm_ffn.reshape(depth, 1, D_MODEL)
    for l in range(depth):
        g_attn, g_lru, g_sc = (norm_grp[l, :D_ATTN], norm_grp[l, D_ATTN:D_ATTN + D_LRU],
                               norm_grp[l, D_ATTN + D_LRU:])
        lw = dict(
            conv_w=lru_conv_w[l], conv_b=row(lru_conv_b[l]),
            w_gates=jnp.concatenate([_block_diag(lru_w_a[l]), _block_diag(lru_w_i[l])], axis=1).astype(BF16),
            b_a=row(lru_b_a[l]), b_i=row(lru_b_i[l]), lam=row(lru_lambda[l]),
            sc_w=sc_conv_w[l], g_lru=row(g_lru), g_sc=row(g_sc),
            sel=sel,
            sink_tab=jnp.broadcast_to(attn_sinks[l][:, None], (N_HEADS, LANES)),
            g_attn_tab=jnp.tile(g_attn.reshape(N_HEADS, HEAD_DIM), (1, LANES // HEAD_DIM)),
        )
        last = l == depth - 1

        if not piggy:
            w_in_l, w_out_l, w_gu_l, w_d_l, l_in, l_w = w_in_bf, w_out_bf, w_gu_bf, w_d_bf, l, l
        if piggy:
            qkv, mb, h8, x8, g8, w_d_l = _proj_lru(xp, g_mix, w_in_l, cos_p, sin_p, lw, layer=l, w_layer=l_in,
                                                   n_seq=n_p, seq=t_p, bm=bm_p, cast=[(ffn_w_down, l)])
            ma, w_gu_l, w_out_l = _attn_prompt(qkv, attn_sinks[l], row(g_attn), n_seq=n_p, seq=t_p,
                                               cast=[(ffn_w_gu, l), (w_out, l)])
            l_w = 0
        else:
            qkv, mb, h8, x8, g8 = _proj_lru(xp, g_mix, w_in_l, cos_p, sin_p, lw, layer=l, w_layer=l_in,
                                            n_seq=n_p, seq=t_p, bm=bm_p)
            ma = _attn_prompt(qkv, attn_sinks[l], row(g_attn), n_seq=n_p, seq=t_p)
        x1, hf = _out_proj(ma, mb, xp, w_out_l, g_ffn, layer=l, w_layer=l_w, bm=bm_p)
        w_in_cur = w_in_l
        if piggy and not last:
            xp_new, w_in_l = _ffn(hf, x1, w_gu_l, w_d_l, row(norm_final), layer=l_w, bm=bm_ffn,
                                  final_norm=last, cast_next=(w_in, l + 1))
        else:
            xp_new = _ffn(hf, x1, w_gu_l, w_d_l, row(norm_final), layer=l_w, bm=bm_ffn, final_norm=last)
        z3 = qkv.reshape(n_p, t_p, _D_QKV)
        wbp = min(WINDOW, t_p)
        p_states.append((
            h8[:, SUBLANES - 1],
            x8[:, SUBLANES - (LRU_CONV_W - 1):],
            z3[:, t_p - wbp:, D_ATTN:D_ATTN + D_KV].reshape(n_p, wbp, N_KV_HEADS, HEAD_DIM),
            z3[:, t_p - wbp:, D_ATTN + D_KV:D_ATTN + 2 * D_KV].reshape(n_p, wbp, N_KV_HEADS, HEAD_DIM),
            g8[:, SUBLANES - (SC_CONV_W - 1):],
        ))
        xp = xp_new

        zs = _in_proj(xs, g_mix, w_in_cur, cos_s, sin_s, layer=l, w_layer=l_in, bm=bm_s)
        knt = zs[:, D_ATTN:D_ATTN + 2 * D_KV].reshape(n_s // nbk, nbk, 2 * D_KV).transpose(0, 2, 1)
        oat, ors, k_stack, v_stack, h_new, c_new, s_new = _decode_mix(
            zs, knt, kct_all, vct_all, state_lru_h, cbuf_all, state_sconv, lw, layer=l, nbk=nbk,
            stacked=None if l == 0 else (k_stack, v_stack))
        c_new = c_new.transpose(1, 0, 2)
        mas = oat[:, :, :HEAD_DIM].reshape(n_s, D_ATTN)
        x1s, hfs = _out_proj(mas, ors, xs, w_out_l, g_ffn, layer=l, w_layer=l_w, bm=bm_s)
        xs = _ffn(hfs, x1s, w_gu_l, w_d_l, row(norm_final), layer=l_w, bm=bm_s, final_norm=last)
        s_states.append((h_new, c_new, s_new))

    y_prompt = xp.reshape(n_p, t_p, D_MODEL)
    y_sample = xs.reshape(n_s, t_s, D_MODEL)
    stack = lambda states, k: jnp.stack([st[k] for st in states])
    untranspose = lambda c: c.reshape(depth, n_s, N_KV_HEADS, HEAD_DIM, wb).transpose(0, 1, 4, 2, 3)
    return (y_prompt, y_sample,
            stack(p_states, 0), stack(p_states, 1), stack(p_states, 2), stack(p_states, 3), stack(p_states, 4),
            stack(s_states, 0), stack(s_states, 1), untranspose(k_stack), untranspose(v_stack),
            stack(s_states, 2))
```

```python
import functools

import jax
import jax.numpy as jnp
from jax import lax
from jax.experimental import pallas as pl
from jax.experimental.pallas import tpu as pltpu

F32 = jnp.float32
BF16 = jnp.bfloat16

D_MODEL = 2048
D_ATTN = 1024
D_LRU = 512
D_SC = 512
HEAD_DIM = 64
N_HEADS = 16
N_KV_HEADS = 4
N_GROUP = 4
D_KV = 256
WINDOW = 128
ROPE_THETA = 10000.0
N_LRU_HEADS = 8
LRU_BLK = 64
LRU_CONV_W = 4
LRU_C = 8.0
SC_CONV_W = 3
D_FF = 5632
D_IN = 4096
RMS_EPS = 1e-6
PAST_LEN = 8192

LANES = 128
SUBLANES = 8
VMEM_LIMIT_BYTES = 56 * 1024 * 1024

_COL_UX, _COL_GATE, _COL_B, _COL_C, _COL_H = 3, 4, 5, 6, 7


def _cparams(sem):
    return pltpu.CompilerParams(dimension_semantics=sem, vmem_limit_bytes=VMEM_LIMIT_BYTES)


def _rms(x, g):
    return x * lax.rsqrt(jnp.mean(x * x, axis=-1, keepdims=True) + RMS_EPS) * g


_IN_PROJ_CHUNK = 512


def _in_proj_kernel(x_ref, g_ref, w_ref, cos_ref, sin_ref, z_ref):
    h = _rms(x_ref[...], g_ref[...]).astype(BF16)
    bm = h.shape[0]
    lane = lax.broadcasted_iota(jnp.int32, (bm, LANES), 1)
    lo32 = (lane % HEAD_DIM) < (HEAD_DIM // 2)
    cos = cos_ref[...]
    sin = sin_ref[...]

    def rope(a):
        sw = jnp.where(lo32, pltpu.roll(a, LANES - HEAD_DIM // 2, 1), pltpu.roll(a, HEAD_DIM // 2, 1))
        return a * cos + sw * sin

    rope_cols = D_ATTN + D_KV
    for c0 in range(0, D_IN, _IN_PROJ_CHUNK):
        acc = jnp.dot(h, w_ref[:, c0:c0 + _IN_PROJ_CHUNK], preferred_element_type=F32)
        for c in range(0, _IN_PROJ_CHUNK, LANES):
            a = acc[:, c:c + LANES]
            z_ref[:, c0 + c:c0 + c + LANES] = rope(a) if c0 + c < rope_cols else a


def _resident(block_shape, index_map):
    return pl.BlockSpec(block_shape, index_map, pipeline_mode=pl.Buffered(1))


def _cast_specs(w_all, layer, steps, step_index):
    _, k, n = w_all.shape
    r = k // steps
    assert r * steps == k and r % (2 * SUBLANES) == 0
    return (pl.BlockSpec((None, r, n), lambda *ids: (layer, step_index(*ids), 0)),
            pl.BlockSpec((None, r, n), lambda *ids: (0, step_index(*ids), 0)),
            jax.ShapeDtypeStruct((1, k, n), BF16))


def _attn_prompt_kernel(sink_ref, q_ref, kc_ref, kp_ref, vc_ref, vp_ref, g_ref, *rest, n_sub, n_cast):
    o_ref = rest[n_cast]
    for src, dst in zip(rest[:n_cast], rest[n_cast + 1:]):
        dst[...] = src[...].astype(BF16)
    b = pl.program_id(1)
    L = WINDOW

    lane = lax.broadcasted_iota(jnp.int32, (2 * L, LANES), 1)
    lo = lane < HEAD_DIM
    row = lax.broadcasted_iota(jnp.int32, (2 * L, 1), 0)
    top = row < L

    qi = lax.broadcasted_iota(jnp.int32, (2 * L, 4 * L), 0) % L
    sj = lax.broadcasted_iota(jnp.int32, (2 * L, 4 * L), 1) % (2 * L)
    diff = L + qi - sj
    band = (diff >= 0) & (diff < WINDOW)
    bias_inner = jnp.where(band, 0.0, -jnp.inf).astype(F32)
    bias_first = jnp.where(band & ((sj >= L) | (b > 0)), 0.0, -jnp.inf).astype(F32)

    zeros = jnp.zeros((2 * L, LANES), F32)
    ones_lo = jnp.where(lo, 1.0, 0.0).astype(F32)
    ones_hi = 1.0 - ones_lo

    for sub in range(n_sub):
        rows = slice(sub * L, (sub + 1) * L)
        k_prev = kp_ref[...] if sub == 0 else kc_ref[(sub - 1) * L:sub * L, :]
        v_prev = vp_ref[...] if sub == 0 else vc_ref[(sub - 1) * L:sub * L, :]
        out = _attn_block(sink_ref, q_ref[rows, :], k_prev, kc_ref[rows, :], v_prev, vc_ref[rows, :],
                          bias_first if sub == 0 else bias_inner, lo, top, zeros, ones_lo, ones_hi)
        o_ref[rows, :] = _rms(out, g_ref[...]).astype(o_ref.dtype)


def _attn_block(sink_ref, q, k_prev, k_cur, v_prev, v_cur, bias, lo, top, zeros, ones_lo, ones_hi):
    L = WINDOW
    qb = (q * (HEAD_DIM ** -0.5)).astype(BF16)
    kk = jnp.concatenate([k_prev, k_cur], axis=0)
    vv = jnp.concatenate([v_prev, v_cur], axis=0)
    outs = []
    for kh in range(N_KV_HEADS):
        c0 = LANES * (kh // 2)
        kx = kk[:, c0:c0 + LANES]
        vx = vv[:, c0:c0 + LANES]
        kr = pltpu.roll(kx, HEAD_DIM, 1)
        vr = pltpu.roll(vx, HEAD_DIM, 1)
        if kh % 2 == 0:
            k_lo, k_hi = jnp.where(lo, kx, zeros), jnp.where(lo, zeros, kr)
            v_lo, v_hi = jnp.where(lo, vx, zeros), jnp.where(lo, zeros, vr)
        else:
            k_lo, k_hi = jnp.where(lo, kr, zeros), jnp.where(lo, zeros, kx)
            v_lo, v_hi = jnp.where(lo, vr, zeros), jnp.where(lo, zeros, vx)
        kmat = jnp.concatenate([k_lo, k_hi], axis=0).astype(BF16)
        qs = jnp.concatenate([qb[:, 2 * LANES * kh:2 * LANES * kh + LANES],
                              qb[:, 2 * LANES * kh + LANES:2 * LANES * (kh + 1)]], axis=0)
        s = lax.dot_general(qs, kmat, (((1,), (1,)), ((), ())), preferred_element_type=F32)
        s = s + bias
        sink_lo = jnp.where(top, sink_ref[4 * kh + 0], sink_ref[4 * kh + 2])
        sink_hi = jnp.where(top, sink_ref[4 * kh + 1], sink_ref[4 * kh + 3])
        m_lo = jnp.maximum(jnp.max(s[:, :2 * L], axis=1, keepdims=True), sink_lo)
        m_hi = jnp.maximum(jnp.max(s[:, 2 * L:], axis=1, keepdims=True), sink_hi)
        p = jnp.concatenate([jnp.exp(s[:, :2 * L] - m_lo), jnp.exp(s[:, 2 * L:] - m_hi)], axis=1).astype(BF16)
        vmat = jnp.concatenate([jnp.concatenate([v_lo, ones_lo], axis=1),
                                jnp.concatenate([v_hi, ones_hi], axis=1)], axis=0).astype(BF16)
        oe = jnp.dot(p, vmat, preferred_element_type=F32)
        denom = oe[:, LANES:] + jnp.where(lo, jnp.exp(sink_lo - m_lo), jnp.exp(sink_hi - m_hi))
        o = oe[:, :LANES] / denom
        outs.append(o[:L])
        outs.append(o[L:])
    return jnp.concatenate(outs, axis=1)


_ATTN_SUB_BLOCKS = 2


_CAST_SLAB_BYTES = 8 * 1024 * 1024


def _attn_steps(n_seq, seq):
    return n_seq * (seq // (WINDOW * _ATTN_SUB_BLOCKS))


def _can_cast_in_attn(n_seq, seq):
    steps = _attn_steps(n_seq, seq)
    bf16_rows = 2 * SUBLANES
    return (D_MODEL % steps == 0 and D_FF % steps == 0
            and (D_MODEL // steps) % bf16_rows == 0 and (D_FF // steps) % bf16_rows == 0
            and (D_MODEL // steps) * 2 * D_FF * 4 <= _CAST_SLAB_BYTES)


def _attn_prompt(z, sinks, g_attn, *, n_seq, seq, cast=()):
    L = WINDOW
    n_sub = _ATTN_SUB_BLOCKS
    nb = seq // (L * n_sub)
    kcol = D_ATTN // D_KV
    vcol = kcol + 1
    cur = lambda col: (lambda n, b: (n * nb + b, col))
    prev = lambda col: (lambda n, b: (jnp.maximum((n * nb + b) * n_sub - 1, 0), col))
    in_specs = [
        pl.BlockSpec(memory_space=pltpu.SMEM),
        pl.BlockSpec((L * n_sub, D_ATTN), cur(0)),
        pl.BlockSpec((L * n_sub, D_KV), cur(kcol)),
        pl.BlockSpec((L, D_KV), prev(kcol)),
        pl.BlockSpec((L * n_sub, D_KV), cur(vcol)),
        pl.BlockSpec((L, D_KV), prev(vcol)),
        pl.BlockSpec((1, D_ATTN), lambda n, b: (0, 0)),
    ]
    out_specs = [pl.BlockSpec((L * n_sub, D_ATTN), cur(0))]
    out_shape = [jax.ShapeDtypeStruct((n_seq * seq, D_ATTN), BF16)]
    args = [sinks, z, z, z, z, z, g_attn]
    for w_all, layer in cast:
        i_spec, o_spec, o_shape = _cast_specs(w_all, layer, n_seq * nb, lambda n, b: n * nb + b)
        in_specs.append(i_spec)
        out_specs.append(o_spec)
        out_shape.append(o_shape)
        args.append(w_all)
    outs = pl.pallas_call(
        functools.partial(_attn_prompt_kernel, n_sub=n_sub, n_cast=len(cast)),
        grid=(n_seq, nb),
        in_specs=in_specs,
        out_specs=out_specs,
        out_shape=out_shape,
        compiler_params=_cparams(("parallel", "arbitrary")),
        name="attn_prompt",
    )(*args)
    return outs[0] if not cast else outs


def _lru_gates(xc, wg_ref, ba, bi, lam):
    g = jnp.dot(xc.astype(BF16), wg_ref[...], preferred_element_type=F32)
    r = jax.nn.sigmoid(g[:, :D_LRU] + ba)
    gi = jax.nn.sigmoid(g[:, D_LRU:] + bi)
    nl = -lam
    softplus = jnp.maximum(nl, 0.0) + jnp.log1p(jnp.exp(-jnp.abs(nl)))
    log_a = -LRU_C * r * softplus
    a = jnp.exp(log_a)
    th = jnp.tanh(log_a)
    m2 = (-2.0 * th) / (1.0 - th)
    mult = jnp.where(m2 > 0.0, m2 * lax.rsqrt(m2), 0.0)
    return a, mult, gi


def _shift_rows(u, prev8, k):
    r = pltpu.roll(u, k, 0)
    pr = pltpu.roll(prev8, k, 0)
    row8 = lax.broadcasted_iota(jnp.int32, prev8.shape, 0)
    head = jnp.where(row8 < k, pr, r[:SUBLANES])
    return jnp.concatenate([head, r[SUBLANES:]], axis=0)


def _chunk_scan(a, b):
    n = a.shape[0]
    row = lax.broadcasted_iota(jnp.int32, a.shape, 0)
    d = 1
    while d < n:
        if d < SUBLANES:
            keep = row >= d
            b = jnp.where(keep, b + a * pltpu.roll(b, d, 0), b)
            a = jnp.where(keep, a * pltpu.roll(a, d, 0), a)
        else:
            b = jnp.concatenate([b[:d], b[d:] + a[d:] * b[:n - d]], axis=0)
            a = jnp.concatenate([a[:d], a[d:] * a[:n - d]], axis=0)
        d *= 2
    return a, b


def _scan_pitch(ln):
    assert ln % SUBLANES == 0
    return ln if ln % (2 * SUBLANES) == SUBLANES else ln + SUBLANES


def _strided_scan(a, b, h_prev, a_scr, b_scr):
    tc, ch = a.shape
    ln = tc // SUBLANES
    pitch = _scan_pitch(ln)
    nslab = ch // LANES
    for s in range(SUBLANES):
        for c in range(nslab):
            a_scr[c, pitch * s:pitch * s + ln, :] = a[ln * s:ln * (s + 1), c * LANES:(c + 1) * LANES]
            b_scr[c, pitch * s:pitch * s + ln, :] = b[ln * s:ln * (s + 1), c * LANES:(c + 1) * LANES]
    row8 = lax.broadcasted_iota(jnp.int32, (SUBLANES, LANES), 0)
    for c in range(nslab):
        h = jnp.zeros((SUBLANES, LANES), F32)
        acum = jnp.ones((SUBLANES, LANES), F32)
        for j in range(ln):
            idx = pl.ds(j, SUBLANES, stride=pitch)
            at = a_scr[c, idx, :]
            h = at * h + b_scr[c, idx, :]
            acum = at * acum
            b_scr[c, idx, :] = h
            a_scr[c, idx, :] = acum
        a_tot, b_tot = _chunk_scan(acum, h)
        hp = h_prev[:, c * LANES:(c + 1) * LANES]
        cin = jnp.where(row8 == 0, hp, pltpu.roll(b_tot + a_tot * hp, 1, 0))
        for j in range(ln):
            idx = pl.ds(j, SUBLANES, stride=pitch)
            b_scr[c, idx, :] = b_scr[c, idx, :] + a_scr[c, idx, :] * cin
    return jnp.concatenate(
        [jnp.concatenate([b_scr[c, pitch * s:pitch * s + ln, :] for c in range(nslab)], axis=1)
         for s in range(SUBLANES)], axis=0)


def _lru_sc_rows(ux, gate, ub, uc, uh, pos0, wrefs, carries, scan_scr):
    cw_ref, cb_ref, wg_ref, ba_ref, bi_ref, lam_ref, scw_ref, glru_ref, gsc_ref = wrefs
    cx_scr, cg_scr, ch_scr = carries
    tc = ux.shape[0]
    px = cx_scr[...]
    xc = _shift_rows(ux, px, 3) * cw_ref[0:1, :]
    xc = xc + _shift_rows(ux, px, 2) * cw_ref[1:2, :]
    xc = xc + _shift_rows(ux, px, 1) * cw_ref[2:3, :]
    xc = xc + ux * cw_ref[3:4, :]
    xc = xc + cb_ref[...]

    a, mult, gi = _lru_gates(xc, wg_ref, ba_ref[...], bi_ref[...], lam_ref[...])
    pos = pos0 + lax.broadcasted_iota(jnp.int32, (tc, 1), 0)
    mult = jnp.where(pos == 0, 1.0, mult)
    h = _strided_scan(a, mult * gi * xc, ch_scr[SUBLANES - 1:SUBLANES, :], *scan_scr)
    o_lru = h * jax.nn.gelu(gate, approximate=True)

    gch = uc * uh
    pg = cg_scr[...]
    y = _shift_rows(gch, pg, 2) * scw_ref[0:1, :]
    y = y + _shift_rows(gch, pg, 1) * scw_ref[1:2, :]
    y = y + gch * scw_ref[2:3, :]
    o_sc = ub * y

    h8, x8, g8 = h[tc - SUBLANES:], ux[tc - SUBLANES:], gch[tc - SUBLANES:]
    cx_scr[...] = x8
    cg_scr[...] = g8
    ch_scr[...] = h8
    return _rms(o_lru, glru_ref[...]), _rms(o_sc, gsc_ref[...]), h8, x8, g8


_LRU_CHUNK = WINDOW * _ATTN_SUB_BLOCKS


_D_QKV = D_ATTN + 2 * D_KV
_D_U = D_IN - _D_QKV


def _proj_lru_kernel(x_ref, g_ref, w_ref, cos_ref, sin_ref, xs_ref, coss_ref, sins_ref,
                     cw_ref, cb_ref, wg_ref, ba_ref, bi_ref,
                     lam_ref, scw_ref, glru_ref, gsc_ref, *rest, blocks_per_seq, n_cast):
    qkv_ref, mb_ref, h8_ref, x8_ref, g8_ref, zs_ref = rest[n_cast:n_cast + 6]
    zu_scr, cx_scr, cg_scr, ch_scr, sa_scr, sb_scr = rest[2 * n_cast + 6:]
    for src, dst in zip(rest[:n_cast], rest[n_cast + 6:2 * n_cast + 6]):
        dst[...] = src[...].astype(BF16)
    i = pl.program_id(0)

    @pl.when(i == pl.num_programs(0) - 1)
    def _():
        _in_proj_kernel(xs_ref, g_ref, w_ref, coss_ref, sins_ref, zs_ref)

    blk = i % blocks_per_seq

    @pl.when(blk == 0)
    def _():
        cx_scr[...] = jnp.zeros_like(cx_scr)
        cg_scr[...] = jnp.zeros_like(cg_scr)
        ch_scr[...] = jnp.zeros_like(ch_scr)

    h = _rms(x_ref[...], g_ref[...]).astype(BF16)
    bm = h.shape[0]
    for c0 in range(_D_QKV, D_IN, _IN_PROJ_CHUNK):
        zu_scr[:, c0 - _D_QKV:c0 - _D_QKV + _IN_PROJ_CHUNK] = jnp.dot(
            h, w_ref[:, c0:c0 + _IN_PROJ_CHUNK], preferred_element_type=F32)

    wrefs = (cw_ref, cb_ref, wg_ref, ba_ref, bi_ref, lam_ref, scw_ref, glru_ref, gsc_ref)
    tc = _LRU_CHUNK
    col = lambda k: slice(k * D_LRU, (k + 1) * D_LRU)
    for r0 in range(0, bm, tc):
        rows = slice(r0, r0 + tc)
        lru_n, sc_n, h8, x8, g8 = _lru_sc_rows(
            zu_scr[rows, col(0)], zu_scr[rows, col(1)], zu_scr[rows, col(2)], zu_scr[rows, col(3)],
            zu_scr[rows, col(4)], blk * bm + r0, wrefs, (cx_scr, cg_scr, ch_scr), (sa_scr, sb_scr))
        mb_ref[rows, :D_LRU] = lru_n.astype(mb_ref.dtype)
        mb_ref[rows, D_LRU:] = sc_n.astype(mb_ref.dtype)
    h8_ref[0] = h8
    x8_ref[0] = x8
    g8_ref[0] = g8

    lane = lax.broadcasted_iota(jnp.int32, (bm, LANES), 1)
    lo32 = (lane % HEAD_DIM) < (HEAD_DIM // 2)
    cos = cos_ref[...]
    sin = sin_ref[...]

    def rope(a):
        sw = jnp.where(lo32, pltpu.roll(a, LANES - HEAD_DIM // 2, 1), pltpu.roll(a, HEAD_DIM // 2, 1))
        return a * cos + sw * sin

    rope_cols = D_ATTN + D_KV
    for c0 in range(0, _D_QKV, _IN_PROJ_CHUNK):
        acc = jnp.dot(h, w_ref[:, c0:c0 + _IN_PROJ_CHUNK], preferred_element_type=F32)
        for c in range(0, _IN_PROJ_CHUNK, LANES):
            a = acc[:, c:c + LANES]
            qkv_ref[:, c0 + c:c0 + c + LANES] = rope(a) if c0 + c < rope_cols else a


def _proj_lru(x, g_all, w_all_bf, cos_t, sin_t, xs, cos_s, sin_s, lw, *, layer, w_layer, n_seq, seq, bm,
              cast=()):
    m = x.shape[0]
    ns = xs.shape[0]
    n_tab = cos_t.shape[0] // bm
    bps = seq // bm
    const = lambda shape: pl.BlockSpec(shape, lambda i: (0,) * len(shape))
    st = pl.BlockSpec((1, SUBLANES, D_LRU), lambda i: (i // bps, 0, 0))
    st_shape = jax.ShapeDtypeStruct((n_seq, SUBLANES, D_LRU), F32)
    in_specs = [pl.BlockSpec((bm, D_MODEL), lambda i: (i, 0)),
                pl.BlockSpec((None, 1, D_MODEL), lambda i: (layer, 0, 0)),
                _resident((None, D_MODEL, D_IN), lambda i: (w_layer, 0, 0)),
                pl.BlockSpec((bm, LANES), lambda i: (i % n_tab, 0)),
                pl.BlockSpec((bm, LANES), lambda i: (i % n_tab, 0)),
                const((ns, D_MODEL)), const((ns, LANES)), const((ns, LANES)),
                const((LRU_CONV_W, D_LRU)), const((1, D_LRU)), const((D_LRU, 2 * D_LRU)),
                const((1, D_LRU)), const((1, D_LRU)), const((1, D_LRU)),
                const((SC_CONV_W, D_SC)), const((1, D_LRU)), const((1, D_SC))]
    out_specs = [pl.BlockSpec((bm, _D_QKV), lambda i: (i, 0)),
                 pl.BlockSpec((bm, D_LRU + D_SC), lambda i: (i, 0)), st, st, st, const((ns, D_IN))]
    out_shape = [jax.ShapeDtypeStruct((m, _D_QKV), F32),
                 jax.ShapeDtypeStruct((m, D_LRU + D_SC), BF16), st_shape, st_shape, st_shape,
                 jax.ShapeDtypeStruct((ns, D_IN), F32)]
    args = [x, g_all, w_all_bf, cos_t, sin_t, xs, cos_s, sin_s, lw["conv_w"], lw["conv_b"], lw["w_gates"],
            lw["b_a"], lw["b_i"], lw["lam"], lw["sc_w"], lw["g_lru"], lw["g_sc"]]
    for w_all, lyr in cast:
        i_spec, o_spec, o_shape = _cast_specs(w_all, lyr, m // bm, lambda i: i)
        in_specs.append(i_spec)
        out_specs.append(o_spec)
        out_shape.append(o_shape)
        args.append(w_all)
    return pl.pallas_call(
        functools.partial(_proj_lru_kernel, blocks_per_seq=bps, n_cast=len(cast)),
        grid=(m // bm,),
        in_specs=in_specs,
        out_specs=out_specs,
        out_shape=out_shape,
        scratch_shapes=[pltpu.VMEM((bm, _D_U), F32)] + [pltpu.VMEM((SUBLANES, D_LRU), F32)] * 3
        + [pltpu.VMEM((D_LRU // LANES, SUBLANES * _scan_pitch(_LRU_CHUNK // SUBLANES), LANES), F32)] * 2,
        compiler_params=_cparams(("arbitrary",)),
        name="proj_lru",
    )(*args)


def _decode_kernel(q_ref, knt_ref, kc_ref, vc_ref, sel_ref, sink_ref, gat_ref,
                   ux_ref, gate_ref, ub_ref, uc_ref, uh_ref, h0_ref, cbuf_ref, sbuf_ref,
                   cw_ref, cb_ref, wg_ref, ba_ref, bi_ref, lam_ref, scw_ref, glru_ref, gsc_ref,
                   *rest, nbk):
    oat_ref, ors_ref, ko_ref, vo_ref, hn_ref, cn_ref, sn_ref = rest[-7:]
    if len(ko_ref.shape) == 4:
        for d in range(1, ko_ref.shape[0]):
            ko_ref[d] = jnp.zeros(ko_ref.shape[1:], ko_ref.dtype)
            vo_ref[d] = jnp.zeros(vo_ref.shape[1:], vo_ref.dtype)
        ko_ref, vo_ref = ko_ref.at[0], vo_ref.at[0]
    wb = kc_ref.shape[2]
    hrow = lax.broadcasted_iota(jnp.int32, (N_HEADS, D_ATTN), 0)
    hcol = lax.broadcasted_iota(jnp.int32, (N_HEADS, D_ATTN), 1) // HEAD_DIM
    own = (hrow == hcol)
    qexp = jnp.concatenate(
        [jnp.where(own, jnp.broadcast_to(q_ref[i:i + 1, :] * (HEAD_DIM ** -0.5), (N_HEADS, D_ATTN)), 0.0)
         for i in range(nbk)], axis=0)
    qrow = jnp.dot(qexp.astype(BF16), sel_ref[...], preferred_element_type=F32)
    rows = nbk * N_HEADS
    grow = (lax.broadcasted_iota(jnp.int32, (rows, D_KV), 0) % N_HEADS) // N_GROUP
    gcol = lax.broadcasted_iota(jnp.int32, (rows, D_KV), 1) // HEAD_DIM
    kvmask = (grow == gcol)
    qm = jnp.where(kvmask, qrow, 0.0).astype(BF16)
    sink = jnp.concatenate([sink_ref[...][:, 0:1]] * nbk, axis=0)
    gat = jnp.concatenate([gat_ref[...]] * nbk, axis=0)
    lanek = lax.broadcasted_iota(jnp.int32, (D_KV, wb), 1)
    newest = lanek == wb - 1
    for i in range(nbk):
        ko_ref[i] = jnp.where(newest, jnp.broadcast_to(knt_ref[:D_KV, i:i + 1], (D_KV, wb)),
                              pltpu.roll(kc_ref[i], wb - 1, 1))
        vo_ref[i] = jnp.where(newest, jnp.broadcast_to(knt_ref[D_KV:, i:i + 1], (D_KV, wb)),
                              pltpu.roll(vc_ref[i], wb - 1, 1))
    s = jnp.concatenate(
        [jnp.dot(qm[i * N_HEADS:(i + 1) * N_HEADS], ko_ref[i].astype(BF16), preferred_element_type=F32)
         for i in range(nbk)], axis=0)
    m = jnp.maximum(jnp.max(s, axis=1, keepdims=True), sink)
    p = jnp.exp(s - m)
    p = (p / (jnp.sum(p, axis=1, keepdims=True) + jnp.exp(sink - m))).astype(BF16)
    of = jnp.concatenate(
        [lax.dot_general(p[i * N_HEADS:(i + 1) * N_HEADS], vo_ref[i].astype(BF16), (((1,), (1,)), ((), ())),
                         preferred_element_type=F32) for i in range(nbk)], axis=0)
    of = jnp.where(kvmask, of, 0.0)
    t = of[:, :LANES] + of[:, LANES:]
    o = t + pltpu.roll(t, HEAD_DIM, 1)
    rs = jnp.sum(o * o, axis=1, keepdims=True)
    for i in range(nbk):
        sl = slice(i * N_HEADS, (i + 1) * N_HEADS)
        ms = jnp.sum(rs[sl], axis=0, keepdims=True) * (0.5 / D_ATTN)
        oat_ref[i] = (o[sl] * lax.rsqrt(ms + RMS_EPS) * gat[sl]).astype(oat_ref.dtype)

    ux = ux_ref[...]
    xc = cbuf_ref[0] * cw_ref[0:1, :]
    xc = xc + cbuf_ref[1] * cw_ref[1:2, :]
    xc = xc + cbuf_ref[2] * cw_ref[2:3, :]
    xc = xc + ux * cw_ref[3:4, :]
    xc = xc + cb_ref[...]
    a, mult, gi = _lru_gates(xc, wg_ref, ba_ref[...], bi_ref[...], lam_ref[...])
    h = a * h0_ref[...] + mult * gi * xc
    o_lru = h * jax.nn.gelu(gate_ref[...], approximate=True)
    hn_ref[...] = h
    cn_ref[0] = cbuf_ref[1]
    cn_ref[1] = cbuf_ref[2]
    cn_ref[2] = ux
    gch = uc_ref[...] * uh_ref[...]
    y = sbuf_ref[:, 0, :] * scw_ref[0:1, :]
    y = y + sbuf_ref[:, 1, :] * scw_ref[1:2, :]
    y = y + gch * scw_ref[2:3, :]
    o_sc = ub_ref[...] * y
    sn_ref[:, 0, :] = sbuf_ref[:, 1, :]
    sn_ref[:, 1, :] = gch
    ors_ref[:, :D_LRU] = _rms(o_lru, glru_ref[...]).astype(ors_ref.dtype)
    ors_ref[:, D_LRU:] = _rms(o_sc, gsc_ref[...]).astype(ors_ref.dtype)


def _decode_mix(z, knt, kct_all, vct_all, h0_all, cbuf_all, sbuf_all, lw, *, layer, nbk=16, stacked=None):
    depth, ns, _, wb = kct_all.shape
    z512 = lambda col: pl.BlockSpec((nbk, D_LRU), lambda i: (i, col))
    const = lambda shape: pl.BlockSpec(shape, lambda i: (0,) * len(shape))
    cache_in = pl.BlockSpec((None, nbk, D_KV, wb), lambda i: (layer, i, 0, 0))
    cache_out = cache_in if stacked is not None else pl.BlockSpec((depth, nbk, D_KV, wb), lambda i: (0, i, 0, 0))
    n_in = 24
    extra_specs = [] if stacked is None else [pl.BlockSpec(memory_space=pl.ANY)] * 2
    extra_args = () if stacked is None else tuple(stacked)
    aliases = {} if stacked is None else {n_in: 2, n_in + 1: 3}
    outs = pl.pallas_call(
        functools.partial(_decode_kernel, nbk=nbk),
        grid=(ns // nbk,),
        input_output_aliases=aliases,
        in_specs=[pl.BlockSpec((nbk, D_ATTN), lambda i: (i, 0)),
                  pl.BlockSpec((None, 2 * D_KV, nbk), lambda i: (i, 0, 0)),
                  cache_in, cache_in,
                  const((D_ATTN, D_KV)), const((N_HEADS, LANES)), const((N_HEADS, LANES)),
                  z512(_COL_UX), z512(_COL_GATE), z512(_COL_B), z512(_COL_C), z512(_COL_H),
                  pl.BlockSpec((None, nbk, D_LRU), lambda i: (layer, i, 0)),
                  pl.BlockSpec((None, LRU_CONV_W - 1, nbk, D_LRU), lambda i: (layer, 0, i, 0)),
                  pl.BlockSpec((None, nbk, SC_CONV_W - 1, D_SC), lambda i: (layer, i, 0, 0)),
                  const((LRU_CONV_W, D_LRU)), const((1, D_LRU)), const((D_LRU, 2 * D_LRU)),
                  const((1, D_LRU)), const((1, D_LRU)), const((1, D_LRU)),
                  const((SC_CONV_W, D_SC)), const((1, D_LRU)), const((1, D_SC))] + extra_specs,
        out_specs=[pl.BlockSpec((nbk, N_HEADS, LANES), lambda i: (i, 0, 0)),
                   pl.BlockSpec((nbk, D_LRU + D_SC), lambda i: (i, 0)),
                   cache_out, cache_out,
                   pl.BlockSpec((nbk, D_LRU), lambda i: (i, 0)),
                   pl.BlockSpec((LRU_CONV_W - 1, nbk, D_LRU), lambda i: (0, i, 0)),
                   pl.BlockSpec((nbk, SC_CONV_W - 1, D_SC), lambda i: (i, 0, 0))],
        out_shape=[jax.ShapeDtypeStruct((ns, N_HEADS, LANES), BF16),
                   jax.ShapeDtypeStruct((ns, D_LRU + D_SC), BF16),
                   jax.ShapeDtypeStruct((depth, ns, D_KV, wb), F32),
                   jax.ShapeDtypeStruct((depth, ns, D_KV, wb), F32),
                   jax.ShapeDtypeStruct((ns, D_LRU), F32),
                   jax.ShapeDtypeStruct((LRU_CONV_W - 1, ns, D_LRU), F32),
                   jax.ShapeDtypeStruct((ns, SC_CONV_W - 1, D_SC), F32)],
        compiler_params=_cparams(("parallel",)),
        name="decode_mix",
    )(z, knt, kct_all, vct_all, lw["sel"], lw["sink_tab"], lw["g_attn_tab"],
      z, z, z, z, z, h0_all, cbuf_all, sbuf_all,
      lw["conv_w"], lw["conv_b"], lw["w_gates"], lw["b_a"], lw["b_i"], lw["lam"],
      lw["sc_w"], lw["g_lru"], lw["g_sc"], *extra_args)
    return outs


_OUT_PROJ_CHUNK = 512


def _out_proj_kernel(ma_ref, mb_ref, x_ref, mas_ref, mbs_ref, xs_ref, w_ref, g_ref,
                     x1_ref, hf_ref, x1s_ref, hfs_ref):
    _out_proj_rows(ma_ref, mb_ref, x_ref, w_ref, g_ref, x1_ref, hf_ref)

    @pl.when(pl.program_id(0) == pl.num_programs(0) - 1)
    def _():
        _out_proj_rows(mas_ref, mbs_ref, xs_ref, w_ref, g_ref, x1s_ref, hfs_ref)


def _out_proj_rows(ma_ref, mb_ref, x_ref, w_ref, g_ref, x1_ref, hf_ref):
    ma = ma_ref[...]
    mb = mb_ref[...]
    ssq = None
    for c0 in range(0, D_MODEL, _OUT_PROJ_CHUNK):
        cs = slice(c0, c0 + _OUT_PROJ_CHUNK)
        acc = jnp.dot(ma, w_ref[:D_ATTN, cs], preferred_element_type=F32)
        acc = acc + jnp.dot(mb, w_ref[D_ATTN:, cs], preferred_element_type=F32)
        x1 = x_ref[:, cs] + acc
        x1_ref[:, cs] = x1
        part = jnp.sum(x1 * x1, axis=-1, keepdims=True)
        ssq = part if ssq is None else ssq + part
    scale = lax.rsqrt(ssq * (1.0 / D_MODEL) + RMS_EPS)
    for c0 in range(0, D_MODEL, _OUT_PROJ_CHUNK):
        cs = slice(c0, c0 + _OUT_PROJ_CHUNK)
        hf_ref[:, cs] = (x1_ref[:, cs] * scale * g_ref[:, cs]).astype(hf_ref.dtype)


def _out_proj(ma, mb, x, mas, mbs, xs, w_all_bf, g_all, *, layer, w_layer, bm):
    m = x.shape[0]
    ns = xs.shape[0]
    blk = lambda width: pl.BlockSpec((bm, width), lambda i: (i, 0))
    whole = lambda width: pl.BlockSpec((ns, width), lambda i: (0, 0))
    return pl.pallas_call(
        _out_proj_kernel,
        grid=(m // bm,),
        in_specs=[blk(D_ATTN), blk(D_LRU + D_SC), blk(D_MODEL),
                  whole(D_ATTN), whole(D_LRU + D_SC), whole(D_MODEL),
                  _resident((None, D_MODEL, D_MODEL), lambda i: (w_layer, 0, 0)),
                  pl.BlockSpec((None, 1, D_MODEL), lambda i: (layer, 0, 0))],
        out_specs=[blk(D_MODEL), blk(D_MODEL), whole(D_MODEL), whole(D_MODEL)],
        out_shape=[jax.ShapeDtypeStruct((m, D_MODEL), F32), jax.ShapeDtypeStruct((m, D_MODEL), BF16),
                   jax.ShapeDtypeStruct((ns, D_MODEL), F32), jax.ShapeDtypeStruct((ns, D_MODEL), BF16)],
        compiler_params=_cparams(("arbitrary",)),
        name="out_proj",
    )(ma, mb, x, mas, mbs, xs, w_all_bf, g_all)


_FFN_DOWN_CHUNK = 512
_FFN_X1_CHUNK = 256


def _ffn_kernel(hf_ref, x1_ref, hfs_ref, x1s_ref, wg_ref, wu_ref, wd_ref, gfin_ref, *rest, final_norm):
    cast_refs = (rest[0], rest[3]) if len(rest) == 4 else ()
    o_ref, os_ref = rest[-2:] if not cast_refs else rest[1:3]
    f = pl.program_id(1)
    n_f = pl.num_programs(1)
    n_x1 = D_MODEL // _FFN_X1_CHUNK

    @pl.when(pl.program_id(0) == pl.num_programs(0) - 1)
    def _():
        hfs = hfs_ref[...]
        gate_s = jnp.dot(hfs, wg_ref[...], preferred_element_type=F32)
        up_s = jnp.dot(hfs, wu_ref[...], preferred_element_type=F32)
        hid_s = (gate_s * jax.nn.sigmoid(gate_s) * up_s).astype(BF16)
        down_s = jnp.dot(hid_s, wd_ref[...], preferred_element_type=F32)

        @pl.when(f == 0)
        def _():
            os_ref[...] = down_s

        @pl.when(f > 0)
        def _():
            os_ref[...] += down_s

        @pl.when(f == n_f - 1)
        def _():
            x2s = os_ref[...] + x1s_ref[...]
            os_ref[...] = _rms(x2s, gfin_ref[...]) if final_norm else x2s

    @pl.when(f == 0)
    def _():
        o_ref[...] = jnp.zeros_like(o_ref)

    hf = hf_ref[...]
    gate = jnp.dot(hf, wg_ref[...], preferred_element_type=F32)
    up = jnp.dot(hf, wu_ref[...], preferred_element_type=F32)
    hid = (gate * jax.nn.sigmoid(gate) * up).astype(BF16)
    for c0 in range(0, D_MODEL, _FFN_DOWN_CHUNK):
        cs = slice(c0, c0 + _FFN_DOWN_CHUNK)
        o_ref[:, cs] += jnp.dot(hid, wd_ref[:, cs], preferred_element_type=F32)

    for c in range(n_x1):
        @pl.when(f == c)
        def _(c=c):
            cs = slice(c * _FFN_X1_CHUNK, (c + 1) * _FFN_X1_CHUNK)
            o_ref[:, cs] += x1_ref[...]

    if final_norm:
        @pl.when(f == pl.num_programs(1) - 1)
        def _():
            o_ref[...] = _rms(o_ref[...], gfin_ref[...])

    if cast_refs:
        src, dst = cast_refs

        @pl.when(f < D_IN // _FFN_CAST_COLS)
        def _():
            dst[...] = src[...].astype(BF16)


_FFN_CAST_COLS = 512


def _can_cast_in_ffn(m, bm, tf=512):
    n_i = m // bm
    return (D_MODEL % n_i == 0 and (D_MODEL // n_i) % (2 * SUBLANES) == 0
            and D_FF // tf >= D_IN // _FFN_CAST_COLS)


def _ffn(hf, x1, hfs, x1s, w_gu_bf, w_d_bf, g_final, *, layer, bm, tf=512, final_norm, cast_next=None):
    m = hf.shape[0]
    ns = hfs.shape[0]
    nf = D_FF // tf
    n_x1 = D_MODEL // _FFN_X1_CHUNK
    assert nf >= n_x1
    whole = lambda rows: pl.BlockSpec((rows, D_MODEL), lambda i, f: (0, 0))
    in_specs = [pl.BlockSpec((bm, D_MODEL), lambda i, f: (i, 0)),
                pl.BlockSpec((bm, _FFN_X1_CHUNK), lambda i, f: (i, jnp.minimum(f, n_x1 - 1))),
                whole(ns), whole(ns),
                pl.BlockSpec((None, D_MODEL, tf), lambda i, f: (layer, 0, f)),
                pl.BlockSpec((None, D_MODEL, tf), lambda i, f: (layer, 0, nf + f)),
                pl.BlockSpec((None, tf, D_MODEL), lambda i, f: (layer, f, 0)),
                whole(1)]
    out_specs = [pl.BlockSpec((bm, D_MODEL), lambda i, f: (i, 0)), whole(ns)]
    out_shape = [jax.ShapeDtypeStruct((m, D_MODEL), F32), jax.ShapeDtypeStruct((ns, D_MODEL), F32)]
    args = [hf, x1, hfs, x1s, w_gu_bf, w_gu_bf, w_d_bf, g_final]
    if cast_next is not None:
        w_next, layer_next = cast_next
        rows = D_MODEL // (m // bm)
        n_cc = D_IN // _FFN_CAST_COLS
        col = lambda f: jnp.minimum(f, n_cc - 1)
        in_specs.append(pl.BlockSpec((None, rows, _FFN_CAST_COLS), lambda i, f: (layer_next, i, col(f))))
        out_specs.append(pl.BlockSpec((None, rows, _FFN_CAST_COLS), lambda i, f: (0, i, col(f))))
        out_shape.append(jax.ShapeDtypeStruct((1, D_MODEL, D_IN), BF16))
        args.append(w_next)
    return pl.pallas_call(
        functools.partial(_ffn_kernel, final_norm=final_norm),
        grid=(m // bm, nf), in_specs=in_specs, out_specs=out_specs, out_shape=out_shape,
        compiler_params=_cparams(("arbitrary", "arbitrary")), name="ffn",
    )(*args)


def _rope_tables(pos):
    half = HEAD_DIM // 2
    inv = ROPE_THETA ** (-jnp.arange(half, dtype=F32) / half)
    ang = pos.astype(F32)[:, None] * inv[None, :]
    cos, sin = jnp.cos(ang), jnp.sin(ang)
    cos_t = jnp.tile(cos, (1, LANES // half))
    sin_t = jnp.tile(jnp.concatenate([-sin, sin], axis=1), (1, LANES // HEAD_DIM))
    return cos_t, sin_t


def _block_diag(w):
    hh, blk, _ = w.shape
    eye = jnp.eye(hh, dtype=w.dtype)
    return (eye[:, None, :, None] * w[:, :, None, :]).reshape(hh * blk, hh * blk)


def kernel(x_prompt, x_sample, state_lru_h, state_lru_conv, cache_swa_k, cache_swa_v, state_sconv,
           norm_mix, w_in, norm_grp, w_out, lru_conv_w, lru_conv_b, lru_w_a, lru_b_a, lru_w_i, lru_b_i,
           lru_lambda, sc_conv_w, attn_sinks, norm_ffn, ffn_w_gu, ffn_w_down, norm_final):
    n_p, t_p, _ = x_prompt.shape
    n_s, t_s, _ = x_sample.shape
    depth = w_in.shape[0]
    wb = cache_swa_k.shape[2]
    assert t_s == 1 and wb == WINDOW and t_p % 256 == 0 and n_s % 16 == 0

    bm_p = 512 if (n_p * t_p) % 512 == 0 and t_p % 512 == 0 else 256
    bm_ffn = 1024 if (n_p * t_p) % 1024 == 0 else bm_p

    cos_p, sin_p = _rope_tables(jnp.arange(t_p, dtype=jnp.int32))
    cos_s, sin_s = _rope_tables(jnp.full((n_s,), PAST_LEN, dtype=jnp.int32))

    sel = (jnp.arange(D_ATTN)[:, None] % HEAD_DIM == jnp.arange(D_KV)[None, :] % HEAD_DIM).astype(BF16)

    xp = x_prompt.reshape(n_p * t_p, D_MODEL)
    xs = x_sample.reshape(n_s, D_MODEL)
    row = lambda v: v.reshape(1, -1)
    p_states, s_states = [], []
    piggy = _can_cast_in_attn(n_p, t_p) and _can_cast_in_ffn(n_p * t_p, bm_ffn)
    if piggy:
        w_in_l, l_in = w_in[:1].astype(BF16), 0
    else:
        w_in_bf = w_in.astype(BF16)
        w_out_bf = w_out.astype(BF16)
        w_gu_bf = ffn_w_gu.astype(BF16)
        w_d_bf = ffn_w_down.astype(BF16)
    nbk = 16
    kct_all = cache_swa_k.transpose(0, 1, 3, 4, 2).reshape(depth, n_s, D_KV, wb)
    vct_all = cache_swa_v.transpose(0, 1, 3, 4, 2).reshape(depth, n_s, D_KV, wb)
    cbuf_all = state_lru_conv.transpose(0, 2, 1, 3)
    g_mix = norm_mix.reshape(depth, 1, D_MODEL)
    g_ffn = norm_ffn.reshape(depth, 1, D_MODEL)
    for l in range(depth):
        g_attn, g_lru, g_sc = (norm_grp[l, :D_ATTN], norm_grp[l, D_ATTN:D_ATTN + D_LRU],
                               norm_grp[l, D_ATTN + D_LRU:])
        lw = dict(
            conv_w=lru_conv_w[l], conv_b=row(lru_conv_b[l]),
            w_gates=jnp.concatenate([_block_diag(lru_w_a[l]), _block_diag(lru_w_i[l])], axis=1).astype(BF16),
            b_a=row(lru_b_a[l]), b_i=row(lru_b_i[l]), lam=row(lru_lambda[l]),
            sc_w=sc_conv_w[l], g_lru=row(g_lru), g_sc=row(g_sc),
            sel=sel,
            sink_tab=jnp.broadcast_to(attn_sinks[l][:, None], (N_HEADS, LANES)),
            g_attn_tab=jnp.tile(g_attn.reshape(N_HEADS, HEAD_DIM), (1, LANES // HEAD_DIM)),
        )
        last = l == depth - 1

        if not piggy:
            w_in_l, w_out_l, w_gu_l, w_d_l, l_in, l_w = w_in_bf, w_out_bf, w_gu_bf, w_d_bf, l, l
        qkv, mb, h8, x8, g8, zs = _proj_lru(xp, g_mix, w_in_l, cos_p, sin_p, xs, cos_s, sin_s, lw, layer=l,
                                            w_layer=l_in, n_seq=n_p, seq=t_p, bm=bm_p)
        if piggy:
            ma, w_gu_l, w_out_l, w_d_l = _attn_prompt(qkv, attn_sinks[l], row(g_attn), n_seq=n_p, seq=t_p,
                                                      cast=[(ffn_w_gu, l), (w_out, l), (ffn_w_down, l)])
            l_w = 0
        else:
            ma = _attn_prompt(qkv, attn_sinks[l], row(g_attn), n_seq=n_p, seq=t_p)
        z3 = qkv.reshape(n_p, t_p, _D_QKV)
        wbp = min(WINDOW, t_p)
        p_states.append((
            h8[:, SUBLANES - 1],
            x8[:, SUBLANES - (LRU_CONV_W - 1):],
            z3[:, t_p - wbp:, D_ATTN:D_ATTN + D_KV].reshape(n_p, wbp, N_KV_HEADS, HEAD_DIM),
            z3[:, t_p - wbp:, D_ATTN + D_KV:D_ATTN + 2 * D_KV].reshape(n_p, wbp, N_KV_HEADS, HEAD_DIM),
            g8[:, SUBLANES - (SC_CONV_W - 1):],
        ))

        knt =zs[:, D_ATTN:D_ATTN + 2 * D_KV].reshape(n_s // nbk, nbk, 2 * D_KV).transpose(0, 2, 1)
        oat, ors, k_stack, v_stack, h_new, c_new, s_new = _decode_mix(
            zs, knt, kct_all, vct_all, state_lru_h, cbuf_all, state_sconv, lw, layer=l, nbk=nbk,
            stacked=None if l == 0 else (k_stack, v_stack))
        c_new = c_new.transpose(1, 0, 2)
        mas = oat[:, :, :HEAD_DIM].reshape(n_s, D_ATTN)
        s_states.append((h_new, c_new, s_new))

        x1, hf, x1s, hfs = _out_proj(ma, mb, xp, mas, ors, xs, w_out_l, g_ffn, layer=l, w_layer=l_w, bm=bm_p)
        if piggy and not last:
            xp, xs, w_in_l = _ffn(hf, x1, hfs, x1s, w_gu_l, w_d_l, row(norm_final), layer=l_w, bm=bm_ffn,
                                  final_norm=last, cast_next=(w_in, l + 1))
        else:
            xp, xs = _ffn(hf, x1, hfs, x1s, w_gu_l, w_d_l, row(norm_final), layer=l_w, bm=bm_ffn,
                          final_norm=last)

    y_prompt = xp.reshape(n_p, t_p, D_MODEL)
    y_sample = xs.reshape(n_s, t_s, D_MODEL)
    stack = lambda states, k: jnp.stack([st[k] for st in states])
    untranspose = lambda c: c.reshape(depth, n_s, N_KV_HEADS, HEAD_DIM, wb).transpose(0, 1, 4, 2, 3)
    return (y_prompt, y_sample,
            stack(p_states, 0), stack(p_states, 1), stack(p_states, 2), stack(p_states, 3), stack(p_states, 4),
            stack(s_states, 0), stack(s_states, 1), untranspose(k_stack), untranspose(v_stack),
            stack(s_states, 2))
```

```python
import functools

import jax
import jax.numpy as jnp
from jax import lax
from jax.experimental import pallas as pl
from jax.experimental.pallas import tpu as pltpu

F32 = jnp.float32
BF16 = jnp.bfloat16

D_MODEL = 2048
D_ATTN = 1024
D_LRU = 512
D_SC = 512
HEAD_DIM = 64
N_HEADS = 16
N_KV_HEADS = 4
N_GROUP = 4
D_KV = 256
WINDOW = 128
ROPE_THETA = 10000.0
N_LRU_HEADS = 8
LRU_BLK = 64
LRU_CONV_W = 4
LRU_C = 8.0
SC_CONV_W = 3
D_FF = 5632
D_IN = 4096
RMS_EPS = 1e-6
PAST_LEN = 8192

LANES = 128
SUBLANES = 8
VMEM_LIMIT_BYTES = 56 * 1024 * 1024

_COL_UX, _COL_GATE, _COL_B, _COL_C, _COL_H = 3, 4, 5, 6, 7


def _cparams(sem):
    return pltpu.CompilerParams(dimension_semantics=sem, vmem_limit_bytes=VMEM_LIMIT_BYTES)


def _rms(x, g):
    return x * lax.rsqrt(jnp.mean(x * x, axis=-1, keepdims=True) + RMS_EPS) * g


_IN_PROJ_CHUNK = 512


def _in_proj_kernel(x_ref, g_ref, w_ref, cos_ref, sin_ref, z_ref):
    h = _rms(x_ref[...], g_ref[...]).astype(BF16)
    bm = h.shape[0]
    lane = lax.broadcasted_iota(jnp.int32, (bm, LANES), 1)
    lo32 = (lane % HEAD_DIM) < (HEAD_DIM // 2)
    cos = cos_ref[...]
    sin = sin_ref[...]

    def rope(a):
        sw = jnp.where(lo32, pltpu.roll(a, LANES - HEAD_DIM // 2, 1), pltpu.roll(a, HEAD_DIM // 2, 1))
        return a * cos + sw * sin

    rope_cols = D_ATTN + D_KV
    for c0 in range(0, D_IN, _IN_PROJ_CHUNK):
        acc = jnp.dot(h, w_ref[:, c0:c0 + _IN_PROJ_CHUNK], preferred_element_type=F32)
        for c in range(0, _IN_PROJ_CHUNK, LANES):
            a = acc[:, c:c + LANES]
            z_ref[:, c0 + c:c0 + c + LANES] = rope(a) if c0 + c < rope_cols else a


def _resident(block_shape, index_map):
    return pl.BlockSpec(block_shape, index_map, pipeline_mode=pl.Buffered(1))


def _cast_specs(w_all, layer, steps, step_index):
    _, k, n = w_all.shape
    r = k // steps
    assert r * steps == k and r % (2 * SUBLANES) == 0
    return (pl.BlockSpec((None, r, n), lambda *ids: (layer, step_index(*ids), 0)),
            pl.BlockSpec((None, r, n), lambda *ids: (0, step_index(*ids), 0)),
            jax.ShapeDtypeStruct((1, k, n), BF16))


def _attn_prompt_kernel(sink_ref, q_ref, kc_ref, kp_ref, vc_ref, vp_ref, g_ref, *rest, n_sub, n_cast):
    o_ref = rest[n_cast]
    for src, dst in zip(rest[:n_cast], rest[n_cast + 1:]):
        dst[...] = src[...].astype(BF16)
    b = pl.program_id(1)
    L = WINDOW

    lane = lax.broadcasted_iota(jnp.int32, (2 * L, LANES), 1)
    lo = lane < HEAD_DIM
    row = lax.broadcasted_iota(jnp.int32, (2 * L, 1), 0)
    top = row < L

    qi = lax.broadcasted_iota(jnp.int32, (2 * L, 4 * L), 0) % L
    sj = lax.broadcasted_iota(jnp.int32, (2 * L, 4 * L), 1) % (2 * L)
    diff = L + qi - sj
    band = (diff >= 0) & (diff < WINDOW)
    bias_inner = jnp.where(band, 0.0, -jnp.inf).astype(F32)
    bias_first = jnp.where(band & ((sj >= L) | (b > 0)), 0.0, -jnp.inf).astype(F32)

    zeros = jnp.zeros((2 * L, LANES), F32)
    ones_lo = jnp.where(lo, 1.0, 0.0).astype(F32)
    ones_hi = 1.0 - ones_lo

    units = [(sub, kh) for sub in range(n_sub) for kh in range(N_KV_HEADS)]
    prep, scores, vmats, probs, sink_terms = {}, {}, {}, {}, {}
    outs = {sub: [] for sub in range(n_sub)}

    def stage_scores(sub, kh):
        if sub not in prep:
            rows = slice(sub * L, (sub + 1) * L)
            k_prev = kp_ref[...] if sub == 0 else kc_ref[(sub - 1) * L:sub * L, :]
            v_prev = vp_ref[...] if sub == 0 else vc_ref[(sub - 1) * L:sub * L, :]
            prep[sub] = ((q_ref[rows, :] * (HEAD_DIM ** -0.5)).astype(BF16),
                         jnp.concatenate([k_prev, kc_ref[rows, :]], axis=0),
                         jnp.concatenate([v_prev, vc_ref[rows, :]], axis=0))
        qb, kk, vv = prep[sub]
        c0 = LANES * (kh // 2)
        kx = kk[:, c0:c0 + LANES]
        vx = vv[:, c0:c0 + LANES]
        kr = pltpu.roll(kx, HEAD_DIM, 1)
        vr = pltpu.roll(vx, HEAD_DIM, 1)
        if kh % 2 == 0:
            k_lo, k_hi = jnp.where(lo, kx, zeros), jnp.where(lo, zeros, kr)
            v_lo, v_hi = jnp.where(lo, vx, zeros), jnp.where(lo, zeros, vr)
        else:
            k_lo, k_hi = jnp.where(lo, kr, zeros), jnp.where(lo, zeros, kx)
            v_lo, v_hi = jnp.where(lo, vr, zeros), jnp.where(lo, zeros, vx)
        kmat = jnp.concatenate([k_lo, k_hi], axis=0).astype(BF16)
        qs = jnp.concatenate([qb[:, 2 * LANES * kh:2 * LANES * kh + LANES],
                              qb[:, 2 * LANES * kh + LANES:2 * LANES * (kh + 1)]], axis=0)
        scores[sub, kh] = lax.dot_general(qs, kmat, (((1,), (1,)), ((), ())), preferred_element_type=F32)
        vmats[sub, kh] = jnp.concatenate([jnp.concatenate([v_lo, ones_lo], axis=1),
                                          jnp.concatenate([v_hi, ones_hi], axis=1)], axis=0).astype(BF16)

    def stage_softmax(sub, kh):
        s = scores.pop((sub, kh)) + (bias_first if sub == 0 else bias_inner)
        sink_lo = jnp.where(top, sink_ref[4 * kh + 0], sink_ref[4 * kh + 2])
        sink_hi = jnp.where(top, sink_ref[4 * kh + 1], sink_ref[4 * kh + 3])
        m_lo = jnp.maximum(jnp.max(s[:, :2 * L], axis=1, keepdims=True), sink_lo)
        m_hi = jnp.maximum(jnp.max(s[:, 2 * L:], axis=1, keepdims=True), sink_hi)
        probs[sub, kh] = jnp.concatenate([jnp.exp(s[:, :2 * L] - m_lo), jnp.exp(s[:, 2 * L:] - m_hi)],
                                         axis=1).astype(BF16)
        sink_terms[sub, kh] = jnp.where(lo, jnp.exp(sink_lo - m_lo), jnp.exp(sink_hi - m_hi))

    def stage_values(sub, kh):
        oe = jnp.dot(probs.pop((sub, kh)), vmats.pop((sub, kh)), preferred_element_type=F32)
        o = oe[:, :LANES] / (oe[:, LANES:] + sink_terms.pop((sub, kh)))
        outs[sub] += [o[:L], o[L:]]
        if kh == N_KV_HEADS - 1:
            out = jnp.concatenate(outs[sub], axis=1)
            o_ref[sub * L:(sub + 1) * L, :] = _rms(out, g_ref[...]).astype(o_ref.dtype)

    for stage in (stage_scores, stage_softmax, stage_values):
        for unit in units:
            stage(*unit)


_ATTN_SUB_BLOCKS = 4


_CAST_SLAB_BYTES = 8 * 1024 * 1024


def _attn_steps(n_seq, seq):
    return n_seq * (seq // (WINDOW * _ATTN_SUB_BLOCKS))


def _can_cast_in_attn(n_seq, seq):
    steps = _attn_steps(n_seq, seq)
    bf16_rows = 2 * SUBLANES
    return (D_MODEL % steps == 0 and D_FF % steps == 0
            and (D_MODEL // steps) % bf16_rows == 0 and (D_FF // steps) % bf16_rows == 0
            and (D_MODEL // steps) * 2 * D_FF * 4 <= _CAST_SLAB_BYTES)


def _attn_prompt(z, sinks, g_attn, *, n_seq, seq, cast=()):
    L = WINDOW
    n_sub = _ATTN_SUB_BLOCKS
    nb = seq // (L * n_sub)
    kcol = D_ATTN // D_KV
    vcol = kcol + 1
    cur = lambda col: (lambda n, b: (n * nb + b, col))
    prev = lambda col: (lambda n, b: (jnp.maximum((n * nb + b) * n_sub - 1, 0), col))
    in_specs = [
        pl.BlockSpec(memory_space=pltpu.SMEM),
        pl.BlockSpec((L * n_sub, D_ATTN), cur(0)),
        pl.BlockSpec((L * n_sub, D_KV), cur(kcol)),
        pl.BlockSpec((L, D_KV), prev(kcol)),
        pl.BlockSpec((L * n_sub, D_KV), cur(vcol)),
        pl.BlockSpec((L, D_KV), prev(vcol)),
        pl.BlockSpec((1, D_ATTN), lambda n, b: (0, 0)),
    ]
    out_specs = [pl.BlockSpec((L * n_sub, D_ATTN), cur(0))]
    out_shape = [jax.ShapeDtypeStruct((n_seq * seq, D_ATTN), BF16)]
    args = [sinks, z, z, z, z, z, g_attn]
    for w_all, layer in cast:
        i_spec, o_spec, o_shape = _cast_specs(w_all, layer, n_seq * nb, lambda n, b: n * nb + b)
        in_specs.append(i_spec)
        out_specs.append(o_spec)
        out_shape.append(o_shape)
        args.append(w_all)
    outs = pl.pallas_call(
        functools.partial(_attn_prompt_kernel, n_sub=n_sub, n_cast=len(cast)),
        grid=(n_seq, nb),
        in_specs=in_specs,
        out_specs=out_specs,
        out_shape=out_shape,
        compiler_params=_cparams(("parallel", "arbitrary")),
        name="attn_prompt",
    )(*args)
    return outs[0] if not cast else outs


def _lru_gates(xc, wg_ref, ba, bi, lam):
    g = jnp.dot(xc.astype(BF16), wg_ref[...], preferred_element_type=F32)
    r = jax.nn.sigmoid(g[:, :D_LRU] + ba)
    gi = jax.nn.sigmoid(g[:, D_LRU:] + bi)
    nl = -lam
    softplus = jnp.maximum(nl, 0.0) + jnp.log1p(jnp.exp(-jnp.abs(nl)))
    log_a = -LRU_C * r * softplus
    a = jnp.exp(log_a)
    th = jnp.tanh(log_a)
    m2 = (-2.0 * th) / (1.0 - th)
    mult = jnp.where(m2 > 0.0, m2 * lax.rsqrt(m2), 0.0)
    return a, mult, gi


def _shift_rows(u, prev8, k):
    r = pltpu.roll(u, k, 0)
    pr = pltpu.roll(prev8, k, 0)
    row8 = lax.broadcasted_iota(jnp.int32, prev8.shape, 0)
    head = jnp.where(row8 < k, pr, r[:SUBLANES])
    return jnp.concatenate([head, r[SUBLANES:]], axis=0)


def _chunk_scan(a, b):
    n = a.shape[0]
    row = lax.broadcasted_iota(jnp.int32, a.shape, 0)
    d = 1
    while d < n:
        if d < SUBLANES:
            keep = row >= d
            b = jnp.where(keep, b + a * pltpu.roll(b, d, 0), b)
            a = jnp.where(keep, a * pltpu.roll(a, d, 0), a)
        else:
            b = jnp.concatenate([b[:d], b[d:] + a[d:] * b[:n - d]], axis=0)
            a = jnp.concatenate([a[:d], a[d:] * a[:n - d]], axis=0)
        d *= 2
    return a, b


def _scan_pitch(ln):
    assert ln % SUBLANES == 0
    return ln if ln % (2 * SUBLANES) == SUBLANES else ln + SUBLANES


def _strided_scan(a, b, h_prev, a_scr, b_scr):
    tc, ch = a.shape
    ln = tc // SUBLANES
    pitch = _scan_pitch(ln)
    nslab = ch // LANES
    for s in range(SUBLANES):
        for c in range(nslab):
            a_scr[c, pitch * s:pitch * s + ln, :] = a[ln * s:ln * (s + 1), c * LANES:(c + 1) * LANES]
            b_scr[c, pitch * s:pitch * s + ln, :] = b[ln * s:ln * (s + 1), c * LANES:(c + 1) * LANES]
    row8 = lax.broadcasted_iota(jnp.int32, (SUBLANES, LANES), 0)
    for c in range(nslab):
        h = jnp.zeros((SUBLANES, LANES), F32)
        acum = jnp.ones((SUBLANES, LANES), F32)
        for j in range(ln):
            idx = pl.ds(j, SUBLANES, stride=pitch)
            at = a_scr[c, idx, :]
            h = at * h + b_scr[c, idx, :]
            acum = at * acum
            b_scr[c, idx, :] = h
            a_scr[c, idx, :] = acum
        a_tot, b_tot = _chunk_scan(acum, h)
        hp = h_prev[:, c * LANES:(c + 1) * LANES]
        cin = jnp.where(row8 == 0, hp, pltpu.roll(b_tot + a_tot * hp, 1, 0))
        for j in range(ln):
            idx = pl.ds(j, SUBLANES, stride=pitch)
            b_scr[c, idx, :] = b_scr[c, idx, :] + a_scr[c, idx, :] * cin
    return jnp.concatenate(
        [jnp.concatenate([b_scr[c, pitch * s:pitch * s + ln, :] for c in range(nslab)], axis=1)
         for s in range(SUBLANES)], axis=0)


def _lru_sc_rows(ux, gate, ub, uc, uh, pos0, wrefs, carries, scan_scr):
    cw_ref, cb_ref, wg_ref, ba_ref, bi_ref, lam_ref, scw_ref, glru_ref, gsc_ref = wrefs
    cx_scr, cg_scr, ch_scr = carries
    tc = ux.shape[0]
    px = cx_scr[...]
    xc = _shift_rows(ux, px, 3) * cw_ref[0:1, :]
    xc = xc + _shift_rows(ux, px, 2) * cw_ref[1:2, :]
    xc = xc + _shift_rows(ux, px, 1) * cw_ref[2:3, :]
    xc = xc + ux * cw_ref[3:4, :]
    xc = xc + cb_ref[...]

    a, mult, gi = _lru_gates(xc, wg_ref, ba_ref[...], bi_ref[...], lam_ref[...])
    pos = pos0 + lax.broadcasted_iota(jnp.int32, (tc, 1), 0)
    mult = jnp.where(pos == 0, 1.0, mult)
    h = _strided_scan(a, mult * gi * xc, ch_scr[SUBLANES - 1:SUBLANES, :], *scan_scr)
    o_lru = h * jax.nn.gelu(gate, approximate=True)

    gch = uc * uh
    pg = cg_scr[...]
    y = _shift_rows(gch, pg, 2) * scw_ref[0:1, :]
    y = y + _shift_rows(gch, pg, 1) * scw_ref[1:2, :]
    y = y + gch * scw_ref[2:3, :]
    o_sc = ub * y

    h8, x8, g8 = h[tc - SUBLANES:], ux[tc - SUBLANES:], gch[tc - SUBLANES:]
    cx_scr[...] = x8
    cg_scr[...] = g8
    ch_scr[...] = h8
    return _rms(o_lru, glru_ref[...]), _rms(o_sc, gsc_ref[...]), h8, x8, g8


_LRU_CHUNK = 256


_D_QKV = D_ATTN + 2 * D_KV
_D_U = D_IN - _D_QKV


def _proj_lru_kernel(x_ref, g_ref, w_ref, cos_ref, sin_ref, xs_ref, coss_ref, sins_ref,
                     cw_ref, cb_ref, wg_ref, ba_ref, bi_ref,
                     lam_ref, scw_ref, glru_ref, gsc_ref, *rest, blocks_per_seq, n_cast):
    qkv_ref, mb_ref, h8_ref, x8_ref, g8_ref, zs_ref = rest[n_cast:n_cast + 6]
    zu_scr, cx_scr, cg_scr, ch_scr, sa_scr, sb_scr = rest[2 * n_cast + 6:]
    for src, dst in zip(rest[:n_cast], rest[n_cast + 6:2 * n_cast + 6]):
        dst[...] = src[...].astype(BF16)
    i = pl.program_id(0)

    @pl.when(i == pl.num_programs(0) - 1)
    def _():
        _in_proj_kernel(xs_ref, g_ref, w_ref, coss_ref, sins_ref, zs_ref)

    blk = i % blocks_per_seq

    @pl.when(blk == 0)
    def _():
        cx_scr[...] = jnp.zeros_like(cx_scr)
        cg_scr[...] = jnp.zeros_like(cg_scr)
        ch_scr[...] = jnp.zeros_like(ch_scr)

    h = _rms(x_ref[...], g_ref[...]).astype(BF16)
    bm = h.shape[0]
    for c0 in range(_D_QKV, D_IN, _IN_PROJ_CHUNK):
        zu_scr[:, c0 - _D_QKV:c0 - _D_QKV + _IN_PROJ_CHUNK] = jnp.dot(
            h, w_ref[:, c0:c0 + _IN_PROJ_CHUNK], preferred_element_type=F32)

    wrefs = (cw_ref, cb_ref, wg_ref, ba_ref, bi_ref, lam_ref, scw_ref, glru_ref, gsc_ref)
    tc = _LRU_CHUNK
    col = lambda k: slice(k * D_LRU, (k + 1) * D_LRU)
    for r0 in range(0, bm, tc):
        rows = slice(r0, r0 + tc)
        lru_n, sc_n, h8, x8, g8 = _lru_sc_rows(
            zu_scr[rows, col(0)], zu_scr[rows, col(1)], zu_scr[rows, col(2)], zu_scr[rows, col(3)],
            zu_scr[rows, col(4)], blk * bm + r0, wrefs, (cx_scr, cg_scr, ch_scr), (sa_scr, sb_scr))
        mb_ref[rows, :D_LRU] = lru_n.astype(mb_ref.dtype)
        mb_ref[rows, D_LRU:] = sc_n.astype(mb_ref.dtype)
    h8_ref[0] = h8
    x8_ref[0] = x8
    g8_ref[0] = g8

    lane = lax.broadcasted_iota(jnp.int32, (bm, LANES), 1)
    lo32 = (lane % HEAD_DIM) < (HEAD_DIM // 2)
    cos = cos_ref[...]
    sin = sin_ref[...]

    def rope(a):
        sw = jnp.where(lo32, pltpu.roll(a, LANES - HEAD_DIM // 2, 1), pltpu.roll(a, HEAD_DIM // 2, 1))
        return a * cos + sw * sin

    rope_cols = D_ATTN + D_KV
    for c0 in range(0, _D_QKV, _IN_PROJ_CHUNK):
        acc = jnp.dot(h, w_ref[:, c0:c0 + _IN_PROJ_CHUNK], preferred_element_type=F32)
        for c in range(0, _IN_PROJ_CHUNK, LANES):
            a = acc[:, c:c + LANES]
            qkv_ref[:, c0 + c:c0 + c + LANES] = rope(a) if c0 + c < rope_cols else a


def _proj_lru(x, g_all, w_all_bf, cos_t, sin_t, xs, cos_s, sin_s, lw, *, layer, w_layer, n_seq, seq, bm,
              cast=()):
    m = x.shape[0]
    ns = xs.shape[0]
    n_tab = cos_t.shape[0] // bm
    bps = seq // bm
    const = lambda shape: pl.BlockSpec(shape, lambda i: (0,) * len(shape))
    st = pl.BlockSpec((1, SUBLANES, D_LRU), lambda i: (i // bps, 0, 0))
    st_shape = jax.ShapeDtypeStruct((n_seq, SUBLANES, D_LRU), F32)
    in_specs = [pl.BlockSpec((bm, D_MODEL), lambda i: (i, 0)),
                pl.BlockSpec((None, 1, D_MODEL), lambda i: (layer, 0, 0)),
                _resident((None, D_MODEL, D_IN), lambda i: (w_layer, 0, 0)),
                pl.BlockSpec((bm, LANES), lambda i: (i % n_tab, 0)),
                pl.BlockSpec((bm, LANES), lambda i: (i % n_tab, 0)),
                const((ns, D_MODEL)), const((ns, LANES)), const((ns, LANES)),
                const((LRU_CONV_W, D_LRU)), const((1, D_LRU)), const((D_LRU, 2 * D_LRU)),
                const((1, D_LRU)), const((1, D_LRU)), const((1, D_LRU)),
                const((SC_CONV_W, D_SC)), const((1, D_LRU)), const((1, D_SC))]
    out_specs = [pl.BlockSpec((bm, _D_QKV), lambda i: (i, 0)),
                 pl.BlockSpec((bm, D_LRU + D_SC), lambda i: (i, 0)), st, st, st, const((ns, D_IN))]
    out_shape = [jax.ShapeDtypeStruct((m, _D_QKV), F32),
                 jax.ShapeDtypeStruct((m, D_LRU + D_SC), BF16), st_shape, st_shape, st_shape,
                 jax.ShapeDtypeStruct((ns, D_IN), F32)]
    args = [x, g_all, w_all_bf, cos_t, sin_t, xs, cos_s, sin_s, lw["conv_w"], lw["conv_b"], lw["w_gates"],
            lw["b_a"], lw["b_i"], lw["lam"], lw["sc_w"], lw["g_lru"], lw["g_sc"]]
    for w_all, lyr in cast:
        i_spec, o_spec, o_shape = _cast_specs(w_all, lyr, m // bm, lambda i: i)
        in_specs.append(i_spec)
        out_specs.append(o_spec)
        out_shape.append(o_shape)
        args.append(w_all)
    return pl.pallas_call(
        functools.partial(_proj_lru_kernel, blocks_per_seq=bps, n_cast=len(cast)),
        grid=(m // bm,),
        in_specs=in_specs,
        out_specs=out_specs,
        out_shape=out_shape,
        scratch_shapes=[pltpu.VMEM((bm, _D_U), F32)] + [pltpu.VMEM((SUBLANES, D_LRU), F32)] * 3
        + [pltpu.VMEM((D_LRU // LANES, SUBLANES * _scan_pitch(_LRU_CHUNK // SUBLANES), LANES), F32)] * 2,
        compiler_params=_cparams(("arbitrary",)),
        name="proj_lru",
    )(*args)


def _decode_kernel(q_ref, knt_ref, kc_ref, vc_ref, sel_ref, sink_ref, gat_ref,
                   ux_ref, gate_ref, ub_ref, uc_ref, uh_ref, h0_ref, cbuf_ref, sbuf_ref,
                   cw_ref, cb_ref, wg_ref, ba_ref, bi_ref, lam_ref, scw_ref, glru_ref, gsc_ref,
                   *rest, nbk):
    oat_ref, ors_ref, ko_ref, vo_ref, hn_ref, cn_ref, sn_ref = rest[-7:]
    if len(ko_ref.shape) == 4:
        for d in range(1, ko_ref.shape[0]):
            ko_ref[d] = jnp.zeros(ko_ref.shape[1:], ko_ref.dtype)
            vo_ref[d] = jnp.zeros(vo_ref.shape[1:], vo_ref.dtype)
        ko_ref, vo_ref = ko_ref.at[0], vo_ref.at[0]
    wb = kc_ref.shape[2]
    hrow = lax.broadcasted_iota(jnp.int32, (N_HEADS, D_ATTN), 0)
    hcol = lax.broadcasted_iota(jnp.int32, (N_HEADS, D_ATTN), 1) // HEAD_DIM
    own = (hrow == hcol)
    qexp = jnp.concatenate(
        [jnp.where(own, jnp.broadcast_to(q_ref[i:i + 1, :] * (HEAD_DIM ** -0.5), (N_HEADS, D_ATTN)), 0.0)
         for i in range(nbk)], axis=0)
    qrow = jnp.dot(qexp.astype(BF16), sel_ref[...], preferred_element_type=F32)
    rows = nbk * N_HEADS
    grow = (lax.broadcasted_iota(jnp.int32, (rows, D_KV), 0) % N_HEADS) // N_GROUP
    gcol = lax.broadcasted_iota(jnp.int32, (rows, D_KV), 1) // HEAD_DIM
    kvmask = (grow == gcol)
    qm = jnp.where(kvmask, qrow, 0.0).astype(BF16)
    sink = jnp.concatenate([sink_ref[...][:, 0:1]] * nbk, axis=0)
    gat = jnp.concatenate([gat_ref[...]] * nbk, axis=0)
    lanek = lax.broadcasted_iota(jnp.int32, (D_KV, wb), 1)
    newest = lanek == wb - 1
    for i in range(nbk):
        ko_ref[i] = jnp.where(newest, jnp.broadcast_to(knt_ref[:D_KV, i:i + 1], (D_KV, wb)),
                              pltpu.roll(kc_ref[i], wb - 1, 1))
        vo_ref[i] = jnp.where(newest, jnp.broadcast_to(knt_ref[D_KV:, i:i + 1], (D_KV, wb)),
                              pltpu.roll(vc_ref[i], wb - 1, 1))
    s = jnp.concatenate(
        [jnp.dot(qm[i * N_HEADS:(i + 1) * N_HEADS], ko_ref[i].astype(BF16), preferred_element_type=F32)
         for i in range(nbk)], axis=0)
    m = jnp.maximum(jnp.max(s, axis=1, keepdims=True), sink)
    p = jnp.exp(s - m)
    p = (p / (jnp.sum(p, axis=1, keepdims=True) + jnp.exp(sink - m))).astype(BF16)
    of = jnp.concatenate(
        [lax.dot_general(p[i * N_HEADS:(i + 1) * N_HEADS], vo_ref[i].astype(BF16), (((1,), (1,)), ((), ())),
                         preferred_element_type=F32) for i in range(nbk)], axis=0)
    of = jnp.where(kvmask, of, 0.0)
    t = of[:, :LANES] + of[:, LANES:]
    o = t + pltpu.roll(t, HEAD_DIM, 1)
    rs = jnp.sum(o * o, axis=1, keepdims=True)
    for i in range(nbk):
        sl = slice(i * N_HEADS, (i + 1) * N_HEADS)
        ms = jnp.sum(rs[sl], axis=0, keepdims=True) * (0.5 / D_ATTN)
        oat_ref[i] = (o[sl] * lax.rsqrt(ms + RMS_EPS) * gat[sl]).astype(oat_ref.dtype)

    ux = ux_ref[...]
    xc = cbuf_ref[0] * cw_ref[0:1, :]
    xc = xc + cbuf_ref[1] * cw_ref[1:2, :]
    xc = xc + cbuf_ref[2] * cw_ref[2:3, :]
    xc = xc + ux * cw_ref[3:4, :]
    xc = xc + cb_ref[...]
    a, mult, gi = _lru_gates(xc, wg_ref, ba_ref[...], bi_ref[...], lam_ref[...])
    h = a * h0_ref[...] + mult * gi * xc
    o_lru = h * jax.nn.gelu(gate_ref[...], approximate=True)
    hn_ref[...] = h
    cn_ref[0] = cbuf_ref[1]
    cn_ref[1] = cbuf_ref[2]
    cn_ref[2] = ux
    gch = uc_ref[...] * uh_ref[...]
    y = sbuf_ref[:, 0, :] * scw_ref[0:1, :]
    y = y + sbuf_ref[:, 1, :] * scw_ref[1:2, :]
    y = y + gch * scw_ref[2:3, :]
    o_sc = ub_ref[...] * y
    sn_ref[:, 0, :] = sbuf_ref[:, 1, :]
    sn_ref[:, 1, :] = gch
    ors_ref[:, :D_LRU] = _rms(o_lru, glru_ref[...]).astype(ors_ref.dtype)
    ors_ref[:, D_LRU:] = _rms(o_sc, gsc_ref[...]).astype(ors_ref.dtype)


def _decode_mix(z, knt, kct_all, vct_all, h0_all, cbuf_all, sbuf_all, lw, *, layer, nbk=16, stacked=None):
    depth, ns, _, wb = kct_all.shape
    z512 = lambda col: pl.BlockSpec((nbk, D_LRU), lambda i: (i, col))
    const = lambda shape: pl.BlockSpec(shape, lambda i: (0,) * len(shape))
    cache_in = pl.BlockSpec((None, nbk, D_KV, wb), lambda i: (layer, i, 0, 0))
    cache_out = cache_in if stacked is not None else pl.BlockSpec((depth, nbk, D_KV, wb), lambda i: (0, i, 0, 0))
    n_in = 24
    extra_specs = [] if stacked is None else [pl.BlockSpec(memory_space=pl.ANY)] * 2
    extra_args = () if stacked is None else tuple(stacked)
    aliases = {} if stacked is None else {n_in: 2, n_in + 1: 3}
    outs = pl.pallas_call(
        functools.partial(_decode_kernel, nbk=nbk),
        grid=(ns // nbk,),
        input_output_aliases=aliases,
        in_specs=[pl.BlockSpec((nbk, D_ATTN), lambda i: (i, 0)),
                  pl.BlockSpec((None, 2 * D_KV, nbk), lambda i: (i, 0, 0)),
                  cache_in, cache_in,
                  const((D_ATTN, D_KV)), const((N_HEADS, LANES)), const((N_HEADS, LANES)),
                  z512(_COL_UX), z512(_COL_GATE), z512(_COL_B), z512(_COL_C), z512(_COL_H),
                  pl.BlockSpec((None, nbk, D_LRU), lambda i: (layer, i, 0)),
                  pl.BlockSpec((None, LRU_CONV_W - 1, nbk, D_LRU), lambda i: (layer, 0, i, 0)),
                  pl.BlockSpec((None, nbk, SC_CONV_W - 1, D_SC), lambda i: (layer, i, 0, 0)),
                  const((LRU_CONV_W, D_LRU)), const((1, D_LRU)), const((D_LRU, 2 * D_LRU)),
                  const((1, D_LRU)), const((1, D_LRU)), const((1, D_LRU)),
                  const((SC_CONV_W, D_SC)), const((1, D_LRU)), const((1, D_SC))] + extra_specs,
        out_specs=[pl.BlockSpec((nbk, N_HEADS, LANES), lambda i: (i, 0, 0)),
                   pl.BlockSpec((nbk, D_LRU + D_SC), lambda i: (i, 0)),
                   cache_out, cache_out,
                   pl.BlockSpec((nbk, D_LRU), lambda i: (i, 0)),
                   pl.BlockSpec((LRU_CONV_W - 1, nbk, D_LRU), lambda i: (0, i, 0)),
                   pl.BlockSpec((nbk, SC_CONV_W - 1, D_SC), lambda i: (i, 0, 0))],
        out_shape=[jax.ShapeDtypeStruct((ns, N_HEADS, LANES), BF16),
                   jax.ShapeDtypeStruct((ns, D_LRU + D_SC), BF16),
                   jax.ShapeDtypeStruct((depth, ns, D_KV, wb), F32),
                   jax.ShapeDtypeStruct((depth, ns, D_KV, wb), F32),
                   jax.ShapeDtypeStruct((ns, D_LRU), F32),
                   jax.ShapeDtypeStruct((LRU_CONV_W - 1, ns, D_LRU), F32),
                   jax.ShapeDtypeStruct((ns, SC_CONV_W - 1, D_SC), F32)],
        compiler_params=_cparams(("parallel",)),
        name="decode_mix",
    )(z, knt, kct_all, vct_all, lw["sel"], lw["sink_tab"], lw["g_attn_tab"],
      z, z, z, z, z, h0_all, cbuf_all, sbuf_all,
      lw["conv_w"], lw["conv_b"], lw["w_gates"], lw["b_a"], lw["b_i"], lw["lam"],
      lw["sc_w"], lw["g_lru"], lw["g_sc"], *extra_args)
    return outs


_OUT_PROJ_CHUNK = 512


def _out_proj_kernel(ma_ref, mb_ref, x_ref, mas_ref, mbs_ref, xs_ref, w_ref, g_ref,
                     x1_ref, hf_ref, x1s_ref, hfs_ref):
    _out_proj_rows(ma_ref, mb_ref, x_ref, w_ref, g_ref, x1_ref, hf_ref)

    @pl.when(pl.program_id(0) == pl.num_programs(0) - 1)
    def _():
        _out_proj_rows(mas_ref, mbs_ref, xs_ref, w_ref, g_ref, x1s_ref, hfs_ref)


def _out_proj_rows(ma_ref, mb_ref, x_ref, w_ref, g_ref, x1_ref, hf_ref):
    ma = ma_ref[...]
    mb = mb_ref[...]
    ssq = None
    for c0 in range(0, D_MODEL, _OUT_PROJ_CHUNK):
        cs = slice(c0, c0 + _OUT_PROJ_CHUNK)
        acc = jnp.dot(ma, w_ref[:D_ATTN, cs], preferred_element_type=F32)
        acc = acc + jnp.dot(mb, w_ref[D_ATTN:, cs], preferred_element_type=F32)
        x1 = x_ref[:, cs] + acc
        x1_ref[:, cs] = x1
        part = jnp.sum(x1 * x1, axis=-1, keepdims=True)
        ssq = part if ssq is None else ssq + part
    scale = lax.rsqrt(ssq * (1.0 / D_MODEL) + RMS_EPS)
    for c0 in range(0, D_MODEL, _OUT_PROJ_CHUNK):
        cs = slice(c0, c0 + _OUT_PROJ_CHUNK)
        hf_ref[:, cs] = (x1_ref[:, cs] * scale * g_ref[:, cs]).astype(hf_ref.dtype)


def _out_proj(ma, mb, x, mas, mbs, xs, w_all_bf, g_all, *, layer, w_layer, bm):
    m = x.shape[0]
    ns = xs.shape[0]
    blk = lambda width: pl.BlockSpec((bm, width), lambda i: (i, 0))
    whole = lambda width: pl.BlockSpec((ns, width), lambda i: (0, 0))
    return pl.pallas_call(
        _out_proj_kernel,
        grid=(m // bm,),
        in_specs=[blk(D_ATTN), blk(D_LRU + D_SC), blk(D_MODEL),
                  whole(D_ATTN), whole(D_LRU + D_SC), whole(D_MODEL),
                  _resident((None, D_MODEL, D_MODEL), lambda i: (w_layer, 0, 0)),
                  pl.BlockSpec((None, 1, D_MODEL), lambda i: (layer, 0, 0))],
        out_specs=[blk(D_MODEL), blk(D_MODEL), whole(D_MODEL), whole(D_MODEL)],
        out_shape=[jax.ShapeDtypeStruct((m, D_MODEL), F32), jax.ShapeDtypeStruct((m, D_MODEL), BF16),
                   jax.ShapeDtypeStruct((ns, D_MODEL), F32), jax.ShapeDtypeStruct((ns, D_MODEL), BF16)],
        compiler_params=_cparams(("arbitrary",)),
        name="out_proj",
    )(ma, mb, x, mas, mbs, xs, w_all_bf, g_all)


_FFN_DOWN_CHUNK = 512
_FFN_X1_CHUNK = 256


def _ffn_kernel(hf_ref, x1_ref, hfs_ref, x1s_ref, wg_ref, wu_ref, wd_ref, gfin_ref, *rest, final_norm):
    cast_refs = (rest[0], rest[3]) if len(rest) == 4 else ()
    o_ref, os_ref = rest[-2:] if not cast_refs else rest[1:3]
    f = pl.program_id(1)
    n_f = pl.num_programs(1)
    n_x1 = D_MODEL // _FFN_X1_CHUNK

    @pl.when(pl.program_id(0) == pl.num_programs(0) - 1)
    def _():
        hfs = hfs_ref[...]
        gate_s = jnp.dot(hfs, wg_ref[...], preferred_element_type=F32)
        up_s = jnp.dot(hfs, wu_ref[...], preferred_element_type=F32)
        hid_s = (gate_s * jax.nn.sigmoid(gate_s) * up_s).astype(BF16)
        down_s = jnp.dot(hid_s, wd_ref[...], preferred_element_type=F32)

        @pl.when(f == 0)
        def _():
            os_ref[...] = down_s

        @pl.when(f > 0)
        def _():
            os_ref[...] += down_s

        @pl.when(f == n_f - 1)
        def _():
            x2s = os_ref[...] + x1s_ref[...]
            os_ref[...] = _rms(x2s, gfin_ref[...]) if final_norm else x2s

    @pl.when(f == 0)
    def _():
        o_ref[...] = jnp.zeros_like(o_ref)

    hf = hf_ref[...]
    gate = jnp.dot(hf, wg_ref[...], preferred_element_type=F32)
    up = jnp.dot(hf, wu_ref[...], preferred_element_type=F32)
    hid = (gate * jax.nn.sigmoid(gate) * up).astype(BF16)
    for c0 in range(0, D_MODEL, _FFN_DOWN_CHUNK):
        cs = slice(c0, c0 + _FFN_DOWN_CHUNK)
        o_ref[:, cs] += jnp.dot(hid, wd_ref[:, cs], preferred_element_type=F32)

    for c in range(n_x1):
        @pl.when(f == c)
        def _(c=c):
            cs = slice(c * _FFN_X1_CHUNK, (c + 1) * _FFN_X1_CHUNK)
            o_ref[:, cs] += x1_ref[...]

    if final_norm:
        @pl.when(f == pl.num_programs(1) - 1)
        def _():
            o_ref[...] = _rms(o_ref[...], gfin_ref[...])

    if cast_refs:
        src, dst = cast_refs

        @pl.when(f < D_IN // _FFN_CAST_COLS)
        def _():
            dst[...] = src[...].astype(BF16)


_FFN_CAST_COLS = 512


def _can_cast_in_ffn(m, bm, tf=512):
    n_i = m // bm
    return (D_MODEL % n_i == 0 and (D_MODEL // n_i) % (2 * SUBLANES) == 0
            and D_FF // tf >= D_IN // _FFN_CAST_COLS)


def _ffn(hf, x1, hfs, x1s, w_gu_bf, w_d_bf, g_final, *, layer, bm, tf=512, final_norm, cast_next=None):
    m = hf.shape[0]
    ns = hfs.shape[0]
    nf = D_FF // tf
    n_x1 = D_MODEL // _FFN_X1_CHUNK
    assert nf >= n_x1
    whole = lambda rows: pl.BlockSpec((rows, D_MODEL), lambda i, f: (0, 0))
    in_specs = [pl.BlockSpec((bm, D_MODEL), lambda i, f: (i, 0)),
                pl.BlockSpec((bm, _FFN_X1_CHUNK), lambda i, f: (i, jnp.minimum(f, n_x1 - 1))),
                whole(ns), whole(ns),
                pl.BlockSpec((None, D_MODEL, tf), lambda i, f: (layer, 0, f)),
                pl.BlockSpec((None, D_MODEL, tf), lambda i, f: (layer, 0, nf + f)),
                pl.BlockSpec((None, tf, D_MODEL), lambda i, f: (layer, f, 0)),
                whole(1)]
    out_specs = [pl.BlockSpec((bm, D_MODEL), lambda i, f: (i, 0)), whole(ns)]
    out_shape = [jax.ShapeDtypeStruct((m, D_MODEL), F32), jax.ShapeDtypeStruct((ns, D_MODEL), F32)]
    args = [hf, x1, hfs, x1s, w_gu_bf, w_gu_bf, w_d_bf, g_final]
    if cast_next is not None:
        w_next, layer_next = cast_next
        rows = D_MODEL // (m // bm)
        n_cc = D_IN // _FFN_CAST_COLS
        col = lambda f: jnp.minimum(f, n_cc - 1)
        in_specs.append(pl.BlockSpec((None, rows, _FFN_CAST_COLS), lambda i, f: (layer_next, i, col(f))))
        out_specs.append(pl.BlockSpec((None, rows, _FFN_CAST_COLS), lambda i, f: (0, i, col(f))))
        out_shape.append(jax.ShapeDtypeStruct((1, D_MODEL, D_IN), BF16))
        args.append(w_next)
    return pl.pallas_call(
        functools.partial(_ffn_kernel, final_norm=final_norm),
        grid=(m // bm, nf), in_specs=in_specs, out_specs=out_specs, out_shape=out_shape,
        compiler_params=_cparams(("arbitrary", "arbitrary")), name="ffn",
    )(*args)


def _rope_tables(pos):
    half = HEAD_DIM // 2
    inv = ROPE_THETA ** (-jnp.arange(half, dtype=F32) / half)
    ang = pos.astype(F32)[:, None] * inv[None, :]
    cos, sin = jnp.cos(ang), jnp.sin(ang)
    cos_t = jnp.tile(cos, (1, LANES // half))
    sin_t = jnp.tile(jnp.concatenate([-sin, sin], axis=1), (1, LANES // HEAD_DIM))
    return cos_t, sin_t


def _block_diag(w):
    hh, blk, _ = w.shape
    eye = jnp.eye(hh, dtype=w.dtype)
    return (eye[:, None, :, None] * w[:, :, None, :]).reshape(hh * blk, hh * blk)


def kernel(x_prompt, x_sample, state_lru_h, state_lru_conv, cache_swa_k, cache_swa_v, state_sconv,
           norm_mix, w_in, norm_grp, w_out, lru_conv_w, lru_conv_b, lru_w_a, lru_b_a, lru_w_i, lru_b_i,
           lru_lambda, sc_conv_w, attn_sinks, norm_ffn, ffn_w_gu, ffn_w_down, norm_final):
    n_p, t_p, _ = x_prompt.shape
    n_s, t_s, _ = x_sample.shape
    depth = w_in.shape[0]
    wb = cache_swa_k.shape[2]
    assert t_s == 1 and wb == WINDOW and n_s % 16 == 0
    assert t_p % (WINDOW * _ATTN_SUB_BLOCKS) == 0 and t_p % _LRU_CHUNK == 0

    bm_p = 512 if (n_p * t_p) % 512 == 0 and t_p % 512 == 0 else 256
    bm_ffn = 1024 if (n_p * t_p) % 1024 == 0 else bm_p

    cos_p, sin_p = _rope_tables(jnp.arange(t_p, dtype=jnp.int32))
    cos_s, sin_s = _rope_tables(jnp.full((n_s,), PAST_LEN, dtype=jnp.int32))

    sel = (jnp.arange(D_ATTN)[:, None] % HEAD_DIM == jnp.arange(D_KV)[None, :] % HEAD_DIM).astype(BF16)

    xp = x_prompt.reshape(n_p * t_p, D_MODEL)
    xs = x_sample.reshape(n_s, D_MODEL)
    row = lambda v: v.reshape(1, -1)
    p_states, s_states = [], []
    piggy = _can_cast_in_attn(n_p, t_p) and _can_cast_in_ffn(n_p * t_p, bm_ffn)
    if piggy:
        w_in_l, l_in = w_in[:1].astype(BF16), 0
    else:
        w_in_bf = w_in.astype(BF16)
        w_out_bf = w_out.astype(BF16)
        w_gu_bf = ffn_w_gu.astype(BF16)
        w_d_bf = ffn_w_down.astype(BF16)
    nbk = 16
    kct_all = cache_swa_k.transpose(0, 1, 3, 4, 2).reshape(depth, n_s, D_KV, wb)
    vct_all = cache_swa_v.transpose(0, 1, 3, 4, 2).reshape(depth, n_s, D_KV, wb)
    cbuf_all = state_lru_conv.transpose(0, 2, 1, 3)
    g_mix = norm_mix.reshape(depth, 1, D_MODEL)
    g_ffn = norm_ffn.reshape(depth, 1, D_MODEL)
    for l in range(depth):
        g_attn, g_lru, g_sc = (norm_grp[l, :D_ATTN], norm_grp[l, D_ATTN:D_ATTN + D_LRU],
                               norm_grp[l, D_ATTN + D_LRU:])
        lw = dict(
            conv_w=lru_conv_w[l], conv_b=row(lru_conv_b[l]),
            w_gates=jnp.concatenate([_block_diag(lru_w_a[l]), _block_diag(lru_w_i[l])], axis=1).astype(BF16),
            b_a=row(lru_b_a[l]), b_i=row(lru_b_i[l]), lam=row(lru_lambda[l]),
            sc_w=sc_conv_w[l], g_lru=row(g_lru), g_sc=row(g_sc),
            sel=sel,
            sink_tab=jnp.broadcast_to(attn_sinks[l][:, None], (N_HEADS, LANES)),
            g_attn_tab=jnp.tile(g_attn.reshape(N_HEADS, HEAD_DIM), (1, LANES // HEAD_DIM)),
        )
        last = l == depth - 1

        if not piggy:
            w_in_l, w_out_l, w_gu_l, w_d_l, l_in, l_w = w_in_bf, w_out_bf, w_gu_bf, w_d_bf, l, l
        qkv, mb, h8, x8, g8, zs = _proj_lru(xp, g_mix, w_in_l, cos_p, sin_p, xs, cos_s, sin_s, lw, layer=l,
                                            w_layer=l_in, n_seq=n_p, seq=t_p, bm=bm_p)
        if piggy:
            ma, w_gu_l, w_out_l, w_d_l = _attn_prompt(qkv, attn_sinks[l], row(g_attn), n_seq=n_p, seq=t_p,
                                                      cast=[(ffn_w_gu, l), (w_out, l), (ffn_w_down, l)])
            l_w = 0
        else:
            ma = _attn_prompt(qkv, attn_sinks[l], row(g_attn), n_seq=n_p, seq=t_p)
        z3 = qkv.reshape(n_p, t_p, _D_QKV)
        wbp = min(WINDOW, t_p)
        p_states.append((
            h8[:, SUBLANES - 1],
            x8[:, SUBLANES - (LRU_CONV_W - 1):],
            z3[:, t_p - wbp:, D_ATTN:D_ATTN + D_KV].reshape(n_p, wbp, N_KV_HEADS, HEAD_DIM),
            z3[:, t_p - wbp:, D_ATTN + D_KV:D_ATTN + 2 * D_KV].reshape(n_p, wbp, N_KV_HEADS, HEAD_DIM),
            g8[:, SUBLANES - (SC_CONV_W - 1):],
        ))

        knt =zs[:, D_ATTN:D_ATTN + 2 * D_KV].reshape(n_s // nbk, nbk, 2 * D_KV).transpose(0, 2, 1)
        oat, ors, k_stack, v_stack, h_new, c_new, s_new = _decode_mix(
            zs, knt, kct_all, vct_all, state_lru_h, cbuf_all, state_sconv, lw, layer=l, nbk=nbk,
            stacked=None if l == 0 else (k_stack, v_stack))
        c_new = c_new.transpose(1, 0, 2)
        mas = oat[:, :, :HEAD_DIM].reshape(n_s, D_ATTN)
        s_states.append((h_new, c_new, s_new))

        x1, hf, x1s, hfs = _out_proj(ma, mb, xp, mas, ors, xs, w_out_l, g_ffn, layer=l, w_layer=l_w, bm=bm_p)
        if piggy and not last:
            xp, xs, w_in_l = _ffn(hf, x1, hfs, x1s, w_gu_l, w_d_l, row(norm_final), layer=l_w, bm=bm_ffn,
                                  final_norm=last, cast_next=(w_in, l + 1))
        else:
            xp, xs = _ffn(hf, x1, hfs, x1s, w_gu_l, w_d_l, row(norm_final), layer=l_w, bm=bm_ffn,
                          final_norm=last)

    y_prompt = xp.reshape(n_p, t_p, D_MODEL)
    y_sample = xs.reshape(n_s, t_s, D_MODEL)
    stack = lambda states, k: jnp.stack([st[k] for st in states])
    untranspose = lambda c: c.reshape(depth, n_s, N_KV_HEADS, HEAD_DIM, wb).transpose(0, 1, 4, 2, 3)
    return (y_prompt, y_sample,
            stack(p_states, 0), stack(p_states, 1), stack(p_states, 2), stack(p_states, 3), stack(p_states, 4),
            stack(s_states, 0), stack(s_states, 1), untranspose(k_stack), untranspose(v_stack),
            stack(s_states, 2))
```

```python
import functools

import jax
import jax.numpy as jnp
from jax import lax
from jax.experimental import pallas as pl
from jax.experimental.pallas import tpu as pltpu

F32 = jnp.float32
BF16 = jnp.bfloat16

D_MODEL = 2048
D_ATTN = 1024
D_LRU = 512
D_SC = 512
HEAD_DIM = 64
N_HEADS = 16
N_KV_HEADS = 4
N_GROUP = 4
D_KV = 256
WINDOW = 128
ROPE_THETA = 10000.0
N_LRU_HEADS = 8
LRU_BLK = 64
LRU_CONV_W = 4
LRU_C = 8.0
SC_CONV_W = 3
D_FF = 5632
D_IN = 4096
RMS_EPS = 1e-6
PAST_LEN = 8192

LANES = 128
SUBLANES = 8
VMEM_LIMIT_BYTES = 56 * 1024 * 1024

_COL_UX, _COL_GATE, _COL_B, _COL_C, _COL_H = 3, 4, 5, 6, 7

_PROJ_ROWS = 512
_FFN_ROWS = 1024
_DECODE_SEQS = 16


def _cparams(sem):
    return pltpu.CompilerParams(dimension_semantics=sem, vmem_limit_bytes=VMEM_LIMIT_BYTES)


def _rms(x, g):
    return x * lax.rsqrt(jnp.mean(x * x, axis=-1, keepdims=True) + RMS_EPS) * g


_IN_PROJ_CHUNK = 512


def _in_proj_kernel(x_ref, g_ref, w_ref, cos_ref, sin_ref, z_ref):
    h = _rms(x_ref[...], g_ref[...]).astype(BF16)
    bm = h.shape[0]
    lane = lax.broadcasted_iota(jnp.int32, (bm, LANES), 1)
    lo32 = (lane % HEAD_DIM) < (HEAD_DIM // 2)
    cos = cos_ref[...]
    sin = sin_ref[...]

    def rope(a):
        sw = jnp.where(lo32, pltpu.roll(a, LANES - HEAD_DIM // 2, 1), pltpu.roll(a, HEAD_DIM // 2, 1))
        return a * cos + sw * sin

    rope_cols = D_ATTN + D_KV
    for c0 in range(0, D_IN, _IN_PROJ_CHUNK):
        acc = jnp.dot(h, w_ref[:, c0:c0 + _IN_PROJ_CHUNK], preferred_element_type=F32)
        for c in range(0, _IN_PROJ_CHUNK, LANES):
            a = acc[:, c:c + LANES]
            z_ref[:, c0 + c:c0 + c + LANES] = rope(a) if c0 + c < rope_cols else a


def _resident(block_shape, index_map):
    return pl.BlockSpec(block_shape, index_map, pipeline_mode=pl.Buffered(1))


def _layer_spec(layer, shape):
    return pl.BlockSpec((None,) + shape, lambda *ids: (layer,) + (0,) * len(shape))


def _mixer_param_specs(layer):
    return [_layer_spec(layer, s) for s in (
        (LRU_CONV_W, D_LRU), (1, D_LRU), (D_LRU, 2 * D_LRU), (1, D_LRU), (1, D_LRU), (1, D_LRU),
        (SC_CONV_W, D_SC), (1, D_LRU), (1, D_SC))]


def _mixer_params(lw):
    return [lw[k] for k in ("conv_w", "conv_b", "w_gates", "b_a", "b_i", "lam", "sc_w", "g_lru", "g_sc")]


def _cast_specs(w_all, layer, steps, step_index):
    _, k, n = w_all.shape
    r = k // steps
    assert r * steps == k and r % (2 * SUBLANES) == 0
    return (pl.BlockSpec((None, r, n), lambda *ids: (layer, step_index(*ids), 0)),
            pl.BlockSpec((None, r, n), lambda *ids: (0, step_index(*ids), 0)),
            jax.ShapeDtypeStruct((1, k, n), BF16))


def _attn_prompt_kernel(sink_ref, q_ref, kc_ref, kp_ref, vc_ref, vp_ref, g_ref, *rest, n_sub, n_cast, layer):
    o_ref = rest[n_cast]
    for src, dst in zip(rest[:n_cast], rest[n_cast + 1:]):
        dst[...] = src[...].astype(BF16)
    b = pl.program_id(1)
    L = WINDOW

    lane = lax.broadcasted_iota(jnp.int32, (2 * L, LANES), 1)
    lo = lane < HEAD_DIM
    row = lax.broadcasted_iota(jnp.int32, (2 * L, 1), 0)
    top = row < L

    qi = lax.broadcasted_iota(jnp.int32, (2 * L, 4 * L), 0) % L
    sj = lax.broadcasted_iota(jnp.int32, (2 * L, 4 * L), 1) % (2 * L)
    diff = L + qi - sj
    band = (diff >= 0) & (diff < WINDOW)
    bias_inner = jnp.where(band, 0.0, -jnp.inf).astype(F32)
    bias_first = jnp.where(band & ((sj >= L) | (b > 0)), 0.0, -jnp.inf).astype(F32)

    zeros = jnp.zeros((2 * L, LANES), F32)
    ones_lo = jnp.where(lo, 1.0, 0.0).astype(F32)
    ones_hi = 1.0 - ones_lo

    units = [(sub, kh) for sub in range(n_sub) for kh in range(N_KV_HEADS)]
    prep, scores, vmats, probs, sink_terms = {}, {}, {}, {}, {}
    outs = {sub: [] for sub in range(n_sub)}

    def stage_scores(sub, kh):
        if sub not in prep:
            rows = slice(sub * L, (sub + 1) * L)
            k_prev = kp_ref[...] if sub == 0 else kc_ref[(sub - 1) * L:sub * L, :]
            v_prev = vp_ref[...] if sub == 0 else vc_ref[(sub - 1) * L:sub * L, :]
            prep[sub] = ((q_ref[rows, :] * (HEAD_DIM ** -0.5)).astype(BF16),
                         jnp.concatenate([k_prev, kc_ref[rows, :]], axis=0),
                         jnp.concatenate([v_prev, vc_ref[rows, :]], axis=0))
        qb, kk, vv = prep[sub]
        c0 = LANES * (kh // 2)
        kx = kk[:, c0:c0 + LANES]
        vx = vv[:, c0:c0 + LANES]
        kr = pltpu.roll(kx, HEAD_DIM, 1)
        vr = pltpu.roll(vx, HEAD_DIM, 1)
        if kh % 2 == 0:
            k_lo, k_hi = jnp.where(lo, kx, zeros), jnp.where(lo, zeros, kr)
            v_lo, v_hi = jnp.where(lo, vx, zeros), jnp.where(lo, zeros, vr)
        else:
            k_lo, k_hi = jnp.where(lo, kr, zeros), jnp.where(lo, zeros, kx)
            v_lo, v_hi = jnp.where(lo, vr, zeros), jnp.where(lo, zeros, vx)
        kmat = jnp.concatenate([k_lo, k_hi], axis=0).astype(BF16)
        qs = jnp.concatenate([qb[:, 2 * LANES * kh:2 * LANES * kh + LANES],
                              qb[:, 2 * LANES * kh + LANES:2 * LANES * (kh + 1)]], axis=0)
        scores[sub, kh] = lax.dot_general(qs, kmat, (((1,), (1,)), ((), ())), preferred_element_type=F32)
        vmats[sub, kh] = jnp.concatenate([jnp.concatenate([v_lo, ones_lo], axis=1),
                                          jnp.concatenate([v_hi, ones_hi], axis=1)], axis=0).astype(BF16)

    def stage_softmax(sub, kh):
        s = scores.pop((sub, kh)) + (bias_first if sub == 0 else bias_inner)
        sink_lo = jnp.where(top, sink_ref[layer, 4 * kh + 0], sink_ref[layer, 4 * kh + 2])
        sink_hi = jnp.where(top, sink_ref[layer, 4 * kh + 1], sink_ref[layer, 4 * kh + 3])
        m_lo = jnp.maximum(jnp.max(s[:, :2 * L], axis=1, keepdims=True), sink_lo)
        m_hi = jnp.maximum(jnp.max(s[:, 2 * L:], axis=1, keepdims=True), sink_hi)
        probs[sub, kh] = jnp.concatenate([jnp.exp(s[:, :2 * L] - m_lo), jnp.exp(s[:, 2 * L:] - m_hi)],
                                         axis=1).astype(BF16)
        sink_terms[sub, kh] = jnp.where(lo, jnp.exp(sink_lo - m_lo), jnp.exp(sink_hi - m_hi))

    def stage_values(sub, kh):
        oe = jnp.dot(probs.pop((sub, kh)), vmats.pop((sub, kh)), preferred_element_type=F32)
        o = oe[:, :LANES] / (oe[:, LANES:] + sink_terms.pop((sub, kh)))
        outs[sub] += [o[:L], o[L:]]
        if kh == N_KV_HEADS - 1:
            out = jnp.concatenate(outs[sub], axis=1)
            o_ref[sub * L:(sub + 1) * L, :] = _rms(out, g_ref[...]).astype(o_ref.dtype)

    for stage in (stage_scores, stage_softmax, stage_values):
        for unit in units:
            stage(*unit)


_ATTN_SUB_BLOCKS = 4


_CAST_SLAB_BYTES = 8 * 1024 * 1024


def _attn_steps(n_seq, seq):
    return n_seq * (seq // (WINDOW * _ATTN_SUB_BLOCKS))


def _can_cast_in_attn(n_seq, seq):
    steps = _attn_steps(n_seq, seq)
    bf16_rows = 2 * SUBLANES
    return (D_MODEL % steps == 0 and D_FF % steps == 0
            and (D_MODEL // steps) % bf16_rows == 0 and (D_FF // steps) % bf16_rows == 0
            and (D_MODEL // steps) * 2 * D_FF * 4 <= _CAST_SLAB_BYTES)


def _attn_prompt(z, sinks_all, g_attn_all, *, layer, n_seq, seq, cast=()):
    L = WINDOW
    n_sub = _ATTN_SUB_BLOCKS
    nb = seq // (L * n_sub)
    kcol = D_ATTN // D_KV
    vcol = kcol + 1
    cur = lambda col: (lambda n, b: (n * nb + b, col))
    prev = lambda col: (lambda n, b: (jnp.maximum((n * nb + b) * n_sub - 1, 0), col))
    in_specs = [
        pl.BlockSpec(memory_space=pltpu.SMEM),
        pl.BlockSpec((L * n_sub, D_ATTN), cur(0)),
        pl.BlockSpec((L * n_sub, D_KV), cur(kcol)),
        pl.BlockSpec((L, D_KV), prev(kcol)),
        pl.BlockSpec((L * n_sub, D_KV), cur(vcol)),
        pl.BlockSpec((L, D_KV), prev(vcol)),
        pl.BlockSpec((None, 1, D_ATTN), lambda n, b: (layer, 0, 0)),
    ]
    out_specs = [pl.BlockSpec((L * n_sub, D_ATTN), cur(0))]
    out_shape = [jax.ShapeDtypeStruct((n_seq * seq, D_ATTN), BF16)]
    args = [sinks_all, z, z, z, z, z, g_attn_all]
    for w_all, w_layer in cast:
        i_spec, o_spec, o_shape = _cast_specs(w_all, w_layer, n_seq * nb, lambda n, b: n * nb + b)
        in_specs.append(i_spec)
        out_specs.append(o_spec)
        out_shape.append(o_shape)
        args.append(w_all)
    outs = pl.pallas_call(
        functools.partial(_attn_prompt_kernel, n_sub=n_sub, n_cast=len(cast), layer=layer),
        grid=(n_seq, nb),
        in_specs=in_specs,
        out_specs=out_specs,
        out_shape=out_shape,
        compiler_params=_cparams(("parallel", "arbitrary")),
        name="attn_prompt",
    )(*args)
    return outs[0] if not cast else outs


def _lru_gates(xc, wg_ref, ba, bi, lam):
    g = jnp.dot(xc.astype(BF16), wg_ref[...], preferred_element_type=F32)
    r = jax.nn.sigmoid(g[:, :D_LRU] + ba)
    gi = jax.nn.sigmoid(g[:, D_LRU:] + bi)
    nl = -lam
    softplus = jnp.maximum(nl, 0.0) + jnp.log1p(jnp.exp(-jnp.abs(nl)))
    log_a = -LRU_C * r * softplus
    a = jnp.exp(log_a)
    th = jnp.tanh(log_a)
    m2 = (-2.0 * th) / (1.0 - th)
    mult = jnp.where(m2 > 0.0, m2 * lax.rsqrt(m2), 0.0)
    return a, mult, gi


def _shift_rows(u, prev8, k):
    r = pltpu.roll(u, k, 0)
    pr = pltpu.roll(prev8, k, 0)
    row8 = lax.broadcasted_iota(jnp.int32, prev8.shape, 0)
    head = jnp.where(row8 < k, pr, r[:SUBLANES])
    return jnp.concatenate([head, r[SUBLANES:]], axis=0)


def _chunk_scan(a, b):
    n = a.shape[0]
    row = lax.broadcasted_iota(jnp.int32, a.shape, 0)
    d = 1
    while d < n:
        if d < SUBLANES:
            keep = row >= d
            b = jnp.where(keep, b + a * pltpu.roll(b, d, 0), b)
            a = jnp.where(keep, a * pltpu.roll(a, d, 0), a)
        else:
            b = jnp.concatenate([b[:d], b[d:] + a[d:] * b[:n - d]], axis=0)
            a = jnp.concatenate([a[:d], a[d:] * a[:n - d]], axis=0)
        d *= 2
    return a, b


def _scan_pitch(ln):
    assert ln % SUBLANES == 0
    return ln if ln % (2 * SUBLANES) == SUBLANES else ln + SUBLANES


def _strided_scan(a, b, h_prev, a_scr, b_scr):
    tc, ch = a.shape
    ln = tc // SUBLANES
    pitch = _scan_pitch(ln)
    nslab = ch // LANES
    for s in range(SUBLANES):
        for c in range(nslab):
            a_scr[c, pitch * s:pitch * s + ln, :] = a[ln * s:ln * (s + 1), c * LANES:(c + 1) * LANES]
            b_scr[c, pitch * s:pitch * s + ln, :] = b[ln * s:ln * (s + 1), c * LANES:(c + 1) * LANES]
    row8 = lax.broadcasted_iota(jnp.int32, (SUBLANES, LANES), 0)
    for c in range(nslab):
        h = jnp.zeros((SUBLANES, LANES), F32)
        acum = jnp.ones((SUBLANES, LANES), F32)
        for j in range(ln):
            idx = pl.ds(j, SUBLANES, stride=pitch)
            at = a_scr[c, idx, :]
            h = at * h + b_scr[c, idx, :]
            acum = at * acum
            b_scr[c, idx, :] = h
            a_scr[c, idx, :] = acum
        a_tot, b_tot = _chunk_scan(acum, h)
        hp = h_prev[:, c * LANES:(c + 1) * LANES]
        cin = jnp.where(row8 == 0, hp, pltpu.roll(b_tot + a_tot * hp, 1, 0))
        for j in range(ln):
            idx = pl.ds(j, SUBLANES, stride=pitch)
            b_scr[c, idx, :] = b_scr[c, idx, :] + a_scr[c, idx, :] * cin
    return jnp.concatenate(
        [jnp.concatenate([b_scr[c, pitch * s:pitch * s + ln, :] for c in range(nslab)], axis=1)
         for s in range(SUBLANES)], axis=0)


def _lru_sc_rows(ux, gate, ub, uc, uh, pos0, wrefs, carries, scan_scr):
    cw_ref, cb_ref, wg_ref, ba_ref, bi_ref, lam_ref, scw_ref, glru_ref, gsc_ref = wrefs
    cx_scr, cg_scr, ch_scr = carries
    tc = ux.shape[0]
    px = cx_scr[...]
    xc = _shift_rows(ux, px, 3) * cw_ref[0:1, :]
    xc = xc + _shift_rows(ux, px, 2) * cw_ref[1:2, :]
    xc = xc + _shift_rows(ux, px, 1) * cw_ref[2:3, :]
    xc = xc + ux * cw_ref[3:4, :]
    xc = xc + cb_ref[...]

    a, mult, gi = _lru_gates(xc, wg_ref, ba_ref[...], bi_ref[...], lam_ref[...])
    pos = pos0 + lax.broadcasted_iota(jnp.int32, (tc, 1), 0)
    mult = jnp.where(pos == 0, 1.0, mult)
    h = _strided_scan(a, mult * gi * xc, ch_scr[SUBLANES - 1:SUBLANES, :], *scan_scr)
    o_lru = h * jax.nn.gelu(gate, approximate=True)

    gch = uc * uh
    pg = cg_scr[...]
    y = _shift_rows(gch, pg, 2) * scw_ref[0:1, :]
    y = y + _shift_rows(gch, pg, 1) * scw_ref[1:2, :]
    y = y + gch * scw_ref[2:3, :]
    o_sc = ub * y

    h8, x8, g8 = h[tc - SUBLANES:], ux[tc - SUBLANES:], gch[tc - SUBLANES:]
    cx_scr[...] = x8
    cg_scr[...] = g8
    ch_scr[...] = h8
    return _rms(o_lru, glru_ref[...]), _rms(o_sc, gsc_ref[...]), h8, x8, g8


_LRU_CHUNK = 256


_D_QKV = D_ATTN + 2 * D_KV
_D_U = D_IN - _D_QKV


def _proj_lru_kernel(x_ref, g_ref, w_ref, cos_ref, sin_ref, xs_ref, coss_ref, sins_ref,
                     cw_ref, cb_ref, wg_ref, ba_ref, bi_ref,
                     lam_ref, scw_ref, glru_ref, gsc_ref, *rest, blocks_per_seq, n_cast):
    qkv_ref, mb_ref, h8_ref, x8_ref, g8_ref, zs_ref = rest[n_cast:n_cast + 6]
    zu_scr, cx_scr, cg_scr, ch_scr, sa_scr, sb_scr = rest[2 * n_cast + 6:]
    for src, dst in zip(rest[:n_cast], rest[n_cast + 6:2 * n_cast + 6]):
        dst[...] = src[...].astype(BF16)
    i = pl.program_id(0)

    @pl.when(i == pl.num_programs(0) - 1)
    def _():
        _in_proj_kernel(xs_ref, g_ref, w_ref, coss_ref, sins_ref, zs_ref)

    blk = i % blocks_per_seq

    @pl.when(blk == 0)
    def _():
        cx_scr[...] = jnp.zeros_like(cx_scr)
        cg_scr[...] = jnp.zeros_like(cg_scr)
        ch_scr[...] = jnp.zeros_like(ch_scr)

    h = _rms(x_ref[...], g_ref[...]).astype(BF16)
    bm = h.shape[0]
    for c0 in range(_D_QKV, D_IN, _IN_PROJ_CHUNK):
        zu_scr[:, c0 - _D_QKV:c0 - _D_QKV + _IN_PROJ_CHUNK] = jnp.dot(
            h, w_ref[:, c0:c0 + _IN_PROJ_CHUNK], preferred_element_type=F32)

    wrefs = (cw_ref, cb_ref, wg_ref, ba_ref, bi_ref, lam_ref, scw_ref, glru_ref, gsc_ref)
    tc = _LRU_CHUNK
    col = lambda k: slice(k * D_LRU, (k + 1) * D_LRU)
    for r0 in range(0, bm, tc):
        rows = slice(r0, r0 + tc)
        lru_n, sc_n, h8, x8, g8 = _lru_sc_rows(
            zu_scr[rows, col(0)], zu_scr[rows, col(1)], zu_scr[rows, col(2)], zu_scr[rows, col(3)],
            zu_scr[rows, col(4)], blk * bm + r0, wrefs, (cx_scr, cg_scr, ch_scr), (sa_scr, sb_scr))
        mb_ref[rows, :D_LRU] = lru_n.astype(mb_ref.dtype)
        mb_ref[rows, D_LRU:] = sc_n.astype(mb_ref.dtype)
    h8_ref[0] = h8
    x8_ref[0] = x8
    g8_ref[0] = g8

    lane = lax.broadcasted_iota(jnp.int32, (bm, LANES), 1)
    lo32 = (lane % HEAD_DIM) < (HEAD_DIM // 2)
    cos = cos_ref[...]
    sin = sin_ref[...]

    def rope(a):
        sw = jnp.where(lo32, pltpu.roll(a, LANES - HEAD_DIM // 2, 1), pltpu.roll(a, HEAD_DIM // 2, 1))
        return a * cos + sw * sin

    rope_cols = D_ATTN + D_KV
    for c0 in range(0, _D_QKV, _IN_PROJ_CHUNK):
        acc = jnp.dot(h, w_ref[:, c0:c0 + _IN_PROJ_CHUNK], preferred_element_type=F32)
        for c in range(0, _IN_PROJ_CHUNK, LANES):
            a = acc[:, c:c + LANES]
            qkv_ref[:, c0 + c:c0 + c + LANES] = rope(a) if c0 + c < rope_cols else a


def _proj_lru(x, g_all, w_all_bf, cos_t, sin_t, xs, cos_s, sin_s, lw, *, layer, w_layer, n_seq, seq, bm,
              cast=()):
    m = x.shape[0]
    ns = xs.shape[0]
    n_tab = cos_t.shape[0] // bm
    bps = seq // bm
    const = lambda shape: pl.BlockSpec(shape, lambda i: (0,) * len(shape))
    st = pl.BlockSpec((1, SUBLANES, D_LRU), lambda i: (i // bps, 0, 0))
    st_shape = jax.ShapeDtypeStruct((n_seq, SUBLANES, D_LRU), F32)
    in_specs = [pl.BlockSpec((bm, D_MODEL), lambda i: (i, 0)),
                pl.BlockSpec((None, 1, D_MODEL), lambda i: (layer, 0, 0)),
                _resident((None, D_MODEL, D_IN), lambda i: (w_layer, 0, 0)),
                pl.BlockSpec((bm, LANES), lambda i: (i % n_tab, 0)),
                pl.BlockSpec((bm, LANES), lambda i: (i % n_tab, 0)),
                const((ns, D_MODEL)), const((ns, LANES)), const((ns, LANES))] + _mixer_param_specs(layer)
    out_specs = [pl.BlockSpec((bm, _D_QKV), lambda i: (i, 0)),
                 pl.BlockSpec((bm, D_LRU + D_SC), lambda i: (i, 0)), st, st, st, const((ns, D_IN))]
    out_shape = [jax.ShapeDtypeStruct((m, _D_QKV), F32),
                 jax.ShapeDtypeStruct((m, D_LRU + D_SC), BF16), st_shape, st_shape, st_shape,
                 jax.ShapeDtypeStruct((ns, D_IN), F32)]
    args = [x, g_all, w_all_bf, cos_t, sin_t, xs, cos_s, sin_s] + _mixer_params(lw)
    for w_all, lyr in cast:
        i_spec, o_spec, o_shape = _cast_specs(w_all, lyr, m // bm, lambda i: i)
        in_specs.append(i_spec)
        out_specs.append(o_spec)
        out_shape.append(o_shape)
        args.append(w_all)
    return pl.pallas_call(
        functools.partial(_proj_lru_kernel, blocks_per_seq=bps, n_cast=len(cast)),
        grid=(m // bm,),
        in_specs=in_specs,
        out_specs=out_specs,
        out_shape=out_shape,
        scratch_shapes=[pltpu.VMEM((bm, _D_U), F32)] + [pltpu.VMEM((SUBLANES, D_LRU), F32)] * 3
        + [pltpu.VMEM((D_LRU // LANES, SUBLANES * _scan_pitch(_LRU_CHUNK // SUBLANES), LANES), F32)] * 2,
        compiler_params=_cparams(("arbitrary",)),
        name="proj_lru",
    )(*args)


def _decode_kernel(q_ref, knt_ref, kc_ref, vc_ref, sel_ref, sink_ref, gat_ref,
                   ux_ref, gate_ref, ub_ref, uc_ref, uh_ref, h0_ref, cbuf_ref, sbuf_ref,
                   cw_ref, cb_ref, wg_ref, ba_ref, bi_ref, lam_ref, scw_ref, glru_ref, gsc_ref,
                   *rest, nbk):
    oat_ref, ors_ref, ko_ref, vo_ref, hn_ref, cn_ref, sn_ref = rest[-7:]
    if len(ko_ref.shape) == 4:
        for d in range(1, ko_ref.shape[0]):
            ko_ref[d] = jnp.zeros(ko_ref.shape[1:], ko_ref.dtype)
            vo_ref[d] = jnp.zeros(vo_ref.shape[1:], vo_ref.dtype)
        ko_ref, vo_ref = ko_ref.at[0], vo_ref.at[0]
    wb = kc_ref.shape[2]
    hrow = lax.broadcasted_iota(jnp.int32, (N_HEADS, D_ATTN), 0)
    hcol = lax.broadcasted_iota(jnp.int32, (N_HEADS, D_ATTN), 1) // HEAD_DIM
    own = (hrow == hcol)
    qexp = jnp.concatenate(
        [jnp.where(own, jnp.broadcast_to(q_ref[i:i + 1, :] * (HEAD_DIM ** -0.5), (N_HEADS, D_ATTN)), 0.0)
         for i in range(nbk)], axis=0)
    qrow = jnp.dot(qexp.astype(BF16), sel_ref[...], preferred_element_type=F32)
    rows = nbk * N_HEADS
    grow = (lax.broadcasted_iota(jnp.int32, (rows, D_KV), 0) % N_HEADS) // N_GROUP
    gcol = lax.broadcasted_iota(jnp.int32, (rows, D_KV), 1) // HEAD_DIM
    kvmask = (grow == gcol)
    qm = jnp.where(kvmask, qrow, 0.0).astype(BF16)
    sink = jnp.concatenate([sink_ref[...][:, 0:1]] * nbk, axis=0)
    gat = jnp.concatenate([gat_ref[...]] * nbk, axis=0)
    lanek = lax.broadcasted_iota(jnp.int32, (D_KV, wb), 1)
    newest = lanek == wb - 1
    for i in range(nbk):
        ko_ref[i] = jnp.where(newest, jnp.broadcast_to(knt_ref[:D_KV, i:i + 1], (D_KV, wb)),
                              pltpu.roll(kc_ref[i], wb - 1, 1))
        vo_ref[i] = jnp.where(newest, jnp.broadcast_to(knt_ref[D_KV:, i:i + 1], (D_KV, wb)),
                              pltpu.roll(vc_ref[i], wb - 1, 1))
    s = jnp.concatenate(
        [jnp.dot(qm[i * N_HEADS:(i + 1) * N_HEADS], ko_ref[i].astype(BF16), preferred_element_type=F32)
         for i in range(nbk)], axis=0)
    m = jnp.maximum(jnp.max(s, axis=1, keepdims=True), sink)
    p = jnp.exp(s - m)
    p = (p / (jnp.sum(p, axis=1, keepdims=True) + jnp.exp(sink - m))).astype(BF16)
    of = jnp.concatenate(
        [lax.dot_general(p[i * N_HEADS:(i + 1) * N_HEADS], vo_ref[i].astype(BF16), (((1,), (1,)), ((), ())),
                         preferred_element_type=F32) for i in range(nbk)], axis=0)
    of = jnp.where(kvmask, of, 0.0)
    t = of[:, :LANES] + of[:, LANES:]
    o = t + pltpu.roll(t, HEAD_DIM, 1)
    rs = jnp.sum(o * o, axis=1, keepdims=True)
    for i in range(nbk):
        sl = slice(i * N_HEADS, (i + 1) * N_HEADS)
        ms = jnp.sum(rs[sl], axis=0, keepdims=True) * (0.5 / D_ATTN)
        oat_ref[i] = (o[sl] * lax.rsqrt(ms + RMS_EPS) * gat[sl]).astype(oat_ref.dtype)

    ux = ux_ref[...]
    xc = cbuf_ref[0] * cw_ref[0:1, :]
    xc = xc + cbuf_ref[1] * cw_ref[1:2, :]
    xc = xc + cbuf_ref[2] * cw_ref[2:3, :]
    xc = xc + ux * cw_ref[3:4, :]
    xc = xc + cb_ref[...]
    a, mult, gi = _lru_gates(xc, wg_ref, ba_ref[...], bi_ref[...], lam_ref[...])
    h = a * h0_ref[...] + mult * gi * xc
    o_lru = h * jax.nn.gelu(gate_ref[...], approximate=True)
    hn_ref[...] = h
    cn_ref[0] = cbuf_ref[1]
    cn_ref[1] = cbuf_ref[2]
    cn_ref[2] = ux
    gch = uc_ref[...] * uh_ref[...]
    y = sbuf_ref[:, 0, :] * scw_ref[0:1, :]
    y = y + sbuf_ref[:, 1, :] * scw_ref[1:2, :]
    y = y + gch * scw_ref[2:3, :]
    o_sc = ub_ref[...] * y
    sn_ref[:, 0, :] = sbuf_ref[:, 1, :]
    sn_ref[:, 1, :] = gch
    ors_ref[:, :D_LRU] = _rms(o_lru, glru_ref[...]).astype(ors_ref.dtype)
    ors_ref[:, D_LRU:] = _rms(o_sc, gsc_ref[...]).astype(ors_ref.dtype)


def _decode_mix(z, knt, kct_all, vct_all, h0_all, cbuf_all, sbuf_all, lw, *, layer, nbk=16, stacked=None):
    depth, ns, _, wb = kct_all.shape
    z512 = lambda col: pl.BlockSpec((nbk, D_LRU), lambda i: (i, col))
    const = lambda shape: pl.BlockSpec(shape, lambda i: (0,) * len(shape))
    cache_in = pl.BlockSpec((None, nbk, D_KV, wb), lambda i: (layer, i, 0, 0))
    cache_out = cache_in if stacked is not None else pl.BlockSpec((depth, nbk, D_KV, wb), lambda i: (0, i, 0, 0))
    n_in = 24
    extra_specs = [] if stacked is None else [pl.BlockSpec(memory_space=pl.ANY)] * 2
    extra_args = () if stacked is None else tuple(stacked)
    aliases = {} if stacked is None else {n_in: 2, n_in + 1: 3}
    outs = pl.pallas_call(
        functools.partial(_decode_kernel, nbk=nbk),
        grid=(ns // nbk,),
        input_output_aliases=aliases,
        in_specs=[pl.BlockSpec((nbk, D_ATTN), lambda i: (i, 0)),
                  pl.BlockSpec((None, 2 * D_KV, nbk), lambda i: (i, 0, 0)),
                  cache_in, cache_in,
                  const((D_ATTN, D_KV)), _layer_spec(layer, (N_HEADS, LANES)), _layer_spec(layer, (N_HEADS, LANES)),
                  z512(_COL_UX), z512(_COL_GATE), z512(_COL_B), z512(_COL_C), z512(_COL_H),
                  pl.BlockSpec((None, nbk, D_LRU), lambda i: (layer, i, 0)),
                  pl.BlockSpec((None, LRU_CONV_W - 1, nbk, D_LRU), lambda i: (layer, 0, i, 0)),
                  pl.BlockSpec((None, nbk, SC_CONV_W - 1, D_SC), lambda i: (layer, i, 0, 0))]
        + _mixer_param_specs(layer) + extra_specs,
        out_specs=[pl.BlockSpec((nbk, N_HEADS, LANES), lambda i: (i, 0, 0)),
                   pl.BlockSpec((nbk, D_LRU + D_SC), lambda i: (i, 0)),
                   cache_out, cache_out,
                   pl.BlockSpec((nbk, D_LRU), lambda i: (i, 0)),
                   pl.BlockSpec((LRU_CONV_W - 1, nbk, D_LRU), lambda i: (0, i, 0)),
                   pl.BlockSpec((nbk, SC_CONV_W - 1, D_SC), lambda i: (i, 0, 0))],
        out_shape=[jax.ShapeDtypeStruct((ns, N_HEADS, LANES), BF16),
                   jax.ShapeDtypeStruct((ns, D_LRU + D_SC), BF16),
                   jax.ShapeDtypeStruct((depth, ns, D_KV, wb), F32),
                   jax.ShapeDtypeStruct((depth, ns, D_KV, wb), F32),
                   jax.ShapeDtypeStruct((ns, D_LRU), F32),
                   jax.ShapeDtypeStruct((LRU_CONV_W - 1, ns, D_LRU), F32),
                   jax.ShapeDtypeStruct((ns, SC_CONV_W - 1, D_SC), F32)],
        compiler_params=_cparams(("parallel",)),
        name="decode_mix",
    )(z, knt, kct_all, vct_all, lw["sel"], lw["sink_tab"], lw["g_attn_tab"],
      z, z, z, z, z, h0_all, cbuf_all, sbuf_all, *_mixer_params(lw), *extra_args)
    return outs


_OUT_PROJ_CHUNK = 512


def _out_proj_kernel(ma_ref, mb_ref, x_ref, mas_ref, mbs_ref, xs_ref, w_ref, g_ref,
                     x1_ref, hf_ref, x1s_ref, hfs_ref):
    _out_proj_rows(ma_ref, mb_ref, x_ref, w_ref, g_ref, x1_ref, hf_ref)

    @pl.when(pl.program_id(0) == pl.num_programs(0) - 1)
    def _():
        _out_proj_rows(mas_ref, mbs_ref, xs_ref, w_ref, g_ref, x1s_ref, hfs_ref)


def _out_proj_rows(ma_ref, mb_ref, x_ref, w_ref, g_ref, x1_ref, hf_ref):
    ma = ma_ref[...]
    mb = mb_ref[...]
    ssq = None
    for c0 in range(0, D_MODEL, _OUT_PROJ_CHUNK):
        cs = slice(c0, c0 + _OUT_PROJ_CHUNK)
        acc = jnp.dot(ma, w_ref[:D_ATTN, cs], preferred_element_type=F32)
        acc = acc + jnp.dot(mb, w_ref[D_ATTN:, cs], preferred_element_type=F32)
        x1 = x_ref[:, cs] + acc
        x1_ref[:, cs] = x1
        part = jnp.sum(x1 * x1, axis=-1, keepdims=True)
        ssq = part if ssq is None else ssq + part
    scale = lax.rsqrt(ssq * (1.0 / D_MODEL) + RMS_EPS)
    for c0 in range(0, D_MODEL, _OUT_PROJ_CHUNK):
        cs = slice(c0, c0 + _OUT_PROJ_CHUNK)
        hf_ref[:, cs] = (x1_ref[:, cs] * scale * g_ref[:, cs]).astype(hf_ref.dtype)


def _out_proj(ma, mb, x, mas, mbs, xs, w_all_bf, g_all, *, layer, w_layer, bm):
    m = x.shape[0]
    ns = xs.shape[0]
    blk = lambda width: pl.BlockSpec((bm, width), lambda i: (i, 0))
    whole = lambda width: pl.BlockSpec((ns, width), lambda i: (0, 0))
    return pl.pallas_call(
        _out_proj_kernel,
        grid=(m // bm,),
        in_specs=[blk(D_ATTN), blk(D_LRU + D_SC), blk(D_MODEL),
                  whole(D_ATTN), whole(D_LRU + D_SC), whole(D_MODEL),
                  _resident((None, D_MODEL, D_MODEL), lambda i: (w_layer, 0, 0)),
                  pl.BlockSpec((None, 1, D_MODEL), lambda i: (layer, 0, 0))],
        out_specs=[blk(D_MODEL), blk(D_MODEL), whole(D_MODEL), whole(D_MODEL)],
        out_shape=[jax.ShapeDtypeStruct((m, D_MODEL), F32), jax.ShapeDtypeStruct((m, D_MODEL), BF16),
                   jax.ShapeDtypeStruct((ns, D_MODEL), F32), jax.ShapeDtypeStruct((ns, D_MODEL), BF16)],
        compiler_params=_cparams(("arbitrary",)),
        name="out_proj",
    )(ma, mb, x, mas, mbs, xs, w_all_bf, g_all)


_FFN_DOWN_CHUNK = 512
_FFN_X1_CHUNK = 256


def _ffn_kernel(hf_ref, x1_ref, hfs_ref, x1s_ref, wg_ref, wu_ref, wd_ref, gfin_ref, *rest, final_norm,
                cast_cols):
    n_cast = len(cast_cols)
    o_ref, os_ref = rest[n_cast:n_cast + 2]
    f = pl.program_id(1)
    n_f = pl.num_programs(1)

    @pl.when(pl.program_id(0) == pl.num_programs(0) - 1)
    def _():
        hfs = hfs_ref[...]
        gate_s = jnp.dot(hfs, wg_ref[...], preferred_element_type=F32)
        up_s = jnp.dot(hfs, wu_ref[...], preferred_element_type=F32)
        hid_s = (gate_s * jax.nn.sigmoid(gate_s) * up_s).astype(BF16)
        down_s = jnp.dot(hid_s, wd_ref[...], preferred_element_type=F32)

        @pl.when(f == 0)
        def _():
            os_ref[...] = down_s

        @pl.when(f > 0)
        def _():
            os_ref[...] += down_s

        @pl.when(f == n_f - 1)
        def _():
            x2s = os_ref[...] + x1s_ref[...]
            os_ref[...] = _rms(x2s, gfin_ref[...]) if final_norm else x2s

    def tile_step(first):
        hf = hf_ref[...]
        gate = jnp.dot(hf, wg_ref[...], preferred_element_type=F32)
        up = jnp.dot(hf, wu_ref[...], preferred_element_type=F32)
        hid = (gate * jax.nn.sigmoid(gate) * up).astype(BF16)
        x1c = x1_ref[...]
        for c0 in range(0, D_MODEL, _FFN_DOWN_CHUNK):
            down = jnp.dot(hid, wd_ref[:, c0:c0 + _FFN_DOWN_CHUNK], preferred_element_type=F32)
            for h0 in range(0, _FFN_DOWN_CHUNK, _FFN_X1_CHUNK):
                cs = slice(c0 + h0, c0 + h0 + _FFN_X1_CHUNK)
                part = down[:, h0:h0 + _FFN_X1_CHUNK] + jnp.where(f == (c0 + h0) // _FFN_X1_CHUNK, x1c, 0.0)
                o_ref[:, cs] = part if first else o_ref[:, cs] + part

    @pl.when(f == 0)
    def _():
        tile_step(True)

    @pl.when(f > 0)
    def _():
        tile_step(False)

    if final_norm:
        @pl.when(f == pl.num_programs(1) - 1)
        def _():
            o_ref[...] = _rms(o_ref[...], gfin_ref[...])

    for src, dst, n_col in zip(rest[:n_cast], rest[n_cast + 2:], cast_cols):
        @pl.when(f < n_col)
        def _(src=src, dst=dst):
            dst[...] = src[...].astype(BF16)


def _ffn_cast_tiling(w_all, n_i, nf):
    _, k, n = w_all.shape
    if k % n_i or (k // n_i) % (2 * SUBLANES):
        return None
    for n_col in range(nf, 0, -1):
        if n % n_col == 0 and (n // n_col) % LANES == 0:
            return k // n_i, n // n_col, n_col
    return None


def _can_cast_in_ffn(weights, m, bm, tf=512):
    return all(_ffn_cast_tiling(w, m // bm, D_FF // tf) is not None for w in weights)


def _ffn(hf, x1, hfs, x1s, w_gu_bf, w_d_bf, g_final, *, layer, bm, tf=512, final_norm, cast_next=()):
    m = hf.shape[0]
    ns = hfs.shape[0]
    nf = D_FF // tf
    n_x1 = D_MODEL // _FFN_X1_CHUNK
    assert nf >= n_x1
    whole = lambda rows: pl.BlockSpec((rows, D_MODEL), lambda i, f: (0, 0))
    in_specs = [pl.BlockSpec((bm, D_MODEL), lambda i, f: (i, 0)),
                pl.BlockSpec((bm, _FFN_X1_CHUNK), lambda i, f: (i, jnp.minimum(f, n_x1 - 1))),
                whole(ns), whole(ns),
                pl.BlockSpec((None, D_MODEL, tf), lambda i, f: (layer, 0, f)),
                pl.BlockSpec((None, D_MODEL, tf), lambda i, f: (layer, 0, nf + f)),
                pl.BlockSpec((None, tf, D_MODEL), lambda i, f: (layer, f, 0)),
                whole(1)]
    out_specs = [pl.BlockSpec((bm, D_MODEL), lambda i, f: (i, 0)), whole(ns)]
    out_shape = [jax.ShapeDtypeStruct((m, D_MODEL), F32), jax.ShapeDtypeStruct((ns, D_MODEL), F32)]
    args = [hf, x1, hfs, x1s, w_gu_bf, w_gu_bf, w_d_bf, g_final]
    cast_cols = []
    for w_next, layer_next in cast_next:
        rows, width, n_col = _ffn_cast_tiling(w_next, m // bm, nf)
        col = lambda f, n_col=n_col: jnp.minimum(f, n_col - 1)
        in_specs.append(pl.BlockSpec((None, rows, width),
                                     lambda i, f, col=col, lyr=layer_next: (lyr, i, col(f))))
        out_specs.append(pl.BlockSpec((None, rows, width), lambda i, f, col=col: (0, i, col(f))))
        out_shape.append(jax.ShapeDtypeStruct((1,) + w_next.shape[1:], BF16))
        args.append(w_next)
        cast_cols.append(n_col)
    return pl.pallas_call(
        functools.partial(_ffn_kernel, final_norm=final_norm, cast_cols=tuple(cast_cols)),
        grid=(m // bm, nf), in_specs=in_specs, out_specs=out_specs, out_shape=out_shape,
        compiler_params=_cparams(("arbitrary", "arbitrary")), name="ffn",
    )(*args)


def _rope_tables(pos):
    half = HEAD_DIM // 2
    inv = ROPE_THETA ** (-jnp.arange(half, dtype=F32) / half)
    ang = pos.astype(F32)[:, None] * inv[None, :]
    cos, sin = jnp.cos(ang), jnp.sin(ang)
    cos_t = jnp.tile(cos, (1, LANES // half))
    sin_t = jnp.tile(jnp.concatenate([-sin, sin], axis=1), (1, LANES // HEAD_DIM))
    return cos_t, sin_t


def _block_diag(w):
    dd, hh, blk, _ = w.shape
    eye = jnp.eye(hh, dtype=w.dtype)
    return (eye[None, :, None, :, None] * w[:, :, :, None, :]).reshape(dd, hh * blk, hh * blk)


def kernel(x_prompt, x_sample, state_lru_h, state_lru_conv, cache_swa_k, cache_swa_v, state_sconv,
           norm_mix, w_in, norm_grp, w_out, lru_conv_w, lru_conv_b, lru_w_a, lru_b_a, lru_w_i, lru_b_i,
           lru_lambda, sc_conv_w, attn_sinks, norm_ffn, ffn_w_gu, ffn_w_down, norm_final):
    n_p, t_p, _ = x_prompt.shape
    n_s, t_s, _ = x_sample.shape
    depth = w_in.shape[0]
    wb = cache_swa_k.shape[2]
    assert t_s == 1 and wb == WINDOW and n_s % _DECODE_SEQS == 0
    assert t_p % (WINDOW * _ATTN_SUB_BLOCKS) == 0 and t_p % _LRU_CHUNK == 0

    bm_p = _PROJ_ROWS if t_p % _PROJ_ROWS == 0 else _LRU_CHUNK
    bm_ffn = _FFN_ROWS if (n_p * t_p) % _FFN_ROWS == 0 else bm_p

    cos_p, sin_p = _rope_tables(jnp.arange(t_p, dtype=jnp.int32))
    cos_s, sin_s = _rope_tables(jnp.full((n_s,), PAST_LEN, dtype=jnp.int32))

    sel = (jnp.arange(D_ATTN)[:, None] % HEAD_DIM == jnp.arange(D_KV)[None, :] % HEAD_DIM).astype(BF16)

    xp = x_prompt.reshape(n_p * t_p, D_MODEL)
    xs = x_sample.reshape(n_s, D_MODEL)
    row = lambda v: v.reshape(1, -1)
    p_states, s_states = [], []
    piggy = (_can_cast_in_attn(n_p, t_p)
             and _can_cast_in_ffn((w_in, ffn_w_gu, ffn_w_down, w_out), n_p * t_p, bm_ffn))
    if piggy:
        w_in_l, l_in = w_in[:1].astype(BF16), 0
    else:
        w_in_bf = w_in.astype(BF16)
        w_out_bf = w_out.astype(BF16)
        w_gu_bf = ffn_w_gu.astype(BF16)
        w_d_bf = ffn_w_down.astype(BF16)
    nbk = _DECODE_SEQS
    kct_all = cache_swa_k.transpose(0, 1, 3, 4, 2).reshape(depth, n_s, D_KV, wb)
    vct_all = cache_swa_v.transpose(0, 1, 3, 4, 2).reshape(depth, n_s, D_KV, wb)
    cbuf_all = state_lru_conv.transpose(0, 2, 1, 3)
    g_mix = norm_mix.reshape(depth, 1, D_MODEL)
    g_ffn = norm_ffn.reshape(depth, 1, D_MODEL)
    rows3 = lambda v: v.reshape(depth, 1, -1)
    g_attn_all = rows3(norm_grp[:, :D_ATTN])
    lw = dict(
        conv_w=lru_conv_w, conv_b=rows3(lru_conv_b),
        w_gates=jnp.concatenate([_block_diag(lru_w_a), _block_diag(lru_w_i)], axis=-1).astype(BF16),
        b_a=rows3(lru_b_a), b_i=rows3(lru_b_i), lam=rows3(lru_lambda),
        sc_w=sc_conv_w, g_lru=rows3(norm_grp[:, D_ATTN:D_ATTN + D_LRU]), g_sc=rows3(norm_grp[:, D_ATTN + D_LRU:]),
        sel=sel,
        sink_tab=jnp.broadcast_to(attn_sinks[:, :, None], (depth, N_HEADS, LANES)),
        g_attn_tab=jnp.tile(norm_grp[:, :D_ATTN].reshape(depth, N_HEADS, HEAD_DIM), (1, 1, LANES // HEAD_DIM)),
    )
    for l in range(depth):
        last = l == depth - 1

        if not piggy:
            w_in_l, w_out_l, w_gu_l, w_d_l, l_in, l_w = w_in_bf, w_out_bf, w_gu_bf, w_d_bf, l, l
        qkv, mb, h8, x8, g8, zs = _proj_lru(xp, g_mix, w_in_l, cos_p, sin_p, xs, cos_s, sin_s, lw, layer=l,
                                            w_layer=l_in, n_seq=n_p, seq=t_p, bm=bm_p)
        if piggy and l == 0:
            ma, w_gu_l, w_out_l, w_d_l = _attn_prompt(qkv, attn_sinks, g_attn_all, layer=l, n_seq=n_p, seq=t_p,
                                                      cast=[(ffn_w_gu, l), (w_out, l), (ffn_w_down, l)])
            l_w = 0
        else:
            ma = _attn_prompt(qkv, attn_sinks, g_attn_all, layer=l, n_seq=n_p, seq=t_p)
        z3 = qkv.reshape(n_p, t_p, _D_QKV)
        wbp = min(WINDOW, t_p)
        p_states.append((
            h8[:, SUBLANES - 1],
            x8[:, SUBLANES - (LRU_CONV_W - 1):],
            z3[:, t_p - wbp:, D_ATTN:D_ATTN + D_KV].reshape(n_p, wbp, N_KV_HEADS, HEAD_DIM),
            z3[:, t_p - wbp:, D_ATTN + D_KV:D_ATTN + 2 * D_KV].reshape(n_p, wbp, N_KV_HEADS, HEAD_DIM),
            g8[:, SUBLANES - (SC_CONV_W - 1):],
        ))

        knt = zs[:, D_ATTN:D_ATTN + 2 * D_KV].reshape(n_s // nbk, nbk, 2 * D_KV).transpose(0, 2, 1)
        oat, ors, k_stack, v_stack, h_new, c_new, s_new = _decode_mix(
            zs, knt, kct_all, vct_all, state_lru_h, cbuf_all, state_sconv, lw, layer=l, nbk=nbk,
            stacked=None if l == 0 else (k_stack, v_stack))
        c_new = c_new.transpose(1, 0, 2)
        mas = oat[:, :, :HEAD_DIM].reshape(n_s, D_ATTN)
        s_states.append((h_new, c_new, s_new))

        x1, hf, x1s, hfs = _out_proj(ma, mb, xp, mas, ors, xs, w_out_l, g_ffn, layer=l, w_layer=l_w, bm=bm_p)
        if piggy and not last:
            nxt = [(w, l + 1) for w in (w_in, ffn_w_gu, ffn_w_down, w_out)]
            xp, xs, w_in_l, w_gu_l, w_d_l, w_out_l = _ffn(hf, x1, hfs, x1s, w_gu_l, w_d_l, row(norm_final),
                                                          layer=l_w, bm=bm_ffn, final_norm=last, cast_next=nxt)
        else:
            xp, xs = _ffn(hf, x1, hfs, x1s, w_gu_l, w_d_l, row(norm_final), layer=l_w, bm=bm_ffn,
                          final_norm=last)

    y_prompt = xp.reshape(n_p, t_p, D_MODEL)
    y_sample = xs.reshape(n_s, t_s, D_MODEL)
    stack = lambda states, k: jnp.stack([st[k] for st in states])
    untranspose = lambda c: c.reshape(depth, n_s, N_KV_HEADS, HEAD_DIM, wb).transpose(0, 1, 4, 2, 3)
    return (y_prompt, y_sample,
            stack(p_states, 0), stack(p_states, 1), stack(p_states, 2), stack(p_states, 3), stack(p_states, 4),
            stack(s_states, 0), stack(s_states, 1), untranspose(k_stack), untranspose(v_stack),
            stack(s_states, 2))
```

```python
import functools

import jax
import jax.numpy as jnp
from jax import lax
from jax.experimental import pallas as pl
from jax.experimental.pallas import tpu as pltpu

F32 = jnp.float32
BF16 = jnp.bfloat16

D_MODEL = 2048
D_ATTN = 1024
D_LRU = 512
D_SC = 512
HEAD_DIM = 64
N_HEADS = 16
N_KV_HEADS = 4
N_GROUP = 4
D_KV = 256
WINDOW = 128
ROPE_THETA = 10000.0
N_LRU_HEADS = 8
LRU_BLK = 64
LRU_CONV_W = 4
LRU_C = 8.0
SC_CONV_W = 3
D_FF = 5632
D_IN = 4096
RMS_EPS = 1e-6
PAST_LEN = 8192

LANES = 128
SUBLANES = 8
VMEM_LIMIT_BYTES = 56 * 1024 * 1024

_COL_UX, _COL_GATE, _COL_B, _COL_C, _COL_H = 3, 4, 5, 6, 7

_PROJ_ROWS = 512
_FFN_ROWS = 1024
_DECODE_SEQS = 16


def _cparams(sem):
    return pltpu.CompilerParams(dimension_semantics=sem, vmem_limit_bytes=VMEM_LIMIT_BYTES)


def _rms(x, g):
    return x * lax.rsqrt(jnp.mean(x * x, axis=-1, keepdims=True) + RMS_EPS) * g


_IN_PROJ_CHUNK = 512


def _in_proj_kernel(x_ref, g_ref, w_ref, cos_ref, sin_ref, z_ref):
    h = _rms(x_ref[...], g_ref[...]).astype(BF16)
    bm = h.shape[0]
    lane = lax.broadcasted_iota(jnp.int32, (bm, LANES), 1)
    lo32 = (lane % HEAD_DIM) < (HEAD_DIM // 2)
    cos = cos_ref[...]
    sin = sin_ref[...]

    def rope(a):
        sw = jnp.where(lo32, pltpu.roll(a, LANES - HEAD_DIM // 2, 1), pltpu.roll(a, HEAD_DIM // 2, 1))
        return a * cos + sw * sin

    rope_cols = D_ATTN + D_KV
    for c0 in range(0, D_IN, _IN_PROJ_CHUNK):
        acc = jnp.dot(h, w_ref[:, c0:c0 + _IN_PROJ_CHUNK], preferred_element_type=F32)
        for c in range(0, _IN_PROJ_CHUNK, LANES):
            a = acc[:, c:c + LANES]
            z_ref[:, c0 + c:c0 + c + LANES] = rope(a) if c0 + c < rope_cols else a


def _resident(block_shape, index_map):
    return pl.BlockSpec(block_shape, index_map, pipeline_mode=pl.Buffered(1))


def _layer_spec(layer, shape):
    return pl.BlockSpec((None,) + shape, lambda *ids: (layer,) + (0,) * len(shape))


def _mixer_param_specs(layer):
    return [_layer_spec(layer, s) for s in (
        (LRU_CONV_W, D_LRU), (1, D_LRU), (D_LRU, 2 * D_LRU), (1, D_LRU), (1, D_LRU), (1, D_LRU),
        (SC_CONV_W, D_SC), (1, D_LRU), (1, D_SC))]


def _mixer_params(lw):
    return [lw[k] for k in ("conv_w", "conv_b", "w_gates", "b_a", "b_i", "lam", "sc_w", "g_lru", "g_sc")]


def _cast_specs(w_all, layer, steps, step_index):
    _, k, n = w_all.shape
    r = k // steps
    assert r * steps == k and r % (2 * SUBLANES) == 0
    return (pl.BlockSpec((None, r, n), lambda *ids: (layer, step_index(*ids), 0)),
            pl.BlockSpec((None, r, n), lambda *ids: (0, step_index(*ids), 0)),
            jax.ShapeDtypeStruct((1, k, n), BF16))


def _attn_prompt_kernel(sink_ref, q_ref, kc_ref, kp_ref, vc_ref, vp_ref, g_ref, *rest, n_sub, n_cast, layer):
    o_ref = rest[n_cast]
    for src, dst in zip(rest[:n_cast], rest[n_cast + 1:]):
        dst[...] = src[...].astype(BF16)
    b = pl.program_id(1)
    L = WINDOW

    lane = lax.broadcasted_iota(jnp.int32, (2 * L, LANES), 1)
    lo = lane < HEAD_DIM
    row = lax.broadcasted_iota(jnp.int32, (2 * L, 1), 0)
    top = row < L

    qi = lax.broadcasted_iota(jnp.int32, (2 * L, 4 * L), 0) % L
    sj = lax.broadcasted_iota(jnp.int32, (2 * L, 4 * L), 1) % (2 * L)
    diff = L + qi - sj
    band = (diff >= 0) & (diff < WINDOW)
    bias_inner = jnp.where(band, 0.0, -jnp.inf).astype(F32)
    bias_first = jnp.where(band & ((sj >= L) | (b > 0)), 0.0, -jnp.inf).astype(F32)

    zeros = jnp.zeros((2 * L, LANES), F32)
    ones_lo = jnp.where(lo, 1.0, 0.0).astype(F32)
    ones_hi = 1.0 - ones_lo

    units = [(sub, kh) for sub in range(n_sub) for kh in range(N_KV_HEADS)]
    prep, scores, vmats, probs, sink_terms = {}, {}, {}, {}, {}
    outs = {sub: [] for sub in range(n_sub)}

    def stage_scores(sub, kh):
        if sub not in prep:
            rows = slice(sub * L, (sub + 1) * L)
            k_prev = kp_ref[...] if sub == 0 else kc_ref[(sub - 1) * L:sub * L, :]
            v_prev = vp_ref[...] if sub == 0 else vc_ref[(sub - 1) * L:sub * L, :]
            prep[sub] = ((q_ref[rows, :] * (HEAD_DIM ** -0.5)).astype(BF16),
                         jnp.concatenate([k_prev, kc_ref[rows, :]], axis=0),
                         jnp.concatenate([v_prev, vc_ref[rows, :]], axis=0))
        qb, kk, vv = prep[sub]
        c0 = LANES * (kh // 2)
        kx = kk[:, c0:c0 + LANES]
        vx = vv[:, c0:c0 + LANES]
        kr = pltpu.roll(kx, HEAD_DIM, 1)
        vr = pltpu.roll(vx, HEAD_DIM, 1)
        if kh % 2 == 0:
            k_lo, k_hi = jnp.where(lo, kx, zeros), jnp.where(lo, zeros, kr)
            v_lo, v_hi = jnp.where(lo, vx, zeros), jnp.where(lo, zeros, vr)
        else:
            k_lo, k_hi = jnp.where(lo, kr, zeros), jnp.where(lo, zeros, kx)
            v_lo, v_hi = jnp.where(lo, vr, zeros), jnp.where(lo, zeros, vx)
        kmat = jnp.concatenate([k_lo, k_hi], axis=0).astype(BF16)
        qs = jnp.concatenate([qb[:, 2 * LANES * kh:2 * LANES * kh + LANES],
                              qb[:, 2 * LANES * kh + LANES:2 * LANES * (kh + 1)]], axis=0)
        scores[sub, kh] = lax.dot_general(qs, kmat, (((1,), (1,)), ((), ())), preferred_element_type=F32)
        vmats[sub, kh] = jnp.concatenate([jnp.concatenate([v_lo, ones_lo], axis=1),
                                          jnp.concatenate([v_hi, ones_hi], axis=1)], axis=0).astype(BF16)

    def stage_softmax(sub, kh):
        s = scores.pop((sub, kh)) + (bias_first if sub == 0 else bias_inner)
        sink_lo = jnp.where(top, sink_ref[layer, 4 * kh + 0], sink_ref[layer, 4 * kh + 2])
        sink_hi = jnp.where(top, sink_ref[layer, 4 * kh + 1], sink_ref[layer, 4 * kh + 3])
        m_lo = jnp.maximum(jnp.max(s[:, :2 * L], axis=1, keepdims=True), sink_lo)
        m_hi = jnp.maximum(jnp.max(s[:, 2 * L:], axis=1, keepdims=True), sink_hi)
        probs[sub, kh] = jnp.concatenate([jnp.exp(s[:, :2 * L] - m_lo), jnp.exp(s[:, 2 * L:] - m_hi)],
                                         axis=1).astype(BF16)
        sink_terms[sub, kh] = jnp.where(lo, jnp.exp(sink_lo - m_lo), jnp.exp(sink_hi - m_hi))

    def stage_values(sub, kh):
        oe = jnp.dot(probs.pop((sub, kh)), vmats.pop((sub, kh)), preferred_element_type=F32)
        o = oe[:, :LANES] / (oe[:, LANES:] + sink_terms.pop((sub, kh)))
        outs[sub] += [o[:L], o[L:]]
        if kh == N_KV_HEADS - 1:
            out = jnp.concatenate(outs[sub], axis=1)
            o_ref[sub * L:(sub + 1) * L, :] = _rms(out, g_ref[...]).astype(o_ref.dtype)

    for stage in (stage_scores, stage_softmax, stage_values):
        for unit in units:
            stage(*unit)


_ATTN_SUB_BLOCKS = 4


_CAST_SLAB_BYTES = 8 * 1024 * 1024


def _attn_steps(n_seq, seq):
    return n_seq * (seq // (WINDOW * _ATTN_SUB_BLOCKS))


def _can_cast_in_attn(n_seq, seq):
    steps = _attn_steps(n_seq, seq)
    bf16_rows = 2 * SUBLANES
    return (D_MODEL % steps == 0 and D_FF % steps == 0
            and (D_MODEL // steps) % bf16_rows == 0 and (D_FF // steps) % bf16_rows == 0
            and (D_MODEL // steps) * 2 * D_FF * 4 <= _CAST_SLAB_BYTES)


def _attn_prompt(z, sinks_all, g_attn_all, *, layer, n_seq, seq, cast=()):
    L = WINDOW
    n_sub = _ATTN_SUB_BLOCKS
    nb = seq // (L * n_sub)
    kcol = D_ATTN // D_KV
    vcol = kcol + 1
    cur = lambda col: (lambda n, b: (n * nb + b, col))
    prev = lambda col: (lambda n, b: (jnp.maximum((n * nb + b) * n_sub - 1, 0), col))
    in_specs = [
        pl.BlockSpec(memory_space=pltpu.SMEM),
        pl.BlockSpec((L * n_sub, D_ATTN), cur(0)),
        pl.BlockSpec((L * n_sub, D_KV), cur(kcol)),
        pl.BlockSpec((L, D_KV), prev(kcol)),
        pl.BlockSpec((L * n_sub, D_KV), cur(vcol)),
        pl.BlockSpec((L, D_KV), prev(vcol)),
        pl.BlockSpec((None, 1, D_ATTN), lambda n, b: (layer, 0, 0)),
    ]
    out_specs = [pl.BlockSpec((L * n_sub, D_ATTN), cur(0))]
    out_shape = [jax.ShapeDtypeStruct((n_seq * seq, D_ATTN), BF16)]
    args = [sinks_all, z, z, z, z, z, g_attn_all]
    for w_all, w_layer in cast:
        i_spec, o_spec, o_shape = _cast_specs(w_all, w_layer, n_seq * nb, lambda n, b: n * nb + b)
        in_specs.append(i_spec)
        out_specs.append(o_spec)
        out_shape.append(o_shape)
        args.append(w_all)
    outs = pl.pallas_call(
        functools.partial(_attn_prompt_kernel, n_sub=n_sub, n_cast=len(cast), layer=layer),
        grid=(n_seq, nb),
        in_specs=in_specs,
        out_specs=out_specs,
        out_shape=out_shape,
        compiler_params=_cparams(("parallel", "arbitrary")),
        name="attn_prompt",
    )(*args)
    return outs[0] if not cast else outs


def _lru_gates(xc, wg_ref, ba, bi, lam):
    g = jnp.dot(xc.astype(BF16), wg_ref[...], preferred_element_type=F32)
    r = jax.nn.sigmoid(g[:, :D_LRU] + ba)
    gi = jax.nn.sigmoid(g[:, D_LRU:] + bi)
    nl = -lam
    softplus = jnp.maximum(nl, 0.0) + jnp.log1p(jnp.exp(-jnp.abs(nl)))
    log_a = -LRU_C * r * softplus
    a = jnp.exp(log_a)
    th = jnp.tanh(log_a)
    m2 = (-2.0 * th) / (1.0 - th)
    mult = jnp.where(m2 > 0.0, m2 * lax.rsqrt(m2), 0.0)
    return a, mult, gi


def _shift_rows(u, prev8, k):
    r = pltpu.roll(u, k, 0)
    pr = pltpu.roll(prev8, k, 0)
    row8 = lax.broadcasted_iota(jnp.int32, prev8.shape, 0)
    head = jnp.where(row8 < k, pr, r[:SUBLANES])
    return jnp.concatenate([head, r[SUBLANES:]], axis=0)


def _chunk_scan(a, b):
    n = a.shape[0]
    row = lax.broadcasted_iota(jnp.int32, a.shape, 0)
    d = 1
    while d < n:
        if d < SUBLANES:
            keep = row >= d
            b = jnp.where(keep, b + a * pltpu.roll(b, d, 0), b)
            a = jnp.where(keep, a * pltpu.roll(a, d, 0), a)
        else:
            b = jnp.concatenate([b[:d], b[d:] + a[d:] * b[:n - d]], axis=0)
            a = jnp.concatenate([a[:d], a[d:] * a[:n - d]], axis=0)
        d *= 2
    return a, b


def _scan_pitch(ln):
    assert ln % SUBLANES == 0
    return ln if ln % (2 * SUBLANES) == SUBLANES else ln + SUBLANES


def _strided_scan(a, b, h_prev, a_scr, b_scr):
    tc, ch = a.shape
    ln = tc // SUBLANES
    pitch = _scan_pitch(ln)
    nslab = ch // LANES
    for s in range(SUBLANES):
        for c in range(nslab):
            a_scr[c, pitch * s:pitch * s + ln, :] = a[ln * s:ln * (s + 1), c * LANES:(c + 1) * LANES]
            b_scr[c, pitch * s:pitch * s + ln, :] = b[ln * s:ln * (s + 1), c * LANES:(c + 1) * LANES]
    row8 = lax.broadcasted_iota(jnp.int32, (SUBLANES, LANES), 0)
    for c in range(nslab):
        h = jnp.zeros((SUBLANES, LANES), F32)
        acum = jnp.ones((SUBLANES, LANES), F32)
        for j in range(ln):
            idx = pl.ds(j, SUBLANES, stride=pitch)
            at = a_scr[c, idx, :]
            h = at * h + b_scr[c, idx, :]
            acum = at * acum
            b_scr[c, idx, :] = h
            a_scr[c, idx, :] = acum
        a_tot, b_tot = _chunk_scan(acum, h)
        hp = h_prev[:, c * LANES:(c + 1) * LANES]
        cin = jnp.where(row8 == 0, hp, pltpu.roll(b_tot + a_tot * hp, 1, 0))
        for j in range(ln):
            idx = pl.ds(j, SUBLANES, stride=pitch)
            b_scr[c, idx, :] = b_scr[c, idx, :] + a_scr[c, idx, :] * cin
    return jnp.concatenate(
        [jnp.concatenate([b_scr[c, pitch * s:pitch * s + ln, :] for c in range(nslab)], axis=1)
         for s in range(SUBLANES)], axis=0)


def _lru_sc_rows(ux, gate, ub, uc, uh, pos0, wrefs, carries, scan_scr):
    cw_ref, cb_ref, wg_ref, ba_ref, bi_ref, lam_ref, scw_ref, glru_ref, gsc_ref = wrefs
    cx_scr, cg_scr, ch_scr = carries
    tc = ux.shape[0]
    px = cx_scr[...]
    xc = _shift_rows(ux, px, 3) * cw_ref[0:1, :]
    xc = xc + _shift_rows(ux, px, 2) * cw_ref[1:2, :]
    xc = xc + _shift_rows(ux, px, 1) * cw_ref[2:3, :]
    xc = xc + ux * cw_ref[3:4, :]
    xc = xc + cb_ref[...]

    a, mult, gi = _lru_gates(xc, wg_ref, ba_ref[...], bi_ref[...], lam_ref[...])
    pos = pos0 + lax.broadcasted_iota(jnp.int32, (tc, 1), 0)
    mult = jnp.where(pos == 0, 1.0, mult)
    h = _strided_scan(a, mult * gi * xc, ch_scr[SUBLANES - 1:SUBLANES, :], *scan_scr)
    o_lru = h * jax.nn.gelu(gate, approximate=True)

    gch = uc * uh
    pg = cg_scr[...]
    y = _shift_rows(gch, pg, 2) * scw_ref[0:1, :]
    y = y + _shift_rows(gch, pg, 1) * scw_ref[1:2, :]
    y = y + gch * scw_ref[2:3, :]
    o_sc = ub * y

    h8, x8, g8 = h[tc - SUBLANES:], ux[tc - SUBLANES:], gch[tc - SUBLANES:]
    cx_scr[...] = x8
    cg_scr[...] = g8
    ch_scr[...] = h8
    return _rms(o_lru, glru_ref[...]), _rms(o_sc, gsc_ref[...]), h8, x8, g8


_LRU_CHUNK = 256


_D_QKV = D_ATTN + 2 * D_KV
_D_U = D_IN - _D_QKV


def _proj_lru_kernel(x_ref, g_ref, w_ref, cos_ref, sin_ref, xs_ref, coss_ref, sins_ref,
                     cw_ref, cb_ref, wg_ref, ba_ref, bi_ref,
                     lam_ref, scw_ref, glru_ref, gsc_ref, *rest, blocks_per_seq, n_cast):
    qkv_ref, mb_ref, h8_ref, x8_ref, g8_ref, zs_ref = rest[n_cast:n_cast + 6]
    zu_scr, cx_scr, cg_scr, ch_scr, sa_scr, sb_scr = rest[2 * n_cast + 6:]
    for src, dst in zip(rest[:n_cast], rest[n_cast + 6:2 * n_cast + 6]):
        dst[...] = src[...].astype(BF16)
    i = pl.program_id(0)

    @pl.when(i == pl.num_programs(0) - 1)
    def _():
        _in_proj_kernel(xs_ref, g_ref, w_ref, coss_ref, sins_ref, zs_ref)

    blk = i % blocks_per_seq

    @pl.when(blk == 0)
    def _():
        cx_scr[...] = jnp.zeros_like(cx_scr)
        cg_scr[...] = jnp.zeros_like(cg_scr)
        ch_scr[...] = jnp.zeros_like(ch_scr)

    h = _rms(x_ref[...], g_ref[...]).astype(BF16)
    bm = h.shape[0]
    for c0 in range(_D_QKV, D_IN, _IN_PROJ_CHUNK):
        zu_scr[:, c0 - _D_QKV:c0 - _D_QKV + _IN_PROJ_CHUNK] = jnp.dot(
            h, w_ref[:, c0:c0 + _IN_PROJ_CHUNK], preferred_element_type=F32)

    wrefs = (cw_ref, cb_ref, wg_ref, ba_ref, bi_ref, lam_ref, scw_ref, glru_ref, gsc_ref)
    tc = _LRU_CHUNK
    col = lambda k: slice(k * D_LRU, (k + 1) * D_LRU)
    for r0 in range(0, bm, tc):
        rows = slice(r0, r0 + tc)
        lru_n, sc_n, h8, x8, g8 = _lru_sc_rows(
            zu_scr[rows, col(0)], zu_scr[rows, col(1)], zu_scr[rows, col(2)], zu_scr[rows, col(3)],
            zu_scr[rows, col(4)], blk * bm + r0, wrefs, (cx_scr, cg_scr, ch_scr), (sa_scr, sb_scr))
        mb_ref[rows, :D_LRU] = lru_n.astype(mb_ref.dtype)
        mb_ref[rows, D_LRU:] = sc_n.astype(mb_ref.dtype)
    h8_ref[0] = h8
    x8_ref[0] = x8
    g8_ref[0] = g8

    lane = lax.broadcasted_iota(jnp.int32, (bm, LANES), 1)
    lo32 = (lane % HEAD_DIM) < (HEAD_DIM // 2)
    cos = cos_ref[...]
    sin = sin_ref[...]

    def rope(a):
        sw = jnp.where(lo32, pltpu.roll(a, LANES - HEAD_DIM // 2, 1), pltpu.roll(a, HEAD_DIM // 2, 1))
        return a * cos + sw * sin

    rope_cols = D_ATTN + D_KV
    for c0 in range(0, _D_QKV, _IN_PROJ_CHUNK):
        acc = jnp.dot(h, w_ref[:, c0:c0 + _IN_PROJ_CHUNK], preferred_element_type=F32)
        for c in range(0, _IN_PROJ_CHUNK, LANES):
            a = acc[:, c:c + LANES]
            qkv_ref[:, c0 + c:c0 + c + LANES] = rope(a) if c0 + c < rope_cols else a


def _proj_lru(x, g_all, w_all_bf, cos_t, sin_t, xs, cos_s, sin_s, lw, *, layer, w_layer, n_seq, seq, bm,
              cast=()):
    m = x.shape[0]
    ns = xs.shape[0]
    n_tab = cos_t.shape[0] // bm
    bps = seq // bm
    const = lambda shape: pl.BlockSpec(shape, lambda i: (0,) * len(shape))
    st = pl.BlockSpec((1, SUBLANES, D_LRU), lambda i: (i // bps, 0, 0))
    st_shape = jax.ShapeDtypeStruct((n_seq, SUBLANES, D_LRU), F32)
    in_specs = [pl.BlockSpec((bm, D_MODEL), lambda i: (i, 0)),
                pl.BlockSpec((None, 1, D_MODEL), lambda i: (layer, 0, 0)),
                _resident((None, D_MODEL, D_IN), lambda i: (w_layer, 0, 0)),
                pl.BlockSpec((bm, LANES), lambda i: (i % n_tab, 0)),
                pl.BlockSpec((bm, LANES), lambda i: (i % n_tab, 0)),
                const((ns, D_MODEL)), const((ns, LANES)), const((ns, LANES))] + _mixer_param_specs(layer)
    out_specs = [pl.BlockSpec((bm, _D_QKV), lambda i: (i, 0)),
                 pl.BlockSpec((bm, D_LRU + D_SC), lambda i: (i, 0)), st, st, st, const((ns, D_IN))]
    out_shape = [jax.ShapeDtypeStruct((m, _D_QKV), F32),
                 jax.ShapeDtypeStruct((m, D_LRU + D_SC), BF16), st_shape, st_shape, st_shape,
                 jax.ShapeDtypeStruct((ns, D_IN), F32)]
    args = [x, g_all, w_all_bf, cos_t, sin_t, xs, cos_s, sin_s] + _mixer_params(lw)
    for w_all, lyr in cast:
        i_spec, o_spec, o_shape = _cast_specs(w_all, lyr, m // bm, lambda i: i)
        in_specs.append(i_spec)
        out_specs.append(o_spec)
        out_shape.append(o_shape)
        args.append(w_all)
    return pl.pallas_call(
        functools.partial(_proj_lru_kernel, blocks_per_seq=bps, n_cast=len(cast)),
        grid=(m // bm,),
        in_specs=in_specs,
        out_specs=out_specs,
        out_shape=out_shape,
        scratch_shapes=[pltpu.VMEM((bm, _D_U), F32)] + [pltpu.VMEM((SUBLANES, D_LRU), F32)] * 3
        + [pltpu.VMEM((D_LRU // LANES, SUBLANES * _scan_pitch(_LRU_CHUNK // SUBLANES), LANES), F32)] * 2,
        compiler_params=_cparams(("arbitrary",)),
        name="proj_lru",
    )(*args)


def _decode_kernel(q_ref, knt_ref, kc_ref, vc_ref, sel_ref, sink_ref, gat_ref,
                   ux_ref, gate_ref, ub_ref, uc_ref, uh_ref, h0_ref, cbuf_ref, sbuf_ref,
                   cw_ref, cb_ref, wg_ref, ba_ref, bi_ref, lam_ref, scw_ref, glru_ref, gsc_ref,
                   *rest, nbk):
    oat_ref, ors_ref, ko_ref, vo_ref, hn_ref, cn_ref, sn_ref = rest[-7:]
    if len(ko_ref.shape) == 4:
        for d in range(1, ko_ref.shape[0]):
            ko_ref[d] = jnp.zeros(ko_ref.shape[1:], ko_ref.dtype)
            vo_ref[d] = jnp.zeros(vo_ref.shape[1:], vo_ref.dtype)
        ko_ref, vo_ref = ko_ref.at[0], vo_ref.at[0]
    wb = kc_ref.shape[2]
    hrow = lax.broadcasted_iota(jnp.int32, (N_HEADS, D_ATTN), 0)
    hcol = lax.broadcasted_iota(jnp.int32, (N_HEADS, D_ATTN), 1) // HEAD_DIM
    own = (hrow == hcol)
    qexp = jnp.concatenate(
        [jnp.where(own, jnp.broadcast_to(q_ref[i:i + 1, :] * (HEAD_DIM ** -0.5), (N_HEADS, D_ATTN)), 0.0)
         for i in range(nbk)], axis=0)
    qrow = jnp.dot(qexp.astype(BF16), sel_ref[...], preferred_element_type=F32)
    rows = nbk * N_HEADS
    grow = (lax.broadcasted_iota(jnp.int32, (rows, D_KV), 0) % N_HEADS) // N_GROUP
    gcol = lax.broadcasted_iota(jnp.int32, (rows, D_KV), 1) // HEAD_DIM
    kvmask = (grow == gcol)
    qm = jnp.where(kvmask, qrow, 0.0).astype(BF16)
    sink = jnp.concatenate([sink_ref[...][:, 0:1]] * nbk, axis=0)
    gat = jnp.concatenate([gat_ref[...]] * nbk, axis=0)
    lanek = lax.broadcasted_iota(jnp.int32, (D_KV, wb), 1)
    newest = lanek == wb - 1
    for i in range(nbk):
        ko_ref[i] = jnp.where(newest, jnp.broadcast_to(knt_ref[:D_KV, i:i + 1], (D_KV, wb)),
                              pltpu.roll(kc_ref[i], wb - 1, 1))
        vo_ref[i] = jnp.where(newest, jnp.broadcast_to(knt_ref[D_KV:, i:i + 1], (D_KV, wb)),
                              pltpu.roll(vc_ref[i], wb - 1, 1))
    s = jnp.concatenate(
        [jnp.dot(qm[i * N_HEADS:(i + 1) * N_HEADS], ko_ref[i].astype(BF16), preferred_element_type=F32)
         for i in range(nbk)], axis=0)
    m = jnp.maximum(jnp.max(s, axis=1, keepdims=True), sink)
    p = jnp.exp(s - m)
    p = (p / (jnp.sum(p, axis=1, keepdims=True) + jnp.exp(sink - m))).astype(BF16)
    of = jnp.concatenate(
        [lax.dot_general(p[i * N_HEADS:(i + 1) * N_HEADS], vo_ref[i].astype(BF16), (((1,), (1,)), ((), ())),
                         preferred_element_type=F32) for i in range(nbk)], axis=0)
    of = jnp.where(kvmask, of, 0.0)
    t = of[:, :LANES] + of[:, LANES:]
    o = t + pltpu.roll(t, HEAD_DIM, 1)
    rs = jnp.sum(o * o, axis=1, keepdims=True)
    for i in range(nbk):
        sl = slice(i * N_HEADS, (i + 1) * N_HEADS)
        ms = jnp.sum(rs[sl], axis=0, keepdims=True) * (0.5 / D_ATTN)
        oat_ref[i] = (o[sl] * lax.rsqrt(ms + RMS_EPS) * gat[sl]).astype(oat_ref.dtype)

    ux = ux_ref[...]
    xc = cbuf_ref[0] * cw_ref[0:1, :]
    xc = xc + cbuf_ref[1] * cw_ref[1:2, :]
    xc = xc + cbuf_ref[2] * cw_ref[2:3, :]
    xc = xc + ux * cw_ref[3:4, :]
    xc = xc + cb_ref[...]
    a, mult, gi = _lru_gates(xc, wg_ref, ba_ref[...], bi_ref[...], lam_ref[...])
    h = a * h0_ref[...] + mult * gi * xc
    o_lru = h * jax.nn.gelu(gate_ref[...], approximate=True)
    hn_ref[...] = h
    cn_ref[0] = cbuf_ref[1]
    cn_ref[1] = cbuf_ref[2]
    cn_ref[2] = ux
    gch = uc_ref[...] * uh_ref[...]
    y = sbuf_ref[:, 0, :] * scw_ref[0:1, :]
    y = y + sbuf_ref[:, 1, :] * scw_ref[1:2, :]
    y = y + gch * scw_ref[2:3, :]
    o_sc = ub_ref[...] * y
    sn_ref[:, 0, :] = sbuf_ref[:, 1, :]
    sn_ref[:, 1, :] = gch
    ors_ref[:, :D_LRU] = _rms(o_lru, glru_ref[...]).astype(ors_ref.dtype)
    ors_ref[:, D_LRU:] = _rms(o_sc, gsc_ref[...]).astype(ors_ref.dtype)


def _decode_mix(z, knt, kct_all, vct_all, h0_all, cbuf_all, sbuf_all, lw, *, layer, nbk=16, stacked=None):
    depth, ns, _, wb = kct_all.shape
    z512 = lambda col: pl.BlockSpec((nbk, D_LRU), lambda i: (i, col))
    const = lambda shape: pl.BlockSpec(shape, lambda i: (0,) * len(shape))
    cache_in = pl.BlockSpec((None, nbk, D_KV, wb), lambda i: (layer, i, 0, 0))
    cache_out = cache_in if stacked is not None else pl.BlockSpec((depth, nbk, D_KV, wb), lambda i: (0, i, 0, 0))
    n_in = 24
    extra_specs = [] if stacked is None else [pl.BlockSpec(memory_space=pl.ANY)] * 2
    extra_args = () if stacked is None else tuple(stacked)
    aliases = {} if stacked is None else {n_in: 2, n_in + 1: 3}
    outs = pl.pallas_call(
        functools.partial(_decode_kernel, nbk=nbk),
        grid=(ns // nbk,),
        input_output_aliases=aliases,
        in_specs=[pl.BlockSpec((nbk, D_ATTN), lambda i: (i, 0)),
                  pl.BlockSpec((None, 2 * D_KV, nbk), lambda i: (i, 0, 0)),
                  cache_in, cache_in,
                  const((D_ATTN, D_KV)), _layer_spec(layer, (N_HEADS, LANES)), _layer_spec(layer, (N_HEADS, LANES)),
                  z512(_COL_UX), z512(_COL_GATE), z512(_COL_B), z512(_COL_C), z512(_COL_H),
                  pl.BlockSpec((None, nbk, D_LRU), lambda i: (layer, i, 0)),
                  pl.BlockSpec((None, LRU_CONV_W - 1, nbk, D_LRU), lambda i: (layer, 0, i, 0)),
                  pl.BlockSpec((None, nbk, SC_CONV_W - 1, D_SC), lambda i: (layer, i, 0, 0))]
        + _mixer_param_specs(layer) + extra_specs,
        out_specs=[pl.BlockSpec((nbk, N_HEADS, LANES), lambda i: (i, 0, 0)),
                   pl.BlockSpec((nbk, D_LRU + D_SC), lambda i: (i, 0)),
                   cache_out, cache_out,
                   pl.BlockSpec((nbk, D_LRU), lambda i: (i, 0)),
                   pl.BlockSpec((LRU_CONV_W - 1, nbk, D_LRU), lambda i: (0, i, 0)),
                   pl.BlockSpec((nbk, SC_CONV_W - 1, D_SC), lambda i: (i, 0, 0))],
        out_shape=[jax.ShapeDtypeStruct((ns, N_HEADS, LANES), BF16),
                   jax.ShapeDtypeStruct((ns, D_LRU + D_SC), BF16),
                   jax.ShapeDtypeStruct((depth, ns, D_KV, wb), F32),
                   jax.ShapeDtypeStruct((depth, ns, D_KV, wb), F32),
                   jax.ShapeDtypeStruct((ns, D_LRU), F32),
                   jax.ShapeDtypeStruct((LRU_CONV_W - 1, ns, D_LRU), F32),
                   jax.ShapeDtypeStruct((ns, SC_CONV_W - 1, D_SC), F32)],
        compiler_params=_cparams(("parallel",)),
        name="decode_mix",
    )(z, knt, kct_all, vct_all, lw["sel"], lw["sink_tab"], lw["g_attn_tab"],
      z, z, z, z, z, h0_all, cbuf_all, sbuf_all, *_mixer_params(lw), *extra_args)
    return outs


_OUT_PROJ_CHUNK = 512


def _out_proj_kernel(ma_ref, mb_ref, x_ref, mas_ref, mbs_ref, xs_ref, w_ref, g_ref,
                     x1_ref, hf_ref, x1s_ref, hfs_ref):
    _out_proj_rows(ma_ref, mb_ref, x_ref, w_ref, g_ref, x1_ref, hf_ref)

    @pl.when(pl.program_id(0) == pl.num_programs(0) - 1)
    def _():
        _out_proj_rows(mas_ref, mbs_ref, xs_ref, w_ref, g_ref, x1s_ref, hfs_ref)


def _out_proj_rows(ma_ref, mb_ref, x_ref, w_ref, g_ref, x1_ref, hf_ref):
    ma = ma_ref[...]
    mb = mb_ref[...]
    ssq = None
    for c0 in range(0, D_MODEL, _OUT_PROJ_CHUNK):
        cs = slice(c0, c0 + _OUT_PROJ_CHUNK)
        acc = jnp.dot(ma, w_ref[:D_ATTN, cs], preferred_element_type=F32)
        acc = acc + jnp.dot(mb, w_ref[D_ATTN:, cs], preferred_element_type=F32)
        x1 = x_ref[:, cs] + acc
        x1_ref[:, cs] = x1
        part = jnp.sum(x1 * x1, axis=-1, keepdims=True)
        ssq = part if ssq is None else ssq + part
    scale = lax.rsqrt(ssq * (1.0 / D_MODEL) + RMS_EPS)
    for c0 in range(0, D_MODEL, _OUT_PROJ_CHUNK):
        cs = slice(c0, c0 + _OUT_PROJ_CHUNK)
        hf_ref[:, cs] = (x1_ref[:, cs] * scale * g_ref[:, cs]).astype(hf_ref.dtype)


def _out_proj(ma, mb, x, mas, mbs, xs, w_all_bf, g_all, *, layer, w_layer, bm):
    m = x.shape[0]
    ns = xs.shape[0]
    blk = lambda width: pl.BlockSpec((bm, width), lambda i: (i, 0))
    whole = lambda width: pl.BlockSpec((ns, width), lambda i: (0, 0))
    return pl.pallas_call(
        _out_proj_kernel,
        grid=(m // bm,),
        in_specs=[blk(D_ATTN), blk(D_LRU + D_SC), blk(D_MODEL),
                  whole(D_ATTN), whole(D_LRU + D_SC), whole(D_MODEL),
                  _resident((None, D_MODEL, D_MODEL), lambda i: (w_layer, 0, 0)),
                  pl.BlockSpec((None, 1, D_MODEL), lambda i: (layer, 0, 0))],
        out_specs=[blk(D_MODEL), blk(D_MODEL), whole(D_MODEL), whole(D_MODEL)],
        out_shape=[jax.ShapeDtypeStruct((m, D_MODEL), F32), jax.ShapeDtypeStruct((m, D_MODEL), BF16),
                   jax.ShapeDtypeStruct((ns, D_MODEL), F32), jax.ShapeDtypeStruct((ns, D_MODEL), BF16)],
        compiler_params=_cparams(("arbitrary",)),
        name="out_proj",
    )(ma, mb, x, mas, mbs, xs, w_all_bf, g_all)


_FFN_DOWN_CHUNK = 512
_FFN_X1_CHUNK = 256


def _ffn_kernel(hf_ref, x1_ref, hfs_ref, x1s_ref, wg_ref, wu_ref, wd_ref, gfin_ref, *rest, final_norm,
                n_cast):
    o_ref, os_ref = rest[n_cast:n_cast + 2]
    f = pl.program_id(1)
    n_f = pl.num_programs(1)

    @pl.when(pl.program_id(0) == pl.num_programs(0) - 1)
    def _():
        hfs = hfs_ref[...]
        gate_s = jnp.dot(hfs, wg_ref[...], preferred_element_type=F32)
        up_s = jnp.dot(hfs, wu_ref[...], preferred_element_type=F32)
        hid_s = (gate_s * jax.nn.sigmoid(gate_s) * up_s).astype(BF16)
        down_s = jnp.dot(hid_s, wd_ref[...], preferred_element_type=F32)

        @pl.when(f == 0)
        def _():
            os_ref[...] = down_s

        @pl.when(f > 0)
        def _():
            os_ref[...] += down_s

        @pl.when(f == n_f - 1)
        def _():
            x2s = os_ref[...] + x1s_ref[...]
            os_ref[...] = _rms(x2s, gfin_ref[...]) if final_norm else x2s

    def tile_step(first):
        hf = hf_ref[...]
        gate = jnp.dot(hf, wg_ref[...], preferred_element_type=F32)
        up = jnp.dot(hf, wu_ref[...], preferred_element_type=F32)
        hid = (gate * jax.nn.sigmoid(gate) * up).astype(BF16)
        x1c = x1_ref[...]
        for c0 in range(0, D_MODEL, _FFN_DOWN_CHUNK):
            down = jnp.dot(hid, wd_ref[:, c0:c0 + _FFN_DOWN_CHUNK], preferred_element_type=F32)
            for h0 in range(0, _FFN_DOWN_CHUNK, _FFN_X1_CHUNK):
                cs = slice(c0 + h0, c0 + h0 + _FFN_X1_CHUNK)
                part = down[:, h0:h0 + _FFN_X1_CHUNK] + jnp.where(f == (c0 + h0) // _FFN_X1_CHUNK, x1c, 0.0)
                o_ref[:, cs] = part if first else o_ref[:, cs] + part
        for src, dst in zip(rest[:n_cast], rest[n_cast + 2:]):
            dst[...] = src[...].astype(BF16)

    @pl.when(f == 0)
    def _():
        tile_step(True)

    @pl.when(f > 0)
    def _():
        tile_step(False)

    if final_norm:
        @pl.when(f == pl.num_programs(1) - 1)
        def _():
            o_ref[...] = _rms(o_ref[...], gfin_ref[...])


def _ffn_cast_tiling(w_all, n_i, nf):
    _, k, n = w_all.shape
    if k % n_i or (k // n_i) % (2 * SUBLANES):
        return None
    for n_col in range(nf, 0, -1):
        if n % n_col == 0 and (n // n_col) % LANES == 0:
            return k // n_i, n // n_col, n_col
    return None


def _can_cast_in_ffn(weights, m, bm, tf=512):
    return all(_ffn_cast_tiling(w, m // bm, D_FF // tf) is not None for w in weights)


def _ffn(hf, x1, hfs, x1s, w_gu_bf, w_d_bf, g_final, *, layer, bm, tf=512, final_norm, cast_next=()):
    m = hf.shape[0]
    ns = hfs.shape[0]
    nf = D_FF // tf
    n_x1 = D_MODEL // _FFN_X1_CHUNK
    assert nf >= n_x1
    whole = lambda rows: pl.BlockSpec((rows, D_MODEL), lambda i, f: (0, 0))
    in_specs = [pl.BlockSpec((bm, D_MODEL), lambda i, f: (i, 0)),
                pl.BlockSpec((bm, _FFN_X1_CHUNK), lambda i, f: (i, jnp.minimum(f, n_x1 - 1))),
                whole(ns), whole(ns),
                pl.BlockSpec((None, D_MODEL, tf), lambda i, f: (layer, 0, f)),
                pl.BlockSpec((None, D_MODEL, tf), lambda i, f: (layer, 0, nf + f)),
                pl.BlockSpec((None, tf, D_MODEL), lambda i, f: (layer, f, 0)),
                whole(1)]
    out_specs = [pl.BlockSpec((bm, D_MODEL), lambda i, f: (i, 0)), whole(ns)]
    out_shape = [jax.ShapeDtypeStruct((m, D_MODEL), F32), jax.ShapeDtypeStruct((ns, D_MODEL), F32)]
    args = [hf, x1, hfs, x1s, w_gu_bf, w_gu_bf, w_d_bf, g_final]
    for w_next, layer_next in cast_next:
        rows, width, n_col = _ffn_cast_tiling(w_next, m // bm, nf)
        col = lambda f, n_col=n_col: jnp.minimum(f, n_col - 1)
        in_specs.append(pl.BlockSpec((None, rows, width),
                                     lambda i, f, col=col, lyr=layer_next: (lyr, i, col(f))))
        out_specs.append(pl.BlockSpec((None, rows, width), lambda i, f, col=col: (0, i, col(f))))
        out_shape.append(jax.ShapeDtypeStruct((1,) + w_next.shape[1:], BF16))
        args.append(w_next)
    return pl.pallas_call(
        functools.partial(_ffn_kernel, final_norm=final_norm, n_cast=len(cast_next)),
        grid=(m // bm, nf), in_specs=in_specs, out_specs=out_specs, out_shape=out_shape,
        compiler_params=_cparams(("arbitrary", "arbitrary")), name="ffn",
    )(*args)


def _rope_tables(pos):
    half = HEAD_DIM // 2
    inv = ROPE_THETA ** (-jnp.arange(half, dtype=F32) / half)
    ang = pos.astype(F32)[:, None] * inv[None, :]
    cos, sin = jnp.cos(ang), jnp.sin(ang)
    cos_t = jnp.tile(cos, (1, LANES // half))
    sin_t = jnp.tile(jnp.concatenate([-sin, sin], axis=1), (1, LANES // HEAD_DIM))
    return cos_t, sin_t


def _block_diag(w):
    dd, hh, blk, _ = w.shape
    eye = jnp.eye(hh, dtype=w.dtype)
    return (eye[None, :, None, :, None] * w[:, :, :, None, :]).reshape(dd, hh * blk, hh * blk)


def kernel(x_prompt, x_sample, state_lru_h, state_lru_conv, cache_swa_k, cache_swa_v, state_sconv,
           norm_mix, w_in, norm_grp, w_out, lru_conv_w, lru_conv_b, lru_w_a, lru_b_a, lru_w_i, lru_b_i,
           lru_lambda, sc_conv_w, attn_sinks, norm_ffn, ffn_w_gu, ffn_w_down, norm_final):
    n_p, t_p, _ = x_prompt.shape
    n_s, t_s, _ = x_sample.shape
    depth = w_in.shape[0]
    wb = cache_swa_k.shape[2]
    assert t_s == 1 and wb == WINDOW and n_s % _DECODE_SEQS == 0
    assert t_p % (WINDOW * _ATTN_SUB_BLOCKS) == 0 and t_p % _LRU_CHUNK == 0

    bm_p = _PROJ_ROWS if t_p % _PROJ_ROWS == 0 else _LRU_CHUNK
    bm_ffn = _FFN_ROWS if (n_p * t_p) % _FFN_ROWS == 0 else bm_p

    cos_p, sin_p = _rope_tables(jnp.arange(t_p, dtype=jnp.int32))
    cos_s, sin_s = _rope_tables(jnp.full((n_s,), PAST_LEN, dtype=jnp.int32))

    sel = (jnp.arange(D_ATTN)[:, None] % HEAD_DIM == jnp.arange(D_KV)[None, :] % HEAD_DIM).astype(BF16)

    xp = x_prompt.reshape(n_p * t_p, D_MODEL)
    xs = x_sample.reshape(n_s, D_MODEL)
    row = lambda v: v.reshape(1, -1)
    p_states, s_states = [], []
    n_blk = (n_p * t_p) // bm_p
    piggy = (_can_cast_in_attn(n_p, t_p)
             and _can_cast_in_ffn((w_in, ffn_w_gu, ffn_w_down, w_out), n_p * t_p, bm_ffn)
             and D_MODEL % n_blk == 0 and (D_MODEL // n_blk) % (2 * SUBLANES) == 0)
    if piggy:
        w_in_l, l_in = w_in[:1].astype(BF16), 0
    else:
        w_in_bf = w_in.astype(BF16)
        w_out_bf = w_out.astype(BF16)
        w_gu_bf = ffn_w_gu.astype(BF16)
        w_d_bf = ffn_w_down.astype(BF16)
    nbk = _DECODE_SEQS
    kct_all = cache_swa_k.transpose(0, 1, 3, 4, 2).reshape(depth, n_s, D_KV, wb)
    vct_all = cache_swa_v.transpose(0, 1, 3, 4, 2).reshape(depth, n_s, D_KV, wb)
    cbuf_all = state_lru_conv.transpose(0, 2, 1, 3)
    g_mix = norm_mix.reshape(depth, 1, D_MODEL)
    g_ffn = norm_ffn.reshape(depth, 1, D_MODEL)
    rows3 = lambda v: v.reshape(depth, 1, -1)
    g_attn_all = rows3(norm_grp[:, :D_ATTN])
    lw = dict(
        conv_w=lru_conv_w, conv_b=rows3(lru_conv_b),
        w_gates=jnp.concatenate([_block_diag(lru_w_a), _block_diag(lru_w_i)], axis=-1).astype(BF16),
        b_a=rows3(lru_b_a), b_i=rows3(lru_b_i), lam=rows3(lru_lambda),
        sc_w=sc_conv_w, g_lru=rows3(norm_grp[:, D_ATTN:D_ATTN + D_LRU]), g_sc=rows3(norm_grp[:, D_ATTN + D_LRU:]),
        sel=sel,
        sink_tab=jnp.broadcast_to(attn_sinks[:, :, None], (depth, N_HEADS, LANES)),
        g_attn_tab=jnp.tile(norm_grp[:, :D_ATTN].reshape(depth, N_HEADS, HEAD_DIM), (1, 1, LANES // HEAD_DIM)),
    )
    for l in range(depth):
        last = l == depth - 1

        if not piggy:
            w_in_l, w_out_l, w_gu_l, w_d_l, l_in, l_w = w_in_bf, w_out_bf, w_gu_bf, w_d_bf, l, l
        first_casts = piggy and l == 0
        qkv, mb, h8, x8, g8, zs, *w_out_0 = _proj_lru(
            xp, g_mix, w_in_l, cos_p, sin_p, xs, cos_s, sin_s, lw, layer=l, w_layer=l_in, n_seq=n_p, seq=t_p,
            bm=bm_p, cast=[(w_out, l)] if first_casts else ())
        if first_casts:
            (w_out_l,) = w_out_0
            ma, w_gu_l, w_d_l = _attn_prompt(qkv, attn_sinks, g_attn_all, layer=l, n_seq=n_p, seq=t_p,
                                             cast=[(ffn_w_gu, l), (ffn_w_down, l)])
            l_w = 0
        else:
            ma = _attn_prompt(qkv, attn_sinks, g_attn_all, layer=l, n_seq=n_p, seq=t_p)
        z3 = qkv.reshape(n_p, t_p, _D_QKV)
        wbp = min(WINDOW, t_p)
        p_states.append((
            h8[:, SUBLANES - 1],
            x8[:, SUBLANES - (LRU_CONV_W - 1):],
            z3[:, t_p - wbp:, D_ATTN:D_ATTN + D_KV].reshape(n_p, wbp, N_KV_HEADS, HEAD_DIM),
            z3[:, t_p - wbp:, D_ATTN + D_KV:D_ATTN + 2 * D_KV].reshape(n_p, wbp, N_KV_HEADS, HEAD_DIM),
            g8[:, SUBLANES - (SC_CONV_W - 1):],
        ))

        knt = zs[:, D_ATTN:D_ATTN + 2 * D_KV].reshape(n_s // nbk, nbk, 2 * D_KV).transpose(0, 2, 1)
        oat, ors, k_stack, v_stack, h_new, c_new, s_new = _decode_mix(
            zs, knt, kct_all, vct_all, state_lru_h, cbuf_all, state_sconv, lw, layer=l, nbk=nbk,
            stacked=None if l == 0 else (k_stack, v_stack))
        c_new = c_new.transpose(1, 0, 2)
        mas = oat[:, :, :HEAD_DIM].reshape(n_s, D_ATTN)
        s_states.append((h_new, c_new, s_new))

        x1, hf, x1s, hfs = _out_proj(ma, mb, xp, mas, ors, xs, w_out_l, g_ffn, layer=l, w_layer=l_w, bm=bm_p)
        if piggy and not last:
            nxt = [(w, l + 1) for w in (w_in, ffn_w_gu, ffn_w_down, w_out)]
            xp, xs, w_in_l, w_gu_l, w_d_l, w_out_l = _ffn(hf, x1, hfs, x1s, w_gu_l, w_d_l, row(norm_final),
                                                          layer=l_w, bm=bm_ffn, final_norm=last, cast_next=nxt)
        else:
            xp, xs = _ffn(hf, x1, hfs, x1s, w_gu_l, w_d_l, row(norm_final), layer=l_w, bm=bm_ffn,
                          final_norm=last)

    y_prompt = xp.reshape(n_p, t_p, D_MODEL)
    y_sample = xs.reshape(n_s, t_s, D_MODEL)
    stack = lambda states, k: jnp.stack([st[k] for st in states])
    untranspose = lambda c: c.reshape(depth, n_s, N_KV_HEADS, HEAD_DIM, wb).transpose(0, 1, 4, 2, 3)
    return (y_prompt, y_sample,
            stack(p_states, 0), stack(p_states, 1), stack(p_states, 2), stack(p_states, 3), stack(p_states, 4),
            stack(s_states, 0), stack(s_states, 1), untranspose(k_stack), untranspose(v_stack),
            stack(s_states, 2))
```

```python
import functools

import jax
import jax.numpy as jnp
from jax import lax
from jax.experimental import pallas as pl
from jax.experimental.pallas import tpu as pltpu

F32 = jnp.float32
BF16 = jnp.bfloat16

D_MODEL = 2048
D_ATTN = 1024
D_LRU = 512
D_SC = 512
HEAD_DIM = 64
N_HEADS = 16
N_KV_HEADS = 4
N_GROUP = 4
D_KV = 256
WINDOW = 128
ROPE_THETA = 10000.0
N_LRU_HEADS = 8
LRU_BLK = 64
LRU_CONV_W = 4
LRU_C = 8.0
SC_CONV_W = 3
D_FF = 5632
D_IN = 4096
RMS_EPS = 1e-6
PAST_LEN = 8192

LANES = 128
SUBLANES = 8
VMEM_LIMIT_BYTES = 56 * 1024 * 1024

_COL_UX, _COL_GATE, _COL_B, _COL_C, _COL_H = 3, 4, 5, 6, 7

_PROJ_ROWS = 512
_FFN_ROWS = 1024
_DECODE_SEQS = 16


def _cparams(sem):
    return pltpu.CompilerParams(dimension_semantics=sem, vmem_limit_bytes=VMEM_LIMIT_BYTES)


def _rms(x, g):
    return x * lax.rsqrt(jnp.mean(x * x, axis=-1, keepdims=True) + RMS_EPS) * g


_IN_PROJ_CHUNK = 512


def _in_proj_kernel(x_ref, g_ref, w_ref, cos_ref, sin_ref, z_ref):
    h = _rms(x_ref[...], g_ref[...]).astype(BF16)
    bm = h.shape[0]
    lane = lax.broadcasted_iota(jnp.int32, (bm, LANES), 1)
    lo32 = (lane % HEAD_DIM) < (HEAD_DIM // 2)
    cos = cos_ref[...]
    sin = sin_ref[...]

    def rope(a):
        sw = jnp.where(lo32, pltpu.roll(a, LANES - HEAD_DIM // 2, 1), pltpu.roll(a, HEAD_DIM // 2, 1))
        return a * cos + sw * sin

    rope_cols = D_ATTN + D_KV
    for c0 in range(0, D_IN, _IN_PROJ_CHUNK):
        acc = jnp.dot(h, w_ref[:, c0:c0 + _IN_PROJ_CHUNK], preferred_element_type=F32)
        for c in range(0, _IN_PROJ_CHUNK, LANES):
            a = acc[:, c:c + LANES]
            z_ref[:, c0 + c:c0 + c + LANES] = rope(a) if c0 + c < rope_cols else a


def _resident(block_shape, index_map):
    return pl.BlockSpec(block_shape, index_map, pipeline_mode=pl.Buffered(1))


def _layer_spec(layer, shape):
    return pl.BlockSpec((None,) + shape, lambda *ids: (layer,) + (0,) * len(shape))


def _mixer_param_specs(layer):
    return [_layer_spec(layer, s) for s in (
        (LRU_CONV_W, D_LRU), (1, D_LRU), (D_LRU, 2 * D_LRU), (1, D_LRU), (1, D_LRU), (1, D_LRU),
        (SC_CONV_W, D_SC), (1, D_LRU), (1, D_SC))]


def _mixer_params(lw):
    return [lw[k] for k in ("conv_w", "conv_b", "w_gates", "b_a", "b_i", "lam", "sc_w", "g_lru", "g_sc")]


def _cast_specs(w_all, layer, steps, step_index):
    _, k, n = w_all.shape
    r = k // steps
    assert r * steps == k and r % (2 * SUBLANES) == 0
    return (pl.BlockSpec((None, r, n), lambda *ids: (layer, step_index(*ids), 0)),
            pl.BlockSpec((None, r, n), lambda *ids: (0, step_index(*ids), 0)),
            jax.ShapeDtypeStruct((1, k, n), BF16))


def _attn_prompt_kernel(sink_ref, q_ref, kc_ref, kp_ref, vc_ref, vp_ref, g_ref, *rest, n_sub, n_cast, layer):
    o_ref = rest[n_cast]
    for src, dst in zip(rest[:n_cast], rest[n_cast + 1:]):
        dst[...] = src[...].astype(BF16)
    b = pl.program_id(1)
    L = WINDOW

    lane = lax.broadcasted_iota(jnp.int32, (2 * L, LANES), 1)
    lo = lane < HEAD_DIM
    row = lax.broadcasted_iota(jnp.int32, (2 * L, 1), 0)
    top = row < L

    qi = lax.broadcasted_iota(jnp.int32, (2 * L, 4 * L), 0) % L
    sj = lax.broadcasted_iota(jnp.int32, (2 * L, 4 * L), 1) % (2 * L)
    diff = L + qi - sj
    band = (diff >= 0) & (diff < WINDOW)
    bias_inner = jnp.where(band, 0.0, -jnp.inf).astype(F32)
    bias_first = jnp.where(band & ((sj >= L) | (b > 0)), 0.0, -jnp.inf).astype(F32)

    zeros = jnp.zeros((2 * L, LANES), F32)
    ones_lo = jnp.where(lo, 1.0, 0.0).astype(F32)
    ones_hi = 1.0 - ones_lo

    units = [(sub, kh) for sub in range(n_sub) for kh in range(N_KV_HEADS)]
    prep, scores, vmats, probs, sink_terms = {}, {}, {}, {}, {}
    outs = {sub: [] for sub in range(n_sub)}

    def stage_scores(sub, kh):
        if sub not in prep:
            rows = slice(sub * L, (sub + 1) * L)
            k_prev = kp_ref[...] if sub == 0 else kc_ref[(sub - 1) * L:sub * L, :]
            v_prev = vp_ref[...] if sub == 0 else vc_ref[(sub - 1) * L:sub * L, :]
            prep[sub] = ((q_ref[rows, :] * (HEAD_DIM ** -0.5)).astype(BF16),
                         jnp.concatenate([k_prev, kc_ref[rows, :]], axis=0),
                         jnp.concatenate([v_prev, vc_ref[rows, :]], axis=0))
        qb, kk, vv = prep[sub]
        c0 = LANES * (kh // 2)
        kx = kk[:, c0:c0 + LANES]
        vx = vv[:, c0:c0 + LANES]
        kr = pltpu.roll(kx, HEAD_DIM, 1)
        vr = pltpu.roll(vx, HEAD_DIM, 1)
        if kh % 2 == 0:
            k_lo, k_hi = jnp.where(lo, kx, zeros), jnp.where(lo, zeros, kr)
            v_lo, v_hi = jnp.where(lo, vx, zeros), jnp.where(lo, zeros, vr)
        else:
            k_lo, k_hi = jnp.where(lo, kr, zeros), jnp.where(lo, zeros, kx)
            v_lo, v_hi = jnp.where(lo, vr, zeros), jnp.where(lo, zeros, vx)
        kmat = jnp.concatenate([k_lo, k_hi], axis=0).astype(BF16)
        qs = jnp.concatenate([qb[:, 2 * LANES * kh:2 * LANES * kh + LANES],
                              qb[:, 2 * LANES * kh + LANES:2 * LANES * (kh + 1)]], axis=0)
        scores[sub, kh] = lax.dot_general(qs, kmat, (((1,), (1,)), ((), ())), preferred_element_type=F32)
        vmats[sub, kh] = jnp.concatenate([jnp.concatenate([v_lo, ones_lo], axis=1),
                                          jnp.concatenate([v_hi, ones_hi], axis=1)], axis=0).astype(BF16)

    def stage_softmax(sub, kh):
        s = scores.pop((sub, kh)) + (bias_first if sub == 0 else bias_inner)
        sink_lo = jnp.where(top, sink_ref[layer, 4 * kh + 0], sink_ref[layer, 4 * kh + 2])
        sink_hi = jnp.where(top, sink_ref[layer, 4 * kh + 1], sink_ref[layer, 4 * kh + 3])
        m_lo = jnp.maximum(jnp.max(s[:, :2 * L], axis=1, keepdims=True), sink_lo)
        m_hi = jnp.maximum(jnp.max(s[:, 2 * L:], axis=1, keepdims=True), sink_hi)
        probs[sub, kh] = jnp.concatenate([jnp.exp(s[:, :2 * L] - m_lo), jnp.exp(s[:, 2 * L:] - m_hi)],
                                         axis=1).astype(BF16)
        sink_terms[sub, kh] = jnp.where(lo, jnp.exp(sink_lo - m_lo), jnp.exp(sink_hi - m_hi))

    def stage_values(sub, kh):
        oe = jnp.dot(probs.pop((sub, kh)), vmats.pop((sub, kh)), preferred_element_type=F32)
        o = oe[:, :LANES] / (oe[:, LANES:] + sink_terms.pop((sub, kh)))
        outs[sub] += [o[:L], o[L:]]
        if kh == N_KV_HEADS - 1:
            out = jnp.concatenate(outs[sub], axis=1)
            o_ref[sub * L:(sub + 1) * L, :] = _rms(out, g_ref[...]).astype(o_ref.dtype)

    for stage in (stage_scores, stage_softmax, stage_values):
        for unit in units:
            stage(*unit)


_ATTN_SUB_BLOCKS = 4


_CAST_SLAB_BYTES = 8 * 1024 * 1024


def _attn_steps(n_seq, seq):
    return n_seq * (seq // (WINDOW * _ATTN_SUB_BLOCKS))


def _can_cast_in_attn(n_seq, seq):
    steps = _attn_steps(n_seq, seq)
    bf16_rows = 2 * SUBLANES
    return (D_MODEL % steps == 0 and D_FF % steps == 0
            and (D_MODEL // steps) % bf16_rows == 0 and (D_FF // steps) % bf16_rows == 0
            and (D_MODEL // steps) * 2 * D_FF * 4 <= _CAST_SLAB_BYTES)


def _attn_prompt(z, sinks_all, g_attn_all, *, layer, n_seq, seq, cast=()):
    L = WINDOW
    n_sub = _ATTN_SUB_BLOCKS
    nb = seq // (L * n_sub)
    kcol = D_ATTN // D_KV
    vcol = kcol + 1
    cur = lambda col: (lambda n, b: (n * nb + b, col))
    prev = lambda col: (lambda n, b: (jnp.maximum((n * nb + b) * n_sub - 1, 0), col))
    in_specs = [
        pl.BlockSpec(memory_space=pltpu.SMEM),
        pl.BlockSpec((L * n_sub, D_ATTN), cur(0)),
        pl.BlockSpec((L * n_sub, D_KV), cur(kcol)),
        pl.BlockSpec((L, D_KV), prev(kcol)),
        pl.BlockSpec((L * n_sub, D_KV), cur(vcol)),
        pl.BlockSpec((L, D_KV), prev(vcol)),
        pl.BlockSpec((None, 1, D_ATTN), lambda n, b: (layer, 0, 0)),
    ]
    out_specs = [pl.BlockSpec((L * n_sub, D_ATTN), cur(0))]
    out_shape = [jax.ShapeDtypeStruct((n_seq * seq, D_ATTN), BF16)]
    args = [sinks_all, z, z, z, z, z, g_attn_all]
    for w_all, w_layer in cast:
        i_spec, o_spec, o_shape = _cast_specs(w_all, w_layer, n_seq * nb, lambda n, b: n * nb + b)
        in_specs.append(i_spec)
        out_specs.append(o_spec)
        out_shape.append(o_shape)
        args.append(w_all)
    outs = pl.pallas_call(
        functools.partial(_attn_prompt_kernel, n_sub=n_sub, n_cast=len(cast), layer=layer),
        grid=(n_seq, nb),
        in_specs=in_specs,
        out_specs=out_specs,
        out_shape=out_shape,
        compiler_params=_cparams(("parallel", "arbitrary")),
        name="attn_prompt",
    )(*args)
    return outs[0] if not cast else outs


def _lru_gates(xc, wg_ref, ba, bi, lam):
    g = jnp.dot(xc.astype(BF16), wg_ref[...], preferred_element_type=F32)
    r = jax.nn.sigmoid(g[:, :D_LRU] + ba)
    gi = jax.nn.sigmoid(g[:, D_LRU:] + bi)
    nl = -lam
    softplus = jnp.maximum(nl, 0.0) + jnp.log1p(jnp.exp(-jnp.abs(nl)))
    log_a = -LRU_C * r * softplus
    a = jnp.exp(log_a)
    th = jnp.tanh(log_a)
    m2 = (-2.0 * th) / (1.0 - th)
    mult = jnp.where(m2 > 0.0, m2 * lax.rsqrt(m2), 0.0)
    return a, mult, gi


def _shift_rows(u, prev8, k):
    r = pltpu.roll(u, k, 0)
    pr = pltpu.roll(prev8, k, 0)
    row8 = lax.broadcasted_iota(jnp.int32, prev8.shape, 0)
    head = jnp.where(row8 < k, pr, r[:SUBLANES])
    return jnp.concatenate([head, r[SUBLANES:]], axis=0)


def _chunk_scan(a, b):
    n = a.shape[0]
    row = lax.broadcasted_iota(jnp.int32, a.shape, 0)
    d = 1
    while d < n:
        if d < SUBLANES:
            keep = row >= d
            b = jnp.where(keep, b + a * pltpu.roll(b, d, 0), b)
            a = jnp.where(keep, a * pltpu.roll(a, d, 0), a)
        else:
            b = jnp.concatenate([b[:d], b[d:] + a[d:] * b[:n - d]], axis=0)
            a = jnp.concatenate([a[:d], a[d:] * a[:n - d]], axis=0)
        d *= 2
    return a, b


def _scan_pitch(ln):
    assert ln % SUBLANES == 0
    return ln if ln % (2 * SUBLANES) == SUBLANES else ln + SUBLANES


def _strided_scan(a, b, h_prev, a_scr, b_scr):
    tc, ch = a.shape
    ln = tc // SUBLANES
    pitch = _scan_pitch(ln)
    nslab = ch // LANES
    for s in range(SUBLANES):
        for c in range(nslab):
            a_scr[c, pitch * s:pitch * s + ln, :] = a[ln * s:ln * (s + 1), c * LANES:(c + 1) * LANES]
            b_scr[c, pitch * s:pitch * s + ln, :] = b[ln * s:ln * (s + 1), c * LANES:(c + 1) * LANES]
    row8 = lax.broadcasted_iota(jnp.int32, (SUBLANES, LANES), 0)
    for c in range(nslab):
        h = jnp.zeros((SUBLANES, LANES), F32)
        acum = jnp.ones((SUBLANES, LANES), F32)
        for j in range(ln):
            idx = pl.ds(j, SUBLANES, stride=pitch)
            at = a_scr[c, idx, :]
            h = at * h + b_scr[c, idx, :]
            acum = at * acum
            b_scr[c, idx, :] = h
            a_scr[c, idx, :] = acum
        a_tot, b_tot = _chunk_scan(acum, h)
        hp = h_prev[:, c * LANES:(c + 1) * LANES]
        cin = jnp.where(row8 == 0, hp, pltpu.roll(b_tot + a_tot * hp, 1, 0))
        for j in range(ln):
            idx = pl.ds(j, SUBLANES, stride=pitch)
            b_scr[c, idx, :] = b_scr[c, idx, :] + a_scr[c, idx, :] * cin
    return jnp.concatenate(
        [jnp.concatenate([b_scr[c, pitch * s:pitch * s + ln, :] for c in range(nslab)], axis=1)
         for s in range(SUBLANES)], axis=0)


def _lru_sc_rows(ux, gate, ub, uc, uh, pos0, wrefs, carries, scan_scr):
    cw_ref, cb_ref, wg_ref, ba_ref, bi_ref, lam_ref, scw_ref, glru_ref, gsc_ref = wrefs
    cx_scr, cg_scr, ch_scr = carries
    tc = ux.shape[0]
    px = cx_scr[...]
    xc = _shift_rows(ux, px, 3) * cw_ref[0:1, :]
    xc = xc + _shift_rows(ux, px, 2) * cw_ref[1:2, :]
    xc = xc + _shift_rows(ux, px, 1) * cw_ref[2:3, :]
    xc = xc + ux * cw_ref[3:4, :]
    xc = xc + cb_ref[...]

    a, mult, gi = _lru_gates(xc, wg_ref, ba_ref[...], bi_ref[...], lam_ref[...])
    pos = pos0 + lax.broadcasted_iota(jnp.int32, (tc, 1), 0)
    mult = jnp.where(pos == 0, 1.0, mult)
    h = _strided_scan(a, mult * gi * xc, ch_scr[SUBLANES - 1:SUBLANES, :], *scan_scr)
    o_lru = h * jax.nn.gelu(gate, approximate=True)

    gch = uc * uh
    pg = cg_scr[...]
    y = _shift_rows(gch, pg, 2) * scw_ref[0:1, :]
    y = y + _shift_rows(gch, pg, 1) * scw_ref[1:2, :]
    y = y + gch * scw_ref[2:3, :]
    o_sc = ub * y

    h8, x8, g8 = h[tc - SUBLANES:], ux[tc - SUBLANES:], gch[tc - SUBLANES:]
    cx_scr[...] = x8
    cg_scr[...] = g8
    ch_scr[...] = h8
    return _rms(o_lru, glru_ref[...]), _rms(o_sc, gsc_ref[...]), h8, x8, g8


_LRU_CHUNK = 256


_D_QKV = D_ATTN + 2 * D_KV
_D_U = D_IN - _D_QKV


def _proj_lru_kernel(x_ref, g_ref, w_ref, cos_ref, sin_ref, xs_ref, coss_ref, sins_ref,
                     cw_ref, cb_ref, wg_ref, ba_ref, bi_ref,
                     lam_ref, scw_ref, glru_ref, gsc_ref, *rest, blocks_per_seq, n_cast):
    qkv_ref, mb_ref, h8_ref, x8_ref, g8_ref, zs_ref = rest[n_cast:n_cast + 6]
    zu_scr, cx_scr, cg_scr, ch_scr, sa_scr, sb_scr = rest[2 * n_cast + 6:]
    for src, dst in zip(rest[:n_cast], rest[n_cast + 6:2 * n_cast + 6]):
        dst[...] = src[...].astype(BF16)
    i = pl.program_id(0)

    @pl.when(i == pl.num_programs(0) - 1)
    def _():
        _in_proj_kernel(xs_ref, g_ref, w_ref, coss_ref, sins_ref, zs_ref)

    blk = i % blocks_per_seq

    @pl.when(blk == 0)
    def _():
        cx_scr[...] = jnp.zeros_like(cx_scr)
        cg_scr[...] = jnp.zeros_like(cg_scr)
        ch_scr[...] = jnp.zeros_like(ch_scr)

    h = _rms(x_ref[...], g_ref[...]).astype(BF16)
    bm = h.shape[0]
    for c0 in range(_D_QKV, D_IN, _IN_PROJ_CHUNK):
        zu_scr[:, c0 - _D_QKV:c0 - _D_QKV + _IN_PROJ_CHUNK] = jnp.dot(
            h, w_ref[:, c0:c0 + _IN_PROJ_CHUNK], preferred_element_type=F32)

    wrefs = (cw_ref, cb_ref, wg_ref, ba_ref, bi_ref, lam_ref, scw_ref, glru_ref, gsc_ref)
    tc = _LRU_CHUNK
    col = lambda k: slice(k * D_LRU, (k + 1) * D_LRU)
    for r0 in range(0, bm, tc):
        rows = slice(r0, r0 + tc)
        lru_n, sc_n, h8, x8, g8 = _lru_sc_rows(
            zu_scr[rows, col(0)], zu_scr[rows, col(1)], zu_scr[rows, col(2)], zu_scr[rows, col(3)],
            zu_scr[rows, col(4)], blk * bm + r0, wrefs, (cx_scr, cg_scr, ch_scr), (sa_scr, sb_scr))
        mb_ref[rows, :D_LRU] = lru_n.astype(mb_ref.dtype)
        mb_ref[rows, D_LRU:] = sc_n.astype(mb_ref.dtype)
    h8_ref[0] = h8
    x8_ref[0] = x8
    g8_ref[0] = g8

    lane = lax.broadcasted_iota(jnp.int32, (bm, LANES), 1)
    lo32 = (lane % HEAD_DIM) < (HEAD_DIM // 2)
    cos = cos_ref[...]
    sin = sin_ref[...]

    def rope(a):
        sw = jnp.where(lo32, pltpu.roll(a, LANES - HEAD_DIM // 2, 1), pltpu.roll(a, HEAD_DIM // 2, 1))
        return a * cos + sw * sin

    rope_cols = D_ATTN + D_KV
    for c0 in range(0, _D_QKV, _IN_PROJ_CHUNK):
        acc = jnp.dot(h, w_ref[:, c0:c0 + _IN_PROJ_CHUNK], preferred_element_type=F32)
        for c in range(0, _IN_PROJ_CHUNK, LANES):
            a = acc[:, c:c + LANES]
            qkv_ref[:, c0 + c:c0 + c + LANES] = rope(a) if c0 + c < rope_cols else a


def _proj_lru(x, g_all, w_all_bf, cos_t, sin_t, xs, cos_s, sin_s, lw, *, layer, w_layer, n_seq, seq, bm,
              cast=()):
    m = x.shape[0]
    ns = xs.shape[0]
    n_tab = cos_t.shape[0] // bm
    bps = seq // bm
    const = lambda shape: pl.BlockSpec(shape, lambda i: (0,) * len(shape))
    st = pl.BlockSpec((1, SUBLANES, D_LRU), lambda i: (i // bps, 0, 0))
    st_shape = jax.ShapeDtypeStruct((n_seq, SUBLANES, D_LRU), F32)
    in_specs = [pl.BlockSpec((bm, D_MODEL), lambda i: (i, 0)),
                pl.BlockSpec((None, 1, D_MODEL), lambda i: (layer, 0, 0)),
                _resident((None, D_MODEL, D_IN), lambda i: (w_layer, 0, 0)),
                pl.BlockSpec((bm, LANES), lambda i: (i % n_tab, 0)),
                pl.BlockSpec((bm, LANES), lambda i: (i % n_tab, 0)),
                const((ns, D_MODEL)), const((ns, LANES)), const((ns, LANES))] + _mixer_param_specs(layer)
    out_specs = [pl.BlockSpec((bm, _D_QKV), lambda i: (i, 0)),
                 pl.BlockSpec((bm, D_LRU + D_SC), lambda i: (i, 0)), st, st, st, const((ns, D_IN))]
    out_shape = [jax.ShapeDtypeStruct((m, _D_QKV), F32),
                 jax.ShapeDtypeStruct((m, D_LRU + D_SC), BF16), st_shape, st_shape, st_shape,
                 jax.ShapeDtypeStruct((ns, D_IN), F32)]
    args = [x, g_all, w_all_bf, cos_t, sin_t, xs, cos_s, sin_s] + _mixer_params(lw)
    for w_all, lyr in cast:
        i_spec, o_spec, o_shape = _cast_specs(w_all, lyr, m // bm, lambda i: i)
        in_specs.append(i_spec)
        out_specs.append(o_spec)
        out_shape.append(o_shape)
        args.append(w_all)
    return pl.pallas_call(
        functools.partial(_proj_lru_kernel, blocks_per_seq=bps, n_cast=len(cast)),
        grid=(m // bm,),
        in_specs=in_specs,
        out_specs=out_specs,
        out_shape=out_shape,
        scratch_shapes=[pltpu.VMEM((bm, _D_U), F32)] + [pltpu.VMEM((SUBLANES, D_LRU), F32)] * 3
        + [pltpu.VMEM((D_LRU // LANES, SUBLANES * _scan_pitch(_LRU_CHUNK // SUBLANES), LANES), F32)] * 2,
        compiler_params=_cparams(("arbitrary",)),
        name="proj_lru",
    )(*args)


def _decode_kernel(q_ref, knt_ref, kc_ref, vc_ref, sel_ref, sink_ref, gat_ref,
                   ux_ref, gate_ref, ub_ref, uc_ref, uh_ref, h0_ref, cbuf_ref, sbuf_ref,
                   cw_ref, cb_ref, wg_ref, ba_ref, bi_ref, lam_ref, scw_ref, glru_ref, gsc_ref,
                   *rest, nbk):
    oat_ref, ors_ref, ko_ref, vo_ref, hn_ref, cn_ref, sn_ref = rest[-7:]
    if len(ko_ref.shape) == 4:
        for d in range(1, ko_ref.shape[0]):
            ko_ref[d] = jnp.zeros(ko_ref.shape[1:], ko_ref.dtype)
            vo_ref[d] = jnp.zeros(vo_ref.shape[1:], vo_ref.dtype)
        ko_ref, vo_ref = ko_ref.at[0], vo_ref.at[0]
    wb = kc_ref.shape[2]
    hrow = lax.broadcasted_iota(jnp.int32, (N_HEADS, D_ATTN), 0)
    hcol = lax.broadcasted_iota(jnp.int32, (N_HEADS, D_ATTN), 1) // HEAD_DIM
    own = (hrow == hcol)
    qexp = jnp.concatenate(
        [jnp.where(own, jnp.broadcast_to(q_ref[i:i + 1, :] * (HEAD_DIM ** -0.5), (N_HEADS, D_ATTN)), 0.0)
         for i in range(nbk)], axis=0)
    qrow = jnp.dot(qexp.astype(BF16), sel_ref[...], preferred_element_type=F32)
    rows = nbk * N_HEADS
    grow = (lax.broadcasted_iota(jnp.int32, (rows, D_KV), 0) % N_HEADS) // N_GROUP
    gcol = lax.broadcasted_iota(jnp.int32, (rows, D_KV), 1) // HEAD_DIM
    kvmask = (grow == gcol)
    qm = jnp.where(kvmask, qrow, 0.0).astype(BF16)
    sink = jnp.concatenate([sink_ref[...][:, 0:1]] * nbk, axis=0)
    gat = jnp.concatenate([gat_ref[...]] * nbk, axis=0)
    lanek = lax.broadcasted_iota(jnp.int32, (D_KV, wb), 1)
    newest = lanek == wb - 1
    for i in range(nbk):
        ko_ref[i] = jnp.where(newest, jnp.broadcast_to(knt_ref[:D_KV, i:i + 1], (D_KV, wb)),
                              pltpu.roll(kc_ref[i], wb - 1, 1))
        vo_ref[i] = jnp.where(newest, jnp.broadcast_to(knt_ref[D_KV:, i:i + 1], (D_KV, wb)),
                              pltpu.roll(vc_ref[i], wb - 1, 1))
    s = jnp.concatenate(
        [jnp.dot(qm[i * N_HEADS:(i + 1) * N_HEADS], ko_ref[i].astype(BF16), preferred_element_type=F32)
         for i in range(nbk)], axis=0)
    m = jnp.maximum(jnp.max(s, axis=1, keepdims=True), sink)
    p = jnp.exp(s - m)
    p = (p / (jnp.sum(p, axis=1, keepdims=True) + jnp.exp(sink - m))).astype(BF16)
    of = jnp.concatenate(
        [lax.dot_general(p[i * N_HEADS:(i + 1) * N_HEADS], vo_ref[i].astype(BF16), (((1,), (1,)), ((), ())),
                         preferred_element_type=F32) for i in range(nbk)], axis=0)
    of = jnp.where(kvmask, of, 0.0)
    t = of[:, :LANES] + of[:, LANES:]
    o = t + pltpu.roll(t, HEAD_DIM, 1)
    rs = jnp.sum(o * o, axis=1, keepdims=True)
    for i in range(nbk):
        sl = slice(i * N_HEADS, (i + 1) * N_HEADS)
        ms = jnp.sum(rs[sl], axis=0, keepdims=True) * (0.5 / D_ATTN)
        oat_ref[i] = (o[sl] * lax.rsqrt(ms + RMS_EPS) * gat[sl]).astype(oat_ref.dtype)

    ux = ux_ref[...]
    xc = cbuf_ref[0] * cw_ref[0:1, :]
    xc = xc + cbuf_ref[1] * cw_ref[1:2, :]
    xc = xc + cbuf_ref[2] * cw_ref[2:3, :]
    xc = xc + ux * cw_ref[3:4, :]
    xc = xc + cb_ref[...]
    a, mult, gi = _lru_gates(xc, wg_ref, ba_ref[...], bi_ref[...], lam_ref[...])
    h = a * h0_ref[...] + mult * gi * xc
    o_lru = h * jax.nn.gelu(gate_ref[...], approximate=True)
    hn_ref[...] = h
    cn_ref[0] = cbuf_ref[1]
    cn_ref[1] = cbuf_ref[2]
    cn_ref[2] = ux
    gch = uc_ref[...] * uh_ref[...]
    y = sbuf_ref[:, 0, :] * scw_ref[0:1, :]
    y = y + sbuf_ref[:, 1, :] * scw_ref[1:2, :]
    y = y + gch * scw_ref[2:3, :]
    o_sc = ub_ref[...] * y
    sn_ref[:, 0, :] = sbuf_ref[:, 1, :]
    sn_ref[:, 1, :] = gch
    ors_ref[:, :D_LRU] = _rms(o_lru, glru_ref[...]).astype(ors_ref.dtype)
    ors_ref[:, D_LRU:] = _rms(o_sc, gsc_ref[...]).astype(ors_ref.dtype)


def _decode_mix(z, knt, kct_all, vct_all, h0_all, cbuf_all, sbuf_all, lw, *, layer, nbk=16, stacked=None):
    depth, ns, _, wb = kct_all.shape
    z512 = lambda col: pl.BlockSpec((nbk, D_LRU), lambda i: (i, col))
    const = lambda shape: pl.BlockSpec(shape, lambda i: (0,) * len(shape))
    cache_in = pl.BlockSpec((None, nbk, D_KV, wb), lambda i: (layer, i, 0, 0))
    cache_out = cache_in if stacked is not None else pl.BlockSpec((depth, nbk, D_KV, wb), lambda i: (0, i, 0, 0))
    n_in = 24
    extra_specs = [] if stacked is None else [pl.BlockSpec(memory_space=pl.ANY)] * 2
    extra_args = () if stacked is None else tuple(stacked)
    aliases = {} if stacked is None else {n_in: 2, n_in + 1: 3}
    outs = pl.pallas_call(
        functools.partial(_decode_kernel, nbk=nbk),
        grid=(ns // nbk,),
        input_output_aliases=aliases,
        in_specs=[pl.BlockSpec((nbk, D_ATTN), lambda i: (i, 0)),
                  pl.BlockSpec((None, 2 * D_KV, nbk), lambda i: (i, 0, 0)),
                  cache_in, cache_in,
                  const((D_ATTN, D_KV)), _layer_spec(layer, (N_HEADS, LANES)), _layer_spec(layer, (N_HEADS, LANES)),
                  z512(_COL_UX), z512(_COL_GATE), z512(_COL_B), z512(_COL_C), z512(_COL_H),
                  pl.BlockSpec((None, nbk, D_LRU), lambda i: (layer, i, 0)),
                  pl.BlockSpec((None, LRU_CONV_W - 1, nbk, D_LRU), lambda i: (layer, 0, i, 0)),
                  pl.BlockSpec((None, nbk, SC_CONV_W - 1, D_SC), lambda i: (layer, i, 0, 0))]
        + _mixer_param_specs(layer) + extra_specs,
        out_specs=[pl.BlockSpec((nbk, N_HEADS, LANES), lambda i: (i, 0, 0)),
                   pl.BlockSpec((nbk, D_LRU + D_SC), lambda i: (i, 0)),
                   cache_out, cache_out,
                   pl.BlockSpec((nbk, D_LRU), lambda i: (i, 0)),
                   pl.BlockSpec((LRU_CONV_W - 1, nbk, D_LRU), lambda i: (0, i, 0)),
                   pl.BlockSpec((nbk, SC_CONV_W - 1, D_SC), lambda i: (i, 0, 0))],
        out_shape=[jax.ShapeDtypeStruct((ns, N_HEADS, LANES), BF16),
                   jax.ShapeDtypeStruct((ns, D_LRU + D_SC), BF16),
                   jax.ShapeDtypeStruct((depth, ns, D_KV, wb), F32),
                   jax.ShapeDtypeStruct((depth, ns, D_KV, wb), F32),
                   jax.ShapeDtypeStruct((ns, D_LRU), F32),
                   jax.ShapeDtypeStruct((LRU_CONV_W - 1, ns, D_LRU), F32),
                   jax.ShapeDtypeStruct((ns, SC_CONV_W - 1, D_SC), F32)],
        compiler_params=_cparams(("parallel",)),
        name="decode_mix",
    )(z, knt, kct_all, vct_all, lw["sel"], lw["sink_tab"], lw["g_attn_tab"],
      z, z, z, z, z, h0_all, cbuf_all, sbuf_all, *_mixer_params(lw), *extra_args)
    return outs


_OUT_PROJ_CHUNK = 512


_X_RING = 3


def _out_proj_kernel(ma_ref, mb_ref, x_hbm, mas_ref, mbs_ref, xs_ref, w_ref, g_ref,
                     x1_ref, hf_ref, x1s_ref, hfs_ref, x_ring, x_sem, *, bm):
    i = pl.program_id(0)
    n = pl.num_programs(0)
    ahead = _X_RING - 1

    def x_copy(block):
        slot = block % _X_RING
        return pltpu.make_async_copy(x_hbm.at[pl.ds(block * bm, bm), :], x_ring.at[slot], x_sem.at[slot])

    @pl.when(i == 0)
    def _():
        for k in range(ahead):
            @pl.when(k < n)
            def _(k=k):
                x_copy(k).start()

    @pl.when(i + ahead < n)
    def _():
        x_copy(i + ahead).start()

    x_copy(i).wait()
    _out_proj_rows(ma_ref, mb_ref, x_ring.at[i % _X_RING], w_ref, g_ref, x1_ref, hf_ref)

    @pl.when(i == n - 1)
    def _():
        _out_proj_rows(mas_ref, mbs_ref, xs_ref, w_ref, g_ref, x1s_ref, hfs_ref)


def _out_proj_rows(ma_ref, mb_ref, x_ref, w_ref, g_ref, x1_ref, hf_ref):
    ma = ma_ref[...]
    mb = mb_ref[...]
    ssq = None
    for c0 in range(0, D_MODEL, _OUT_PROJ_CHUNK):
        cs = slice(c0, c0 + _OUT_PROJ_CHUNK)
        acc = jnp.dot(ma, w_ref[:D_ATTN, cs], preferred_element_type=F32)
        acc = acc + jnp.dot(mb, w_ref[D_ATTN:, cs], preferred_element_type=F32)
        x1 = x_ref[:, cs] + acc
        x1_ref[:, cs] = x1
        part = jnp.sum(x1 * x1, axis=-1, keepdims=True)
        ssq = part if ssq is None else ssq + part
    scale = lax.rsqrt(ssq * (1.0 / D_MODEL) + RMS_EPS)
    for c0 in range(0, D_MODEL, _OUT_PROJ_CHUNK):
        cs = slice(c0, c0 + _OUT_PROJ_CHUNK)
        hf_ref[:, cs] = (x1_ref[:, cs] * scale * g_ref[:, cs]).astype(hf_ref.dtype)


def _out_proj(ma, mb, x, mas, mbs, xs, w_all_bf, g_all, *, layer, w_layer, bm):
    m = x.shape[0]
    ns = xs.shape[0]
    blk = lambda width: pl.BlockSpec((bm, width), lambda i: (i, 0))
    whole = lambda width: pl.BlockSpec((ns, width), lambda i: (0, 0))
    return pl.pallas_call(
        functools.partial(_out_proj_kernel, bm=bm),
        grid=(m // bm,),
        in_specs=[blk(D_ATTN), blk(D_LRU + D_SC), pl.BlockSpec(memory_space=pl.ANY),
                  whole(D_ATTN), whole(D_LRU + D_SC), whole(D_MODEL),
                  _resident((None, D_MODEL, D_MODEL), lambda i: (w_layer, 0, 0)),
                  pl.BlockSpec((None, 1, D_MODEL), lambda i: (layer, 0, 0))],
        out_specs=[blk(D_MODEL), blk(D_MODEL), whole(D_MODEL), whole(D_MODEL)],
        out_shape=[jax.ShapeDtypeStruct((m, D_MODEL), F32), jax.ShapeDtypeStruct((m, D_MODEL), BF16),
                   jax.ShapeDtypeStruct((ns, D_MODEL), F32), jax.ShapeDtypeStruct((ns, D_MODEL), BF16)],
        scratch_shapes=[pltpu.VMEM((_X_RING, bm, D_MODEL), F32), pltpu.SemaphoreType.DMA((_X_RING,))],
        compiler_params=_cparams(("arbitrary",)),
        name="out_proj",
    )(ma, mb, x, mas, mbs, xs, w_all_bf, g_all)


_FFN_DOWN_CHUNK = 512
_FFN_X1_CHUNK = 256


def _ffn_kernel(hf_ref, x1_ref, hfs_ref, x1s_ref, wg_ref, wu_ref, wd_ref, gfin_ref, *rest, final_norm,
                n_cast):
    o_ref, os_ref = rest[n_cast:n_cast + 2]
    f = pl.program_id(1)
    n_f = pl.num_programs(1)

    @pl.when(pl.program_id(0) == pl.num_programs(0) - 1)
    def _():
        hfs = hfs_ref[...]
        gate_s = jnp.dot(hfs, wg_ref[...], preferred_element_type=F32)
        up_s = jnp.dot(hfs, wu_ref[...], preferred_element_type=F32)
        hid_s = (gate_s * jax.nn.sigmoid(gate_s) * up_s).astype(BF16)
        down_s = jnp.dot(hid_s, wd_ref[...], preferred_element_type=F32)

        @pl.when(f == 0)
        def _():
            os_ref[...] = down_s

        @pl.when(f > 0)
        def _():
            os_ref[...] += down_s

        @pl.when(f == n_f - 1)
        def _():
            x2s = os_ref[...] + x1s_ref[...]
            os_ref[...] = _rms(x2s, gfin_ref[...]) if final_norm else x2s

    def tile_step(first):
        hf = hf_ref[...]
        gate = jnp.dot(hf, wg_ref[...], preferred_element_type=F32)
        up = jnp.dot(hf, wu_ref[...], preferred_element_type=F32)
        hid = (gate * jax.nn.sigmoid(gate) * up).astype(BF16)
        x1c = x1_ref[...]
        for c0 in range(0, D_MODEL, _FFN_DOWN_CHUNK):
            down = jnp.dot(hid, wd_ref[:, c0:c0 + _FFN_DOWN_CHUNK], preferred_element_type=F32)
            for h0 in range(0, _FFN_DOWN_CHUNK, _FFN_X1_CHUNK):
                cs = slice(c0 + h0, c0 + h0 + _FFN_X1_CHUNK)
                part = down[:, h0:h0 + _FFN_X1_CHUNK] + jnp.where(f == (c0 + h0) // _FFN_X1_CHUNK, x1c, 0.0)
                o_ref[:, cs] = part if first else o_ref[:, cs] + part
        for src, dst in zip(rest[:n_cast], rest[n_cast + 2:]):
            dst[...] = src[...].astype(BF16)

    @pl.when(f == 0)
    def _():
        tile_step(True)

    @pl.when(f > 0)
    def _():
        tile_step(False)

    if final_norm:
        @pl.when(f == pl.num_programs(1) - 1)
        def _():
            o_ref[...] = _rms(o_ref[...], gfin_ref[...])


def _ffn_cast_tiling(w_all, n_i, nf):
    _, k, n = w_all.shape
    if k % n_i or (k // n_i) % (2 * SUBLANES):
        return None
    for n_col in range(nf, 0, -1):
        if n % n_col == 0 and (n // n_col) % LANES == 0:
            return k // n_i, n // n_col, n_col
    return None


def _can_cast_in_ffn(weights, m, bm, tf=512):
    return all(_ffn_cast_tiling(w, m // bm, D_FF // tf) is not None for w in weights)


def _ffn(hf, x1, hfs, x1s, w_gu_bf, w_d_bf, g_final, *, layer, bm, tf=512, final_norm, cast_next=()):
    m = hf.shape[0]
    ns = hfs.shape[0]
    nf = D_FF // tf
    n_x1 = D_MODEL // _FFN_X1_CHUNK
    assert nf >= n_x1
    whole = lambda rows: pl.BlockSpec((rows, D_MODEL), lambda i, f: (0, 0))
    in_specs = [pl.BlockSpec((bm, D_MODEL), lambda i, f: (i, 0)),
                pl.BlockSpec((bm, _FFN_X1_CHUNK), lambda i, f: (i, jnp.minimum(f, n_x1 - 1))),
                whole(ns), whole(ns),
                pl.BlockSpec((None, D_MODEL, tf), lambda i, f: (layer, 0, f)),
                pl.BlockSpec((None, D_MODEL, tf), lambda i, f: (layer, 0, nf + f)),
                pl.BlockSpec((None, tf, D_MODEL), lambda i, f: (layer, f, 0)),
                whole(1)]
    out_specs = [pl.BlockSpec((bm, D_MODEL), lambda i, f: (i, 0)), whole(ns)]
    out_shape = [jax.ShapeDtypeStruct((m, D_MODEL), F32), jax.ShapeDtypeStruct((ns, D_MODEL), F32)]
    args = [hf, x1, hfs, x1s, w_gu_bf, w_gu_bf, w_d_bf, g_final]
    for w_next, layer_next in cast_next:
        rows, width, n_col = _ffn_cast_tiling(w_next, m // bm, nf)
        col = lambda f, n_col=n_col: jnp.minimum(f, n_col - 1)
        in_specs.append(pl.BlockSpec((None, rows, width),
                                     lambda i, f, col=col, lyr=layer_next: (lyr, i, col(f))))
        out_specs.append(pl.BlockSpec((None, rows, width), lambda i, f, col=col: (0, i, col(f))))
        out_shape.append(jax.ShapeDtypeStruct((1,) + w_next.shape[1:], BF16))
        args.append(w_next)
    return pl.pallas_call(
        functools.partial(_ffn_kernel, final_norm=final_norm, n_cast=len(cast_next)),
        grid=(m // bm, nf), in_specs=in_specs, out_specs=out_specs, out_shape=out_shape,
        compiler_params=_cparams(("arbitrary", "arbitrary")), name="ffn",
    )(*args)


def _rope_tables(pos):
    half = HEAD_DIM // 2
    inv = ROPE_THETA ** (-jnp.arange(half, dtype=F32) / half)
    ang = pos.astype(F32)[:, None] * inv[None, :]
    cos, sin = jnp.cos(ang), jnp.sin(ang)
    cos_t = jnp.tile(cos, (1, LANES // half))
    sin_t = jnp.tile(jnp.concatenate([-sin, sin], axis=1), (1, LANES // HEAD_DIM))
    return cos_t, sin_t


def _block_diag(w):
    dd, hh, blk, _ = w.shape
    eye = jnp.eye(hh, dtype=w.dtype)
    return (eye[None, :, None, :, None] * w[:, :, :, None, :]).reshape(dd, hh * blk, hh * blk)


def kernel(x_prompt, x_sample, state_lru_h, state_lru_conv, cache_swa_k, cache_swa_v, state_sconv,
           norm_mix, w_in, norm_grp, w_out, lru_conv_w, lru_conv_b, lru_w_a, lru_b_a, lru_w_i, lru_b_i,
           lru_lambda, sc_conv_w, attn_sinks, norm_ffn, ffn_w_gu, ffn_w_down, norm_final):
    n_p, t_p, _ = x_prompt.shape
    n_s, t_s, _ = x_sample.shape
    depth = w_in.shape[0]
    wb = cache_swa_k.shape[2]
    assert t_s == 1 and wb == WINDOW and n_s % _DECODE_SEQS == 0
    assert t_p % (WINDOW * _ATTN_SUB_BLOCKS) == 0 and t_p % _LRU_CHUNK == 0

    bm_p = _PROJ_ROWS if t_p % _PROJ_ROWS == 0 else _LRU_CHUNK
    bm_ffn = _FFN_ROWS if (n_p * t_p) % _FFN_ROWS == 0 else bm_p

    cos_p, sin_p = _rope_tables(jnp.arange(t_p, dtype=jnp.int32))
    cos_s, sin_s = _rope_tables(jnp.full((n_s,), PAST_LEN, dtype=jnp.int32))

    sel = (jnp.arange(D_ATTN)[:, None] % HEAD_DIM == jnp.arange(D_KV)[None, :] % HEAD_DIM).astype(BF16)

    xp = x_prompt.reshape(n_p * t_p, D_MODEL)
    xs = x_sample.reshape(n_s, D_MODEL)
    row = lambda v: v.reshape(1, -1)
    p_states, s_states = [], []
    n_blk = (n_p * t_p) // bm_p
    piggy = (_can_cast_in_attn(n_p, t_p)
             and _can_cast_in_ffn((w_in, ffn_w_gu, ffn_w_down, w_out), n_p * t_p, bm_ffn)
             and D_MODEL % n_blk == 0 and (D_MODEL // n_blk) % (2 * SUBLANES) == 0)
    if piggy:
        w_in_l, l_in = w_in[:1].astype(BF16), 0
    else:
        w_in_bf = w_in.astype(BF16)
        w_out_bf = w_out.astype(BF16)
        w_gu_bf = ffn_w_gu.astype(BF16)
        w_d_bf = ffn_w_down.astype(BF16)
    nbk = _DECODE_SEQS
    kct_all = cache_swa_k.transpose(0, 1, 3, 4, 2).reshape(depth, n_s, D_KV, wb)
    vct_all = cache_swa_v.transpose(0, 1, 3, 4, 2).reshape(depth, n_s, D_KV, wb)
    cbuf_all = state_lru_conv.transpose(0, 2, 1, 3)
    g_mix = norm_mix.reshape(depth, 1, D_MODEL)
    g_ffn = norm_ffn.reshape(depth, 1, D_MODEL)
    rows3 = lambda v: v.reshape(depth, 1, -1)
    g_attn_all = rows3(norm_grp[:, :D_ATTN])
    lw = dict(
        conv_w=lru_conv_w, conv_b=rows3(lru_conv_b),
        w_gates=jnp.concatenate([_block_diag(lru_w_a), _block_diag(lru_w_i)], axis=-1).astype(BF16),
        b_a=rows3(lru_b_a), b_i=rows3(lru_b_i), lam=rows3(lru_lambda),
        sc_w=sc_conv_w, g_lru=rows3(norm_grp[:, D_ATTN:D_ATTN + D_LRU]), g_sc=rows3(norm_grp[:, D_ATTN + D_LRU:]),
        sel=sel,
        sink_tab=jnp.broadcast_to(attn_sinks[:, :, None], (depth, N_HEADS, LANES)),
        g_attn_tab=jnp.tile(norm_grp[:, :D_ATTN].reshape(depth, N_HEADS, HEAD_DIM), (1, 1, LANES // HEAD_DIM)),
    )
    for l in range(depth):
        last = l == depth - 1

        if not piggy:
            w_in_l, w_out_l, w_gu_l, w_d_l, l_in, l_w = w_in_bf, w_out_bf, w_gu_bf, w_d_bf, l, l
        first_casts = piggy and l == 0
        qkv, mb, h8, x8, g8, zs, *w_out_0 = _proj_lru(
            xp, g_mix, w_in_l, cos_p, sin_p, xs, cos_s, sin_s, lw, layer=l, w_layer=l_in, n_seq=n_p, seq=t_p,
            bm=bm_p, cast=[(w_out, l)] if first_casts else ())
        if first_casts:
            (w_out_l,) = w_out_0
            ma, w_gu_l, w_d_l = _attn_prompt(qkv, attn_sinks, g_attn_all, layer=l, n_seq=n_p, seq=t_p,
                                             cast=[(ffn_w_gu, l), (ffn_w_down, l)])
            l_w = 0
        else:
            ma = _attn_prompt(qkv, attn_sinks, g_attn_all, layer=l, n_seq=n_p, seq=t_p)
        z3 = qkv.reshape(n_p, t_p, _D_QKV)
        wbp = min(WINDOW, t_p)
        p_states.append((
            h8[:, SUBLANES - 1],
            x8[:, SUBLANES - (LRU_CONV_W - 1):],
            z3[:, t_p - wbp:, D_ATTN:D_ATTN + D_KV].reshape(n_p, wbp, N_KV_HEADS, HEAD_DIM),
            z3[:, t_p - wbp:, D_ATTN + D_KV:D_ATTN + 2 * D_KV].reshape(n_p, wbp, N_KV_HEADS, HEAD_DIM),
            g8[:, SUBLANES - (SC_CONV_W - 1):],
        ))

        knt = zs[:, D_ATTN:D_ATTN + 2 * D_KV].reshape(n_s // nbk, nbk, 2 * D_KV).transpose(0, 2, 1)
        oat, ors, k_stack, v_stack, h_new, c_new, s_new = _decode_mix(
            zs, knt, kct_all, vct_all, state_lru_h, cbuf_all, state_sconv, lw, layer=l, nbk=nbk,
            stacked=None if l == 0 else (k_stack, v_stack))
        c_new = c_new.transpose(1, 0, 2)
        mas = oat[:, :, :HEAD_DIM].reshape(n_s, D_ATTN)
        s_states.append((h_new, c_new, s_new))

        x1, hf, x1s, hfs = _out_proj(ma, mb, xp, mas, ors, xs, w_out_l, g_ffn, layer=l, w_layer=l_w, bm=bm_p)
        if piggy and not last:
            nxt = [(w, l + 1) for w in (w_in, ffn_w_gu, ffn_w_down, w_out)]
            xp, xs, w_in_l, w_gu_l, w_d_l, w_out_l = _ffn(hf, x1, hfs, x1s, w_gu_l, w_d_l, row(norm_final),
                                                          layer=l_w, bm=bm_ffn, final_norm=last, cast_next=nxt)
        else:
            xp, xs = _ffn(hf, x1, hfs, x1s, w_gu_l, w_d_l, row(norm_final), layer=l_w, bm=bm_ffn,
                          final_norm=last)

    y_prompt = xp.reshape(n_p, t_p, D_MODEL)
    y_sample = xs.reshape(n_s, t_s, D_MODEL)
    stack = lambda states, k: jnp.stack([st[k] for st in states])
    untranspose = lambda c: c.reshape(depth, n_s, N_KV_HEADS, HEAD_DIM, wb).transpose(0, 1, 4, 2, 3)
    return (y_prompt, y_sample,
            stack(p_states, 0), stack(p_states, 1), stack(p_states, 2), stack(p_states, 3), stack(p_states, 4),
            stack(s_states, 0), stack(s_states, 1), untranspose(k_stack), untranspose(v_stack),
            stack(s_states, 2))
```

```python
import functools

import jax
import jax.numpy as jnp
from jax import lax
from jax.experimental import pallas as pl
from jax.experimental.pallas import tpu as pltpu

F32 = jnp.float32
BF16 = jnp.bfloat16

D_MODEL = 2048
D_ATTN = 1024
D_LRU = 512
D_SC = 512
HEAD_DIM = 64
N_HEADS = 16
N_KV_HEADS = 4
N_GROUP = 4
D_KV = 256
WINDOW = 128
ROPE_THETA = 10000.0
N_LRU_HEADS = 8
LRU_BLK = 64
LRU_CONV_W = 4
LRU_C = 8.0
SC_CONV_W = 3
D_FF = 5632
D_IN = 4096
RMS_EPS = 1e-6
PAST_LEN = 8192

LANES = 128
SUBLANES = 8
VMEM_LIMIT_BYTES = 56 * 1024 * 1024

_COL_UX, _COL_GATE, _COL_B, _COL_C, _COL_H = 3, 4, 5, 6, 7

_PROJ_ROWS = 512
_FFN_ROWS = 1024
_DECODE_SEQS = 16


def _cparams(sem):
    return pltpu.CompilerParams(dimension_semantics=sem, vmem_limit_bytes=VMEM_LIMIT_BYTES)


def _rms(x, g):
    return x * lax.rsqrt(jnp.mean(x * x, axis=-1, keepdims=True) + RMS_EPS) * g


_IN_PROJ_CHUNK = 512


def _in_proj_kernel(x_ref, g_ref, w_ref, cos_ref, sin_ref, z_ref):
    h = _rms(x_ref[...], g_ref[...]).astype(BF16)
    bm = h.shape[0]
    lane = lax.broadcasted_iota(jnp.int32, (bm, LANES), 1)
    lo32 = (lane % HEAD_DIM) < (HEAD_DIM // 2)
    cos = cos_ref[...]
    sin = sin_ref[...]

    def rope(a):
        sw = jnp.where(lo32, pltpu.roll(a, LANES - HEAD_DIM // 2, 1), pltpu.roll(a, HEAD_DIM // 2, 1))
        return a * cos + sw * sin

    rope_cols = D_ATTN + D_KV
    for c0 in range(0, D_IN, _IN_PROJ_CHUNK):
        acc = jnp.dot(h, w_ref[:, c0:c0 + _IN_PROJ_CHUNK], preferred_element_type=F32)
        for c in range(0, _IN_PROJ_CHUNK, LANES):
            a = acc[:, c:c + LANES]
            z_ref[:, c0 + c:c0 + c + LANES] = rope(a) if c0 + c < rope_cols else a


def _resident(block_shape, index_map):
    return pl.BlockSpec(block_shape, index_map, pipeline_mode=pl.Buffered(1))


def _layer_spec(layer, shape):
    return pl.BlockSpec((None,) + shape, lambda *ids: (layer,) + (0,) * len(shape))


def _mixer_param_specs(layer):
    return [_layer_spec(layer, s) for s in (
        (LRU_CONV_W, D_LRU), (1, D_LRU), (D_LRU, 2 * D_LRU), (1, D_LRU), (1, D_LRU), (1, D_LRU),
        (SC_CONV_W, D_SC), (1, D_LRU), (1, D_SC))]


def _mixer_params(lw):
    return [lw[k] for k in ("conv_w", "conv_b", "w_gates", "b_a", "b_i", "lam", "sc_w", "g_lru", "g_sc")]


def _cast_specs(w_all, layer, steps, step_index):
    _, k, n = w_all.shape
    r = k // steps
    assert r * steps == k and r % (2 * SUBLANES) == 0
    return (pl.BlockSpec((None, r, n), lambda *ids: (layer, step_index(*ids), 0)),
            pl.BlockSpec((None, r, n), lambda *ids: (0, step_index(*ids), 0)),
            jax.ShapeDtypeStruct((1, k, n), BF16))


def _attn_prompt_kernel(sink_ref, q_ref, kc_ref, kp_ref, vc_ref, vp_ref, g_ref, *rest, n_sub, n_cast, layer):
    o_ref = rest[n_cast]
    for src, dst in zip(rest[:n_cast], rest[n_cast + 1:]):
        dst[...] = src[...].astype(BF16)
    b = pl.program_id(1)
    L = WINDOW

    lane = lax.broadcasted_iota(jnp.int32, (2 * L, LANES), 1)
    lo = lane < HEAD_DIM
    row = lax.broadcasted_iota(jnp.int32, (2 * L, 1), 0)
    top = row < L

    qi = lax.broadcasted_iota(jnp.int32, (2 * L, 4 * L), 0) % L
    sj = lax.broadcasted_iota(jnp.int32, (2 * L, 4 * L), 1) % (2 * L)
    diff = L + qi - sj
    band = (diff >= 0) & (diff < WINDOW)
    bias_inner = jnp.where(band, 0.0, -jnp.inf).astype(F32)
    bias_first = jnp.where(band & ((sj >= L) | (b > 0)), 0.0, -jnp.inf).astype(F32)

    zeros = jnp.zeros((2 * L, LANES), F32)
    ones_lo = jnp.where(lo, 1.0, 0.0).astype(F32)
    ones_hi = 1.0 - ones_lo

    units = [(sub, kh) for sub in range(n_sub) for kh in range(N_KV_HEADS)]
    prep, scores, vmats, probs, sink_terms = {}, {}, {}, {}, {}
    outs = {sub: [] for sub in range(n_sub)}

    def stage_scores(sub, kh):
        if sub not in prep:
            rows = slice(sub * L, (sub + 1) * L)
            k_prev = kp_ref[...] if sub == 0 else kc_ref[(sub - 1) * L:sub * L, :]
            v_prev = vp_ref[...] if sub == 0 else vc_ref[(sub - 1) * L:sub * L, :]
            prep[sub] = ((q_ref[rows, :] * (HEAD_DIM ** -0.5)).astype(BF16),
                         jnp.concatenate([k_prev, kc_ref[rows, :]], axis=0),
                         jnp.concatenate([v_prev, vc_ref[rows, :]], axis=0))
        qb, kk, vv = prep[sub]
        c0 = LANES * (kh // 2)
        kx = kk[:, c0:c0 + LANES]
        vx = vv[:, c0:c0 + LANES]
        kr = pltpu.roll(kx, HEAD_DIM, 1)
        vr = pltpu.roll(vx, HEAD_DIM, 1)
        if kh % 2 == 0:
            k_lo, k_hi = jnp.where(lo, kx, zeros), jnp.where(lo, zeros, kr)
            v_lo, v_hi = jnp.where(lo, vx, zeros), jnp.where(lo, zeros, vr)
        else:
            k_lo, k_hi = jnp.where(lo, kr, zeros), jnp.where(lo, zeros, kx)
            v_lo, v_hi = jnp.where(lo, vr, zeros), jnp.where(lo, zeros, vx)
        kmat = jnp.concatenate([k_lo, k_hi], axis=0).astype(BF16)
        qs = jnp.concatenate([qb[:, 2 * LANES * kh:2 * LANES * kh + LANES],
                              qb[:, 2 * LANES * kh + LANES:2 * LANES * (kh + 1)]], axis=0)
        scores[sub, kh] = lax.dot_general(qs, kmat, (((1,), (1,)), ((), ())), preferred_element_type=F32)
        vmats[sub, kh] = jnp.concatenate([jnp.concatenate([v_lo, ones_lo], axis=1),
                                          jnp.concatenate([v_hi, ones_hi], axis=1)], axis=0).astype(BF16)

    def stage_softmax(sub, kh):
        s = scores.pop((sub, kh)) + (bias_first if sub == 0 else bias_inner)
        sink_lo = jnp.where(top, sink_ref[layer, 4 * kh + 0], sink_ref[layer, 4 * kh + 2])
        sink_hi = jnp.where(top, sink_ref[layer, 4 * kh + 1], sink_ref[layer, 4 * kh + 3])
        m_lo = jnp.maximum(jnp.max(s[:, :2 * L], axis=1, keepdims=True), sink_lo)
        m_hi = jnp.maximum(jnp.max(s[:, 2 * L:], axis=1, keepdims=True), sink_hi)
        probs[sub, kh] = jnp.concatenate([jnp.exp(s[:, :2 * L] - m_lo), jnp.exp(s[:, 2 * L:] - m_hi)],
                                         axis=1).astype(BF16)
        sink_terms[sub, kh] = jnp.where(lo, jnp.exp(sink_lo - m_lo), jnp.exp(sink_hi - m_hi))

    def stage_values(sub, kh):
        oe = jnp.dot(probs.pop((sub, kh)), vmats.pop((sub, kh)), preferred_element_type=F32)
        o = oe[:, :LANES] / (oe[:, LANES:] + sink_terms.pop((sub, kh)))
        outs[sub] += [o[:L], o[L:]]
        if kh == N_KV_HEADS - 1:
            out = jnp.concatenate(outs[sub], axis=1)
            o_ref[sub * L:(sub + 1) * L, :] = _rms(out, g_ref[...]).astype(o_ref.dtype)

    for stage in (stage_scores, stage_softmax, stage_values):
        for unit in units:
            stage(*unit)


_ATTN_SUB_BLOCKS = 4


_CAST_SLAB_BYTES = 8 * 1024 * 1024


def _attn_steps(n_seq, seq):
    return n_seq * (seq // (WINDOW * _ATTN_SUB_BLOCKS))


def _can_cast_in_attn(n_seq, seq):
    steps = _attn_steps(n_seq, seq)
    bf16_rows = 2 * SUBLANES
    return (D_MODEL % steps == 0 and D_FF % steps == 0
            and (D_MODEL // steps) % bf16_rows == 0 and (D_FF // steps) % bf16_rows == 0
            and (D_MODEL // steps) * 2 * D_FF * 4 <= _CAST_SLAB_BYTES)


def _attn_prompt(z, sinks_all, g_attn_all, *, layer, n_seq, seq, cast=()):
    L = WINDOW
    n_sub = _ATTN_SUB_BLOCKS
    nb = seq // (L * n_sub)
    kcol = D_ATTN // D_KV
    vcol = kcol + 1
    cur = lambda col: (lambda n, b: (n * nb + b, col))
    prev = lambda col: (lambda n, b: (jnp.maximum((n * nb + b) * n_sub - 1, 0), col))
    in_specs = [
        pl.BlockSpec(memory_space=pltpu.SMEM),
        pl.BlockSpec((L * n_sub, D_ATTN), cur(0)),
        pl.BlockSpec((L * n_sub, D_KV), cur(kcol)),
        pl.BlockSpec((L, D_KV), prev(kcol)),
        pl.BlockSpec((L * n_sub, D_KV), cur(vcol)),
        pl.BlockSpec((L, D_KV), prev(vcol)),
        pl.BlockSpec((None, 1, D_ATTN), lambda n, b: (layer, 0, 0)),
    ]
    out_specs = [pl.BlockSpec((L * n_sub, D_ATTN), cur(0))]
    out_shape = [jax.ShapeDtypeStruct((n_seq * seq, D_ATTN), BF16)]
    args = [sinks_all, z, z, z, z, z, g_attn_all]
    for w_all, w_layer in cast:
        i_spec, o_spec, o_shape = _cast_specs(w_all, w_layer, n_seq * nb, lambda n, b: n * nb + b)
        in_specs.append(i_spec)
        out_specs.append(o_spec)
        out_shape.append(o_shape)
        args.append(w_all)
    outs = pl.pallas_call(
        functools.partial(_attn_prompt_kernel, n_sub=n_sub, n_cast=len(cast), layer=layer),
        grid=(n_seq, nb),
        in_specs=in_specs,
        out_specs=out_specs,
        out_shape=out_shape,
        compiler_params=_cparams(("parallel", "arbitrary")),
        name="attn_prompt",
    )(*args)
    return outs[0] if not cast else outs


def _lru_gates(xc, wg_ref, ba, bi, lam):
    g = jnp.dot(xc.astype(BF16), wg_ref[...], preferred_element_type=F32)
    r = jax.nn.sigmoid(g[:, :D_LRU] + ba)
    gi = jax.nn.sigmoid(g[:, D_LRU:] + bi)
    nl = -lam
    softplus = jnp.maximum(nl, 0.0) + jnp.log1p(jnp.exp(-jnp.abs(nl)))
    log_a = -LRU_C * r * softplus
    a = jnp.exp(log_a)
    th = jnp.tanh(log_a)
    m2 = (-2.0 * th) / (1.0 - th)
    mult = jnp.where(m2 > 0.0, m2 * lax.rsqrt(m2), 0.0)
    return a, mult, gi


def _shift_rows(u, prev8, k):
    r = pltpu.roll(u, k, 0)
    pr = pltpu.roll(prev8, k, 0)
    row8 = lax.broadcasted_iota(jnp.int32, prev8.shape, 0)
    head = jnp.where(row8 < k, pr, r[:SUBLANES])
    return jnp.concatenate([head, r[SUBLANES:]], axis=0)


def _chunk_scan(a, b):
    n = a.shape[0]
    row = lax.broadcasted_iota(jnp.int32, a.shape, 0)
    d = 1
    while d < n:
        if d < SUBLANES:
            keep = row >= d
            b = jnp.where(keep, b + a * pltpu.roll(b, d, 0), b)
            a = jnp.where(keep, a * pltpu.roll(a, d, 0), a)
        else:
            b = jnp.concatenate([b[:d], b[d:] + a[d:] * b[:n - d]], axis=0)
            a = jnp.concatenate([a[:d], a[d:] * a[:n - d]], axis=0)
        d *= 2
    return a, b


def _scan_pitch(ln):
    assert ln % SUBLANES == 0
    return ln if ln % (2 * SUBLANES) == SUBLANES else ln + SUBLANES


def _strided_scan(a, b, h_prev, a_scr, b_scr):
    tc, ch = a.shape
    ln = tc // SUBLANES
    pitch = _scan_pitch(ln)
    nslab = ch // LANES
    for s in range(SUBLANES):
        for c in range(nslab):
            a_scr[c, pitch * s:pitch * s + ln, :] = a[ln * s:ln * (s + 1), c * LANES:(c + 1) * LANES]
            b_scr[c, pitch * s:pitch * s + ln, :] = b[ln * s:ln * (s + 1), c * LANES:(c + 1) * LANES]
    row8 = lax.broadcasted_iota(jnp.int32, (SUBLANES, LANES), 0)
    for c in range(nslab):
        h = jnp.zeros((SUBLANES, LANES), F32)
        acum = jnp.ones((SUBLANES, LANES), F32)
        for j in range(ln):
            idx = pl.ds(j, SUBLANES, stride=pitch)
            at = a_scr[c, idx, :]
            h = at * h + b_scr[c, idx, :]
            acum = at * acum
            b_scr[c, idx, :] = h
            a_scr[c, idx, :] = acum
        a_tot, b_tot = _chunk_scan(acum, h)
        hp = h_prev[:, c * LANES:(c + 1) * LANES]
        cin = jnp.where(row8 == 0, hp, pltpu.roll(b_tot + a_tot * hp, 1, 0))
        for j in range(ln):
            idx = pl.ds(j, SUBLANES, stride=pitch)
            b_scr[c, idx, :] = b_scr[c, idx, :] + a_scr[c, idx, :] * cin
    return jnp.concatenate(
        [jnp.concatenate([b_scr[c, pitch * s:pitch * s + ln, :] for c in range(nslab)], axis=1)
         for s in range(SUBLANES)], axis=0)


def _lru_sc_rows(ux, gate, ub, uc, uh, pos0, wrefs, carries, scan_scr):
    cw_ref, cb_ref, wg_ref, ba_ref, bi_ref, lam_ref, scw_ref, glru_ref, gsc_ref = wrefs
    cx_scr, cg_scr, ch_scr = carries
    tc = ux.shape[0]
    px = cx_scr[...]
    xc = _shift_rows(ux, px, 3) * cw_ref[0:1, :]
    xc = xc + _shift_rows(ux, px, 2) * cw_ref[1:2, :]
    xc = xc + _shift_rows(ux, px, 1) * cw_ref[2:3, :]
    xc = xc + ux * cw_ref[3:4, :]
    xc = xc + cb_ref[...]

    a, mult, gi = _lru_gates(xc, wg_ref, ba_ref[...], bi_ref[...], lam_ref[...])
    pos = pos0 + lax.broadcasted_iota(jnp.int32, (tc, 1), 0)
    mult = jnp.where(pos == 0, 1.0, mult)
    h = _strided_scan(a, mult * gi * xc, ch_scr[SUBLANES - 1:SUBLANES, :], *scan_scr)
    o_lru = h * jax.nn.gelu(gate, approximate=True)

    gch = uc * uh
    pg = cg_scr[...]
    y = _shift_rows(gch, pg, 2) * scw_ref[0:1, :]
    y = y + _shift_rows(gch, pg, 1) * scw_ref[1:2, :]
    y = y + gch * scw_ref[2:3, :]
    o_sc = ub * y

    h8, x8, g8 = h[tc - SUBLANES:], ux[tc - SUBLANES:], gch[tc - SUBLANES:]
    cx_scr[...] = x8
    cg_scr[...] = g8
    ch_scr[...] = h8
    return _rms(o_lru, glru_ref[...]), _rms(o_sc, gsc_ref[...]), h8, x8, g8


_LRU_CHUNK = 256


_D_QKV = D_ATTN + 2 * D_KV
_D_U = D_IN - _D_QKV


def _proj_lru_kernel(x_ref, g_ref, w_ref, cos_ref, sin_ref, xs_ref, coss_ref, sins_ref,
                     cw_ref, cb_ref, wg_ref, ba_ref, bi_ref,
                     lam_ref, scw_ref, glru_ref, gsc_ref, *rest, blocks_per_seq, n_cast):
    qkv_ref, mb_ref, h8_ref, x8_ref, g8_ref, zs_ref = rest[n_cast:n_cast + 6]
    zu_scr, cx_scr, cg_scr, ch_scr, sa_scr, sb_scr = rest[2 * n_cast + 6:]
    for src, dst in zip(rest[:n_cast], rest[n_cast + 6:2 * n_cast + 6]):
        dst[...] = src[...].astype(BF16)
    i = pl.program_id(0)

    @pl.when(i == pl.num_programs(0) - 1)
    def _():
        _in_proj_kernel(xs_ref, g_ref, w_ref, coss_ref, sins_ref, zs_ref)

    blk = i % blocks_per_seq

    @pl.when(blk == 0)
    def _():
        cx_scr[...] = jnp.zeros_like(cx_scr)
        cg_scr[...] = jnp.zeros_like(cg_scr)
        ch_scr[...] = jnp.zeros_like(ch_scr)

    h = _rms(x_ref[...], g_ref[...]).astype(BF16)
    bm = h.shape[0]
    for c0 in range(_D_QKV, D_IN, _IN_PROJ_CHUNK):
        zu_scr[:, c0 - _D_QKV:c0 - _D_QKV + _IN_PROJ_CHUNK] = jnp.dot(
            h, w_ref[:, c0:c0 + _IN_PROJ_CHUNK], preferred_element_type=F32)

    wrefs = (cw_ref, cb_ref, wg_ref, ba_ref, bi_ref, lam_ref, scw_ref, glru_ref, gsc_ref)
    tc = _LRU_CHUNK
    col = lambda k: slice(k * D_LRU, (k + 1) * D_LRU)
    for r0 in range(0, bm, tc):
        rows = slice(r0, r0 + tc)
        lru_n, sc_n, h8, x8, g8 = _lru_sc_rows(
            zu_scr[rows, col(0)], zu_scr[rows, col(1)], zu_scr[rows, col(2)], zu_scr[rows, col(3)],
            zu_scr[rows, col(4)], blk * bm + r0, wrefs, (cx_scr, cg_scr, ch_scr), (sa_scr, sb_scr))
        mb_ref[rows, :D_LRU] = lru_n.astype(mb_ref.dtype)
        mb_ref[rows, D_LRU:] = sc_n.astype(mb_ref.dtype)
    h8_ref[0] = h8
    x8_ref[0] = x8
    g8_ref[0] = g8

    lane = lax.broadcasted_iota(jnp.int32, (bm, LANES), 1)
    lo32 = (lane % HEAD_DIM) < (HEAD_DIM // 2)
    cos = cos_ref[...]
    sin = sin_ref[...]

    def rope(a):
        sw = jnp.where(lo32, pltpu.roll(a, LANES - HEAD_DIM // 2, 1), pltpu.roll(a, HEAD_DIM // 2, 1))
        return a * cos + sw * sin

    rope_cols = D_ATTN + D_KV
    for c0 in range(0, _D_QKV, _IN_PROJ_CHUNK):
        acc = jnp.dot(h, w_ref[:, c0:c0 + _IN_PROJ_CHUNK], preferred_element_type=F32)
        for c in range(0, _IN_PROJ_CHUNK, LANES):
            a = acc[:, c:c + LANES]
            qkv_ref[:, c0 + c:c0 + c + LANES] = rope(a) if c0 + c < rope_cols else a


def _proj_lru(x, g_all, w_all_bf, cos_t, sin_t, xs, cos_s, sin_s, lw, *, layer, w_layer, n_seq, seq, bm,
              cast=()):
    m = x.shape[0]
    ns = xs.shape[0]
    n_tab = cos_t.shape[0] // bm
    bps = seq // bm
    const = lambda shape: pl.BlockSpec(shape, lambda i: (0,) * len(shape))
    st = pl.BlockSpec((1, SUBLANES, D_LRU), lambda i: (i // bps, 0, 0))
    st_shape = jax.ShapeDtypeStruct((n_seq, SUBLANES, D_LRU), F32)
    in_specs = [pl.BlockSpec((bm, D_MODEL), lambda i: (i, 0)),
                pl.BlockSpec((None, 1, D_MODEL), lambda i: (layer, 0, 0)),
                _resident((None, D_MODEL, D_IN), lambda i: (w_layer, 0, 0)),
                pl.BlockSpec((bm, LANES), lambda i: (i % n_tab, 0)),
                pl.BlockSpec((bm, LANES), lambda i: (i % n_tab, 0)),
                const((ns, D_MODEL)), const((ns, LANES)), const((ns, LANES))] + _mixer_param_specs(layer)
    out_specs = [pl.BlockSpec((bm, _D_QKV), lambda i: (i, 0)),
                 pl.BlockSpec((bm, D_LRU + D_SC), lambda i: (i, 0)), st, st, st, const((ns, D_IN))]
    out_shape = [jax.ShapeDtypeStruct((m, _D_QKV), F32),
                 jax.ShapeDtypeStruct((m, D_LRU + D_SC), BF16), st_shape, st_shape, st_shape,
                 jax.ShapeDtypeStruct((ns, D_IN), F32)]
    args = [x, g_all, w_all_bf, cos_t, sin_t, xs, cos_s, sin_s] + _mixer_params(lw)
    for w_all, lyr in cast:
        i_spec, o_spec, o_shape = _cast_specs(w_all, lyr, m // bm, lambda i: i)
        in_specs.append(i_spec)
        out_specs.append(o_spec)
        out_shape.append(o_shape)
        args.append(w_all)
    return pl.pallas_call(
        functools.partial(_proj_lru_kernel, blocks_per_seq=bps, n_cast=len(cast)),
        grid=(m // bm,),
        in_specs=in_specs,
        out_specs=out_specs,
        out_shape=out_shape,
        scratch_shapes=[pltpu.VMEM((bm, _D_U), F32)] + [pltpu.VMEM((SUBLANES, D_LRU), F32)] * 3
        + [pltpu.VMEM((D_LRU // LANES, SUBLANES * _scan_pitch(_LRU_CHUNK // SUBLANES), LANES), F32)] * 2,
        compiler_params=_cparams(("arbitrary",)),
        name="proj_lru",
    )(*args)


def _decode_kernel(q_ref, knt_ref, kc_ref, vc_ref, sel_ref, sink_ref, gat_ref,
                   ux_ref, gate_ref, ub_ref, uc_ref, uh_ref, h0_ref, cbuf_ref, sbuf_ref,
                   cw_ref, cb_ref, wg_ref, ba_ref, bi_ref, lam_ref, scw_ref, glru_ref, gsc_ref,
                   *rest, nbk, n_cast):
    n_rest = len(rest)
    oat_ref, ors_ref, ko_ref, vo_ref, hn_ref, cn_ref, sn_ref = rest[n_rest - 7 - n_cast:n_rest - n_cast]
    for src, dst in zip(rest[n_rest - 7 - 2 * n_cast:n_rest - 7 - n_cast], rest[n_rest - n_cast:]):
        dst[...] = src[...].astype(BF16)
    if len(ko_ref.shape) == 4:
        for d in range(1, ko_ref.shape[0]):
            ko_ref[d] = jnp.zeros(ko_ref.shape[1:], ko_ref.dtype)
            vo_ref[d] = jnp.zeros(vo_ref.shape[1:], vo_ref.dtype)
        ko_ref, vo_ref = ko_ref.at[0], vo_ref.at[0]
    wb = kc_ref.shape[2]
    hrow = lax.broadcasted_iota(jnp.int32, (N_HEADS, D_ATTN), 0)
    hcol = lax.broadcasted_iota(jnp.int32, (N_HEADS, D_ATTN), 1) // HEAD_DIM
    own = (hrow == hcol)
    qexp = jnp.concatenate(
        [jnp.where(own, jnp.broadcast_to(q_ref[i:i + 1, :] * (HEAD_DIM ** -0.5), (N_HEADS, D_ATTN)), 0.0)
         for i in range(nbk)], axis=0)
    qrow = jnp.dot(qexp.astype(BF16), sel_ref[...], preferred_element_type=F32)
    rows = nbk * N_HEADS
    grow = (lax.broadcasted_iota(jnp.int32, (rows, D_KV), 0) % N_HEADS) // N_GROUP
    gcol = lax.broadcasted_iota(jnp.int32, (rows, D_KV), 1) // HEAD_DIM
    kvmask = (grow == gcol)
    qm = jnp.where(kvmask, qrow, 0.0).astype(BF16)
    sink = jnp.concatenate([sink_ref[...][:, 0:1]] * nbk, axis=0)
    gat = jnp.concatenate([gat_ref[...]] * nbk, axis=0)
    lanek = lax.broadcasted_iota(jnp.int32, (D_KV, wb), 1)
    newest = lanek == wb - 1
    for i in range(nbk):
        ko_ref[i] = jnp.where(newest, jnp.broadcast_to(knt_ref[:D_KV, i:i + 1], (D_KV, wb)),
                              pltpu.roll(kc_ref[i], wb - 1, 1))
        vo_ref[i] = jnp.where(newest, jnp.broadcast_to(knt_ref[D_KV:, i:i + 1], (D_KV, wb)),
                              pltpu.roll(vc_ref[i], wb - 1, 1))
    s = jnp.concatenate(
        [jnp.dot(qm[i * N_HEADS:(i + 1) * N_HEADS], ko_ref[i].astype(BF16), preferred_element_type=F32)
         for i in range(nbk)], axis=0)
    m = jnp.maximum(jnp.max(s, axis=1, keepdims=True), sink)
    p = jnp.exp(s - m)
    p = (p / (jnp.sum(p, axis=1, keepdims=True) + jnp.exp(sink - m))).astype(BF16)
    of = jnp.concatenate(
        [lax.dot_general(p[i * N_HEADS:(i + 1) * N_HEADS], vo_ref[i].astype(BF16), (((1,), (1,)), ((), ())),
                         preferred_element_type=F32) for i in range(nbk)], axis=0)
    of = jnp.where(kvmask, of, 0.0)
    t = of[:, :LANES] + of[:, LANES:]
    o = t + pltpu.roll(t, HEAD_DIM, 1)
    rs = jnp.sum(o * o, axis=1, keepdims=True)
    for i in range(nbk):
        sl = slice(i * N_HEADS, (i + 1) * N_HEADS)
        ms = jnp.sum(rs[sl], axis=0, keepdims=True) * (0.5 / D_ATTN)
        oat_ref[i] = (o[sl] * lax.rsqrt(ms + RMS_EPS) * gat[sl]).astype(oat_ref.dtype)

    ux = ux_ref[...]
    xc = cbuf_ref[0] * cw_ref[0:1, :]
    xc = xc + cbuf_ref[1] * cw_ref[1:2, :]
    xc = xc + cbuf_ref[2] * cw_ref[2:3, :]
    xc = xc + ux * cw_ref[3:4, :]
    xc = xc + cb_ref[...]
    a, mult, gi = _lru_gates(xc, wg_ref, ba_ref[...], bi_ref[...], lam_ref[...])
    h = a * h0_ref[...] + mult * gi * xc
    o_lru = h * jax.nn.gelu(gate_ref[...], approximate=True)
    hn_ref[...] = h
    cn_ref[0] = cbuf_ref[1]
    cn_ref[1] = cbuf_ref[2]
    cn_ref[2] = ux
    gch = uc_ref[...] * uh_ref[...]
    y = sbuf_ref[:, 0, :] * scw_ref[0:1, :]
    y = y + sbuf_ref[:, 1, :] * scw_ref[1:2, :]
    y = y + gch * scw_ref[2:3, :]
    o_sc = ub_ref[...] * y
    sn_ref[:, 0, :] = sbuf_ref[:, 1, :]
    sn_ref[:, 1, :] = gch
    ors_ref[:, :D_LRU] = _rms(o_lru, glru_ref[...]).astype(ors_ref.dtype)
    ors_ref[:, D_LRU:] = _rms(o_sc, gsc_ref[...]).astype(ors_ref.dtype)


def _can_cast_in_decode(ns, nbk):
    steps = ns // nbk
    return (D_FF % steps == 0 and (D_FF // steps) % (2 * SUBLANES) == 0
            and (D_FF // steps) * D_MODEL * 4 <= _CAST_SLAB_BYTES)


def _decode_mix(z, knt, kct_all, vct_all, h0_all, cbuf_all, sbuf_all, lw, *, layer, nbk=16, stacked=None,
                cast=()):
    depth, ns, _, wb = kct_all.shape
    z512 = lambda col: pl.BlockSpec((nbk, D_LRU), lambda i: (i, col))
    const = lambda shape: pl.BlockSpec(shape, lambda i: (0,) * len(shape))
    cache_in = pl.BlockSpec((None, nbk, D_KV, wb), lambda i: (layer, i, 0, 0))
    cache_out = cache_in if stacked is not None else pl.BlockSpec((depth, nbk, D_KV, wb), lambda i: (0, i, 0, 0))
    n_in = 24
    extra_specs = [] if stacked is None else [pl.BlockSpec(memory_space=pl.ANY)] * 2
    extra_args = () if stacked is None else tuple(stacked)
    aliases = {} if stacked is None else {n_in: 2, n_in + 1: 3}
    cast_in, cast_out, cast_shape = [], [], []
    for w_all, w_layer in cast:
        i_spec, o_spec, o_shape = _cast_specs(w_all, w_layer, ns // nbk, lambda i: i)
        cast_in.append(i_spec)
        cast_out.append(o_spec)
        cast_shape.append(o_shape)
    outs = pl.pallas_call(
        functools.partial(_decode_kernel, nbk=nbk, n_cast=len(cast)),
        grid=(ns // nbk,),
        input_output_aliases=aliases,
        in_specs=[pl.BlockSpec((nbk, D_ATTN), lambda i: (i, 0)),
                  pl.BlockSpec((None, 2 * D_KV, nbk), lambda i: (i, 0, 0)),
                  cache_in, cache_in,
                  const((D_ATTN, D_KV)), _layer_spec(layer, (N_HEADS, LANES)), _layer_spec(layer, (N_HEADS, LANES)),
                  z512(_COL_UX), z512(_COL_GATE), z512(_COL_B), z512(_COL_C), z512(_COL_H),
                  pl.BlockSpec((None, nbk, D_LRU), lambda i: (layer, i, 0)),
                  pl.BlockSpec((None, LRU_CONV_W - 1, nbk, D_LRU), lambda i: (layer, 0, i, 0)),
                  pl.BlockSpec((None, nbk, SC_CONV_W - 1, D_SC), lambda i: (layer, i, 0, 0))]
        + _mixer_param_specs(layer) + extra_specs + cast_in,
        out_specs=[pl.BlockSpec((nbk, N_HEADS, LANES), lambda i: (i, 0, 0)),
                   pl.BlockSpec((nbk, D_LRU + D_SC), lambda i: (i, 0)),
                   cache_out, cache_out,
                   pl.BlockSpec((nbk, D_LRU), lambda i: (i, 0)),
                   pl.BlockSpec((LRU_CONV_W - 1, nbk, D_LRU), lambda i: (0, i, 0)),
                   pl.BlockSpec((nbk, SC_CONV_W - 1, D_SC), lambda i: (i, 0, 0))] + cast_out,
        out_shape=[jax.ShapeDtypeStruct((ns, N_HEADS, LANES), BF16),
                   jax.ShapeDtypeStruct((ns, D_LRU + D_SC), BF16),
                   jax.ShapeDtypeStruct((depth, ns, D_KV, wb), F32),
                   jax.ShapeDtypeStruct((depth, ns, D_KV, wb), F32),
                   jax.ShapeDtypeStruct((ns, D_LRU), F32),
                   jax.ShapeDtypeStruct((LRU_CONV_W - 1, ns, D_LRU), F32),
                   jax.ShapeDtypeStruct((ns, SC_CONV_W - 1, D_SC), F32)] + cast_shape,
        compiler_params=_cparams(("parallel",)),
        name="decode_mix",
    )(z, knt, kct_all, vct_all, lw["sel"], lw["sink_tab"], lw["g_attn_tab"],
      z, z, z, z, z, h0_all, cbuf_all, sbuf_all, *_mixer_params(lw), *extra_args, *[w for w, _ in cast])
    return outs


_OUT_PROJ_CHUNK = 512


def _out_proj_kernel(ma_ref, mb_ref, x_ref, mas_ref, mbs_ref, xs_ref, w_ref, g_ref,
                     x1_ref, hf_ref, x1s_ref, hfs_ref):
    _out_proj_rows(ma_ref, mb_ref, x_ref, w_ref, g_ref, x1_ref, hf_ref)

    @pl.when(pl.program_id(0) == pl.num_programs(0) - 1)
    def _():
        _out_proj_rows(mas_ref, mbs_ref, xs_ref, w_ref, g_ref, x1s_ref, hfs_ref)


def _out_proj_rows(ma_ref, mb_ref, x_ref, w_ref, g_ref, x1_ref, hf_ref):
    ma = ma_ref[...]
    mb = mb_ref[...]
    ssq = None
    for c0 in range(0, D_MODEL, _OUT_PROJ_CHUNK):
        cs = slice(c0, c0 + _OUT_PROJ_CHUNK)
        acc = jnp.dot(ma, w_ref[:D_ATTN, cs], preferred_element_type=F32)
        acc = acc + jnp.dot(mb, w_ref[D_ATTN:, cs], preferred_element_type=F32)
        x1 = x_ref[:, cs] + acc
        x1_ref[:, cs] = x1
        part = jnp.sum(x1 * x1, axis=-1, keepdims=True)
        ssq = part if ssq is None else ssq + part
    scale = lax.rsqrt(ssq * (1.0 / D_MODEL) + RMS_EPS)
    for c0 in range(0, D_MODEL, _OUT_PROJ_CHUNK):
        cs = slice(c0, c0 + _OUT_PROJ_CHUNK)
        hf_ref[:, cs] = (x1_ref[:, cs] * scale * g_ref[:, cs]).astype(hf_ref.dtype)


def _out_proj(ma, mb, x, mas, mbs, xs, w_all_bf, g_all, *, layer, w_layer, bm):
    m = x.shape[0]
    ns = xs.shape[0]
    blk = lambda width: pl.BlockSpec((bm, width), lambda i: (i, 0))
    whole = lambda width: pl.BlockSpec((ns, width), lambda i: (0, 0))
    return pl.pallas_call(
        _out_proj_kernel,
        grid=(m // bm,),
        in_specs=[blk(D_ATTN), blk(D_LRU + D_SC), blk(D_MODEL),
                  whole(D_ATTN), whole(D_LRU + D_SC), whole(D_MODEL),
                  _resident((None, D_MODEL, D_MODEL), lambda i: (w_layer, 0, 0)),
                  pl.BlockSpec((None, 1, D_MODEL), lambda i: (layer, 0, 0))],
        out_specs=[blk(D_MODEL), blk(D_MODEL), whole(D_MODEL), whole(D_MODEL)],
        out_shape=[jax.ShapeDtypeStruct((m, D_MODEL), F32), jax.ShapeDtypeStruct((m, D_MODEL), BF16),
                   jax.ShapeDtypeStruct((ns, D_MODEL), F32), jax.ShapeDtypeStruct((ns, D_MODEL), BF16)],
        compiler_params=_cparams(("arbitrary",)),
        name="out_proj",
    )(ma, mb, x, mas, mbs, xs, w_all_bf, g_all)


_FFN_DOWN_CHUNK = 512
_FFN_X1_CHUNK = 256


def _ffn_kernel(hf_ref, x1_ref, hfs_ref, x1s_ref, wg_ref, wu_ref, wd_ref, gfin_ref, *rest, final_norm,
                n_cast):
    o_ref, os_ref = rest[n_cast:n_cast + 2]
    f = pl.program_id(1)
    n_f = pl.num_programs(1)

    @pl.when(pl.program_id(0) == pl.num_programs(0) - 1)
    def _():
        hfs = hfs_ref[...]
        gate_s = jnp.dot(hfs, wg_ref[...], preferred_element_type=F32)
        up_s = jnp.dot(hfs, wu_ref[...], preferred_element_type=F32)
        hid_s = (gate_s * jax.nn.sigmoid(gate_s) * up_s).astype(BF16)
        down_s = jnp.dot(hid_s, wd_ref[...], preferred_element_type=F32)

        @pl.when(f == 0)
        def _():
            os_ref[...] = down_s

        @pl.when(f > 0)
        def _():
            os_ref[...] += down_s

        @pl.when(f == n_f - 1)
        def _():
            x2s = os_ref[...] + x1s_ref[...]
            os_ref[...] = _rms(x2s, gfin_ref[...]) if final_norm else x2s

    def tile_step(first):
        hf = hf_ref[...]
        gate = jnp.dot(hf, wg_ref[...], preferred_element_type=F32)
        up = jnp.dot(hf, wu_ref[...], preferred_element_type=F32)
        hid = (gate * jax.nn.sigmoid(gate) * up).astype(BF16)
        x1c = x1_ref[...]
        for c0 in range(0, D_MODEL, _FFN_DOWN_CHUNK):
            down = jnp.dot(hid, wd_ref[:, c0:c0 + _FFN_DOWN_CHUNK], preferred_element_type=F32)
            for h0 in range(0, _FFN_DOWN_CHUNK, _FFN_X1_CHUNK):
                cs = slice(c0 + h0, c0 + h0 + _FFN_X1_CHUNK)
                part = down[:, h0:h0 + _FFN_X1_CHUNK] + jnp.where(f == (c0 + h0) // _FFN_X1_CHUNK, x1c, 0.0)
                o_ref[:, cs] = part if first else o_ref[:, cs] + part
        for src, dst in zip(rest[:n_cast], rest[n_cast + 2:]):
            dst[...] = src[...].astype(BF16)

    @pl.when(f == 0)
    def _():
        tile_step(True)

    @pl.when(f > 0)
    def _():
        tile_step(False)

    if final_norm:
        @pl.when(f == pl.num_programs(1) - 1)
        def _():
            o_ref[...] = _rms(o_ref[...], gfin_ref[...])


def _ffn_cast_tiling(w_all, n_i, nf):
    _, k, n = w_all.shape
    if k % n_i or (k // n_i) % (2 * SUBLANES):
        return None
    for n_col in range(nf, 0, -1):
        if n % n_col == 0 and (n // n_col) % LANES == 0:
            return k // n_i, n // n_col, n_col
    return None


def _can_cast_in_ffn(weights, m, bm, tf=512):
    return all(_ffn_cast_tiling(w, m // bm, D_FF // tf) is not None for w in weights)


def _ffn(hf, x1, hfs, x1s, w_gu_bf, w_d_bf, g_final, *, layer, bm, tf=512, final_norm, cast_next=()):
    m = hf.shape[0]
    ns = hfs.shape[0]
    nf = D_FF // tf
    n_x1 = D_MODEL // _FFN_X1_CHUNK
    assert nf >= n_x1
    whole = lambda rows: pl.BlockSpec((rows, D_MODEL), lambda i, f: (0, 0))
    in_specs = [pl.BlockSpec((bm, D_MODEL), lambda i, f: (i, 0)),
                pl.BlockSpec((bm, _FFN_X1_CHUNK), lambda i, f: (i, jnp.minimum(f, n_x1 - 1))),
                whole(ns), whole(ns),
                pl.BlockSpec((None, D_MODEL, tf), lambda i, f: (layer, 0, f)),
                pl.BlockSpec((None, D_MODEL, tf), lambda i, f: (layer, 0, nf + f)),
                pl.BlockSpec((None, tf, D_MODEL), lambda i, f: (layer, f, 0)),
                whole(1)]
    out_specs = [pl.BlockSpec((bm, D_MODEL), lambda i, f: (i, 0)), whole(ns)]
    out_shape = [jax.ShapeDtypeStruct((m, D_MODEL), F32), jax.ShapeDtypeStruct((ns, D_MODEL), F32)]
    args = [hf, x1, hfs, x1s, w_gu_bf, w_gu_bf, w_d_bf, g_final]
    for w_next, layer_next in cast_next:
        rows, width, n_col = _ffn_cast_tiling(w_next, m // bm, nf)
        col = lambda f, n_col=n_col: jnp.minimum(f, n_col - 1)
        in_specs.append(pl.BlockSpec((None, rows, width),
                                     lambda i, f, col=col, lyr=layer_next: (lyr, i, col(f))))
        out_specs.append(pl.BlockSpec((None, rows, width), lambda i, f, col=col: (0, i, col(f))))
        out_shape.append(jax.ShapeDtypeStruct((1,) + w_next.shape[1:], BF16))
        args.append(w_next)
    return pl.pallas_call(
        functools.partial(_ffn_kernel, final_norm=final_norm, n_cast=len(cast_next)),
        grid=(m // bm, nf), in_specs=in_specs, out_specs=out_specs, out_shape=out_shape,
        compiler_params=_cparams(("arbitrary", "arbitrary")), name="ffn",
    )(*args)


def _rope_tables(pos):
    half = HEAD_DIM // 2
    inv = ROPE_THETA ** (-jnp.arange(half, dtype=F32) / half)
    ang = pos.astype(F32)[:, None] * inv[None, :]
    cos, sin = jnp.cos(ang), jnp.sin(ang)
    cos_t = jnp.tile(cos, (1, LANES // half))
    sin_t = jnp.tile(jnp.concatenate([-sin, sin], axis=1), (1, LANES // HEAD_DIM))
    return cos_t, sin_t


def _block_diag(w):
    dd, hh, blk, _ = w.shape
    eye = jnp.eye(hh, dtype=w.dtype)
    return (eye[None, :, None, :, None] * w[:, :, :, None, :]).reshape(dd, hh * blk, hh * blk)


def kernel(x_prompt, x_sample, state_lru_h, state_lru_conv, cache_swa_k, cache_swa_v, state_sconv,
           norm_mix, w_in, norm_grp, w_out, lru_conv_w, lru_conv_b, lru_w_a, lru_b_a, lru_w_i, lru_b_i,
           lru_lambda, sc_conv_w, attn_sinks, norm_ffn, ffn_w_gu, ffn_w_down, norm_final):
    n_p, t_p, _ = x_prompt.shape
    n_s, t_s, _ = x_sample.shape
    depth = w_in.shape[0]
    wb = cache_swa_k.shape[2]
    assert t_s == 1 and wb == WINDOW and n_s % _DECODE_SEQS == 0
    assert t_p % (WINDOW * _ATTN_SUB_BLOCKS) == 0 and t_p % _LRU_CHUNK == 0

    bm_p = _PROJ_ROWS if t_p % _PROJ_ROWS == 0 else _LRU_CHUNK
    bm_ffn = _FFN_ROWS if (n_p * t_p) % _FFN_ROWS == 0 else bm_p

    cos_p, sin_p = _rope_tables(jnp.arange(t_p, dtype=jnp.int32))
    cos_s, sin_s = _rope_tables(jnp.full((n_s,), PAST_LEN, dtype=jnp.int32))

    sel = (jnp.arange(D_ATTN)[:, None] % HEAD_DIM == jnp.arange(D_KV)[None, :] % HEAD_DIM).astype(BF16)

    xp = x_prompt.reshape(n_p * t_p, D_MODEL)
    xs = x_sample.reshape(n_s, D_MODEL)
    row = lambda v: v.reshape(1, -1)
    p_states, s_states = [], []
    n_blk = (n_p * t_p) // bm_p
    piggy = (_can_cast_in_attn(n_p, t_p)
             and _can_cast_in_ffn((w_in, ffn_w_gu, ffn_w_down, w_out), n_p * t_p, bm_ffn)
             and D_MODEL % n_blk == 0 and (D_MODEL // n_blk) % (2 * SUBLANES) == 0)
    if piggy:
        w_in_l, l_in = w_in[:1].astype(BF16), 0
    else:
        w_in_bf = w_in.astype(BF16)
        w_out_bf = w_out.astype(BF16)
        w_gu_bf = ffn_w_gu.astype(BF16)
        w_d_bf = ffn_w_down.astype(BF16)
    nbk = _DECODE_SEQS
    kct_all = cache_swa_k.transpose(0, 1, 3, 4, 2).reshape(depth, n_s, D_KV, wb)
    vct_all = cache_swa_v.transpose(0, 1, 3, 4, 2).reshape(depth, n_s, D_KV, wb)
    cbuf_all = state_lru_conv.transpose(0, 2, 1, 3)
    g_mix = norm_mix.reshape(depth, 1, D_MODEL)
    g_ffn = norm_ffn.reshape(depth, 1, D_MODEL)
    rows3 = lambda v: v.reshape(depth, 1, -1)
    g_attn_all = rows3(norm_grp[:, :D_ATTN])
    lw = dict(
        conv_w=lru_conv_w, conv_b=rows3(lru_conv_b),
        w_gates=jnp.concatenate([_block_diag(lru_w_a), _block_diag(lru_w_i)], axis=-1).astype(BF16),
        b_a=rows3(lru_b_a), b_i=rows3(lru_b_i), lam=rows3(lru_lambda),
        sc_w=sc_conv_w, g_lru=rows3(norm_grp[:, D_ATTN:D_ATTN + D_LRU]), g_sc=rows3(norm_grp[:, D_ATTN + D_LRU:]),
        sel=sel,
        sink_tab=jnp.broadcast_to(attn_sinks[:, :, None], (depth, N_HEADS, LANES)),
        g_attn_tab=jnp.tile(norm_grp[:, :D_ATTN].reshape(depth, N_HEADS, HEAD_DIM), (1, 1, LANES // HEAD_DIM)),
    )
    for l in range(depth):
        last = l == depth - 1

        if not piggy:
            w_in_l, w_out_l, w_gu_l, w_d_l, l_in, l_w = w_in_bf, w_out_bf, w_gu_bf, w_d_bf, l, l
        first_casts = piggy and l == 0
        qkv, mb, h8, x8, g8, zs, *w_out_0 = _proj_lru(
            xp, g_mix, w_in_l, cos_p, sin_p, xs, cos_s, sin_s, lw, layer=l, w_layer=l_in, n_seq=n_p, seq=t_p,
            bm=bm_p, cast=[(w_out, l)] if first_casts else ())
        dec_casts = first_casts and _can_cast_in_decode(n_s, nbk)
        if first_casts:
            (w_out_l,) = w_out_0
            ma, w_gu_l, *w_d_0 = _attn_prompt(
                qkv, attn_sinks, g_attn_all, layer=l, n_seq=n_p, seq=t_p,
                cast=[(ffn_w_gu, l)] if dec_casts else [(ffn_w_gu, l), (ffn_w_down, l)])
            if not dec_casts:
                (w_d_l,) = w_d_0
            l_w = 0
        else:
            ma = _attn_prompt(qkv, attn_sinks, g_attn_all, layer=l, n_seq=n_p, seq=t_p)
        z3 = qkv.reshape(n_p, t_p, _D_QKV)
        wbp = min(WINDOW, t_p)
        p_states.append((
            h8[:, SUBLANES - 1],
            x8[:, SUBLANES - (LRU_CONV_W - 1):],
            z3[:, t_p - wbp:, D_ATTN:D_ATTN + D_KV].reshape(n_p, wbp, N_KV_HEADS, HEAD_DIM),
            z3[:, t_p - wbp:, D_ATTN + D_KV:D_ATTN + 2 * D_KV].reshape(n_p, wbp, N_KV_HEADS, HEAD_DIM),
            g8[:, SUBLANES - (SC_CONV_W - 1):],
        ))

        knt = zs[:, D_ATTN:D_ATTN + 2 * D_KV].reshape(n_s // nbk, nbk, 2 * D_KV).transpose(0, 2, 1)
        oat, ors, k_stack, v_stack, h_new, c_new, s_new, *w_d_0 = _decode_mix(
            zs, knt, kct_all, vct_all, state_lru_h, cbuf_all, state_sconv, lw, layer=l, nbk=nbk,
            stacked=None if l == 0 else (k_stack, v_stack), cast=[(ffn_w_down, l)] if dec_casts else ())
        if dec_casts:
            (w_d_l,) = w_d_0
        c_new = c_new.transpose(1, 0, 2)
        mas = oat[:, :, :HEAD_DIM].reshape(n_s, D_ATTN)
        s_states.append((h_new, c_new, s_new))

        x1, hf, x1s, hfs = _out_proj(ma, mb, xp, mas, ors, xs, w_out_l, g_ffn, layer=l, w_layer=l_w, bm=bm_p)
        if piggy and not last:
            nxt = [(w, l + 1) for w in (w_in, ffn_w_gu, ffn_w_down, w_out)]
            xp, xs, w_in_l, w_gu_l, w_d_l, w_out_l = _ffn(hf, x1, hfs, x1s, w_gu_l, w_d_l, row(norm_final),
                                                          layer=l_w, bm=bm_ffn, final_norm=last, cast_next=nxt)
        else:
            xp, xs = _ffn(hf, x1, hfs, x1s, w_gu_l, w_d_l, row(norm_final), layer=l_w, bm=bm_ffn,
                          final_norm=last)

    y_prompt = xp.reshape(n_p, t_p, D_MODEL)
    y_sample = xs.reshape(n_s, t_s, D_MODEL)
    stack = lambda states, k: jnp.stack([st[k] for st in states])
    untranspose = lambda c: c.reshape(depth, n_s, N_KV_HEADS, HEAD_DIM, wb).transpose(0, 1, 4, 2, 3)
    return (y_prompt, y_sample,
            stack(p_states, 0), stack(p_states, 1), stack(p_states, 2), stack(p_states, 3), stack(p_states, 4),
            stack(s_states, 0), stack(s_states, 1), untranspose(k_stack), untranspose(v_stack),
            stack(s_states, 2))
```
